```python
import math
import jax, jax.numpy as jnp
from jax import lax
import numpy as np


D_MODEL = 1024
BATCH = 4
SEQ = 4096
DEPTH = 4
DEC_BATCH = 8
DEC_SEQ = 2048
PAST_LEN = 128

N_MEM = 256
GRID_W = 64
EPS = 1e-6
HY_WIDTH = 512
HY_SHORT = 3
HY_EMB = 33
HY_FH = 64
HY_SIN_FREQ = 1.0
HY_TARGET = 1e-2
HY_FAST = 0.3
HY_SLOW = 1.5
HY_SHIFT = 0.0
RET_HEADS = 4
RET_DK = 128
RET_DV = 128
RET_WIDTH = RET_HEADS * RET_DV
RET_CHUNK = 128
RET_THETA = 10000.0
ATT_HEADS = 4
ATT_KV_HEADS = 2
ATT_HD = 128
ATT_WIDTH = ATT_HEADS * ATT_HD
ATT_BLOCK = 128
ROPE_THETA = 10000.0
X_HEADS = 4
X_HD = D_MODEL // X_HEADS
D_FF = 4 * D_MODEL
N_BRANCH = 3
MIX_WIDTH = HY_WIDTH + RET_WIDTH + ATT_WIDTH
IN_SPLITS = (3 * HY_WIDTH, RET_HEADS * RET_DK, RET_HEADS * RET_DK, RET_WIDTH, RET_WIDTH, ATT_WIDTH, ATT_KV_HEADS * ATT_HD, ATT_KV_HEADS * ATT_HD, N_BRANCH * D_MODEL)
IN_COLS = sum(IN_SPLITS)

kernel_name = 'hybrid_hyena_retnet_gqa_encoder'


def _split_points(sizes):
    pts, acc = [], 0
    for s in sizes[:-1]:
        acc += s
        pts.append(acc)
    return pts


def rmsnorm(x, g):
    xf = x.astype(jnp.float32)
    y = xf * lax.rsqrt(jnp.mean(xf * xf, axis=-1, keepdims=True) + EPS)
    return (y * g.astype(jnp.float32)).astype(x.dtype)


def rotary_tables(pos, dim, theta):
    inv = theta ** (-jnp.arange(0, dim, 2, dtype=jnp.float32) / dim)
    ang = pos.astype(jnp.float32)[:, None] * inv[None, :]
    return jnp.cos(ang), jnp.sin(ang)


def apply_rotary(x, cos, sin):
    half = x.shape[-1] // 2
    c = cos[None, :, None, :].astype(x.dtype)
    s = sin[None, :, None, :].astype(x.dtype)
    x1, x2 = x[..., :half], x[..., half:]
    return jnp.concatenate([x1 * c - x2 * s, x1 * s + x2 * c], axis=-1)


def short_conv(u, w):
    L = u.shape[1]
    pad = HY_SHORT // 2
    up = jnp.pad(u, ((0, 0), (pad, pad), (0, 0)))
    out = up[:, 0:L] * w[0]
    for j in range(1, HY_SHORT):
        out = out + up[:, j:j + L] * w[j]
    return out


def hyena_spectrum(L, w1, b1, w2, b2, w3):
    f32 = jnp.float32
    t = jnp.linspace(0.0, 1.0, L, dtype=f32)[:, None]
    bands = (HY_EMB - 1) // 2
    w = 2.0 * math.pi * jnp.arange(L, dtype=f32)[:, None] / L
    f = jnp.linspace(1e-4, bands - 1, bands, dtype=f32)[None, :]
    z = jnp.concatenate([t, jnp.cos(f * w), -jnp.sin(f * w)], axis=-1)
    hdn = jnp.sin(HY_SIN_FREQ * (z @ w1.astype(f32) + b1.astype(f32)))
    hdn = jnp.sin(HY_SIN_FREQ * (hdn @ w2.astype(f32) + b2.astype(f32)))
    filt = (hdn @ w3.astype(f32)).reshape(L, 2, HY_WIDTH)
    min_decay = math.log(HY_TARGET) / HY_SLOW
    max_decay = math.log(HY_TARGET) / HY_FAST
    deltas = jnp.abs(jnp.linspace(min_decay, max_decay, HY_WIDTH, dtype=f32))
    mod = jnp.exp(-t * deltas[None, :]) + HY_SHIFT
    filt = filt * mod[:, None, :]
    k_fwd, k_bwd = filt[:, 0], filt[:, 1]
    kfull = jnp.concatenate([k_fwd, jnp.zeros((1, HY_WIDTH), f32), jnp.flip(k_bwd[1:], axis=0)], axis=0)
    return jnp.fft.rfft(kfull, axis=0)


def hyena_mixer(hy_in, conv_w, K, bias):
    u = short_conv(hy_in, conv_w)
    x0, x1, v = jnp.split(u, 3, axis=-1)
    L = u.shape[1]
    zf = (v * x1).astype(jnp.float32)
    Z = jnp.fft.rfft(zf, n=2 * L, axis=1)
    y = jnp.fft.irfft(Z * K[None], n=2 * L, axis=1)[:, :L] + zf * bias.astype(jnp.float32)
    return y.astype(x0.dtype) * x0


def retention_dir(q, k, v, log_gamma, strict):
    B, L, H, dk = q.shape
    dv = v.shape[-1]
    C = RET_CHUNK
    N = L // C
    qc = q.reshape(B, N, C, H, dk)
    kc = k.reshape(B, N, C, H, dk)
    vc = v.reshape(B, N, C, H, dv)
    idx = jnp.arange(C, dtype=jnp.float32)
    diff = idx[:, None] - idx[None, :]
    mask = diff > 0 if strict else diff >= 0
    dmat = jnp.where(mask[None], jnp.exp(jnp.maximum(diff, 0.0)[None] * log_gamma[:, None, None]), 0.0)
    scores = jnp.einsum('bnihd,bnjhd->bnhij', qc, kc) * dmat[None, None]
    intra = jnp.einsum('bnhij,bnjhe->bnihe', scores, vc)
    k_w = jnp.exp((C - 1 - idx)[:, None] * log_gamma[None, :])
    chunk_kv = jnp.einsum('bnjhd,jh,bnjhe->bnhde', kc, k_w, vc)
    chunk_decay = jnp.exp(C * log_gamma)[None, :, None, None]

    def step(S, kv):
        return S * chunk_decay + kv, S

    _, S_prev = lax.scan(step, jnp.zeros((B, H, dk, dv), jnp.float32), jnp.moveaxis(chunk_kv, 1, 0))
    S_prev = jnp.moveaxis(S_prev, 0, 1)
    q_w = jnp.exp((idx + 1.0)[:, None] * log_gamma[None, :])
    inter = jnp.einsum('bnihd,ih,bnhde->bnihe', qc, q_w, S_prev)
    return (intra + inter).reshape(B, L, H, dv)


def retention_mixer(rq, rk, rv, rg, decay_logit, cos, sin):
    B, L, _ = rq.shape
    q = apply_rotary(rq.reshape(B, L, RET_HEADS, RET_DK), cos, sin).astype(jnp.float32)
    k = apply_rotary(rk.reshape(B, L, RET_HEADS, RET_DK), cos, sin).astype(jnp.float32) * (RET_DK ** -0.5)
    v = rv.reshape(B, L, RET_HEADS, RET_DV).astype(jnp.float32)
    lg = jax.nn.log_sigmoid(decay_logit.astype(jnp.float32))
    fwd = retention_dir(q, k, v, lg[0], strict=False)
    bwd = jnp.flip(retention_dir(jnp.flip(q, 1), jnp.flip(k, 1), jnp.flip(v, 1), lg[1], strict=True), 1)
    o = fwd + bwd
    mu = jnp.mean(o, axis=-1, keepdims=True)
    oc = o - mu
    o = oc * lax.rsqrt(jnp.mean(oc * oc, axis=-1, keepdims=True) + EPS)
    o = o.reshape(B, L, RET_WIDTH).astype(rg.dtype)
    return jax.nn.silu(rg) * o


def axial_rotary(x, rc, rs, cc, cs):
    half = x.shape[-1] // 2
    return jnp.concatenate([apply_rotary(x[..., :half], rc, rs), apply_rotary(x[..., half:], cc, cs)], axis=-1)


def block_attention(q, k, v):
    B, L, H, hd = q.shape
    G = H // ATT_KV_HEADS
    nb = L // ATT_BLOCK
    qb = q.reshape(B, nb, ATT_BLOCK, ATT_KV_HEADS, G, hd).transpose(1, 0, 2, 3, 4, 5)
    scale = hd ** -0.5

    def one(qblk):
        s = jnp.einsum('bqkgd,bskd->bkgqs', qblk, k).astype(jnp.float32) * scale
        p = jax.nn.softmax(s, axis=-1).astype(v.dtype)
        return jnp.einsum('bkgqs,bskd->bqkgd', p, v)

    out = lax.map(one, qb)
    return out.transpose(1, 0, 2, 3, 4, 5).reshape(B, L, H * hd)


def attention_mixer(aq, ak, av, qn, kn, rc, rs, cc, cs):
    B, L, _ = aq.shape
    q = rmsnorm(aq.reshape(B, L, ATT_HEADS, ATT_HD), qn)
    k = rmsnorm(ak.reshape(B, L, ATT_KV_HEADS, ATT_HD), kn)
    v = av.reshape(B, L, ATT_KV_HEADS, ATT_HD)
    q = axial_rotary(q, rc, rs, cc, cs)
    k = axial_rotary(k, rc, rs, cc, cs)
    return block_attention(q, k, v)


def cross_attention(h, m, wq, wkv, wo):
    B, L, _ = h.shape
    M = m.shape[1]
    q = (h @ wq).reshape(B, L, X_HEADS, X_HD)
    kk, vv = jnp.split(m @ wkv, 2, axis=-1)
    kk = kk.reshape(B, M, X_HEADS, X_HD)
    vv = vv.reshape(B, M, X_HEADS, X_HD)
    s = jnp.einsum('blhd,bmhd->bhlm', q, kk).astype(jnp.float32) * (X_HD ** -0.5)
    p = jax.nn.softmax(s, axis=-1).astype(vv.dtype)
    o = jnp.einsum('bhlm,bmhd->blhd', p, vv).reshape(B, L, D_MODEL)
    return o @ wo


def trunk(x, mem, g_mix_pre, g_mix_post, w_in, hy_conv, hy_fw1, hy_fb1, hy_fw2, hy_fb2, hy_fw3, hy_bias,
          ret_decay, att_qnorm, att_knorm, w_branch, w_out, g_x_pre, g_x_post, g_mem, w_xq, w_xkv, w_xo,
          g_ff_pre, g_ff_post, w_ff1, w_ff2):
    B, L, _ = x.shape
    rows = L // GRID_W
    pos = jnp.arange(L)
    row_ids = jnp.repeat(jnp.arange(rows), GRID_W)
    col_ids = jnp.tile(jnp.arange(GRID_W), rows)
    ret_cos, ret_sin = rotary_tables(pos, RET_DK, RET_THETA)
    rc, rs = rotary_tables(row_ids, ATT_HD // 2, ROPE_THETA)
    cc, cs = rotary_tables(col_ids, ATT_HD // 2, ROPE_THETA)
    split_pts = _split_points(IN_SPLITS)
    for l in range(DEPTH):
        h = rmsnorm(x, g_mix_pre[l])
        proj = h @ w_in[l]
        hy_in, rq, rk, rv, rg, aq, ak, av, gates = jnp.split(proj, split_pts, axis=-1)
        K = hyena_spectrum(L, hy_fw1[l], hy_fb1[l], hy_fw2[l], hy_fb2[l], hy_fw3[l])
        y_h = hyena_mixer(hy_in, hy_conv[l], K, hy_bias[l])
        y_r = retention_mixer(rq, rk, rv, rg, ret_decay[l], ret_cos, ret_sin)
        y_a = attention_mixer(aq, ak, av, att_qnorm[l], att_knorm[l], rc, rs, cc, cs)
        wb = w_branch[l]
        p_h = y_h @ wb[:HY_WIDTH]
        p_r = y_r @ wb[HY_WIDTH:HY_WIDTH + RET_WIDTH]
        p_a = y_a @ wb[HY_WIDTH + RET_WIDTH:]
        g = jax.nn.sigmoid(gates.reshape(B, L, N_BRANCH, D_MODEL))
        merged = g[:, :, 0] * p_h + g[:, :, 1] * p_r + g[:, :, 2] * p_a
        x = x + rmsnorm(merged @ w_out[l], g_mix_post[l])
        h = rmsnorm(x, g_x_pre[l])
        m = rmsnorm(mem, g_mem[l])
        x = x + rmsnorm(cross_attention(h, m, w_xq[l], w_xkv[l], w_xo[l]), g_x_post[l])
        h = rmsnorm(x, g_ff_pre[l])
        u = jnp.square(jax.nn.relu(h @ w_ff1[l]))
        x = x + rmsnorm(u @ w_ff2[l], g_ff_post[l])
    return x


def setup_inputs(seed: int = 0) -> dict:
    key = jax.random.key(seed)
    ks = iter(jax.random.split(key, 40))

    def nrm(shape, scale):
        return jax.random.normal(next(ks), shape, jnp.float32) * scale

    def gain(shape):
        return 1.0 + nrm(shape, 0.01)

    base = 1.0 - 2.0 ** (-5.0 - jnp.arange(RET_HEADS, dtype=jnp.float32))
    logit = jnp.log(base / (1.0 - base))
    d = {}
    d['x_prompt'] = nrm((BATCH, SEQ, D_MODEL), 1.0)
    d['x_sample'] = nrm((DEC_BATCH, DEC_SEQ, D_MODEL), 1.0)
    d['mem_prompt'] = nrm((BATCH, N_MEM, D_MODEL), 1.0)
    d['mem_sample'] = nrm((DEC_BATCH, N_MEM, D_MODEL), 1.0)
    d['g_mix_pre'] = gain((DEPTH, D_MODEL))
    d['g_mix_post'] = gain((DEPTH, D_MODEL))
    d['w_in'] = nrm((DEPTH, D_MODEL, IN_COLS), D_MODEL ** -0.5)
    d['hy_conv'] = nrm((DEPTH, HY_SHORT, 3 * HY_WIDTH), HY_SHORT ** -0.5)
    d['hy_fw1'] = nrm((DEPTH, HY_EMB, HY_FH), HY_EMB ** -0.5)
    d['hy_fb1'] = nrm((DEPTH, HY_FH), 0.02)
    d['hy_fw2'] = nrm((DEPTH, HY_FH, HY_FH), HY_FH ** -0.5)
    d['hy_fb2'] = nrm((DEPTH, HY_FH), 0.02)
    d['hy_fw3'] = nrm((DEPTH, HY_FH, 2 * HY_WIDTH), 0.02)
    d['hy_bias'] = nrm((DEPTH, HY_WIDTH), 1.0)
    d['ret_decay'] = logit[None, None, :] + nrm((DEPTH, 2, RET_HEADS), 0.1)
    d['att_qnorm'] = gain((DEPTH, ATT_HD))
    d['att_knorm'] = gain((DEPTH, ATT_HD))
    d['w_branch'] = nrm((DEPTH, MIX_WIDTH, D_MODEL), HY_WIDTH ** -0.5)
    d['w_out'] = nrm((DEPTH, D_MODEL, D_MODEL), D_MODEL ** -0.5)
    d['g_x_pre'] = gain((DEPTH, D_MODEL))
    d['g_x_post'] = gain((DEPTH, D_MODEL))
    d['g_mem'] = gain((DEPTH, D_MODEL))
    d['w_xq'] = nrm((DEPTH, D_MODEL, D_MODEL), D_MODEL ** -0.5)
    d['w_xkv'] = nrm((DEPTH, D_MODEL, 2 * D_MODEL), D_MODEL ** -0.5)
    d['w_xo'] = nrm((DEPTH, D_MODEL, D_MODEL), D_MODEL ** -0.5)
    d['g_ff_pre'] = gain((DEPTH, D_MODEL))
    d['g_ff_post'] = gain((DEPTH, D_MODEL))
    d['w_ff1'] = nrm((DEPTH, D_MODEL, D_FF), D_MODEL ** -0.5)
    d['w_ff2'] = nrm((DEPTH, D_FF, D_MODEL), D_FF ** -0.5)
    return d


def reference(x_prompt, x_sample, mem_prompt, mem_sample, g_mix_pre, g_mix_post, w_in, hy_conv, hy_fw1, hy_fb1,
              hy_fw2, hy_fb2, hy_fw3, hy_bias, ret_decay, att_qnorm, att_knorm, w_branch, w_out, g_x_pre, g_x_post,
              g_mem, w_xq, w_xkv, w_xo, g_ff_pre, g_ff_post, w_ff1, w_ff2):
    y_prompt = trunk(x_prompt, mem_prompt, g_mix_pre, g_mix_post, w_in, hy_conv, hy_fw1, hy_fb1, hy_fw2, hy_fb2,
                     hy_fw3, hy_bias, ret_decay, att_qnorm, att_knorm, w_branch, w_out, g_x_pre, g_x_post, g_mem,
                     w_xq, w_xkv, w_xo, g_ff_pre, g_ff_post, w_ff1, w_ff2)
    y_sample = trunk(x_sample, mem_sample, g_mix_pre, g_mix_post, w_in, hy_conv, hy_fw1, hy_fb1, hy_fw2, hy_fb2,
                     hy_fw3, hy_bias, ret_decay, att_qnorm, att_knorm, w_branch, w_out, g_x_pre, g_x_post, g_mem,
                     w_xq, w_xkv, w_xo, g_ff_pre, g_ff_post, w_ff1, w_ff2)
    return (y_prompt, y_sample)
```

```python
import functools
import math

import jax
import jax.numpy as jnp
from jax import lax
from jax.experimental import pallas as pl
from jax.experimental.pallas import tpu as pltpu

F32 = jnp.float32
BF16 = jnp.bfloat16

D_MODEL = 1024
DEPTH = 4
GRID_W = 64
EPS = 1e-6
HY_WIDTH = 512
HY_SHORT = 3
HY_EMB = 33
HY_FH = 64
HY_SIN_FREQ = 1.0
HY_TARGET = 1e-2
HY_FAST = 0.3
HY_SLOW = 1.5
HY_SHIFT = 0.0
RET_HEADS = 4
RET_DK = 128
RET_DV = 128
RET_WIDTH = RET_HEADS * RET_DV
RET_CHUNK = 128
RET_THETA = 10000.0
ATT_HEADS = 4
ATT_KV_HEADS = 2
ATT_HD = 128
ATT_WIDTH = ATT_HEADS * ATT_HD
ROPE_THETA = 10000.0
X_HEADS = 4
X_HD = D_MODEL // X_HEADS
D_FF = 4 * D_MODEL
N_BRANCH = 3
MIX_WIDTH = HY_WIDTH + RET_WIDTH + ATT_WIDTH

COL_HY = 0
COL_RQ = 3 * HY_WIDTH
COL_RK = COL_RQ + RET_HEADS * RET_DK
COL_RV = COL_RK + RET_HEADS * RET_DK
COL_RG = COL_RV + RET_WIDTH
COL_AQ = COL_RG + RET_WIDTH
COL_AK = COL_AQ + ATT_WIDTH
COL_AV = COL_AK + ATT_KV_HEADS * ATT_HD
COL_GATE = COL_AV + ATT_KV_HEADS * ATT_HD
IN_COLS = COL_GATE + N_BRANCH * D_MODEL

LANES = 128
DFT_N2 = 128
VMEM_LIMIT = 56 * 1024 * 1024


def _params(sem):
    return pltpu.CompilerParams(dimension_semantics=sem, vmem_limit_bytes=VMEM_LIMIT)


def _dot(a, b):
    return jnp.dot(a, b, preferred_element_type=F32)


def _dot_nt(a, b):
    return lax.dot_general(a, b, (((1,), (1,)), ((), ())), preferred_element_type=F32)


def _dot_tn(a, b):
    return lax.dot_general(a, b, (((0,), (0,)), ((), ())), preferred_element_type=F32)


def _rms(x, g):
    ms = jnp.mean(x * x, axis=-1, keepdims=True)
    return x * lax.rsqrt(ms + EPS) * g


def _sigmoid(x):
    return 1.0 / (1.0 + jnp.exp(-x))


def _split(x):
    hi = x.astype(BF16)
    lo = (x - hi.astype(F32)).astype(BF16)
    return hi, lo


def _dot3(a, b):
    return _dot(a[0], b[0]) + (_dot(a[0], b[1]) + _dot(a[1], b[0]))


def _nm_body(x_ref, g_ref, w_ref, o_ref, h_ref, *, act):
    @pl.when(pl.program_id(1) == 0)
    def _():
        h_ref[...] = _rms(x_ref[...], g_ref[...]).astype(BF16)

    y = _dot(h_ref[...], w_ref[...])
    if act:
        y = jnp.square(jnp.maximum(y, 0.0))
    o_ref[...] = y.astype(o_ref.dtype)


def _norm_matmul(x, g, w, *, tm, tn, act=False, out_dtype=F32):
    T, D = x.shape
    N = w.shape[1]
    return pl.pallas_call(
        functools.partial(_nm_body, act=act),
        grid=(T // tm, N // tn),
        in_specs=[pl.BlockSpec((tm, D), lambda i, j: (i, 0)),
                  pl.BlockSpec((1, D), lambda i, j: (0, 0)),
                  pl.BlockSpec((D, tn), lambda i, j: (0, j))],
        out_specs=pl.BlockSpec((tm, tn), lambda i, j: (i, j)),
        out_shape=jax.ShapeDtypeStruct((T, N), out_dtype),
        scratch_shapes=[pltpu.VMEM((tm, D), BF16)],
        compiler_params=_params(("parallel", "arbitrary")),
        name="norm_matmul",
    )(x, g.reshape(1, D), w)


def _mnr_body(a_ref, w_ref, g_ref, x_ref, o_ref):
    y = _dot(a_ref[...], w_ref[...])
    o_ref[...] = x_ref[...] + _rms(y, g_ref[...])


def _matmul_norm_residual(a, w, g, x, *, tm):
    T, K = a.shape
    D = w.shape[1]
    return pl.pallas_call(
        _mnr_body,
        grid=(T // tm,),
        in_specs=[pl.BlockSpec((tm, K), lambda i: (i, 0)),
                  pl.BlockSpec((K, D), lambda i: (0, 0)),
                  pl.BlockSpec((1, D), lambda i: (0, 0)),
                  pl.BlockSpec((tm, D), lambda i: (i, 0))],
        out_specs=pl.BlockSpec((tm, D), lambda i: (i, 0)),
        out_shape=jax.ShapeDtypeStruct((T, D), F32),
        compiler_params=_params(("parallel",)),
        name="matmul_norm_residual",
    )(a, w, g.reshape(1, D), x)


def _hyprep_body(x0_ref, x1_ref, v_ref, w_ref, zf_ref, x0c_ref):
    L = x0_ref.shape[1]
    row = lax.broadcasted_iota(jnp.int32, (L, LANES), 0)

    def conv(p, grp):
        w = w_ref[:, grp, :]
        prev = jnp.where(row == 0, 0.0, pltpu.roll(p, 1, 0))
        nxt = jnp.where(row == L - 1, 0.0, pltpu.roll(p, L - 1, 0))
        return prev * w[0:1] + p * w[1:2] + nxt * w[2:3]

    x0c_ref[0] = conv(x0_ref[0], 0)
    zf_ref[0] = conv(v_ref[0], 2) * conv(x1_ref[0], 1)


def _hyena_prep(proj, conv_w):
    B, L, _ = proj.shape
    nb = HY_WIDTH // LANES
    w = conv_w.reshape(HY_SHORT, 3, HY_WIDTH)
    blk = lambda off: pl.BlockSpec((1, L, LANES), lambda b, c, off=off: (b, 0, off + c))
    out = pl.BlockSpec((1, L, LANES), lambda b, c: (b, 0, c))
    return pl.pallas_call(
        _hyprep_body,
        grid=(B, nb),
        in_specs=[blk(0), blk(nb), blk(2 * nb),
                  pl.BlockSpec((HY_SHORT, 3, LANES), lambda b, c: (0, 0, c))],
        out_specs=[out, out],
        out_shape=[jax.ShapeDtypeStruct((B, L, HY_WIDTH), F32)] * 2,
        compiler_params=_params(("parallel", "parallel")),
        name="hyena_prep",
    )(proj, proj, proj, w)


def _filt_body(f_ref, w1_ref, b1_ref, w2_ref, b2_ref, w3_ref, dl_ref, o_ref, *, L):
    tl = f_ref.shape[0]
    hp = functools.partial(jnp.dot, preferred_element_type=F32, precision=lax.Precision.HIGHEST)
    f = f_ref[...]
    h = jnp.sin(HY_SIN_FREQ * (hp(f, w1_ref[...]) + b1_ref[...]))
    h = jnp.sin(HY_SIN_FREQ * (hp(h, w2_ref[...]) + b2_ref[...]))
    filt = hp(h, w3_ref[...])
    mod = jnp.exp(-f[:, 0:1] * dl_ref[...]) + HY_SHIFT
    row = pl.program_id(0) * tl + lax.broadcasted_iota(jnp.int32, (tl, HY_WIDTH), 0)
    sel = jnp.where(row < L, filt[:, :HY_WIDTH], filt[:, HY_WIDTH:])
    o_ref[...] = jnp.where(row == L, 0.0, sel * mod)


def _hyena_features(L):
    t = jnp.linspace(0.0, 1.0, L, dtype=F32)[:, None]
    bands = (HY_EMB - 1) // 2
    w = 2.0 * math.pi * jnp.arange(L, dtype=F32)[:, None] / L
    f = jnp.linspace(1e-4, bands - 1, bands, dtype=F32)[None, :]
    z = jnp.concatenate([t, jnp.cos(f * w), -jnp.sin(f * w)], axis=-1)
    zc = jnp.concatenate([z, z[:1], jnp.flip(z[1:], axis=0)], axis=0)
    return jnp.pad(zc, ((0, 0), (0, LANES - HY_EMB)))


def _hyena_filter(feats, w1, b1, w2, b2, w3, L):
    tl = 512
    pad = LANES - HY_FH
    w1p = jnp.pad(w1, ((0, LANES - HY_EMB), (0, pad)))
    b1p = jnp.pad(b1, (0, pad)).reshape(1, LANES)
    w2p = jnp.pad(w2, ((0, pad), (0, pad)))
    b2p = jnp.pad(b2, (0, pad)).reshape(1, LANES)
    w3p = jnp.pad(w3, ((0, pad), (0, 0)))
    min_decay = math.log(HY_TARGET) / HY_SLOW
    max_decay = math.log(HY_TARGET) / HY_FAST
    deltas = jnp.abs(jnp.linspace(min_decay, max_decay, HY_WIDTH, dtype=F32)).reshape(1, HY_WIDTH)
    full = lambda shape: pl.BlockSpec(shape, lambda i: (0,) * len(shape))
    return pl.pallas_call(
        functools.partial(_filt_body, L=L),
        grid=(2 * L // tl,),
        in_specs=[pl.BlockSpec((tl, LANES), lambda i: (i, 0)),
                  full((LANES, LANES)), full((1, LANES)), full((LANES, LANES)), full((1, LANES)),
                  full((LANES, 2 * HY_WIDTH)), full((1, HY_WIDTH))],
        out_specs=pl.BlockSpec((tl, HY_WIDTH), lambda i: (i, 0)),
        out_shape=jax.ShapeDtypeStruct((2 * L, HY_WIDTH), F32),
        compiler_params=_params(("parallel",)),
        name="hyena_filter",
    )(feats, w1p, b1p, w2p, b2p, w3p, deltas)


def _dft_tables(L):
    N = 2 * L
    N2 = DFT_N2
    N1 = N // N2
    N1h = N1 // 2
    k1 = jnp.arange(N1, dtype=jnp.int32)[:, None]
    n1 = jnp.arange(N1, dtype=jnp.int32)[None, :]
    a1 = ((k1 * n1) % N1).astype(F32) * (2.0 * math.pi / N1)
    c1, s1 = jnp.cos(a1), jnp.sin(a1)
    kk = (jnp.arange(N1, dtype=jnp.int32)[:, None, None]
          + N1 * jnp.arange(N2, dtype=jnp.int32)[None, :, None])
    n2 = jnp.arange(N2, dtype=jnp.int32)[None, None, :]
    a2 = ((kk * n2) % N).astype(F32) * (2.0 * math.pi / N)
    gr, gi = jnp.cos(a2), -jnp.sin(a2)
    return dict(
        N1=N1, N1h=N1h, N2=N2,
        fwd_full=jnp.stack([c1, -s1]),
        fwd_half=jnp.stack([c1[:, :N1h], -s1[:, :N1h]]),
        inv_half=jnp.stack([c1[:N1h], s1[:N1h]]) * (1.0 / N),
        g=jnp.stack([gr, gi]),
        gt=jnp.stack([jnp.swapaxes(gr, 1, 2), jnp.swapaxes(gi, 1, 2)]),
    )


def _dft1_body(x_ref, f_ref, o_ref):
    x = _split(x_ref[0])
    o_ref[0, 0] = _dot3(_split(f_ref[0]), x)
    o_ref[0, 1] = _dot3(_split(f_ref[1]), x)


def _dft_stage1(x, f, *, tc):
    Bx, K, W = x.shape
    N1 = f.shape[1]
    return pl.pallas_call(
        _dft1_body,
        grid=(Bx, W // tc),
        in_specs=[pl.BlockSpec((1, K, tc), lambda b, j: (b, 0, j)),
                  pl.BlockSpec((2, N1, K), lambda b, j: (0, 0, 0))],
        out_specs=pl.BlockSpec((1, 2, N1, tc), lambda b, j: (b, 0, 0, j)),
        out_shape=jax.ShapeDtypeStruct((Bx, 2, N1, W), F32),
        compiler_params=_params(("parallel", "parallel")),
        name="dft_stage1",
    )(x, f)


def _cmatmul(gr, gi, ar, ai):
    gr, gi, ar, ai = _split(gr), _split(gi), _split(ar), _split(ai)
    return _dot3(gr, ar) - _dot3(gi, ai), _dot3(gr, ai) + _dot3(gi, ar)


def _spec_body(a_ref, g_ref, o_ref):
    def step(k, carry):
        yr, yi = _cmatmul(g_ref[0, k], g_ref[1, k], a_ref[0, 0, k], a_ref[0, 1, k])
        o_ref[0, k] = yr
        o_ref[1, k] = yi
        return carry

    lax.fori_loop(0, a_ref.shape[2], step, 0)


def _dft_spectrum(a, g, *, kb):
    _, _, N1, N2, C = a.shape
    return pl.pallas_call(
        _spec_body,
        grid=(N1 // kb,),
        in_specs=[pl.BlockSpec((1, 2, kb, N2, C), lambda i: (0, 0, i, 0, 0)),
                  pl.BlockSpec((2, kb, N2, N2), lambda i: (0, i, 0, 0))],
        out_specs=pl.BlockSpec((2, kb, N2, C), lambda i: (0, i, 0, 0)),
        out_shape=jax.ShapeDtypeStruct((2, N1, N2, C), F32),
        compiler_params=_params(("parallel",)),
        name="dft_spectrum",
    )(a, g)


def _dft2_body(a_ref, g_ref, gt_ref, ks_ref, o_ref):
    def step(k, carry):
        yr, yi = _cmatmul(g_ref[0, k], g_ref[1, k], a_ref[0, 0, k], a_ref[0, 1, k])
        kr, ki = ks_ref[0, k], ks_ref[1, k]
        zr = yr * kr - yi * ki
        zi = yr * ki + yi * kr
        tr, ti = _cmatmul(gt_ref[0, k], -gt_ref[1, k], zr, zi)
        o_ref[0, 0, k] = tr
        o_ref[0, 1, k] = ti
        return carry

    lax.fori_loop(0, a_ref.shape[2], step, 0)


def _dft_stage2(a, g, gt, ks, *, kb):
    B, _, N1, N2, C = a.shape
    return pl.pallas_call(
        _dft2_body,
        grid=(N1 // kb, B),
        in_specs=[pl.BlockSpec((1, 2, kb, N2, C), lambda i, b: (b, 0, i, 0, 0)),
                  pl.BlockSpec((2, kb, N2, N2), lambda i, b: (0, i, 0, 0)),
                  pl.BlockSpec((2, kb, N2, N2), lambda i, b: (0, i, 0, 0)),
                  pl.BlockSpec((2, kb, N2, C), lambda i, b: (0, i, 0, 0))],
        out_specs=pl.BlockSpec((1, 2, kb, N2, C), lambda i, b: (b, 0, i, 0, 0)),
        out_shape=jax.ShapeDtypeStruct((B, 2, N1, N2, C), F32),
        compiler_params=_params(("parallel", "parallel")),
        name="dft_stage2",
    )(a, g, gt, ks)


def _dft3_body(t_ref, f_ref, zf_ref, x0_ref, bias_ref, o_ref):
    y = _dot3(_split(f_ref[0]), _split(t_ref[0, 0])) - _dot3(_split(f_ref[1]), _split(t_ref[0, 1]))
    o_ref[0] = ((y + zf_ref[0] * bias_ref[...]) * x0_ref[0]).astype(o_ref.dtype)


def _dft_stage3(t, f, zf, x0, bias_t, *, tc):
    B, _, N1, W = t.shape
    N1h = f.shape[1]
    half = pl.BlockSpec((1, N1h, tc), lambda b, j: (b, 0, j))
    return pl.pallas_call(
        _dft3_body,
        grid=(B, W // tc),
        in_specs=[pl.BlockSpec((1, 2, N1, tc), lambda b, j: (b, 0, 0, j)),
                  pl.BlockSpec((2, N1h, N1), lambda b, j: (0, 0, 0)),
                  half, half,
                  pl.BlockSpec((1, tc), lambda b, j: (0, j))],
        out_specs=half,
        out_shape=jax.ShapeDtypeStruct((B, N1h, W), BF16),
        compiler_params=_params(("parallel", "parallel")),
        name="dft_stage3",
    )(t, f, zf, x0, bias_t)


def _hyena_mixer(proj, conv_w, kspec, bias, tabs):
    B, L, _ = proj.shape
    N1, N1h, N2 = tabs["N1"], tabs["N1h"], tabs["N2"]
    C = HY_WIDTH
    W = N2 * C
    tc = 8192
    zf, x0c = _hyena_prep(proj, conv_w)
    a = _dft_stage1(zf.reshape(B, N1h, W), tabs["fwd_half"], tc=tc)
    t = _dft_stage2(a.reshape(B, 2, N1, N2, C), tabs["g"], tabs["gt"], kspec, kb=8)
    bias_t = jnp.tile(bias.astype(F32), N2).reshape(1, W)
    y = _dft_stage3(t.reshape(B, 2, N1, W), tabs["inv_half"], zf.reshape(B, N1h, W),
                    x0c.reshape(B, N1h, W), bias_t, tc=tc)
    return y.reshape(B * L, C)


def _hyena_spectrum(feats, w1, b1, w2, b2, w3, L, tabs):
    N1, N2 = tabs["N1"], tabs["N2"]
    C = HY_WIDTH
    kfull = _hyena_filter(feats, w1, b1, w2, b2, w3, L)
    a = _dft_stage1(kfull.reshape(1, N1, N2 * C), tabs["fwd_full"], tc=8192)
    return _dft_spectrum(a.reshape(1, 2, N1, N2, C), tabs["g"], kb=8)


def _ret_body(lg_ref, q_ref, k_ref, v_ref, g_ref, cos_ref, sin_ref, o_ref, qs_ref, ks_ref, of_ref):
    C = RET_CHUNK
    L = q_ref.shape[1]
    nchunk = L // C
    h = pl.program_id(1)
    lgf = lg_ref[0, h]
    lgb = lg_ref[1, h]
    ii = lax.broadcasted_iota(jnp.int32, (C, C), 0)
    jj = lax.broadcasted_iota(jnp.int32, (C, C), 1)
    diff = (ii - jj).astype(F32)
    dm_f = jnp.where(diff >= 0, jnp.exp(jnp.maximum(diff, 0.0) * lgf), 0.0)
    dm_b = jnp.where(diff < 0, jnp.exp(jnp.maximum(-diff, 0.0) * lgb), 0.0)
    pos = lax.broadcasted_iota(jnp.int32, (C, 1), 0).astype(F32)
    qw_f = jnp.exp((pos + 1.0) * lgf)
    kw_f = jnp.exp((C - 1.0 - pos) * lgf)
    cd_f = jnp.exp(C * lgf)
    qw_b = jnp.exp((C - pos) * lgb)
    kw_b = jnp.exp(pos * lgb)
    cd_b = jnp.exp(C * lgb)
    half = RET_DK // 2

    def rot(x, sl):
        return x * cos_ref[sl, :] + pltpu.roll(x, half, 1) * sin_ref[sl, :]

    def fwd(n, S):
        sl = pl.ds(pl.multiple_of(n * C, C), C)
        q = rot(q_ref[0, sl, :], sl)
        k = rot(k_ref[0, sl, :], sl) * (RET_DK ** -0.5)
        v = v_ref[0, sl, :].astype(BF16)
        qs_ref[sl, :] = q
        ks_ref[sl, :] = k
        scores = _dot_nt(q.astype(BF16), k.astype(BF16)) * dm_f
        o = _dot(scores.astype(BF16), v) + _dot((q * qw_f).astype(BF16), S.astype(BF16))
        of_ref[sl, :] = o
        return S * cd_f + _dot_tn((k * kw_f).astype(BF16), v)

    lax.fori_loop(0, nchunk, fwd, jnp.zeros((RET_DK, RET_DV), F32))

    def bwd(m, S):
        n = nchunk - 1 - m
        sl = pl.ds(pl.multiple_of(n * C, C), C)
        q = qs_ref[sl, :]
        k = ks_ref[sl, :]
        v = v_ref[0, sl, :].astype(BF16)
        scores = _dot_nt(q.astype(BF16), k.astype(BF16)) * dm_b
        o = _dot(scores.astype(BF16), v) + _dot((q * qw_b).astype(BF16), S.astype(BF16))
        o = o + of_ref[sl, :]
        mu = jnp.mean(o, axis=-1, keepdims=True)
        oc = o - mu
        on = oc * lax.rsqrt(jnp.mean(oc * oc, axis=-1, keepdims=True) + EPS)
        g = g_ref[0, sl, :]
        o_ref[0, sl, :] = (g * _sigmoid(g) * on).astype(o_ref.dtype)
        return S * cd_b + _dot_tn((k * kw_b).astype(BF16), v)

    lax.fori_loop(0, nchunk, bwd, jnp.zeros((RET_DK, RET_DV), F32))


def _retention_mixer(proj, log_gamma, cos_t, sin_t):
    B, L, _ = proj.shape
    blk = lambda col: pl.BlockSpec((1, L, LANES), lambda b, h, c=col // LANES: (b, 0, c + h))
    tab = pl.BlockSpec((L, LANES), lambda b, h: (0, 0))
    y = pl.pallas_call(
        _ret_body,
        grid=(B, RET_HEADS),
        in_specs=[pl.BlockSpec(memory_space=pltpu.SMEM),
                  blk(COL_RQ), blk(COL_RK), blk(COL_RV), blk(COL_RG), tab, tab],
        out_specs=pl.BlockSpec((1, L, LANES), lambda b, h: (b, 0, h)),
        out_shape=jax.ShapeDtypeStruct((B, L, RET_WIDTH), BF16),
        scratch_shapes=[pltpu.VMEM((L, LANES), F32), pltpu.VMEM((L, LANES), F32),
                        pltpu.VMEM((L, LANES), F32)],
        compiler_params=_params(("parallel", "parallel")),
        name="retention",
    )(log_gamma, proj, proj, proj, proj, cos_t, sin_t)
    return y.reshape(B * L, RET_WIDTH)


def _attprep_body(q_ref, kv_ref, qn_ref, kn_ref, cos_ref, sin_ref, o_ref):
    cos = cos_ref[...]
    sin = sin_ref[...]
    lane = lax.broadcasted_iota(jnp.int32, cos.shape, 1)
    lower = (lane % (ATT_HD // 2)) < (ATT_HD // 4)

    def rope(x, gain, scale):
        x = _rms(x, gain)
        swapped = jnp.where(lower, pltpu.roll(x, LANES - ATT_HD // 4, 1), pltpu.roll(x, ATT_HD // 4, 1))
        return (x * cos + swapped * sin) * scale

    q = q_ref[0]
    kv = kv_ref[0]
    for hh in range(ATT_HEADS):
        sl = slice(hh * ATT_HD, (hh + 1) * ATT_HD)
        o_ref[0, :, sl] = rope(q[:, sl], qn_ref[...], ATT_HD ** -0.5).astype(o_ref.dtype)
    for hh in range(ATT_KV_HEADS):
        sl = slice(hh * ATT_HD, (hh + 1) * ATT_HD)
        o_ref[0, :, ATT_WIDTH + hh * ATT_HD:ATT_WIDTH + (hh + 1) * ATT_HD] = (
            rope(kv[:, sl], kn_ref[...], 1.0).astype(o_ref.dtype))
    nkv = ATT_KV_HEADS * ATT_HD
    o_ref[0, :, ATT_WIDTH + nkv:] = kv[:, nkv:].astype(o_ref.dtype)


def _attention_prep(proj, qn, kn, cos_t, sin_t, *, tl):
    B, L, _ = proj.shape
    wq = ATT_WIDTH
    wkv = 2 * ATT_KV_HEADS * ATT_HD
    return pl.pallas_call(
        _attprep_body,
        grid=(B, L // tl),
        in_specs=[pl.BlockSpec((1, tl, wq), lambda b, i: (b, i, COL_AQ // wq)),
                  pl.BlockSpec((1, tl, wkv), lambda b, i: (b, i, COL_AK // wkv)),
                  pl.BlockSpec((1, ATT_HD), lambda b, i: (0, 0)),
                  pl.BlockSpec((1, ATT_HD), lambda b, i: (0, 0)),
                  pl.BlockSpec((tl, LANES), lambda b, i: (i, 0)),
                  pl.BlockSpec((tl, LANES), lambda b, i: (i, 0))],
        out_specs=pl.BlockSpec((1, tl, wq + wkv), lambda b, i: (b, i, 0)),
        out_shape=jax.ShapeDtypeStruct((B, L, wq + wkv), BF16),
        compiler_params=_params(("parallel", "parallel")),
        name="attention_prep",
    )(proj, proj, qn.reshape(1, ATT_HD), kn.reshape(1, ATT_HD), cos_t, sin_t)


def _att_body(q_ref, k_ref, v_ref, o_ref):
    G = ATT_HEADS // ATT_KV_HEADS
    tq = q_ref.shape[1]
    q = q_ref[0]
    qs = jnp.concatenate([q[:, g * ATT_HD:(g + 1) * ATT_HD] for g in range(G)], axis=0)
    s = _dot_nt(qs, k_ref[0])
    p = jnp.exp(s - jnp.max(s, axis=-1, keepdims=True))
    l = jnp.sum(p, axis=-1, keepdims=True)
    o = _dot(p.astype(BF16), v_ref[0]) * (1.0 / l)
    for g in range(G):
        o_ref[0, :, g * ATT_HD:(g + 1) * ATT_HD] = o[g * tq:(g + 1) * tq].astype(o_ref.dtype)


def _attention(qkv, *, tq):
    B, L, _ = qkv.shape
    G = ATT_HEADS // ATT_KV_HEADS
    kcol = ATT_WIDTH // ATT_HD
    vcol = kcol + ATT_KV_HEADS
    y = pl.pallas_call(
        _att_body,
        grid=(B, ATT_KV_HEADS, L // tq),
        in_specs=[pl.BlockSpec((1, tq, G * ATT_HD), lambda b, h, i: (b, i, h)),
                  pl.BlockSpec((1, L, ATT_HD), lambda b, h, i: (b, 0, kcol + h)),
                  pl.BlockSpec((1, L, ATT_HD), lambda b, h, i: (b, 0, vcol + h))],
        out_specs=pl.BlockSpec((1, tq, G * ATT_HD), lambda b, h, i: (b, i, h)),
        out_shape=jax.ShapeDtypeStruct((B, L, ATT_WIDTH), BF16),
        compiler_params=_params(("parallel", "parallel", "arbitrary")),
        name="attention",
    )(qkv, qkv, qkv)
    return y.reshape(B * L, ATT_WIDTH)


def _merge_body(yh_ref, yr_ref, ya_ref, ga_ref, gb_ref, wb_ref, wo_ref, g_ref, x_ref, o_ref):
    D = D_MODEL
    ph = _dot(yh_ref[...], wb_ref[0:HY_WIDTH, :])
    pr = _dot(yr_ref[...], wb_ref[HY_WIDTH:HY_WIDTH + RET_WIDTH, :])
    pa = _dot(ya_ref[...], wb_ref[HY_WIDTH + RET_WIDTH:, :])
    ga = ga_ref[...]
    gb = gb_ref[...]
    hd = D // 2
    g0 = _sigmoid(ga[:, :D])
    g1 = _sigmoid(jnp.concatenate([ga[:, D:], gb[:, :hd]], axis=1))
    g2 = _sigmoid(gb[:, hd:])
    merged = g0 * ph + g1 * pr + g2 * pa
    y = _dot(merged.astype(BF16), wo_ref[...])
    o_ref[...] = x_ref[...] + _rms(y, g_ref[...])


def _merge(yh, yr, ya, proj2d, wb, wo, g, x, *, tm):
    T = x.shape[0]
    D = D_MODEL
    gw = N_BRANCH * D // 2
    row = lambda w: pl.BlockSpec((tm, w), lambda i: (i, 0))
    return pl.pallas_call(
        _merge_body,
        grid=(T // tm,),
        in_specs=[row(HY_WIDTH), row(RET_WIDTH), row(ATT_WIDTH),
                  pl.BlockSpec((tm, gw), lambda i: (i, COL_GATE // gw)),
                  pl.BlockSpec((tm, gw), lambda i: (i, COL_GATE // gw + 1)),
                  pl.BlockSpec((MIX_WIDTH, D), lambda i: (0, 0)),
                  pl.BlockSpec((D, D), lambda i: (0, 0)),
                  pl.BlockSpec((1, D), lambda i: (0, 0)),
                  row(D)],
        out_specs=row(D),
        out_shape=jax.ShapeDtypeStruct((T, D), F32),
        compiler_params=_params(("parallel",)),
        name="branch_merge",
    )(yh, yr, ya, proj2d, proj2d, wb, wo, g.reshape(1, D), x)


def _xattn_body(x_ref, kv_ref, gpre_ref, wq_ref, wo_ref, gpost_ref, o_ref):
    x = x_ref[0]
    h = _rms(x, gpre_ref[...]).astype(BF16)
    q = (_dot(h, wq_ref[...]) * (X_HD ** -0.5)).astype(BF16)
    kv = kv_ref[0]
    outs = []
    for hh in range(X_HEADS):
        sl = slice(hh * X_HD, (hh + 1) * X_HD)
        s = _dot_nt(q[:, sl], kv[:, sl])
        p = jnp.exp(s - jnp.max(s, axis=-1, keepdims=True))
        l = jnp.sum(p, axis=-1, keepdims=True)
        vh = kv[:, D_MODEL + hh * X_HD:D_MODEL + (hh + 1) * X_HD]
        outs.append(_dot(p.astype(BF16), vh) * (1.0 / l))
    o = jnp.concatenate(outs, axis=1).astype(BF16)
    y = _dot(o, wo_ref[...])
    o_ref[0] = x + _rms(y, gpost_ref[...])


def _cross_attention(x, kv, gpre, wq, wo, gpost, *, tm):
    B, L, D = x.shape
    M = kv.shape[1]
    vec = pl.BlockSpec((1, D), lambda b, i: (0, 0))
    mat = pl.BlockSpec((D, D), lambda b, i: (0, 0))
    return pl.pallas_call(
        _xattn_body,
        grid=(B, L // tm),
        in_specs=[pl.BlockSpec((1, tm, D), lambda b, i: (b, i, 0)),
                  pl.BlockSpec((1, M, 2 * D), lambda b, i: (b, 0, 0)),
                  vec, mat, mat, vec],
        out_specs=pl.BlockSpec((1, tm, D), lambda b, i: (b, i, 0)),
        out_shape=jax.ShapeDtypeStruct((B, L, D), F32),
        compiler_params=_params(("parallel", "parallel")),
        name="cross_attention",
    )(x, kv, gpre.reshape(1, D), wq, wo, gpost.reshape(1, D))


def _rotary_tables(pos, dim, theta):
    inv = theta ** (-jnp.arange(0, dim, 2, dtype=F32) / dim)
    ang = pos.astype(F32)[:, None] * inv[None, :]
    return jnp.cos(ang), jnp.sin(ang)


def _position_tables(L):
    pos = jnp.arange(L)
    rows = L // GRID_W
    row_ids = jnp.repeat(jnp.arange(rows), GRID_W)
    col_ids = jnp.tile(jnp.arange(GRID_W), rows)
    rcos, rsin = _rotary_tables(pos, RET_DK, RET_THETA)
    rc, rs = _rotary_tables(row_ids, ATT_HD // 2, ROPE_THETA)
    cc, cs = _rotary_tables(col_ids, ATT_HD // 2, ROPE_THETA)
    return dict(
        ret_cos=jnp.concatenate([rcos, rcos], axis=-1),
        ret_sin=jnp.concatenate([-rsin, rsin], axis=-1),
        att_cos=jnp.concatenate([rc, rc, cc, cc], axis=-1),
        att_sin=jnp.concatenate([-rs, rs, -cs, cs], axis=-1),
        feats=_hyena_features(L),
        dft=_dft_tables(L),
    )


def _trunk(x, mem, w, pt):
    B, L, D = x.shape
    M = mem.shape[1]
    T = B * L
    x2 = x.reshape(T, D)
    mem2 = mem.reshape(B * M, D)
    for l in range(DEPTH):
        proj = _norm_matmul(x2, w["g_mix_pre"][l], w["w_in"][l], tm=1024, tn=768)
        proj3 = proj.reshape(B, L, IN_COLS)
        kspec = _hyena_spectrum(pt["feats"], w["hy_fw1"][l], w["hy_fb1"][l], w["hy_fw2"][l],
                                w["hy_fb2"][l], w["hy_fw3"][l], L, pt["dft"])
        y_h = _hyena_mixer(proj3, w["hy_conv"][l], kspec, w["hy_bias"][l], pt["dft"])
        y_r = _retention_mixer(proj3, w["ret_lg"][l], pt["ret_cos"], pt["ret_sin"])
        qkv = _attention_prep(proj3, w["att_qnorm"][l], w["att_knorm"][l],
                              pt["att_cos"], pt["att_sin"], tl=512)
        y_a = _attention(qkv, tq=256)
        x2 = _merge(y_h, y_r, y_a, proj, w["w_branch"][l], w["w_out"][l], w["g_mix_post"][l], x2, tm=512)
        kv = _norm_matmul(mem2, w["g_mem"][l], w["w_xkv"][l], tm=512, tn=1024, out_dtype=BF16)
        x3 = _cross_attention(x2.reshape(B, L, D), kv.reshape(B, M, 2 * D), w["g_x_pre"][l],
                              w["w_xq"][l], w["w_xo"][l], w["g_x_post"][l], tm=512)
        x2 = x3.reshape(T, D)
        u = _norm_matmul(x2, w["g_ff_pre"][l], w["w_ff1"][l], tm=1024, tn=1024, act=True, out_dtype=BF16)
        x2 = _matmul_norm_residual(u, w["w_ff2"][l], w["g_ff_post"][l], x2, tm=512)
    return x2.reshape(B, L, D)


def kernel(x_prompt, x_sample, mem_prompt, mem_sample, g_mix_pre, g_mix_post, w_in, hy_conv, hy_fw1, hy_fb1,
           hy_fw2, hy_fb2, hy_fw3, hy_bias, ret_decay, att_qnorm, att_knorm, w_branch, w_out, g_x_pre, g_x_post,
           g_mem, w_xq, w_xkv, w_xo, g_ff_pre, g_ff_post, w_ff1, w_ff2):
    w = dict(
        g_mix_pre=g_mix_pre, g_mix_post=g_mix_post, w_in=w_in.astype(BF16), hy_conv=hy_conv,
        hy_fw1=hy_fw1, hy_fb1=hy_fb1, hy_fw2=hy_fw2, hy_fb2=hy_fb2, hy_fw3=hy_fw3, hy_bias=hy_bias,
        ret_lg=jax.nn.log_sigmoid(ret_decay.astype(F32)), att_qnorm=att_qnorm, att_knorm=att_knorm,
        w_branch=w_branch.astype(BF16), w_out=w_out.astype(BF16), g_x_pre=g_x_pre, g_x_post=g_x_post,
        g_mem=g_mem, w_xq=w_xq.astype(BF16), w_xkv=w_xkv.astype(BF16), w_xo=w_xo.astype(BF16),
        g_ff_pre=g_ff_pre, g_ff_post=g_ff_post, w_ff1=w_ff1.astype(BF16), w_ff2=w_ff2.astype(BF16),
    )
    y_prompt = _trunk(x_prompt, mem_prompt, w, _position_tables(x_prompt.shape[1]))
    y_sample = _trunk(x_sample, mem_sample, w, _position_tables(x_sample.shape[1]))
    return (y_prompt, y_sample)
```

```python
import functools
import math

import jax
import jax.numpy as jnp
from jax import lax
from jax.experimental import pallas as pl
from jax.experimental.pallas import tpu as pltpu

F32 = jnp.float32
BF16 = jnp.bfloat16

D_MODEL = 1024
DEPTH = 4
GRID_W = 64
EPS = 1e-6
HY_WIDTH = 512
HY_SHORT = 3
HY_EMB = 33
HY_FH = 64
HY_SIN_FREQ = 1.0
HY_TARGET = 1e-2
HY_FAST = 0.3
HY_SLOW = 1.5
HY_SHIFT = 0.0
RET_HEADS = 4
RET_DK = 128
RET_DV = 128
RET_WIDTH = RET_HEADS * RET_DV
RET_THETA = 10000.0
ATT_HEADS = 4
ATT_KV_HEADS = 2
ATT_HD = 128
ATT_WIDTH = ATT_HEADS * ATT_HD
ROPE_THETA = 10000.0
X_HEADS = 4
X_HD = D_MODEL // X_HEADS
D_FF = 4 * D_MODEL
N_BRANCH = 3
MIX_WIDTH = HY_WIDTH + RET_WIDTH + ATT_WIDTH

COL_HY = 0
COL_RQ = 3 * HY_WIDTH
COL_RK = COL_RQ + RET_HEADS * RET_DK
COL_RV = COL_RK + RET_HEADS * RET_DK
COL_RG = COL_RV + RET_WIDTH
COL_AQ = COL_RG + RET_WIDTH
COL_AK = COL_AQ + ATT_WIDTH
COL_AV = COL_AK + ATT_KV_HEADS * ATT_HD
COL_GATE = COL_AV + ATT_KV_HEADS * ATT_HD
IN_COLS = COL_GATE + N_BRANCH * D_MODEL

LANES = 128
VMEM_LIMIT = 56 * 1024 * 1024
DFT_N2 = 128
DFT_TC = 8192
DFT_KB = 8
DFT_PASSES = dict(s1=1, s2f=1, s2i=1, s3=1, k1=3, k2=3)
RET_BLOCK = 256


def _params(sem):
    return pltpu.CompilerParams(dimension_semantics=sem, vmem_limit_bytes=VMEM_LIMIT)


def _dot(a, b):
    return jnp.dot(a, b, preferred_element_type=F32)


def _dot_nt(a, b):
    return lax.dot_general(a, b, (((1,), (1,)), ((), ())), preferred_element_type=F32)


def _dot_tn(a, b):
    return lax.dot_general(a, b, (((0,), (0,)), ((), ())), preferred_element_type=F32)


def _rms(x, g):
    ms = jnp.mean(x * x, axis=-1, keepdims=True)
    return x * lax.rsqrt(ms + EPS) * g


def _sigmoid(x):
    return 1.0 / (1.0 + jnp.exp(-x))


def _split(x):
    hi = x.astype(BF16)
    lo = (x - hi.astype(F32)).astype(BF16)
    return hi, lo


def _dot3(a, b):
    return _dot(a[0], b[0]) + (_dot(a[0], b[1]) + _dot(a[1], b[0]))


def _mm(a, b, passes):
    if passes == 1:
        return _dot(a.astype(BF16), b.astype(BF16))
    return _dot3(_split(a.astype(F32)), _split(b.astype(F32)))


def _nm_body(x_ref, g_ref, w_ref, o_ref, h_ref, *, act):
    @pl.when(pl.program_id(1) == 0)
    def _():
        h_ref[...] = _rms(x_ref[...], g_ref[...]).astype(BF16)

    y = _dot(h_ref[...], w_ref[...])
    if act:
        y = jnp.square(jnp.maximum(y, 0.0))
    o_ref[...] = y.astype(o_ref.dtype)


def _norm_matmul(x, g, w, *, tm, tn, act=False):
    T, D = x.shape
    N = w.shape[1]
    return pl.pallas_call(
        functools.partial(_nm_body, act=act),
        grid=(T // tm, N // tn),
        in_specs=[pl.BlockSpec((tm, D), lambda i, j: (i, 0)),
                  pl.BlockSpec((1, D), lambda i, j: (0, 0)),
                  pl.BlockSpec((D, tn), lambda i, j: (0, j))],
        out_specs=pl.BlockSpec((tm, tn), lambda i, j: (i, j)),
        out_shape=jax.ShapeDtypeStruct((T, N), BF16),
        scratch_shapes=[pltpu.VMEM((tm, D), BF16)],
        compiler_params=_params(("parallel", "arbitrary")),
        name="norm_matmul",
    )(x, g.reshape(1, D), w)


def _mnr_body(a_ref, w_ref, g_ref, x_ref, o_ref):
    y = _dot(a_ref[...], w_ref[...])
    o_ref[...] = x_ref[...] + _rms(y, g_ref[...])


def _matmul_norm_residual(a, w, g, x, *, tm):
    T, K = a.shape
    D = w.shape[1]
    return pl.pallas_call(
        _mnr_body,
        grid=(T // tm,),
        in_specs=[pl.BlockSpec((tm, K), lambda i: (i, 0)),
                  pl.BlockSpec((K, D), lambda i: (0, 0)),
                  pl.BlockSpec((1, D), lambda i: (0, 0)),
                  pl.BlockSpec((tm, D), lambda i: (i, 0))],
        out_specs=pl.BlockSpec((tm, D), lambda i: (i, 0)),
        out_shape=jax.ShapeDtypeStruct((T, D), F32),
        compiler_params=_params(("parallel",)),
        name="matmul_norm_residual",
    )(a, w, g.reshape(1, D), x)


def _hyprep_body(x0_ref, x1_ref, v_ref, w_ref, zf_ref, x0c_ref):
    L = x0_ref.shape[1]
    row = lax.broadcasted_iota(jnp.int32, (L, LANES), 0)

    def conv(p_ref, grp):
        p = p_ref[0].astype(F32)
        w = w_ref[:, grp, :]
        prev = jnp.where(row == 0, 0.0, pltpu.roll(p, 1, 0))
        nxt = jnp.where(row == L - 1, 0.0, pltpu.roll(p, L - 1, 0))
        return prev * w[0:1] + p * w[1:2] + nxt * w[2:3]

    x0c_ref[0] = conv(x0_ref, 0).astype(x0c_ref.dtype)
    zf_ref[0] = conv(v_ref, 2) * conv(x1_ref, 1)


def _hyena_prep(proj, conv_w):
    B, L, _ = proj.shape
    nb = HY_WIDTH // LANES
    w = conv_w.reshape(HY_SHORT, 3, HY_WIDTH)
    blk = lambda off: pl.BlockSpec((1, L, LANES), lambda b, c, off=off: (b, 0, off + c))
    out = pl.BlockSpec((1, L, LANES), lambda b, c: (b, 0, c))
    return pl.pallas_call(
        _hyprep_body,
        grid=(B, nb),
        in_specs=[blk(0), blk(nb), blk(2 * nb),
                  pl.BlockSpec((HY_SHORT, 3, LANES), lambda b, c: (0, 0, c))],
        out_specs=[out, out],
        out_shape=[jax.ShapeDtypeStruct((B, L, HY_WIDTH), F32),
                   jax.ShapeDtypeStruct((B, L, HY_WIDTH), BF16)],
        compiler_params=_params(("parallel", "parallel")),
        name="hyena_prep",
    )(proj, proj, proj, w)


def _filt_body(f_ref, w1_ref, b1_ref, w2_ref, b2_ref, w3_ref, dl_ref, o_ref, *, L):
    tl = f_ref.shape[0]
    hp = functools.partial(jnp.dot, preferred_element_type=F32, precision=lax.Precision.HIGHEST)
    f = f_ref[...]
    h = jnp.sin(HY_SIN_FREQ * (hp(f, w1_ref[...]) + b1_ref[...]))
    h = jnp.sin(HY_SIN_FREQ * (hp(h, w2_ref[...]) + b2_ref[...]))
    filt = hp(h, w3_ref[...])
    mod = jnp.exp(-f[:, 0:1] * dl_ref[...]) + HY_SHIFT
    row = pl.program_id(0) * tl + lax.broadcasted_iota(jnp.int32, (tl, HY_WIDTH), 0)
    sel = jnp.where(row < L, filt[:, :HY_WIDTH], filt[:, HY_WIDTH:])
    o_ref[...] = jnp.where(row == L, 0.0, sel * mod)


def _hyena_features(L):
    t = jnp.linspace(0.0, 1.0, L, dtype=F32)[:, None]
    bands = (HY_EMB - 1) // 2
    w = 2.0 * math.pi * jnp.arange(L, dtype=F32)[:, None] / L
    f = jnp.linspace(1e-4, bands - 1, bands, dtype=F32)[None, :]
    z = jnp.concatenate([t, jnp.cos(f * w), -jnp.sin(f * w)], axis=-1)
    zc = jnp.concatenate([z, z[:1], jnp.flip(z[1:], axis=0)], axis=0)
    return jnp.pad(zc, ((0, 0), (0, LANES - HY_EMB)))


def _hyena_filter(feats, w1, b1, w2, b2, w3, L):
    tl = 512
    pad = LANES - HY_FH
    w1p = jnp.pad(w1, ((0, LANES - HY_EMB), (0, pad)))
    b1p = jnp.pad(b1, (0, pad)).reshape(1, LANES)
    w2p = jnp.pad(w2, ((0, pad), (0, pad)))
    b2p = jnp.pad(b2, (0, pad)).reshape(1, LANES)
    w3p = jnp.pad(w3, ((0, pad), (0, 0)))
    min_decay = math.log(HY_TARGET) / HY_SLOW
    max_decay = math.log(HY_TARGET) / HY_FAST
    deltas = jnp.abs(jnp.linspace(min_decay, max_decay, HY_WIDTH, dtype=F32)).reshape(1, HY_WIDTH)
    full = lambda shape: pl.BlockSpec(shape, lambda i: (0,) * len(shape))
    return pl.pallas_call(
        functools.partial(_filt_body, L=L),
        grid=(2 * L // tl,),
        in_specs=[pl.BlockSpec((tl, LANES), lambda i: (i, 0)),
                  full((LANES, LANES)), full((1, LANES)), full((LANES, LANES)), full((1, LANES)),
                  full((LANES, 2 * HY_WIDTH)), full((1, HY_WIDTH))],
        out_specs=pl.BlockSpec((tl, HY_WIDTH), lambda i: (i, 0)),
        out_shape=jax.ShapeDtypeStruct((2 * L, HY_WIDTH), F32),
        compiler_params=_params(("parallel",)),
        name="hyena_filter",
    )(feats, w1p, b1p, w2p, b2p, w3p, deltas)


def _dft_tables(L):
    N = 2 * L
    N2 = DFT_N2
    N1 = N // N2
    N1h = N1 // 2
    k1 = jnp.arange(N1, dtype=jnp.int32)[:, None]
    n1 = jnp.arange(N1, dtype=jnp.int32)[None, :]
    a1 = ((k1 * n1) % N1).astype(F32) * (2.0 * math.pi / N1)
    c1, s1 = jnp.cos(a1), jnp.sin(a1)
    ch, sh = c1[:, :N1h], s1[:, :N1h]
    f1_pair = jnp.stack([jnp.concatenate([ch, sh], axis=1),
                         jnp.concatenate([-sh, ch], axis=1)], axis=1).reshape(2 * N1, 2 * N1h)
    f1_real = jnp.stack([c1, -s1], axis=1).reshape(2 * N1, N1)
    ci, si = c1[:N1h] * (1.0 / N), s1[:N1h] * (1.0 / N)
    f3_pair = jnp.concatenate([jnp.stack([ci, -si], axis=2).reshape(N1h, 2 * N1),
                               jnp.stack([si, ci], axis=2).reshape(N1h, 2 * N1)], axis=0)
    kk = (jnp.arange(N1, dtype=jnp.int32)[:, None, None]
          + N1 * jnp.arange(N2, dtype=jnp.int32)[None, :, None])
    n2 = jnp.arange(N2, dtype=jnp.int32)[None, None, :]
    a2 = ((kk * n2) % N).astype(F32) * (2.0 * math.pi / N)
    gr, gi = jnp.cos(a2), -jnp.sin(a2)
    gtr, gti = jnp.swapaxes(gr, 1, 2), jnp.swapaxes(gi, 1, 2)
    g_fwd = jnp.concatenate([jnp.concatenate([gr, -gi], axis=2),
                             jnp.concatenate([gi, gr], axis=2)], axis=1)
    g_inv = jnp.concatenate([jnp.concatenate([gtr, gti], axis=2),
                             jnp.concatenate([-gti, gtr], axis=2)], axis=1)
    return dict(N1=N1, N1h=N1h, N2=N2, f1_pair=f1_pair, f1_real=f1_real, f3_pair=f3_pair,
                g_fwd=g_fwd, g_inv=g_inv)


def _stage_dtype(passes):
    return BF16 if passes == 1 else F32


def _dft1_body(x_ref, f_ref, o_ref, *, passes):
    P, K, tc = x_ref.shape
    o_ref[0] = _mm(f_ref[...], x_ref[...].reshape(P * K, tc), passes).astype(o_ref.dtype)


def _dft_stage1(x, f, *, P, tc, passes, out_dtype):
    Bx, K, W = x.shape
    R = f.shape[0]
    return pl.pallas_call(
        functools.partial(_dft1_body, passes=passes),
        grid=(Bx // P, W // tc),
        in_specs=[pl.BlockSpec((P, K, tc), lambda b, j: (b, 0, j)),
                  pl.BlockSpec((R, P * K), lambda b, j: (0, 0))],
        out_specs=pl.BlockSpec((1, R, tc), lambda b, j: (b, 0, j)),
        out_shape=jax.ShapeDtypeStruct((Bx // P, R, W), out_dtype),
        compiler_params=_params(("parallel", "parallel")),
        name="dft_stage1",
    )(x, f)


def _spec_body(a_ref, g_ref, o_ref, *, passes):
    def step(k, carry):
        o_ref[k] = _mm(g_ref[k], a_ref[0, k], passes)
        return carry

    lax.fori_loop(0, a_ref.shape[1], step, 0)


def _dft_spectrum(a, g, *, kb, passes):
    _, N1, R, C = a.shape
    return pl.pallas_call(
        functools.partial(_spec_body, passes=passes),
        grid=(N1 // kb,),
        in_specs=[pl.BlockSpec((1, kb, R, C), lambda i: (0, i, 0, 0)),
                  pl.BlockSpec((kb, R, R), lambda i: (i, 0, 0))],
        out_specs=pl.BlockSpec((kb, R, C), lambda i: (i, 0, 0)),
        out_shape=jax.ShapeDtypeStruct((N1, R, C), F32),
        compiler_params=_params(("parallel",)),
        name="dft_spectrum",
    )(a, g)


def _dft2_body(a_ref, gf_ref, gi_ref, ks_ref, o_ref, *, passes_fwd, passes_inv):
    H = a_ref.shape[2] // 2

    def step(k, carry):
        y = _mm(gf_ref[k], a_ref[0, k], passes_fwd)
        yr, yi = y[:H], y[H:]
        kr, ki = ks_ref[k, :H, :], ks_ref[k, H:, :]
        z = jnp.concatenate([yr * kr - yi * ki, yr * ki + yi * kr], axis=0)
        o_ref[0, k] = _mm(gi_ref[k], z, passes_inv).astype(o_ref.dtype)
        return carry

    lax.fori_loop(0, a_ref.shape[1], step, 0)


def _dft_stage2(a, gf, gi, ks, *, kb, passes_fwd, passes_inv, out_dtype):
    Bp, N1, R, C = a.shape
    dat = pl.BlockSpec((1, kb, R, C), lambda i, b: (b, i, 0, 0))
    mat = pl.BlockSpec((kb, R, R), lambda i, b: (i, 0, 0))
    return pl.pallas_call(
        functools.partial(_dft2_body, passes_fwd=passes_fwd, passes_inv=passes_inv),
        grid=(N1 // kb, Bp),
        in_specs=[dat, mat, mat, pl.BlockSpec((kb, R, C), lambda i, b: (i, 0, 0))],
        out_specs=dat,
        out_shape=jax.ShapeDtypeStruct((Bp, N1, R, C), out_dtype),
        compiler_params=_params(("parallel", "parallel")),
        name="dft_stage2",
    )(a, gf, gi, ks)


def _dft3_body(t_ref, f_ref, zf_ref, x0_ref, bias_ref, o_ref, *, passes):
    P, K, tc = zf_ref.shape
    y = _mm(f_ref[...], t_ref[0], passes).reshape(P, K, tc)
    o_ref[...] = ((y + zf_ref[...] * bias_ref[...]) * x0_ref[...].astype(F32)).astype(o_ref.dtype)


def _dft_stage3(t, f, zf, x0, bias_t, *, tc, passes):
    Bp, R, W = t.shape
    B, N1h, _ = zf.shape
    pair = pl.BlockSpec((2, N1h, tc), lambda b, j: (b, 0, j))
    return pl.pallas_call(
        functools.partial(_dft3_body, passes=passes),
        grid=(Bp, W // tc),
        in_specs=[pl.BlockSpec((1, R, tc), lambda b, j: (b, 0, j)),
                  pl.BlockSpec((2 * N1h, R), lambda b, j: (0, 0)),
                  pair, pair,
                  pl.BlockSpec((1, tc), lambda b, j: (0, j))],
        out_specs=pair,
        out_shape=jax.ShapeDtypeStruct((B, N1h, W), BF16),
        compiler_params=_params(("parallel", "parallel")),
        name="dft_stage3",
    )(t, f, zf, x0, bias_t)


def _hyena_mixer(proj, conv_w, kspec, bias, tabs):
    B, L, _ = proj.shape
    N1, N1h, N2 = tabs["N1"], tabs["N1h"], tabs["N2"]
    C = HY_WIDTH
    W = N2 * C
    ps = DFT_PASSES
    zf, x0c = _hyena_prep(proj, conv_w)
    zw = zf.reshape(B, N1h, W)
    a = _dft_stage1(zw, tabs["f1_pair"], P=2, tc=DFT_TC, passes=ps["s1"], out_dtype=_stage_dtype(ps["s2f"]))
    gdt = _stage_dtype(min(ps["s2f"], ps["s2i"]))
    t = _dft_stage2(a.reshape(B // 2, N1, 2 * N2, C), tabs["g_fwd"].astype(gdt), tabs["g_inv"].astype(gdt),
                    kspec, kb=DFT_KB, passes_fwd=ps["s2f"], passes_inv=ps["s2i"],
                    out_dtype=_stage_dtype(ps["s3"]))
    bias_t = jnp.tile(bias.astype(F32), N2).reshape(1, W)
    y = _dft_stage3(t.reshape(B // 2, 2 * N1, W), tabs["f3_pair"], zw, x0c.reshape(B, N1h, W),
                    bias_t, tc=DFT_TC, passes=ps["s3"])
    return y.reshape(B * L, C)


def _hyena_spectrum(feats, w1, b1, w2, b2, w3, L, tabs):
    N1, N2 = tabs["N1"], tabs["N2"]
    C = HY_WIDTH
    ps = DFT_PASSES
    kfull = _hyena_filter(feats, w1, b1, w2, b2, w3, L)
    a = _dft_stage1(kfull.reshape(1, N1, N2 * C), tabs["f1_real"], P=1, tc=DFT_TC, passes=ps["k1"],
                    out_dtype=_stage_dtype(ps["k2"]))
    return _dft_spectrum(a.reshape(1, N1, 2 * N2, C), tabs["g_fwd"], kb=DFT_KB, passes=ps["k2"])


def _ret_body(lg_ref, q_ref, k_ref, v_ref, g_ref, cos_ref, sin_ref, o_ref, qs_ref, ks_ref, st_ref):
    C = RET_BLOCK
    L = q_ref.shape[1]
    nblk = L // C
    dk = RET_DK
    h = pl.program_id(1)
    lgf = lg_ref[0, h]
    lgb = lg_ref[1, h]
    ii = lax.broadcasted_iota(jnp.int32, (C, C), 0)
    jj = lax.broadcasted_iota(jnp.int32, (C, C), 1)
    diff = (ii - jj).astype(F32)
    dm = jnp.where(diff >= 0, jnp.exp(jnp.maximum(diff, 0.0) * lgf), jnp.exp(jnp.maximum(-diff, 0.0) * lgb))
    pos = lax.broadcasted_iota(jnp.int32, (C, dk), 0).astype(F32)
    qw_f = jnp.exp((pos + 1.0) * lgf)
    kw_f = jnp.exp((C - 1.0 - pos) * lgf)
    qw_b = jnp.exp((C - pos) * lgb)
    kw_b = jnp.exp(pos * lgb)
    cd_f = jnp.exp(C * lgf)
    cd_b = jnp.exp(C * lgb)
    half = dk // 2

    def rot(x, sl):
        return x * cos_ref[sl, :] + pltpu.roll(x, half, 1) * sin_ref[sl, :]

    def block_kv(n, carry):
        sl = pl.ds(pl.multiple_of(n * C, C), C)
        q = rot(q_ref[0, sl, :].astype(F32), sl)
        k = rot(k_ref[0, sl, :].astype(F32), sl) * (dk ** -0.5)
        qs_ref[sl, :] = q
        ks_ref[sl, :] = k
        kk = jnp.concatenate([k * kw_f, k * kw_b], axis=1).astype(BF16)
        st_ref[n] = _dot_tn(kk, v_ref[0, sl, :].astype(BF16))
        return carry

    lax.fori_loop(0, nblk, block_kv, 0, unroll=2)

    def scan_f(n, S):
        kv = st_ref[n, :dk, :]
        st_ref[n, :dk, :] = S
        return S * cd_f + kv

    lax.fori_loop(0, nblk, scan_f, jnp.zeros((dk, RET_DV), F32))

    def scan_b(m, S):
        n = nblk - 1 - m
        kv = st_ref[n, dk:, :]
        st_ref[n, dk:, :] = S
        return S * cd_b + kv

    lax.fori_loop(0, nblk, scan_b, jnp.zeros((dk, RET_DV), F32))

    def block_out(n, carry):
        sl = pl.ds(pl.multiple_of(n * C, C), C)
        q = qs_ref[sl, :]
        k = ks_ref[sl, :]
        scores = _dot_nt(q.astype(BF16), k.astype(BF16)) * dm
        qq = jnp.concatenate([q * qw_f, q * qw_b], axis=1).astype(BF16)
        o = _dot(scores.astype(BF16), v_ref[0, sl, :].astype(BF16)) + _dot(qq, st_ref[n].astype(BF16))
        mu = jnp.mean(o, axis=-1, keepdims=True)
        oc = o - mu
        on = oc * lax.rsqrt(jnp.mean(oc * oc, axis=-1, keepdims=True) + EPS)
        g = g_ref[0, sl, :].astype(F32)
        o_ref[0, sl, :] = (g * _sigmoid(g) * on).astype(o_ref.dtype)
        return carry

    lax.fori_loop(0, nblk, block_out, 0, unroll=2)


def _retention_mixer(proj, log_gamma, cos_t, sin_t):
    B, L, _ = proj.shape
    blk = lambda col: pl.BlockSpec((1, L, LANES), lambda b, h, c=col // LANES: (b, 0, c + h))
    tab = pl.BlockSpec((L, LANES), lambda b, h: (0, 0))
    y = pl.pallas_call(
        _ret_body,
        grid=(B, RET_HEADS),
        in_specs=[pl.BlockSpec(memory_space=pltpu.SMEM),
                  blk(COL_RQ), blk(COL_RK), blk(COL_RV), blk(COL_RG), tab, tab],
        out_specs=pl.BlockSpec((1, L, LANES), lambda b, h: (b, 0, h)),
        out_shape=jax.ShapeDtypeStruct((B, L, RET_WIDTH), BF16),
        scratch_shapes=[pltpu.VMEM((L, LANES), F32), pltpu.VMEM((L, LANES), F32),
                        pltpu.VMEM((L // RET_BLOCK, 2 * RET_DK, RET_DV), F32)],
        compiler_params=_params(("parallel", "parallel")),
        name="retention",
    )(log_gamma, proj, proj, proj, proj, cos_t, sin_t)
    return y.reshape(B * L, RET_WIDTH)


def _attprep_body(q_ref, kv_ref, qn_ref, kn_ref, cos_ref, sin_ref, o_ref):
    cos = cos_ref[...]
    sin = sin_ref[...]
    lane = lax.broadcasted_iota(jnp.int32, cos.shape, 1)
    lower = (lane % (ATT_HD // 2)) < (ATT_HD // 4)

    def rope(x, gain, scale):
        x = _rms(x.astype(F32), gain)
        swapped = jnp.where(lower, pltpu.roll(x, LANES - ATT_HD // 4, 1), pltpu.roll(x, ATT_HD // 4, 1))
        return (x * cos + swapped * sin) * scale

    q = q_ref[0]
    kv = kv_ref[0]
    for hh in range(ATT_HEADS):
        sl = slice(hh * ATT_HD, (hh + 1) * ATT_HD)
        o_ref[0, :, sl] = rope(q[:, sl], qn_ref[...], ATT_HD ** -0.5).astype(o_ref.dtype)
    for hh in range(ATT_KV_HEADS):
        sl = slice(hh * ATT_HD, (hh + 1) * ATT_HD)
        o_ref[0, :, ATT_WIDTH + hh * ATT_HD:ATT_WIDTH + (hh + 1) * ATT_HD] = (
            rope(kv[:, sl], kn_ref[...], 1.0).astype(o_ref.dtype))
    nkv = ATT_KV_HEADS * ATT_HD
    o_ref[0, :, ATT_WIDTH + nkv:] = kv[:, nkv:].astype(o_ref.dtype)


def _attention_prep(proj, qn, kn, cos_t, sin_t, *, tl):
    B, L, _ = proj.shape
    wq = ATT_WIDTH
    wkv = 2 * ATT_KV_HEADS * ATT_HD
    return pl.pallas_call(
        _attprep_body,
        grid=(B, L // tl),
        in_specs=[pl.BlockSpec((1, tl, wq), lambda b, i: (b, i, COL_AQ // wq)),
                  pl.BlockSpec((1, tl, wkv), lambda b, i: (b, i, COL_AK // wkv)),
                  pl.BlockSpec((1, ATT_HD), lambda b, i: (0, 0)),
                  pl.BlockSpec((1, ATT_HD), lambda b, i: (0, 0)),
                  pl.BlockSpec((tl, LANES), lambda b, i: (i, 0)),
                  pl.BlockSpec((tl, LANES), lambda b, i: (i, 0))],
        out_specs=pl.BlockSpec((1, tl, wq + wkv), lambda b, i: (b, i, 0)),
        out_shape=jax.ShapeDtypeStruct((B, L, wq + wkv), BF16),
        compiler_params=_params(("parallel", "parallel")),
        name="attention_prep",
    )(proj, proj, qn.reshape(1, ATT_HD), kn.reshape(1, ATT_HD), cos_t, sin_t)


def _att_body(q_ref, k_ref, v_ref, o_ref):
    G = ATT_HEADS // ATT_KV_HEADS
    tq = q_ref.shape[1]
    q = q_ref[0]
    qs = jnp.concatenate([q[:, g * ATT_HD:(g + 1) * ATT_HD] for g in range(G)], axis=0)
    s = _dot_nt(qs, k_ref[0])
    p = jnp.exp(s - jnp.max(s, axis=-1, keepdims=True))
    l = jnp.sum(p, axis=-1, keepdims=True)
    o = _dot(p.astype(BF16), v_ref[0]) * (1.0 / l)
    for g in range(G):
        o_ref[0, :, g * ATT_HD:(g + 1) * ATT_HD] = o[g * tq:(g + 1) * tq].astype(o_ref.dtype)


def _attention(qkv, *, tq):
    B, L, _ = qkv.shape
    G = ATT_HEADS // ATT_KV_HEADS
    kcol = ATT_WIDTH // ATT_HD
    vcol = kcol + ATT_KV_HEADS
    y = pl.pallas_call(
        _att_body,
        grid=(B, ATT_KV_HEADS, L // tq),
        in_specs=[pl.BlockSpec((1, tq, G * ATT_HD), lambda b, h, i: (b, i, h)),
                  pl.BlockSpec((1, L, ATT_HD), lambda b, h, i: (b, 0, kcol + h)),
                  pl.BlockSpec((1, L, ATT_HD), lambda b, h, i: (b, 0, vcol + h))],
        out_specs=pl.BlockSpec((1, tq, G * ATT_HD), lambda b, h, i: (b, i, h)),
        out_shape=jax.ShapeDtypeStruct((B, L, ATT_WIDTH), BF16),
        compiler_params=_params(("parallel", "parallel", "arbitrary")),
        name="attention",
    )(qkv, qkv, qkv)
    return y.reshape(B * L, ATT_WIDTH)


def _merge_body(yh_ref, yr_ref, ya_ref, ga_ref, gb_ref, wb_ref, wo_ref, g_ref, x_ref, o_ref):
    D = D_MODEL
    ph = _dot(yh_ref[...], wb_ref[0:HY_WIDTH, :])
    pr = _dot(yr_ref[...], wb_ref[HY_WIDTH:HY_WIDTH + RET_WIDTH, :])
    pa = _dot(ya_ref[...], wb_ref[HY_WIDTH + RET_WIDTH:, :])
    ga = ga_ref[...].astype(F32)
    gb = gb_ref[...].astype(F32)
    hd = D // 2
    g0 = _sigmoid(ga[:, :D])
    g1 = _sigmoid(jnp.concatenate([ga[:, D:], gb[:, :hd]], axis=1))
    g2 = _sigmoid(gb[:, hd:])
    merged = g0 * ph + g1 * pr + g2 * pa
    y = _dot(merged.astype(BF16), wo_ref[...])
    o_ref[...] = x_ref[...] + _rms(y, g_ref[...])


def _merge(yh, yr, ya, proj2d, wb, wo, g, x, *, tm):
    T = x.shape[0]
    D = D_MODEL
    gw = N_BRANCH * D // 2
    row = lambda w: pl.BlockSpec((tm, w), lambda i: (i, 0))
    return pl.pallas_call(
        _merge_body,
        grid=(T // tm,),
        in_specs=[row(HY_WIDTH), row(RET_WIDTH), row(ATT_WIDTH),
                  pl.BlockSpec((tm, gw), lambda i: (i, COL_GATE // gw)),
                  pl.BlockSpec((tm, gw), lambda i: (i, COL_GATE // gw + 1)),
                  pl.BlockSpec((MIX_WIDTH, D), lambda i: (0, 0)),
                  pl.BlockSpec((D, D), lambda i: (0, 0)),
                  pl.BlockSpec((1, D), lambda i: (0, 0)),
                  row(D)],
        out_specs=row(D),
        out_shape=jax.ShapeDtypeStruct((T, D), F32),
        compiler_params=_params(("parallel",)),
        name="branch_merge",
    )(yh, yr, ya, proj2d, proj2d, wb, wo, g.reshape(1, D), x)


def _xattn_body(x_ref, kv_ref, gpre_ref, wq_ref, wo_ref, gpost_ref, o_ref):
    x = x_ref[0]
    h = _rms(x, gpre_ref[...]).astype(BF16)
    q = (_dot(h, wq_ref[...]) * (X_HD ** -0.5)).astype(BF16)
    kv = kv_ref[0]
    outs = []
    for hh in range(X_HEADS):
        sl = slice(hh * X_HD, (hh + 1) * X_HD)
        s = _dot_nt(q[:, sl], kv[:, sl])
        p = jnp.exp(s - jnp.max(s, axis=-1, keepdims=True))
        l = jnp.sum(p, axis=-1, keepdims=True)
        vh = kv[:, D_MODEL + hh * X_HD:D_MODEL + (hh + 1) * X_HD]
        outs.append(_dot(p.astype(BF16), vh) * (1.0 / l))
    o = jnp.concatenate(outs, axis=1).astype(BF16)
    y = _dot(o, wo_ref[...])
    o_ref[0] = x + _rms(y, gpost_ref[...])


def _cross_attention(x, kv, gpre, wq, wo, gpost, *, tm):
    B, L, D = x.shape
    M = kv.shape[1]
    vec = pl.BlockSpec((1, D), lambda b, i: (0, 0))
    mat = pl.BlockSpec((D, D), lambda b, i: (0, 0))
    return pl.pallas_call(
        _xattn_body,
        grid=(B, L // tm),
        in_specs=[pl.BlockSpec((1, tm, D), lambda b, i: (b, i, 0)),
                  pl.BlockSpec((1, M, 2 * D), lambda b, i: (b, 0, 0)),
                  vec, mat, mat, vec],
        out_specs=pl.BlockSpec((1, tm, D), lambda b, i: (b, i, 0)),
        out_shape=jax.ShapeDtypeStruct((B, L, D), F32),
        compiler_params=_params(("parallel", "parallel")),
        name="cross_attention",
    )(x, kv, gpre.reshape(1, D), wq, wo, gpost.reshape(1, D))


def _rotary_tables(pos, dim, theta):
    inv = theta ** (-jnp.arange(0, dim, 2, dtype=F32) / dim)
    ang = pos.astype(F32)[:, None] * inv[None, :]
    return jnp.cos(ang), jnp.sin(ang)


def _position_tables(L):
    pos = jnp.arange(L)
    rows = L // GRID_W
    row_ids = jnp.repeat(jnp.arange(rows), GRID_W)
    col_ids = jnp.tile(jnp.arange(GRID_W), rows)
    rcos, rsin = _rotary_tables(pos, RET_DK, RET_THETA)
    rc, rs = _rotary_tables(row_ids, ATT_HD // 2, ROPE_THETA)
    cc, cs = _rotary_tables(col_ids, ATT_HD // 2, ROPE_THETA)
    return dict(
        ret_cos=jnp.concatenate([rcos, rcos], axis=-1),
        ret_sin=jnp.concatenate([-rsin, rsin], axis=-1),
        att_cos=jnp.concatenate([rc, rc, cc, cc], axis=-1),
        att_sin=jnp.concatenate([-rs, rs, -cs, cs], axis=-1),
        feats=_hyena_features(L),
        dft=_dft_tables(L),
    )


def _trunk(x, mem, w, pt):
    B, L, D = x.shape
    M = mem.shape[1]
    T = B * L
    x2 = x.reshape(T, D)
    mem2 = mem.reshape(B * M, D)
    for l in range(DEPTH):
        proj = _norm_matmul(x2, w["g_mix_pre"][l], w["w_in"][l], tm=1024, tn=768)
        proj3 = proj.reshape(B, L, IN_COLS)
        kspec = _hyena_spectrum(pt["feats"], w["hy_fw1"][l], w["hy_fb1"][l], w["hy_fw2"][l],
                                w["hy_fb2"][l], w["hy_fw3"][l], L, pt["dft"])
        y_h = _hyena_mixer(proj3, w["hy_conv"][l], kspec, w["hy_bias"][l], pt["dft"])
        y_r = _retention_mixer(proj3, w["ret_lg"][l], pt["ret_cos"], pt["ret_sin"])
        qkv = _attention_prep(proj3, w["att_qnorm"][l], w["att_knorm"][l],
                              pt["att_cos"], pt["att_sin"], tl=512)
        y_a = _attention(qkv, tq=256)
        x2 = _merge(y_h, y_r, y_a, proj, w["w_branch"][l], w["w_out"][l], w["g_mix_post"][l], x2, tm=512)
        kv = _norm_matmul(mem2, w["g_mem"][l], w["w_xkv"][l], tm=512, tn=1024)
        x3 = _cross_attention(x2.reshape(B, L, D), kv.reshape(B, M, 2 * D), w["g_x_pre"][l],
                              w["w_xq"][l], w["w_xo"][l], w["g_x_post"][l], tm=512)
        x2 = x3.reshape(T, D)
        u = _norm_matmul(x2, w["g_ff_pre"][l], w["w_ff1"][l], tm=1024, tn=1024, act=True)
        x2 = _matmul_norm_residual(u, w["w_ff2"][l], w["g_ff_post"][l], x2, tm=512)
    return x2.reshape(B, L, D)


def kernel(x_prompt, x_sample, mem_prompt, mem_sample, g_mix_pre, g_mix_post, w_in, hy_conv, hy_fw1, hy_fb1,
           hy_fw2, hy_fb2, hy_fw3, hy_bias, ret_decay, att_qnorm, att_knorm, w_branch, w_out, g_x_pre, g_x_post,
           g_mem, w_xq, w_xkv, w_xo, g_ff_pre, g_ff_post, w_ff1, w_ff2):
    w = dict(
        g_mix_pre=g_mix_pre, g_mix_post=g_mix_post, w_in=w_in.astype(BF16), hy_conv=hy_conv,
        hy_fw1=hy_fw1, hy_fb1=hy_fb1, hy_fw2=hy_fw2, hy_fb2=hy_fb2, hy_fw3=hy_fw3, hy_bias=hy_bias,
        ret_lg=jax.nn.log_sigmoid(ret_decay.astype(F32)), att_qnorm=att_qnorm, att_knorm=att_knorm,
        w_branch=w_branch.astype(BF16), w_out=w_out.astype(BF16), g_x_pre=g_x_pre, g_x_post=g_x_post,
        g_mem=g_mem, w_xq=w_xq.astype(BF16), w_xkv=w_xkv.astype(BF16), w_xo=w_xo.astype(BF16),
        g_ff_pre=g_ff_pre, g_ff_post=g_ff_post, w_ff1=w_ff1.astype(BF16), w_ff2=w_ff2.astype(BF16),
    )
    y_prompt = _trunk(x_prompt, mem_prompt, w, _position_tables(x_prompt.shape[1]))
    y_sample = _trunk(x_sample, mem_sample, w, _position_tables(x_sample.shape[1]))
    return (y_prompt, y_sample)
```

```python
import functools
import math

import jax
import jax.numpy as jnp
from jax import lax
from jax.experimental import pallas as pl
from jax.experimental.pallas import tpu as pltpu

F32 = jnp.float32
BF16 = jnp.bfloat16

D_MODEL = 1024
DEPTH = 4
GRID_W = 64
EPS = 1e-6
HY_WIDTH = 512
HY_SHORT = 3
HY_EMB = 33
HY_FH = 64
HY_SIN_FREQ = 1.0
HY_TARGET = 1e-2
HY_FAST = 0.3
HY_SLOW = 1.5
HY_SHIFT = 0.0
RET_HEADS = 4
RET_DK = 128
RET_DV = 128
RET_WIDTH = RET_HEADS * RET_DV
RET_THETA = 10000.0
ATT_HEADS = 4
ATT_KV_HEADS = 2
ATT_HD = 128
ATT_WIDTH = ATT_HEADS * ATT_HD
ROPE_THETA = 10000.0
X_HEADS = 4
X_HD = D_MODEL // X_HEADS
D_FF = 4 * D_MODEL
N_BRANCH = 3
MIX_WIDTH = HY_WIDTH + RET_WIDTH + ATT_WIDTH

COL_HY = 0
COL_RQ = 3 * HY_WIDTH
COL_RK = COL_RQ + RET_HEADS * RET_DK
COL_RV = COL_RK + RET_HEADS * RET_DK
COL_RG = COL_RV + RET_WIDTH
COL_AQ = COL_RG + RET_WIDTH
COL_AK = COL_AQ + ATT_WIDTH
COL_AV = COL_AK + ATT_KV_HEADS * ATT_HD
COL_GATE = COL_AV + ATT_KV_HEADS * ATT_HD
IN_COLS = COL_GATE + N_BRANCH * D_MODEL

LANES = 128
VMEM_LIMIT = 56 * 1024 * 1024
DFT_N2 = 128
DFT_NB = 16
DFT_KB = 8
DFT_PASSES = dict(s1=1, s2f=1, s2i=1, s3=1, k1=3, k2=3)
RET_BLOCK = 256
ATT_KEY_CHUNK = 1024
LOG2E = 1.4426950408889634


def _params(sem):
    return pltpu.CompilerParams(dimension_semantics=sem, vmem_limit_bytes=VMEM_LIMIT)


def _dot(a, b):
    return jnp.dot(a, b, preferred_element_type=F32)


def _dot_nt(a, b):
    return lax.dot_general(a, b, (((1,), (1,)), ((), ())), preferred_element_type=F32)


def _dot_tn(a, b):
    return lax.dot_general(a, b, (((0,), (0,)), ((), ())), preferred_element_type=F32)


def _rms(x, g):
    ms = jnp.mean(x * x, axis=-1, keepdims=True)
    return x * lax.rsqrt(ms + EPS) * g


def _sigmoid(x):
    return 1.0 / (1.0 + jnp.exp(-x))


def _split(x):
    hi = x.astype(BF16)
    lo = (x - hi.astype(F32)).astype(BF16)
    return hi, lo


def _dot3(a, b):
    return _dot(a[0], b[0]) + (_dot(a[0], b[1]) + _dot(a[1], b[0]))


def _mm(a, b, passes):
    if passes == 1:
        return _dot(a.astype(BF16), b.astype(BF16))
    return _dot3(_split(a.astype(F32)), _split(b.astype(F32)))


def _nm_body(x_ref, g_ref, w_ref, o_ref, *, tn, act):
    h = _rms(x_ref[...], g_ref[...]).astype(BF16)
    for j in range(w_ref.shape[1] // tn):
        y = _dot(h, w_ref[:, j * tn:(j + 1) * tn])
        if act:
            y = jnp.square(jnp.maximum(y, 0.0))
        o_ref[:, j * tn:(j + 1) * tn] = y.astype(o_ref.dtype)


def _norm_matmul(x, g, w, *, tm, tn, act=False):
    T, D = x.shape
    N = w.shape[1]
    return pl.pallas_call(
        functools.partial(_nm_body, tn=tn, act=act),
        grid=(T // tm,),
        in_specs=[pl.BlockSpec((tm, D), lambda i: (i, 0)),
                  pl.BlockSpec((1, D), lambda i: (0, 0)),
                  pl.BlockSpec((D, N), lambda i: (0, 0), pipeline_mode=pl.Buffered(1))],
        out_specs=pl.BlockSpec((tm, N), lambda i: (i, 0)),
        out_shape=jax.ShapeDtypeStruct((T, N), BF16),
        compiler_params=_params(("parallel",)),
        name="norm_matmul",
    )(x, g.reshape(1, D), w)


def _mnr_body(a_ref, w_ref, g_ref, x_ref, o_ref):
    y = _dot(a_ref[...], w_ref[...])
    o_ref[...] = x_ref[...] + _rms(y, g_ref[...])


def _matmul_norm_residual(a, w, g, x, *, tm):
    T, K = a.shape
    D = w.shape[1]
    return pl.pallas_call(
        _mnr_body,
        grid=(T // tm,),
        in_specs=[pl.BlockSpec((tm, K), lambda i: (i, 0)),
                  pl.BlockSpec((K, D), lambda i: (0, 0)),
                  pl.BlockSpec((1, D), lambda i: (0, 0)),
                  pl.BlockSpec((tm, D), lambda i: (i, 0))],
        out_specs=pl.BlockSpec((tm, D), lambda i: (i, 0)),
        out_shape=jax.ShapeDtypeStruct((T, D), F32),
        compiler_params=_params(("parallel",)),
        name="matmul_norm_residual",
    )(a, w, g.reshape(1, D), x)


def _hyprep_body(x0_ref, x1_ref, v_ref, w_ref, zf_ref, x0c_ref):
    L = x0_ref.shape[1]
    row = lax.broadcasted_iota(jnp.int32, (L, LANES), 0)

    def conv(p_ref, grp):
        p = p_ref[0].astype(F32)
        w = w_ref[:, grp, :]
        prev = jnp.where(row == 0, 0.0, pltpu.roll(p, 1, 0))
        nxt = jnp.where(row == L - 1, 0.0, pltpu.roll(p, L - 1, 0))
        return prev * w[0:1] + p * w[1:2] + nxt * w[2:3]

    x0c_ref[0] = conv(x0_ref, 0).astype(x0c_ref.dtype)
    zf_ref[0] = conv(v_ref, 2) * conv(x1_ref, 1)


def _hyena_prep(proj, conv_w):
    B, L, _ = proj.shape
    nb = HY_WIDTH // LANES
    w = conv_w.reshape(HY_SHORT, 3, HY_WIDTH)
    blk = lambda off: pl.BlockSpec((1, L, LANES), lambda b, c, off=off: (b, 0, off + c))
    out = pl.BlockSpec((1, L, LANES), lambda b, c: (b, 0, c))
    return pl.pallas_call(
        _hyprep_body,
        grid=(B, nb),
        in_specs=[blk(0), blk(nb), blk(2 * nb),
                  pl.BlockSpec((HY_SHORT, 3, LANES), lambda b, c: (0, 0, c))],
        out_specs=[out, out],
        out_shape=[jax.ShapeDtypeStruct((B, L, HY_WIDTH), F32),
                   jax.ShapeDtypeStruct((B, L, HY_WIDTH), BF16)],
        compiler_params=_params(("parallel", "parallel")),
        name="hyena_prep",
    )(proj, proj, proj, w)


def _filt_body(f_ref, w1_ref, b1_ref, w2_ref, b2_ref, w3_ref, dl_ref, o_ref, *, L):
    tl = f_ref.shape[0]
    hp = functools.partial(jnp.dot, preferred_element_type=F32, precision=lax.Precision.HIGHEST)
    f = f_ref[...]
    h = jnp.sin(HY_SIN_FREQ * (hp(f, w1_ref[...]) + b1_ref[...]))
    h = jnp.sin(HY_SIN_FREQ * (hp(h, w2_ref[...]) + b2_ref[...]))
    filt = hp(h, w3_ref[...])
    mod = jnp.exp(-f[:, 0:1] * dl_ref[...]) + HY_SHIFT
    row = pl.program_id(0) * tl + lax.broadcasted_iota(jnp.int32, (tl, HY_WIDTH), 0)
    sel = jnp.where(row < L, filt[:, :HY_WIDTH], filt[:, HY_WIDTH:])
    o_ref[...] = jnp.where(row == L, 0.0, sel * mod)


def _hyena_features(L):
    t = jnp.linspace(0.0, 1.0, L, dtype=F32)[:, None]
    bands = (HY_EMB - 1) // 2
    w = 2.0 * math.pi * jnp.arange(L, dtype=F32)[:, None] / L
    f = jnp.linspace(1e-4, bands - 1, bands, dtype=F32)[None, :]
    z = jnp.concatenate([t, jnp.cos(f * w), -jnp.sin(f * w)], axis=-1)
    zc = jnp.concatenate([z, z[:1], jnp.flip(z[1:], axis=0)], axis=0)
    return jnp.pad(zc, ((0, 0), (0, LANES - HY_EMB)))


def _hyena_filter(feats, w1, b1, w2, b2, w3, L):
    tl = 512
    pad = LANES - HY_FH
    w1p = jnp.pad(w1, ((0, LANES - HY_EMB), (0, pad)))
    b1p = jnp.pad(b1, (0, pad)).reshape(1, LANES)
    w2p = jnp.pad(w2, ((0, pad), (0, pad)))
    b2p = jnp.pad(b2, (0, pad)).reshape(1, LANES)
    w3p = jnp.pad(w3, ((0, pad), (0, 0)))
    min_decay = math.log(HY_TARGET) / HY_SLOW
    max_decay = math.log(HY_TARGET) / HY_FAST
    deltas = jnp.abs(jnp.linspace(min_decay, max_decay, HY_WIDTH, dtype=F32)).reshape(1, HY_WIDTH)
    full = lambda shape: pl.BlockSpec(shape, lambda i: (0,) * len(shape))
    return pl.pallas_call(
        functools.partial(_filt_body, L=L),
        grid=(2 * L // tl,),
        in_specs=[pl.BlockSpec((tl, LANES), lambda i: (i, 0)),
                  full((LANES, LANES)), full((1, LANES)), full((LANES, LANES)), full((1, LANES)),
                  full((LANES, 2 * HY_WIDTH)), full((1, HY_WIDTH))],
        out_specs=pl.BlockSpec((tl, HY_WIDTH), lambda i: (i, 0)),
        out_shape=jax.ShapeDtypeStruct((2 * L, HY_WIDTH), F32),
        compiler_params=_params(("parallel",)),
        name="hyena_filter",
    )(feats, w1p, b1p, w2p, b2p, w3p, deltas)


def _dft_tables(L):
    N = 2 * L
    N2 = DFT_N2
    N1 = N // N2
    N1h = N1 // 2
    k1 = jnp.arange(N1, dtype=jnp.int32)[:, None]
    n1 = jnp.arange(N1, dtype=jnp.int32)[None, :]
    a1 = ((k1 * n1) % N1).astype(F32) * (2.0 * math.pi / N1)
    c1, s1 = jnp.cos(a1), jnp.sin(a1)
    ch, sh = c1[:, :N1h], s1[:, :N1h]
    f1_pair = jnp.stack([jnp.concatenate([ch, sh], axis=1),
                         jnp.concatenate([-sh, ch], axis=1)], axis=1).reshape(2 * N1, 2 * N1h)
    f1_real = jnp.stack([c1, -s1], axis=1).reshape(2 * N1, N1)
    ci, si = c1[:N1h] * (1.0 / N), s1[:N1h] * (1.0 / N)
    f3_pair = jnp.concatenate([jnp.stack([ci, -si], axis=2).reshape(N1h, 2 * N1),
                               jnp.stack([si, ci], axis=2).reshape(N1h, 2 * N1)], axis=0)
    kk = (jnp.arange(N1, dtype=jnp.int32)[:, None, None]
          + N1 * jnp.arange(N2, dtype=jnp.int32)[None, :, None])
    n2 = jnp.arange(N2, dtype=jnp.int32)[None, None, :]
    a2 = ((kk * n2) % N).astype(F32) * (2.0 * math.pi / N)
    gr, gi = jnp.cos(a2), -jnp.sin(a2)
    gtr, gti = jnp.swapaxes(gr, 1, 2), jnp.swapaxes(gi, 1, 2)
    g_fwd = jnp.concatenate([jnp.concatenate([gr, -gi], axis=2),
                             jnp.concatenate([gi, gr], axis=2)], axis=1)
    g_inv = jnp.concatenate([jnp.concatenate([gtr, gti], axis=2),
                             jnp.concatenate([-gti, gtr], axis=2)], axis=1)
    return dict(N1=N1, N1h=N1h, N2=N2, f1_pair=f1_pair, f1_real=f1_real, f3_pair=f3_pair,
                g_fwd=g_fwd, g_inv=g_inv)


def _stage_dtype(passes):
    return BF16 if passes == 1 else F32


def _row_expand(f):
    return jnp.kron(f, jnp.eye(DFT_NB, dtype=f.dtype))


def _dft1_body(x_ref, f_ref, o_ref, *, passes):
    P, K, NB, C = x_ref.shape
    y = _mm(f_ref[...], x_ref[...].reshape(P * K * NB, C), passes)
    o_ref[0] = y.astype(o_ref.dtype).reshape(o_ref.shape[1:])


def _dft_stage1(x, f, *, P, passes, out_dtype):
    Bx, K, N2, C = x.shape
    N1 = f.shape[0] // (2 * DFT_NB)
    return pl.pallas_call(
        functools.partial(_dft1_body, passes=passes),
        grid=(Bx // P, N2 // DFT_NB),
        in_specs=[pl.BlockSpec((P, K, DFT_NB, C), lambda b, j: (b, 0, j, 0)),
                  pl.BlockSpec(f.shape, lambda b, j: (0, 0), pipeline_mode=pl.Buffered(1))],
        out_specs=pl.BlockSpec((1, N1, 2, DFT_NB, C), lambda b, j: (b, 0, 0, j, 0)),
        out_shape=jax.ShapeDtypeStruct((Bx // P, N1, 2, N2, C), out_dtype),
        compiler_params=_params(("parallel", "parallel")),
        name="dft_stage1",
    )(x, f)


def _spec_body(a_ref, g_ref, o_ref, *, passes):
    def step(k, carry):
        o_ref[k] = _mm(g_ref[k], a_ref[0, k], passes)
        return carry

    lax.fori_loop(0, a_ref.shape[1], step, 0)


def _dft_spectrum(a, g, *, kb, passes):
    _, N1, R, C = a.shape
    return pl.pallas_call(
        functools.partial(_spec_body, passes=passes),
        grid=(N1 // kb,),
        in_specs=[pl.BlockSpec((1, kb, R, C), lambda i: (0, i, 0, 0)),
                  pl.BlockSpec((kb, R, R), lambda i: (i, 0, 0))],
        out_specs=pl.BlockSpec((kb, R, C), lambda i: (i, 0, 0)),
        out_shape=jax.ShapeDtypeStruct((N1, R, C), F32),
        compiler_params=_params(("parallel",)),
        name="dft_spectrum",
    )(a, g)


def _dft2_body(a_ref, gf_ref, gi_ref, ks_ref, o_ref, *, passes_fwd, passes_inv):
    H = a_ref.shape[2] // 2

    def step(k, carry):
        y = _mm(gf_ref[k], a_ref[0, k], passes_fwd)
        yr, yi = y[:H], y[H:]
        kr, ki = ks_ref[k, :H, :], ks_ref[k, H:, :]
        z = jnp.concatenate([yr * kr - yi * ki, yr * ki + yi * kr], axis=0)
        o_ref[0, k] = _mm(gi_ref[k], z, passes_inv).astype(o_ref.dtype)
        return carry

    lax.fori_loop(0, a_ref.shape[1], step, 0)


def _dft_stage2(a, gf, gi, ks, *, kb, passes_fwd, passes_inv, out_dtype):
    Bp, N1, R, C = a.shape
    dat = pl.BlockSpec((1, kb, R, C), lambda i, b: (b, i, 0, 0))
    mat = pl.BlockSpec((kb, R, R), lambda i, b: (i, 0, 0))
    return pl.pallas_call(
        functools.partial(_dft2_body, passes_fwd=passes_fwd, passes_inv=passes_inv),
        grid=(N1 // kb, Bp),
        in_specs=[dat, mat, mat, pl.BlockSpec((kb, R, C), lambda i, b: (i, 0, 0))],
        out_specs=dat,
        out_shape=jax.ShapeDtypeStruct((Bp, N1, R, C), out_dtype),
        compiler_params=_params(("parallel", "parallel")),
        name="dft_stage2",
    )(a, gf, gi, ks)


def _dft3_body(t_ref, f_ref, zf_ref, x0_ref, bias_ref, o_ref, *, passes):
    _, N1, _, NB, C = t_ref.shape
    y = _mm(f_ref[...], t_ref[0].reshape(N1 * 2 * NB, C), passes).reshape(zf_ref.shape)
    o_ref[...] = ((y + zf_ref[...] * bias_ref[...]) * x0_ref[...].astype(F32)).astype(o_ref.dtype)


def _dft_stage3(t, f, zf, x0, bias, *, passes):
    Bp, N1, _, N2, C = t.shape
    B, N1h = zf.shape[:2]
    pair = pl.BlockSpec((2, N1h, DFT_NB, C), lambda b, j: (b, 0, j, 0))
    return pl.pallas_call(
        functools.partial(_dft3_body, passes=passes),
        grid=(Bp, N2 // DFT_NB),
        in_specs=[pl.BlockSpec((1, N1, 2, DFT_NB, C), lambda b, j: (b, 0, 0, j, 0)),
                  pl.BlockSpec(f.shape, lambda b, j: (0, 0), pipeline_mode=pl.Buffered(1)),
                  pair, pair,
                  pl.BlockSpec((1, C), lambda b, j: (0, 0))],
        out_specs=pair,
        out_shape=jax.ShapeDtypeStruct((B, N1h, N2, C), BF16),
        compiler_params=_params(("parallel", "parallel")),
        name="dft_stage3",
    )(t, f, zf, x0, bias)


def _hyena_mixer(proj, conv_w, kspec, bias, tabs):
    B, L, _ = proj.shape
    N1, N1h, N2 = tabs["N1"], tabs["N1h"], tabs["N2"]
    C = HY_WIDTH
    ps = DFT_PASSES
    zf, x0c = _hyena_prep(proj, conv_w)
    z4 = zf.reshape(B, N1h, N2, C)
    f1 = _row_expand(tabs["f1_pair"]).astype(_stage_dtype(ps["s1"]))
    a = _dft_stage1(z4, f1, P=2, passes=ps["s1"], out_dtype=_stage_dtype(ps["s2f"]))
    gdt = _stage_dtype(min(ps["s2f"], ps["s2i"]))
    t = _dft_stage2(a.reshape(B // 2, N1, 2 * N2, C), tabs["g_fwd"].astype(gdt), tabs["g_inv"].astype(gdt),
                    kspec, kb=DFT_KB, passes_fwd=ps["s2f"], passes_inv=ps["s2i"],
                    out_dtype=_stage_dtype(ps["s3"]))
    f3 = _row_expand(tabs["f3_pair"]).astype(_stage_dtype(ps["s3"]))
    y = _dft_stage3(t.reshape(B // 2, N1, 2, N2, C), f3, z4, x0c.reshape(B, N1h, N2, C),
                    bias.astype(F32).reshape(1, C), passes=ps["s3"])
    return y.reshape(B * L, C)


def _hyena_spectrum(feats, w1, b1, w2, b2, w3, L, tabs):
    N1, N2 = tabs["N1"], tabs["N2"]
    C = HY_WIDTH
    ps = DFT_PASSES
    kfull = _hyena_filter(feats, w1, b1, w2, b2, w3, L)
    a = _dft_stage1(kfull.reshape(1, N1, N2, C), _row_expand(tabs["f1_real"]), P=1, passes=ps["k1"],
                    out_dtype=_stage_dtype(ps["k2"]))
    return _dft_spectrum(a.reshape(1, N1, 2 * N2, C), tabs["g_fwd"], kb=DFT_KB, passes=ps["k2"])


def _ret_body(lg_ref, q_ref, k_ref, v_ref, g_ref, cos_ref, sin_ref, o_ref, qs_ref, ks_ref, st_ref):
    C = RET_BLOCK
    L = q_ref.shape[1]
    nblk = L // C
    dk = RET_DK
    h = pl.program_id(1)
    lgf = lg_ref[0, h]
    lgb = lg_ref[1, h]
    ii = lax.broadcasted_iota(jnp.int32, (C, C), 0)
    jj = lax.broadcasted_iota(jnp.int32, (C, C), 1)
    diff = (ii - jj).astype(F32)
    dm = jnp.where(diff >= 0, jnp.exp(jnp.maximum(diff, 0.0) * lgf), jnp.exp(jnp.maximum(-diff, 0.0) * lgb))
    pos = lax.broadcasted_iota(jnp.int32, (C, dk), 0).astype(F32)
    qw_f = jnp.exp((pos + 1.0) * lgf)
    kw_f = jnp.exp((C - 1.0 - pos) * lgf)
    qw_b = jnp.exp((C - pos) * lgb)
    kw_b = jnp.exp(pos * lgb)
    cd_f = jnp.exp(C * lgf)
    cd_b = jnp.exp(C * lgb)
    half = dk // 2

    def rot(x, sl):
        return x * cos_ref[sl, :] + pltpu.roll(x, half, 1) * sin_ref[sl, :]

    def block_kv(n, carry):
        sl = pl.ds(pl.multiple_of(n * C, C), C)
        q = rot(q_ref[0, sl, :].astype(F32), sl)
        k = rot(k_ref[0, sl, :].astype(F32), sl) * (dk ** -0.5)
        qs_ref[sl, :] = q
        ks_ref[sl, :] = k
        kk = jnp.concatenate([k * kw_f, k * kw_b], axis=1).astype(BF16)
        st_ref[n] = _dot_tn(kk, v_ref[0, sl, :].astype(BF16))
        return carry

    lax.fori_loop(0, nblk, block_kv, 0, unroll=2)

    def scan_f(n, S):
        kv = st_ref[n, :dk, :]
        st_ref[n, :dk, :] = S
        return S * cd_f + kv

    lax.fori_loop(0, nblk, scan_f, jnp.zeros((dk, RET_DV), F32))

    def scan_b(m, S):
        n = nblk - 1 - m
        kv = st_ref[n, dk:, :]
        st_ref[n, dk:, :] = S
        return S * cd_b + kv

    lax.fori_loop(0, nblk, scan_b, jnp.zeros((dk, RET_DV), F32))

    def block_out(n, carry):
        sl = pl.ds(pl.multiple_of(n * C, C), C)
        q = qs_ref[sl, :]
        k = ks_ref[sl, :]
        scores = _dot_nt(q.astype(BF16), k.astype(BF16)) * dm
        qq = jnp.concatenate([q * qw_f, q * qw_b], axis=1).astype(BF16)
        o = _dot(scores.astype(BF16), v_ref[0, sl, :].astype(BF16)) + _dot(qq, st_ref[n].astype(BF16))
        mu = jnp.mean(o, axis=-1, keepdims=True)
        oc = o - mu
        on = oc * lax.rsqrt(jnp.mean(oc * oc, axis=-1, keepdims=True) + EPS)
        g = g_ref[0, sl, :].astype(F32)
        o_ref[0, sl, :] = (g * _sigmoid(g) * on).astype(o_ref.dtype)
        return carry

    lax.fori_loop(0, nblk, block_out, 0, unroll=2)


def _retention_mixer(proj, log_gamma, cos_t, sin_t):
    B, L, _ = proj.shape
    blk = lambda col: pl.BlockSpec((1, L, LANES), lambda b, h, c=col // LANES: (b, 0, c + h))
    tab = pl.BlockSpec((L, LANES), lambda b, h: (0, 0))
    y = pl.pallas_call(
        _ret_body,
        grid=(B, RET_HEADS),
        in_specs=[pl.BlockSpec(memory_space=pltpu.SMEM),
                  blk(COL_RQ), blk(COL_RK), blk(COL_RV), blk(COL_RG), tab, tab],
        out_specs=pl.BlockSpec((1, L, LANES), lambda b, h: (b, 0, h)),
        out_shape=jax.ShapeDtypeStruct((B, L, RET_WIDTH), BF16),
        scratch_shapes=[pltpu.VMEM((L, LANES), F32), pltpu.VMEM((L, LANES), F32),
                        pltpu.VMEM((L // RET_BLOCK, 2 * RET_DK, RET_DV), F32)],
        compiler_params=_params(("parallel", "parallel")),
        name="retention",
    )(log_gamma, proj, proj, proj, proj, cos_t, sin_t)
    return y.reshape(B * L, RET_WIDTH)


def _attprep_body(q_ref, kv_ref, qn_ref, kn_ref, cos_ref, sin_ref, o_ref):
    cos = cos_ref[...]
    sin = sin_ref[...]
    lane = lax.broadcasted_iota(jnp.int32, cos.shape, 1)
    lower = (lane % (ATT_HD // 2)) < (ATT_HD // 4)

    def rope(x, gain, scale):
        x = _rms(x.astype(F32), gain)
        swapped = jnp.where(lower, pltpu.roll(x, LANES - ATT_HD // 4, 1), pltpu.roll(x, ATT_HD // 4, 1))
        return (x * cos + swapped * sin) * scale

    q = q_ref[0]
    kv = kv_ref[0]
    for hh in range(ATT_HEADS):
        sl = slice(hh * ATT_HD, (hh + 1) * ATT_HD)
        o_ref[0, :, sl] = rope(q[:, sl], qn_ref[...], ATT_HD ** -0.5 * LOG2E).astype(o_ref.dtype)
    for hh in range(ATT_KV_HEADS):
        sl = slice(hh * ATT_HD, (hh + 1) * ATT_HD)
        o_ref[0, :, ATT_WIDTH + hh * ATT_HD:ATT_WIDTH + (hh + 1) * ATT_HD] = (
            rope(kv[:, sl], kn_ref[...], 1.0).astype(o_ref.dtype))
    nkv = ATT_KV_HEADS * ATT_HD
    o_ref[0, :, ATT_WIDTH + nkv:] = kv[:, nkv:].astype(o_ref.dtype)


def _attention_prep(proj, qn, kn, cos_t, sin_t, *, tl):
    B, L, _ = proj.shape
    wq = ATT_WIDTH
    wkv = 2 * ATT_KV_HEADS * ATT_HD
    return pl.pallas_call(
        _attprep_body,
        grid=(B, L // tl),
        in_specs=[pl.BlockSpec((1, tl, wq), lambda b, i: (b, i, COL_AQ // wq)),
                  pl.BlockSpec((1, tl, wkv), lambda b, i: (b, i, COL_AK // wkv)),
                  pl.BlockSpec((1, ATT_HD), lambda b, i: (0, 0)),
                  pl.BlockSpec((1, ATT_HD), lambda b, i: (0, 0)),
                  pl.BlockSpec((tl, LANES), lambda b, i: (i, 0)),
                  pl.BlockSpec((tl, LANES), lambda b, i: (i, 0))],
        out_specs=pl.BlockSpec((1, tl, wq + wkv), lambda b, i: (b, i, 0)),
        out_shape=jax.ShapeDtypeStruct((B, L, wq + wkv), BF16),
        compiler_params=_params(("parallel", "parallel")),
        name="attention_prep",
    )(proj, proj, qn.reshape(1, ATT_HD), kn.reshape(1, ATT_HD), cos_t, sin_t)


def _att_body(q_ref, k_ref, v_ref, o_ref):
    G = ATT_HEADS // ATT_KV_HEADS
    tq = q_ref.shape[1]
    kc = min(ATT_KEY_CHUNK, k_ref.shape[1])
    nchunk = k_ref.shape[1] // kc
    q = q_ref[0]
    qs = jnp.concatenate([q[:, g * ATT_HD:(g + 1) * ATT_HD] for g in range(G)], axis=0)
    m = jnp.full((G * tq, 1), -jnp.inf, F32)
    l = jnp.zeros((G * tq, 1), F32)
    acc = jnp.zeros((G * tq, ATT_HD), F32)
    s_next = _dot_nt(qs, k_ref[0, 0:kc, :])
    for c in range(nchunk):
        s = s_next
        if c + 1 < nchunk:
            s_next = _dot_nt(qs, k_ref[0, (c + 1) * kc:(c + 2) * kc, :])
        m_new = jnp.maximum(m, jnp.max(s, axis=-1, keepdims=True))
        alpha = jnp.exp2(m - m_new)
        p = jnp.exp2(s - m_new)
        l = alpha * l + jnp.sum(p, axis=-1, keepdims=True)
        acc = alpha * acc + _dot(p.astype(BF16), v_ref[0, c * kc:(c + 1) * kc, :])
        m = m_new
    o = acc * (1.0 / l)
    for g in range(G):
        o_ref[0, :, g * ATT_HD:(g + 1) * ATT_HD] = o[g * tq:(g + 1) * tq].astype(o_ref.dtype)


def _attention(qkv, *, tq):
    B, L, _ = qkv.shape
    G = ATT_HEADS // ATT_KV_HEADS
    kcol = ATT_WIDTH // ATT_HD
    vcol = kcol + ATT_KV_HEADS
    y = pl.pallas_call(
        _att_body,
        grid=(B, ATT_KV_HEADS, L // tq),
        in_specs=[pl.BlockSpec((1, tq, G * ATT_HD), lambda b, h, i: (b, i, h)),
                  pl.BlockSpec((1, L, ATT_HD), lambda b, h, i: (b, 0, kcol + h)),
                  pl.BlockSpec((1, L, ATT_HD), lambda b, h, i: (b, 0, vcol + h))],
        out_specs=pl.BlockSpec((1, tq, G * ATT_HD), lambda b, h, i: (b, i, h)),
        out_shape=jax.ShapeDtypeStruct((B, L, ATT_WIDTH), BF16),
        compiler_params=_params(("parallel", "parallel", "arbitrary")),
        name="attention",
    )(qkv, qkv, qkv)
    return y.reshape(B * L, ATT_WIDTH)


def _merge_body(yh_ref, yr_ref, ya_ref, ga_ref, gb_ref, wb_ref, wo_ref, g_ref, x_ref, o_ref):
    D = D_MODEL
    ph = _dot(yh_ref[...], wb_ref[0:HY_WIDTH, :])
    pr = _dot(yr_ref[...], wb_ref[HY_WIDTH:HY_WIDTH + RET_WIDTH, :])
    pa = _dot(ya_ref[...], wb_ref[HY_WIDTH + RET_WIDTH:, :])
    ga = ga_ref[...].astype(F32)
    gb = gb_ref[...].astype(F32)
    hd = D // 2
    g0 = _sigmoid(ga[:, :D])
    g1 = _sigmoid(jnp.concatenate([ga[:, D:], gb[:, :hd]], axis=1))
    g2 = _sigmoid(gb[:, hd:])
    merged = g0 * ph + g1 * pr + g2 * pa
    y = _dot(merged.astype(BF16), wo_ref[...])
    o_ref[...] = x_ref[...] + _rms(y, g_ref[...])


def _merge(yh, yr, ya, proj2d, wb, wo, g, x, *, tm):
    T = x.shape[0]
    D = D_MODEL
    gw = N_BRANCH * D // 2
    row = lambda w: pl.BlockSpec((tm, w), lambda i: (i, 0))
    return pl.pallas_call(
        _merge_body,
        grid=(T // tm,),
        in_specs=[row(HY_WIDTH), row(RET_WIDTH), row(ATT_WIDTH),
                  pl.BlockSpec((tm, gw), lambda i: (i, COL_GATE // gw)),
                  pl.BlockSpec((tm, gw), lambda i: (i, COL_GATE // gw + 1)),
                  pl.BlockSpec((MIX_WIDTH, D), lambda i: (0, 0)),
                  pl.BlockSpec((D, D), lambda i: (0, 0)),
                  pl.BlockSpec((1, D), lambda i: (0, 0)),
                  row(D)],
        out_specs=row(D),
        out_shape=jax.ShapeDtypeStruct((T, D), F32),
        compiler_params=_params(("parallel",)),
        name="branch_merge",
    )(yh, yr, ya, proj2d, proj2d, wb, wo, g.reshape(1, D), x)


def _xattn_body(x_ref, kv_ref, gpre_ref, wq_ref, wo_ref, gpost_ref, o_ref):
    x = x_ref[0]
    h = _rms(x, gpre_ref[...]).astype(BF16)
    q = (_dot(h, wq_ref[...]) * (X_HD ** -0.5)).astype(BF16)
    kv = kv_ref[0]
    outs = []
    for hh in range(X_HEADS):
        sl = slice(hh * X_HD, (hh + 1) * X_HD)
        s = _dot_nt(q[:, sl], kv[:, sl])
        p = jnp.exp(s - jnp.max(s, axis=-1, keepdims=True))
        l = jnp.sum(p, axis=-1, keepdims=True)
        vh = kv[:, D_MODEL + hh * X_HD:D_MODEL + (hh + 1) * X_HD]
        outs.append(_dot(p.astype(BF16), vh) * (1.0 / l))
    o = jnp.concatenate(outs, axis=1).astype(BF16)
    y = _dot(o, wo_ref[...])
    o_ref[0] = x + _rms(y, gpost_ref[...])


def _cross_attention(x, kv, gpre, wq, wo, gpost, *, tm):
    B, L, D = x.shape
    M = kv.shape[1]
    vec = pl.BlockSpec((1, D), lambda b, i: (0, 0))
    mat = pl.BlockSpec((D, D), lambda b, i: (0, 0))
    return pl.pallas_call(
        _xattn_body,
        grid=(B, L // tm),
        in_specs=[pl.BlockSpec((1, tm, D), lambda b, i: (b, i, 0)),
                  pl.BlockSpec((1, M, 2 * D), lambda b, i: (b, 0, 0)),
                  vec, mat, mat, vec],
        out_specs=pl.BlockSpec((1, tm, D), lambda b, i: (b, i, 0)),
        out_shape=jax.ShapeDtypeStruct((B, L, D), F32),
        compiler_params=_params(("parallel", "parallel")),
        name="cross_attention",
    )(x, kv, gpre.reshape(1, D), wq, wo, gpost.reshape(1, D))


def _rotary_tables(pos, dim, theta):
    inv = theta ** (-jnp.arange(0, dim, 2, dtype=F32) / dim)
    ang = pos.astype(F32)[:, None] * inv[None, :]
    return jnp.cos(ang), jnp.sin(ang)


def _position_tables(L):
    pos = jnp.arange(L)
    rows = L // GRID_W
    row_ids = jnp.repeat(jnp.arange(rows), GRID_W)
    col_ids = jnp.tile(jnp.arange(GRID_W), rows)
    rcos, rsin = _rotary_tables(pos, RET_DK, RET_THETA)
    rc, rs = _rotary_tables(row_ids, ATT_HD // 2, ROPE_THETA)
    cc, cs = _rotary_tables(col_ids, ATT_HD // 2, ROPE_THETA)
    return dict(
        ret_cos=jnp.concatenate([rcos, rcos], axis=-1),
        ret_sin=jnp.concatenate([-rsin, rsin], axis=-1),
        att_cos=jnp.concatenate([rc, rc, cc, cc], axis=-1),
        att_sin=jnp.concatenate([-rs, rs, -cs, cs], axis=-1),
        feats=_hyena_features(L),
        dft=_dft_tables(L),
    )


def _trunk(x, mem, w, pt):
    B, L, D = x.shape
    M = mem.shape[1]
    T = B * L
    x2 = x.reshape(T, D)
    mem2 = mem.reshape(B * M, D)
    for l in range(DEPTH):
        proj = _norm_matmul(x2, w["g_mix_pre"][l], w["w_in"][l], tm=512, tn=1280)
        proj3 = proj.reshape(B, L, IN_COLS)
        kspec = _hyena_spectrum(pt["feats"], w["hy_fw1"][l], w["hy_fb1"][l], w["hy_fw2"][l],
                                w["hy_fb2"][l], w["hy_fw3"][l], L, pt["dft"])
        y_h = _hyena_mixer(proj3, w["hy_conv"][l], kspec, w["hy_bias"][l], pt["dft"])
        y_r = _retention_mixer(proj3, w["ret_lg"][l], pt["ret_cos"], pt["ret_sin"])
        qkv = _attention_prep(proj3, w["att_qnorm"][l], w["att_knorm"][l],
                              pt["att_cos"], pt["att_sin"], tl=512)
        y_a = _attention(qkv, tq=512)
        x2 = _merge(y_h, y_r, y_a, proj, w["w_branch"][l], w["w_out"][l], w["g_mix_post"][l], x2, tm=512)
        kv = _norm_matmul(mem2, w["g_mem"][l], w["w_xkv"][l], tm=512, tn=1024)
        x3 = _cross_attention(x2.reshape(B, L, D), kv.reshape(B, M, 2 * D), w["g_x_pre"][l],
                              w["w_xq"][l], w["w_xo"][l], w["g_x_post"][l], tm=512)
        x2 = x3.reshape(T, D)
        u = _norm_matmul(x2, w["g_ff_pre"][l], w["w_ff1"][l], tm=512, tn=1024, act=True)
        x2 = _matmul_norm_residual(u, w["w_ff2"][l], w["g_ff_post"][l], x2, tm=512)
    return x2.reshape(B, L, D)


def kernel(x_prompt, x_sample, mem_prompt, mem_sample, g_mix_pre, g_mix_post, w_in, hy_conv, hy_fw1, hy_fb1,
           hy_fw2, hy_fb2, hy_fw3, hy_bias, ret_decay, att_qnorm, att_knorm, w_branch, w_out, g_x_pre, g_x_post,
           g_mem, w_xq, w_xkv, w_xo, g_ff_pre, g_ff_post, w_ff1, w_ff2):
    w = dict(
        g_mix_pre=g_mix_pre, g_mix_post=g_mix_post, w_in=w_in.astype(BF16), hy_conv=hy_conv,
        hy_fw1=hy_fw1, hy_fb1=hy_fb1, hy_fw2=hy_fw2, hy_fb2=hy_fb2, hy_fw3=hy_fw3, hy_bias=hy_bias,
        ret_lg=jax.nn.log_sigmoid(ret_decay.astype(F32)), att_qnorm=att_qnorm, att_knorm=att_knorm,
        w_branch=w_branch.astype(BF16), w_out=w_out.astype(BF16), g_x_pre=g_x_pre, g_x_post=g_x_post,
        g_mem=g_mem, w_xq=w_xq.astype(BF16), w_xkv=w_xkv.astype(BF16), w_xo=w_xo.astype(BF16),
        g_ff_pre=g_ff_pre, g_ff_post=g_ff_post, w_ff1=w_ff1.astype(BF16), w_ff2=w_ff2.astype(BF16),
    )
    y_prompt = _trunk(x_prompt, mem_prompt, w, _position_tables(x_prompt.shape[1]))
    y_sample = _trunk(x_sample, mem_sample, w, _position_tables(x_sample.shape[1]))
    return (y_prompt, y_sample)
```

```python
import functools
import math

import jax
import jax.numpy as jnp
from jax import lax
from jax.experimental import pallas as pl
from jax.experimental.pallas import tpu as pltpu

F32 = jnp.float32
BF16 = jnp.bfloat16

D_MODEL = 1024
DEPTH = 4
GRID_W = 64
EPS = 1e-6
HY_WIDTH = 512
HY_SHORT = 3
HY_EMB = 33
HY_FH = 64
HY_SIN_FREQ = 1.0
HY_TARGET = 1e-2
HY_FAST = 0.3
HY_SLOW = 1.5
HY_SHIFT = 0.0
RET_HEADS = 4
RET_DK = 128
RET_DV = 128
RET_WIDTH = RET_HEADS * RET_DV
RET_THETA = 10000.0
ATT_HEADS = 4
ATT_KV_HEADS = 2
ATT_HD = 128
ATT_WIDTH = ATT_HEADS * ATT_HD
ROPE_THETA = 10000.0
X_HEADS = 4
X_HD = D_MODEL // X_HEADS
D_FF = 4 * D_MODEL
N_BRANCH = 3
MIX_WIDTH = HY_WIDTH + RET_WIDTH + ATT_WIDTH

COL_HY = 0
COL_RQ = 3 * HY_WIDTH
COL_RK = COL_RQ + RET_HEADS * RET_DK
COL_RV = COL_RK + RET_HEADS * RET_DK
COL_RG = COL_RV + RET_WIDTH
COL_AQ = COL_RG + RET_WIDTH
COL_AK = COL_AQ + ATT_WIDTH
COL_AV = COL_AK + ATT_KV_HEADS * ATT_HD
COL_GATE = COL_AV + ATT_KV_HEADS * ATT_HD
IN_COLS = COL_GATE + N_BRANCH * D_MODEL

LANES = 128
VMEM_LIMIT = 56 * 1024 * 1024
DFT_N2 = 128
DFT_NB = 16
DFT_KB = 8
DFT_PASSES = dict(s1=1, s2f=1, s2i=1, s3=1, k1=1, k2=1)
RET_BLOCK = 256
RET_UNROLL = 8
ATT_KEY_CHUNK = 1024
LOG2E = 1.4426950408889634


def _params(sem):
    return pltpu.CompilerParams(dimension_semantics=sem, vmem_limit_bytes=VMEM_LIMIT)


def _dot(a, b):
    return jnp.dot(a, b, preferred_element_type=F32)


def _dot_nt(a, b):
    return lax.dot_general(a, b, (((1,), (1,)), ((), ())), preferred_element_type=F32)


def _dot_tn(a, b):
    return lax.dot_general(a, b, (((0,), (0,)), ((), ())), preferred_element_type=F32)


def _rms(x, g):
    ms = jnp.mean(x * x, axis=-1, keepdims=True)
    return x * lax.rsqrt(ms + EPS) * g


def _sigmoid(x):
    return 1.0 / (1.0 + jnp.exp(-x))


def _split(x):
    hi = x.astype(BF16)
    lo = (x - hi.astype(F32)).astype(BF16)
    return hi, lo


def _dot3(a, b):
    return _dot(a[0], b[0]) + (_dot(a[0], b[1]) + _dot(a[1], b[0]))


def _mm(a, b, passes):
    if passes == 1:
        return _dot(a.astype(BF16), b.astype(BF16))
    return _dot3(_split(a.astype(F32)), _split(b.astype(F32)))


def _nm_body(x_ref, g_ref, w_ref, o_ref, *, tn, act):
    h = _rms(x_ref[...], g_ref[...]).astype(BF16)
    for j in range(w_ref.shape[1] // tn):
        y = _dot(h, w_ref[:, j * tn:(j + 1) * tn])
        if act:
            y = jnp.square(jnp.maximum(y, 0.0))
        o_ref[:, j * tn:(j + 1) * tn] = y.astype(o_ref.dtype)


def _norm_matmul(x, g, w, *, tm, tn, act=False):
    T, D = x.shape
    N = w.shape[1]
    return pl.pallas_call(
        functools.partial(_nm_body, tn=tn, act=act),
        grid=(T // tm,),
        in_specs=[pl.BlockSpec((tm, D), lambda i: (i, 0)),
                  pl.BlockSpec((1, D), lambda i: (0, 0)),
                  pl.BlockSpec((D, N), lambda i: (0, 0), pipeline_mode=pl.Buffered(1))],
        out_specs=pl.BlockSpec((tm, N), lambda i: (i, 0)),
        out_shape=jax.ShapeDtypeStruct((T, N), BF16),
        compiler_params=_params(("parallel",)),
        name="norm_matmul",
    )(x, g.reshape(1, D), w)


def _mnr_body(a_ref, w_ref, g_ref, x_ref, o_ref):
    y = _dot(a_ref[...], w_ref[...])
    o_ref[...] = x_ref[...] + _rms(y, g_ref[...])


def _matmul_norm_residual(a, w, g, x, *, tm):
    T, K = a.shape
    D = w.shape[1]
    return pl.pallas_call(
        _mnr_body,
        grid=(T // tm,),
        in_specs=[pl.BlockSpec((tm, K), lambda i: (i, 0)),
                  pl.BlockSpec((K, D), lambda i: (0, 0)),
                  pl.BlockSpec((1, D), lambda i: (0, 0)),
                  pl.BlockSpec((tm, D), lambda i: (i, 0))],
        out_specs=pl.BlockSpec((tm, D), lambda i: (i, 0)),
        out_shape=jax.ShapeDtypeStruct((T, D), F32),
        compiler_params=_params(("parallel",)),
        name="matmul_norm_residual",
    )(a, w, g.reshape(1, D), x)


def _hyprep_body(x0_ref, x1_ref, v_ref, w_ref, zf_ref, x0c_ref):
    L = x0_ref.shape[1]
    row = lax.broadcasted_iota(jnp.int32, (L, LANES), 0)

    def conv(p_ref, grp):
        p = p_ref[0].astype(F32)
        w = w_ref[:, grp, :]
        prev = jnp.where(row == 0, 0.0, pltpu.roll(p, 1, 0))
        nxt = jnp.where(row == L - 1, 0.0, pltpu.roll(p, L - 1, 0))
        return prev * w[0:1] + p * w[1:2] + nxt * w[2:3]

    x0c_ref[0] = conv(x0_ref, 0).astype(x0c_ref.dtype)
    zf_ref[0] = (conv(v_ref, 2) * conv(x1_ref, 1)).astype(zf_ref.dtype)


def _hyena_prep(proj, conv_w):
    B, L, _ = proj.shape
    nb = HY_WIDTH // LANES
    w = conv_w.reshape(HY_SHORT, 3, HY_WIDTH)
    blk = lambda off: pl.BlockSpec((1, L, LANES), lambda b, c, off=off: (b, 0, off + c))
    out = pl.BlockSpec((1, L, LANES), lambda b, c: (b, 0, c))
    return pl.pallas_call(
        _hyprep_body,
        grid=(B, nb),
        in_specs=[blk(0), blk(nb), blk(2 * nb),
                  pl.BlockSpec((HY_SHORT, 3, LANES), lambda b, c: (0, 0, c))],
        out_specs=[out, out],
        out_shape=[jax.ShapeDtypeStruct((B, L, HY_WIDTH), BF16)] * 2,
        compiler_params=_params(("parallel", "parallel")),
        name="hyena_prep",
    )(proj, proj, proj, w)


def _filt_body(f_ref, w1_ref, b1_ref, w2_ref, b2_ref, w3_ref, dl_ref, o_ref, *, L):
    tl = f_ref.shape[0]
    hp = functools.partial(jnp.dot, preferred_element_type=F32, precision=lax.Precision.HIGHEST)
    f = f_ref[...]
    h = jnp.sin(HY_SIN_FREQ * (hp(f, w1_ref[...]) + b1_ref[...]))
    h = jnp.sin(HY_SIN_FREQ * (hp(h, w2_ref[...]) + b2_ref[...]))
    filt = hp(h, w3_ref[...])
    mod = jnp.exp(-f[:, 0:1] * dl_ref[...]) + HY_SHIFT
    row = pl.program_id(0) * tl + lax.broadcasted_iota(jnp.int32, (tl, HY_WIDTH), 0)
    sel = jnp.where(row < L, filt[:, :HY_WIDTH], filt[:, HY_WIDTH:])
    o_ref[...] = jnp.where(row == L, 0.0, sel * mod)


def _hyena_features(L):
    t = jnp.linspace(0.0, 1.0, L, dtype=F32)[:, None]
    bands = (HY_EMB - 1) // 2
    w = 2.0 * math.pi * jnp.arange(L, dtype=F32)[:, None] / L
    f = jnp.linspace(1e-4, bands - 1, bands, dtype=F32)[None, :]
    z = jnp.concatenate([t, jnp.cos(f * w), -jnp.sin(f * w)], axis=-1)
    zc = jnp.concatenate([z, z[:1], jnp.flip(z[1:], axis=0)], axis=0)
    return jnp.pad(zc, ((0, 0), (0, LANES - HY_EMB)))


def _hyena_filter(feats, w1, b1, w2, b2, w3, L):
    tl = 512
    pad = LANES - HY_FH
    w1p = jnp.pad(w1, ((0, LANES - HY_EMB), (0, pad)))
    b1p = jnp.pad(b1, (0, pad)).reshape(1, LANES)
    w2p = jnp.pad(w2, ((0, pad), (0, pad)))
    b2p = jnp.pad(b2, (0, pad)).reshape(1, LANES)
    w3p = jnp.pad(w3, ((0, pad), (0, 0)))
    min_decay = math.log(HY_TARGET) / HY_SLOW
    max_decay = math.log(HY_TARGET) / HY_FAST
    deltas = jnp.abs(jnp.linspace(min_decay, max_decay, HY_WIDTH, dtype=F32)).reshape(1, HY_WIDTH)
    full = lambda shape: pl.BlockSpec(shape, lambda i: (0,) * len(shape))
    return pl.pallas_call(
        functools.partial(_filt_body, L=L),
        grid=(2 * L // tl,),
        in_specs=[pl.BlockSpec((tl, LANES), lambda i: (i, 0)),
                  full((LANES, LANES)), full((1, LANES)), full((LANES, LANES)), full((1, LANES)),
                  full((LANES, 2 * HY_WIDTH)), full((1, HY_WIDTH))],
        out_specs=pl.BlockSpec((tl, HY_WIDTH), lambda i: (i, 0)),
        out_shape=jax.ShapeDtypeStruct((2 * L, HY_WIDTH), F32),
        compiler_params=_params(("parallel",)),
        name="hyena_filter",
    )(feats, w1p, b1p, w2p, b2p, w3p, deltas)


def _expanded(fn, rows, cols):
    ri = lax.broadcasted_iota(jnp.int32, (rows * DFT_NB, cols * DFT_NB), 0)
    ci = lax.broadcasted_iota(jnp.int32, (rows * DFT_NB, cols * DFT_NB), 1)
    return jnp.where(ri % DFT_NB == ci % DFT_NB, fn(ri // DFT_NB, ci // DFT_NB), 0.0)


def _dft_tables(L):
    N = 2 * L
    N2 = DFT_N2
    N1 = N // N2
    N1h = N1 // 2

    def cs1(k1, n1):
        ang = ((k1 * n1) % N1).astype(F32) * (2.0 * math.pi / N1)
        return jnp.cos(ang), jnp.sin(ang)

    def f1_pair(r, c):
        cs, sn = cs1(r // 2, c % N1h)
        re_row, first = (r % 2) == 0, c < N1h
        return jnp.where(re_row, jnp.where(first, cs, sn), jnp.where(first, -sn, cs))

    def f1_real(r, c):
        cs, sn = cs1(r // 2, c)
        return jnp.where((r % 2) == 0, cs, -sn)

    def f3_pair(r, c):
        cs, sn = cs1(c // 2, r % N1h)
        re_out, re_in = r < N1h, (c % 2) == 0
        return jnp.where(re_out, jnp.where(re_in, cs, -sn), jnp.where(re_in, sn, cs)) * (1.0 / N)

    kk = (jnp.arange(N1, dtype=jnp.int32)[:, None, None]
          + N1 * jnp.arange(N2, dtype=jnp.int32)[None, :, None])
    n2 = jnp.arange(N2, dtype=jnp.int32)[None, None, :]
    a2 = ((kk * n2) % N).astype(F32) * (2.0 * math.pi / N)
    gr, gi = jnp.cos(a2), -jnp.sin(a2)
    gtr, gti = jnp.swapaxes(gr, 1, 2), jnp.swapaxes(gi, 1, 2)
    g_fwd = jnp.concatenate([jnp.concatenate([gr, -gi], axis=2),
                             jnp.concatenate([gi, gr], axis=2)], axis=1)
    g_inv = jnp.concatenate([jnp.concatenate([gtr, gti], axis=2),
                             jnp.concatenate([-gti, gtr], axis=2)], axis=1)
    return dict(N1=N1, N1h=N1h, N2=N2,
                f1_pair=_expanded(f1_pair, 2 * N1, 2 * N1h), f1_real=_expanded(f1_real, 2 * N1, N1),
                f3_pair=_expanded(f3_pair, 2 * N1h, 2 * N1), g_fwd=g_fwd, g_inv=g_inv)


def _stage_dtype(passes):
    return BF16 if passes == 1 else F32


def _dft1_body(x_ref, f_ref, o_ref, *, passes):
    P, K, NB, C = x_ref.shape
    y = _mm(f_ref[...], x_ref[...].reshape(P * K * NB, C), passes)
    o_ref[0] = y.astype(o_ref.dtype).reshape(o_ref.shape[1:])


def _dft_stage1(x, f, *, P, passes, out_dtype):
    Bx, K, N2, C = x.shape
    N1 = f.shape[0] // (2 * DFT_NB)
    return pl.pallas_call(
        functools.partial(_dft1_body, passes=passes),
        grid=(Bx // P, N2 // DFT_NB),
        in_specs=[pl.BlockSpec((P, K, DFT_NB, C), lambda b, j: (b, 0, j, 0)),
                  pl.BlockSpec(f.shape, lambda b, j: (0, 0), pipeline_mode=pl.Buffered(1))],
        out_specs=pl.BlockSpec((1, N1, 2, DFT_NB, C), lambda b, j: (b, 0, 0, j, 0)),
        out_shape=jax.ShapeDtypeStruct((Bx // P, N1, 2, N2, C), out_dtype),
        compiler_params=_params(("parallel", "parallel")),
        name="dft_stage1",
    )(x, f)


def _spec_body(a_ref, g_ref, o_ref, *, passes):
    def step(k, carry):
        o_ref[k] = _mm(g_ref[k], a_ref[0, k], passes)
        return carry

    lax.fori_loop(0, a_ref.shape[1], step, 0)


def _dft_spectrum(a, g, *, kb, passes):
    _, N1, R, C = a.shape
    return pl.pallas_call(
        functools.partial(_spec_body, passes=passes),
        grid=(N1 // kb,),
        in_specs=[pl.BlockSpec((1, kb, R, C), lambda i: (0, i, 0, 0)),
                  pl.BlockSpec((kb, R, R), lambda i: (i, 0, 0))],
        out_specs=pl.BlockSpec((kb, R, C), lambda i: (i, 0, 0)),
        out_shape=jax.ShapeDtypeStruct((N1, R, C), F32),
        compiler_params=_params(("parallel",)),
        name="dft_spectrum",
    )(a, g)


def _dft2_body(a_ref, gf_ref, gi_ref, ks_ref, o_ref, *, passes_fwd, passes_inv):
    H = a_ref.shape[2] // 2

    def step(k, carry):
        y = _mm(gf_ref[k], a_ref[0, k], passes_fwd)
        yr, yi = y[:H], y[H:]
        kr, ki = ks_ref[k, :H, :], ks_ref[k, H:, :]
        z = jnp.concatenate([yr * kr - yi * ki, yr * ki + yi * kr], axis=0)
        o_ref[0, k] = _mm(gi_ref[k], z, passes_inv).astype(o_ref.dtype)
        return carry

    lax.fori_loop(0, a_ref.shape[1], step, 0)


def _dft_stage2(a, gf, gi, ks, *, kb, passes_fwd, passes_inv, out_dtype):
    Bp, N1, R, C = a.shape
    dat = pl.BlockSpec((1, kb, R, C), lambda i, b: (b, i, 0, 0))
    mat = pl.BlockSpec((kb, R, R), lambda i, b: (i, 0, 0))
    return pl.pallas_call(
        functools.partial(_dft2_body, passes_fwd=passes_fwd, passes_inv=passes_inv),
        grid=(N1 // kb, Bp),
        in_specs=[dat, mat, mat, pl.BlockSpec((kb, R, C), lambda i, b: (i, 0, 0))],
        out_specs=dat,
        out_shape=jax.ShapeDtypeStruct((Bp, N1, R, C), out_dtype),
        compiler_params=_params(("parallel", "parallel")),
        name="dft_stage2",
    )(a, gf, gi, ks)


def _dft3_body(t_ref, f_ref, zf_ref, x0_ref, bias_ref, o_ref, *, passes):
    _, N1, _, NB, C = t_ref.shape
    y = _mm(f_ref[...], t_ref[0].reshape(N1 * 2 * NB, C), passes).reshape(zf_ref.shape)
    o_ref[...] = ((y + zf_ref[...].astype(F32) * bias_ref[...]) * x0_ref[...].astype(F32)).astype(o_ref.dtype)


def _dft_stage3(t, f, zf, x0, bias, *, passes):
    Bp, N1, _, N2, C = t.shape
    B, N1h = zf.shape[:2]
    pair = pl.BlockSpec((2, N1h, DFT_NB, C), lambda b, j: (b, 0, j, 0))
    return pl.pallas_call(
        functools.partial(_dft3_body, passes=passes),
        grid=(Bp, N2 // DFT_NB),
        in_specs=[pl.BlockSpec((1, N1, 2, DFT_NB, C), lambda b, j: (b, 0, 0, j, 0)),
                  pl.BlockSpec(f.shape, lambda b, j: (0, 0), pipeline_mode=pl.Buffered(1)),
                  pair, pair,
                  pl.BlockSpec((1, C), lambda b, j: (0, 0))],
        out_specs=pair,
        out_shape=jax.ShapeDtypeStruct((B, N1h, N2, C), BF16),
        compiler_params=_params(("parallel", "parallel")),
        name="dft_stage3",
    )(t, f, zf, x0, bias)


def _hyena_mixer(proj, conv_w, kspec, bias, tabs):
    B, L, _ = proj.shape
    N1, N1h, N2 = tabs["N1"], tabs["N1h"], tabs["N2"]
    C = HY_WIDTH
    ps = DFT_PASSES
    zf, x0c = _hyena_prep(proj, conv_w)
    z4 = zf.reshape(B, N1h, N2, C)
    f1 = tabs["f1_pair"].astype(_stage_dtype(ps["s1"]))
    a = _dft_stage1(z4, f1, P=2, passes=ps["s1"], out_dtype=_stage_dtype(ps["s2f"]))
    gdt = _stage_dtype(min(ps["s2f"], ps["s2i"]))
    t = _dft_stage2(a.reshape(B // 2, N1, 2 * N2, C), tabs["g_fwd"].astype(gdt), tabs["g_inv"].astype(gdt),
                    kspec, kb=DFT_KB, passes_fwd=ps["s2f"], passes_inv=ps["s2i"],
                    out_dtype=_stage_dtype(ps["s3"]))
    f3 = tabs["f3_pair"].astype(_stage_dtype(ps["s3"]))
    y = _dft_stage3(t.reshape(B // 2, N1, 2, N2, C), f3, z4, x0c.reshape(B, N1h, N2, C),
                    bias.astype(F32).reshape(1, C), passes=ps["s3"])
    return y.reshape(B * L, C)


def _hyena_spectrum(feats, w1, b1, w2, b2, w3, L, tabs):
    N1, N2 = tabs["N1"], tabs["N2"]
    C = HY_WIDTH
    ps = DFT_PASSES
    kfull = _hyena_filter(feats, w1, b1, w2, b2, w3, L)
    a = _dft_stage1(kfull.reshape(1, N1, N2, C), tabs["f1_real"].astype(_stage_dtype(ps["k1"])), P=1, passes=ps["k1"],
                    out_dtype=_stage_dtype(ps["k2"]))
    return _dft_spectrum(a.reshape(1, N1, 2 * N2, C), tabs["g_fwd"].astype(_stage_dtype(ps["k2"])), kb=DFT_KB,
                         passes=ps["k2"])


def _ret_body(lg_ref, q_ref, k_ref, v_ref, g_ref, cos_ref, sin_ref, o_ref, qs_ref, ks_ref, st_ref):
    C = RET_BLOCK
    L = q_ref.shape[1]
    nblk = L // C
    dk = RET_DK
    h = pl.program_id(1)
    lgf = lg_ref[0, h]
    lgb = lg_ref[1, h]
    ii = lax.broadcasted_iota(jnp.int32, (C, C), 0)
    jj = lax.broadcasted_iota(jnp.int32, (C, C), 1)
    diff = (ii - jj).astype(F32)
    dm = jnp.where(diff >= 0, jnp.exp(jnp.maximum(diff, 0.0) * lgf), jnp.exp(jnp.maximum(-diff, 0.0) * lgb))
    pos = lax.broadcasted_iota(jnp.int32, (C, dk), 0).astype(F32)
    qw_f = jnp.exp((pos + 1.0) * lgf)
    kw_f = jnp.exp((C - 1.0 - pos) * lgf)
    qw_b = jnp.exp((C - pos) * lgb)
    kw_b = jnp.exp(pos * lgb)
    cd_f = jnp.exp(C * lgf)
    cd_b = jnp.exp(C * lgb)
    half = dk // 2

    def rot(x, sl):
        return x * cos_ref[sl, :] + pltpu.roll(x, half, 1) * sin_ref[sl, :]

    def block_kv(n, carry):
        sl = pl.ds(pl.multiple_of(n * C, C), C)
        q = rot(q_ref[0, sl, :].astype(F32), sl)
        k = rot(k_ref[0, sl, :].astype(F32), sl) * (dk ** -0.5)
        qs_ref[sl, :] = q
        ks_ref[sl, :] = k
        kk = jnp.concatenate([k * kw_f, k * kw_b], axis=1).astype(BF16)
        st_ref[n] = _dot_tn(kk, v_ref[0, sl, :].astype(BF16))
        return carry

    lax.fori_loop(0, nblk, block_kv, 0, unroll=RET_UNROLL)

    def scan_f(n, S):
        kv = st_ref[n, :dk, :]
        st_ref[n, :dk, :] = S
        return S * cd_f + kv

    lax.fori_loop(0, nblk, scan_f, jnp.zeros((dk, RET_DV), F32))

    def scan_b(m, S):
        n = nblk - 1 - m
        kv = st_ref[n, dk:, :]
        st_ref[n, dk:, :] = S
        return S * cd_b + kv

    lax.fori_loop(0, nblk, scan_b, jnp.zeros((dk, RET_DV), F32))

    def block_out(n, carry):
        sl = pl.ds(pl.multiple_of(n * C, C), C)
        q = qs_ref[sl, :]
        k = ks_ref[sl, :]
        scores = _dot_nt(q.astype(BF16), k.astype(BF16)) * dm
        qq = jnp.concatenate([q * qw_f, q * qw_b], axis=1).astype(BF16)
        o = _dot(scores.astype(BF16), v_ref[0, sl, :].astype(BF16)) + _dot(qq, st_ref[n].astype(BF16))
        mu = jnp.mean(o, axis=-1, keepdims=True)
        oc = o - mu
        on = oc * lax.rsqrt(jnp.mean(oc * oc, axis=-1, keepdims=True) + EPS)
        g = g_ref[0, sl, :].astype(F32)
        o_ref[0, sl, :] = (g * _sigmoid(g) * on).astype(o_ref.dtype)
        return carry

    lax.fori_loop(0, nblk, block_out, 0, unroll=RET_UNROLL)


def _retention_mixer(proj, log_gamma, cos_t, sin_t):
    B, L, _ = proj.shape
    blk = lambda col: pl.BlockSpec((1, L, LANES), lambda b, h, c=col // LANES: (b, 0, c + h))
    tab = pl.BlockSpec((L, LANES), lambda b, h: (0, 0))
    y = pl.pallas_call(
        _ret_body,
        grid=(B, RET_HEADS),
        in_specs=[pl.BlockSpec(memory_space=pltpu.SMEM),
                  blk(COL_RQ), blk(COL_RK), blk(COL_RV), blk(COL_RG), tab, tab],
        out_specs=pl.BlockSpec((1, L, LANES), lambda b, h: (b, 0, h)),
        out_shape=jax.ShapeDtypeStruct((B, L, RET_WIDTH), BF16),
        scratch_shapes=[pltpu.VMEM((L, LANES), F32), pltpu.VMEM((L, LANES), F32),
                        pltpu.VMEM((L // RET_BLOCK, 2 * RET_DK, RET_DV), F32)],
        compiler_params=_params(("parallel", "parallel")),
        name="retention",
    )(log_gamma, proj, proj, proj, proj, cos_t, sin_t)
    return y.reshape(B * L, RET_WIDTH)


def _axial_rope(x, gain, cos, sin, scale):
    x = _rms(x.astype(F32), gain)
    lane = lax.broadcasted_iota(jnp.int32, x.shape, 1)
    lower = (lane % (ATT_HD // 2)) < (ATT_HD // 4)
    swapped = jnp.where(lower, pltpu.roll(x, LANES - ATT_HD // 4, 1), pltpu.roll(x, ATT_HD // 4, 1))
    return ((x * cos + swapped * sin) * scale).astype(BF16)


def _att_body(q_ref, k_ref, v_ref, qn_ref, kn_ref, cq_ref, sq_ref, ck_ref, sk_ref, o_ref, ks_ref):
    G = ATT_HEADS // ATT_KV_HEADS
    tq = q_ref.shape[1]
    kc = min(ATT_KEY_CHUNK, k_ref.shape[1])
    nchunk = k_ref.shape[1] // kc

    @pl.when(pl.program_id(2) == 0)
    def _():
        ks_ref[...] = _axial_rope(k_ref[0], kn_ref[...], ck_ref[...], sk_ref[...], 1.0)

    q = q_ref[0]
    cq = cq_ref[...]
    sq = sq_ref[...]
    qs = jnp.concatenate([_axial_rope(q[:, g * ATT_HD:(g + 1) * ATT_HD], qn_ref[...], cq, sq,
                                      ATT_HD ** -0.5 * LOG2E) for g in range(G)], axis=0)
    m = jnp.full((G * tq, 1), -jnp.inf, F32)
    l = jnp.zeros((G * tq, 1), F32)
    acc = jnp.zeros((G * tq, ATT_HD), F32)
    s_next = _dot_nt(qs, ks_ref[0:kc, :])
    for c in range(nchunk):
        s = s_next
        if c + 1 < nchunk:
            s_next = _dot_nt(qs, ks_ref[(c + 1) * kc:(c + 2) * kc, :])
        m_new = jnp.maximum(m, jnp.max(s, axis=-1, keepdims=True))
        alpha = jnp.exp2(m - m_new)
        p = jnp.exp2(s - m_new)
        l = alpha * l + jnp.sum(p, axis=-1, keepdims=True)
        acc = alpha * acc + _dot(p.astype(BF16), v_ref[0, c * kc:(c + 1) * kc, :].astype(BF16))
        m = m_new
    o = acc * (1.0 / l)
    for g in range(G):
        o_ref[0, :, g * ATT_HD:(g + 1) * ATT_HD] = o[g * tq:(g + 1) * tq].astype(o_ref.dtype)


def _attention_mixer(proj, qn, kn, cos_t, sin_t, *, tq):
    B, L, _ = proj.shape
    G = ATT_HEADS // ATT_KV_HEADS
    gain = pl.BlockSpec((1, ATT_HD), lambda b, h, i: (0, 0))
    tab_q = pl.BlockSpec((tq, LANES), lambda b, h, i: (i, 0))
    tab_k = pl.BlockSpec((L, LANES), lambda b, h, i: (0, 0), pipeline_mode=pl.Buffered(1))
    y = pl.pallas_call(
        _att_body,
        grid=(B, ATT_KV_HEADS, L // tq),
        in_specs=[pl.BlockSpec((1, tq, G * ATT_HD), lambda b, h, i: (b, i, COL_AQ // (G * ATT_HD) + h)),
                  pl.BlockSpec((1, L, ATT_HD), lambda b, h, i: (b, 0, COL_AK // ATT_HD + h)),
                  pl.BlockSpec((1, L, ATT_HD), lambda b, h, i: (b, 0, COL_AV // ATT_HD + h)),
                  gain, gain, tab_q, tab_q, tab_k, tab_k],
        out_specs=pl.BlockSpec((1, tq, G * ATT_HD), lambda b, h, i: (b, i, h)),
        out_shape=jax.ShapeDtypeStruct((B, L, ATT_WIDTH), BF16),
        scratch_shapes=[pltpu.VMEM((L, ATT_HD), BF16)],
        compiler_params=_params(("parallel", "parallel", "arbitrary")),
        name="attention",
    )(proj, proj, proj, qn.reshape(1, ATT_HD), kn.reshape(1, ATT_HD), cos_t, sin_t, cos_t, sin_t)
    return y.reshape(B * L, ATT_WIDTH)


def _merge_body(yh_ref, yr_ref, ya_ref, ga_ref, gb_ref, wb_ref, wo_ref, g_ref, x_ref, o_ref):
    D = D_MODEL
    ph = _dot(yh_ref[...], wb_ref[0:HY_WIDTH, :])
    pr = _dot(yr_ref[...], wb_ref[HY_WIDTH:HY_WIDTH + RET_WIDTH, :])
    pa = _dot(ya_ref[...], wb_ref[HY_WIDTH + RET_WIDTH:, :])
    ga = ga_ref[...].astype(F32)
    gb = gb_ref[...].astype(F32)
    hd = D // 2
    g0 = _sigmoid(ga[:, :D])
    g1 = _sigmoid(jnp.concatenate([ga[:, D:], gb[:, :hd]], axis=1))
    g2 = _sigmoid(gb[:, hd:])
    merged = g0 * ph + g1 * pr + g2 * pa
    y = _dot(merged.astype(BF16), wo_ref[...])
    o_ref[...] = x_ref[...] + _rms(y, g_ref[...])


def _merge(yh, yr, ya, proj2d, wb, wo, g, x, *, tm):
    T = x.shape[0]
    D = D_MODEL
    gw = N_BRANCH * D // 2
    row = lambda w: pl.BlockSpec((tm, w), lambda i: (i, 0))
    return pl.pallas_call(
        _merge_body,
        grid=(T // tm,),
        in_specs=[row(HY_WIDTH), row(RET_WIDTH), row(ATT_WIDTH),
                  pl.BlockSpec((tm, gw), lambda i: (i, COL_GATE // gw)),
                  pl.BlockSpec((tm, gw), lambda i: (i, COL_GATE // gw + 1)),
                  pl.BlockSpec((MIX_WIDTH, D), lambda i: (0, 0)),
                  pl.BlockSpec((D, D), lambda i: (0, 0)),
                  pl.BlockSpec((1, D), lambda i: (0, 0)),
                  row(D)],
        out_specs=row(D),
        out_shape=jax.ShapeDtypeStruct((T, D), F32),
        compiler_params=_params(("parallel",)),
        name="branch_merge",
    )(yh, yr, ya, proj2d, proj2d, wb, wo, g.reshape(1, D), x)


def _xattn_body(x_ref, kv_ref, gpre_ref, wq_ref, wo_ref, gpost_ref, o_ref):
    x = x_ref[0]
    h = _rms(x, gpre_ref[...]).astype(BF16)
    q = (_dot(h, wq_ref[...]) * (X_HD ** -0.5)).astype(BF16)
    kv = kv_ref[0]
    outs = []
    for hh in range(X_HEADS):
        sl = slice(hh * X_HD, (hh + 1) * X_HD)
        s = _dot_nt(q[:, sl], kv[:, sl])
        p = jnp.exp(s - jnp.max(s, axis=-1, keepdims=True))
        l = jnp.sum(p, axis=-1, keepdims=True)
        vh = kv[:, D_MODEL + hh * X_HD:D_MODEL + (hh + 1) * X_HD]
        outs.append(_dot(p.astype(BF16), vh) * (1.0 / l))
    o = jnp.concatenate(outs, axis=1).astype(BF16)
    y = _dot(o, wo_ref[...])
    o_ref[0] = x + _rms(y, gpost_ref[...])


def _cross_attention(x, kv, gpre, wq, wo, gpost, *, tm):
    B, L, D = x.shape
    M = kv.shape[1]
    vec = pl.BlockSpec((1, D), lambda b, i: (0, 0))
    mat = pl.BlockSpec((D, D), lambda b, i: (0, 0))
    return pl.pallas_call(
        _xattn_body,
        grid=(B, L // tm),
        in_specs=[pl.BlockSpec((1, tm, D), lambda b, i: (b, i, 0)),
                  pl.BlockSpec((1, M, 2 * D), lambda b, i: (b, 0, 0)),
                  vec, mat, mat, vec],
        out_specs=pl.BlockSpec((1, tm, D), lambda b, i: (b, i, 0)),
        out_shape=jax.ShapeDtypeStruct((B, L, D), F32),
        compiler_params=_params(("parallel", "parallel")),
        name="cross_attention",
    )(x, kv, gpre.reshape(1, D), wq, wo, gpost.reshape(1, D))


def _rotary_tables(pos, dim, theta):
    inv = theta ** (-jnp.arange(0, dim, 2, dtype=F32) / dim)
    ang = pos.astype(F32)[:, None] * inv[None, :]
    return jnp.cos(ang), jnp.sin(ang)


def _position_tables(L):
    pos = jnp.arange(L)
    rows = L // GRID_W
    row_ids = jnp.repeat(jnp.arange(rows), GRID_W)
    col_ids = jnp.tile(jnp.arange(GRID_W), rows)
    rcos, rsin = _rotary_tables(pos, RET_DK, RET_THETA)
    rc, rs = _rotary_tables(row_ids, ATT_HD // 2, ROPE_THETA)
    cc, cs = _rotary_tables(col_ids, ATT_HD // 2, ROPE_THETA)
    return dict(
        ret_cos=jnp.concatenate([rcos, rcos], axis=-1),
        ret_sin=jnp.concatenate([-rsin, rsin], axis=-1),
        att_cos=jnp.concatenate([rc, rc, cc, cc], axis=-1),
        att_sin=jnp.concatenate([-rs, rs, -cs, cs], axis=-1),
        feats=_hyena_features(L),
        dft=_dft_tables(L),
    )


def _trunk(x, mem, w, pt):
    B, L, D = x.shape
    M = mem.shape[1]
    T = B * L
    x2 = x.reshape(T, D)
    mem2 = mem.reshape(B * M, D)
    for l in range(DEPTH):
        proj = _norm_matmul(x2, w["g_mix_pre"][l], w["w_in"][l], tm=512, tn=1280)
        proj3 = proj.reshape(B, L, IN_COLS)
        kspec = _hyena_spectrum(pt["feats"], w["hy_fw1"][l], w["hy_fb1"][l], w["hy_fw2"][l],
                                w["hy_fb2"][l], w["hy_fw3"][l], L, pt["dft"])
        y_h = _hyena_mixer(proj3, w["hy_conv"][l], kspec, w["hy_bias"][l], pt["dft"])
        y_r = _retention_mixer(proj3, w["ret_lg"][l], pt["ret_cos"], pt["ret_sin"])
        y_a = _attention_mixer(proj3, w["att_qnorm"][l], w["att_knorm"][l],
                               pt["att_cos"], pt["att_sin"], tq=512)
        x2 = _merge(y_h, y_r, y_a, proj, w["w_branch"][l], w["w_out"][l], w["g_mix_post"][l], x2, tm=512)
        kv = _norm_matmul(mem2, w["g_mem"][l], w["w_xkv"][l], tm=512, tn=1024)
        x3 = _cross_attention(x2.reshape(B, L, D), kv.reshape(B, M, 2 * D), w["g_x_pre"][l],
                              w["w_xq"][l], w["w_xo"][l], w["g_x_post"][l], tm=512)
        x2 = x3.reshape(T, D)
        u = _norm_matmul(x2, w["g_ff_pre"][l], w["w_ff1"][l], tm=512, tn=1024, act=True)
        x2 = _matmul_norm_residual(u, w["w_ff2"][l], w["g_ff_post"][l], x2, tm=512)
    return x2.reshape(B, L, D)


def kernel(x_prompt, x_sample, mem_prompt, mem_sample, g_mix_pre, g_mix_post, w_in, hy_conv, hy_fw1, hy_fb1,
           hy_fw2, hy_fb2, hy_fw3, hy_bias, ret_decay, att_qnorm, att_knorm, w_branch, w_out, g_x_pre, g_x_post,
           g_mem, w_xq, w_xkv, w_xo, g_ff_pre, g_ff_post, w_ff1, w_ff2):
    w = dict(
        g_mix_pre=g_mix_pre, g_mix_post=g_mix_post, w_in=w_in.astype(BF16), hy_conv=hy_conv,
        hy_fw1=hy_fw1, hy_fb1=hy_fb1, hy_fw2=hy_fw2, hy_fb2=hy_fb2, hy_fw3=hy_fw3, hy_bias=hy_bias,
        ret_lg=jax.nn.log_sigmoid(ret_decay.astype(F32)), att_qnorm=att_qnorm, att_knorm=att_knorm,
        w_branch=w_branch.astype(BF16), w_out=w_out.astype(BF16), g_x_pre=g_x_pre, g_x_post=g_x_post,
        g_mem=g_mem, w_xq=w_xq.astype(BF16), w_xkv=w_xkv.astype(BF16), w_xo=w_xo.astype(BF16),
        g_ff_pre=g_ff_pre, g_ff_post=g_ff_post, w_ff1=w_ff1.astype(BF16), w_ff2=w_ff2.astype(BF16),
    )
    y_prompt = _trunk(x_prompt, mem_prompt, w, _position_tables(x_prompt.shape[1]))
    y_sample = _trunk(x_sample, mem_sample, w, _position_tables(x_sample.shape[1]))
    return (y_prompt, y_sample)
```

```python
import functools
import math

import jax
import jax.numpy as jnp
from jax import lax
from jax.experimental import pallas as pl
from jax.experimental.pallas import tpu as pltpu

F32 = jnp.float32
BF16 = jnp.bfloat16

D_MODEL = 1024
DEPTH = 4
GRID_W = 64
EPS = 1e-6
HY_WIDTH = 512
HY_SHORT = 3
HY_EMB = 33
HY_FH = 64
HY_SIN_FREQ = 1.0
HY_TARGET = 1e-2
HY_FAST = 0.3
HY_SLOW = 1.5
HY_SHIFT = 0.0
RET_HEADS = 4
RET_DK = 128
RET_DV = 128
RET_WIDTH = RET_HEADS * RET_DV
RET_THETA = 10000.0
ATT_HEADS = 4
ATT_KV_HEADS = 2
ATT_HD = 128
ATT_WIDTH = ATT_HEADS * ATT_HD
ROPE_THETA = 10000.0
X_HEADS = 4
X_HD = D_MODEL // X_HEADS
D_FF = 4 * D_MODEL
N_BRANCH = 3
MIX_WIDTH = HY_WIDTH + RET_WIDTH + ATT_WIDTH

COL_HY = 0
COL_RQ = 3 * HY_WIDTH
COL_RK = COL_RQ + RET_HEADS * RET_DK
COL_RV = COL_RK + RET_HEADS * RET_DK
COL_RG = COL_RV + RET_WIDTH
COL_AQ = COL_RG + RET_WIDTH
COL_AK = COL_AQ + ATT_WIDTH
COL_AV = COL_AK + ATT_KV_HEADS * ATT_HD
COL_GATE = COL_AV + ATT_KV_HEADS * ATT_HD
IN_COLS = COL_GATE + N_BRANCH * D_MODEL

LANES = 128
VMEM_LIMIT = 56 * 1024 * 1024
DFT_N2 = 128
DFT_NB = 16
DFT_KB = 8
DFT_PASSES = dict(s1=1, s2f=1, s2i=1, s3=1, k1=1, k2=1)
RET_BLOCK = 256
RET_UNROLL = 8
ATT_KEY_CHUNK = 1024
LOG2E = 1.4426950408889634


def _params(sem):
    return pltpu.CompilerParams(dimension_semantics=sem, vmem_limit_bytes=VMEM_LIMIT)


def _dot(a, b):
    return jnp.dot(a, b, preferred_element_type=F32)


def _dot_nt(a, b):
    return lax.dot_general(a, b, (((1,), (1,)), ((), ())), preferred_element_type=F32)


def _dot_tn(a, b):
    return lax.dot_general(a, b, (((0,), (0,)), ((), ())), preferred_element_type=F32)


def _rms(x, g):
    ms = jnp.mean(x * x, axis=-1, keepdims=True)
    return x * lax.rsqrt(ms + EPS) * g


def _sigmoid(x):
    return 1.0 / (1.0 + jnp.exp(-x))


def _split(x):
    hi = x.astype(BF16)
    lo = (x - hi.astype(F32)).astype(BF16)
    return hi, lo


def _dot3(a, b):
    return _dot(a[0], b[0]) + (_dot(a[0], b[1]) + _dot(a[1], b[0]))


def _mm(a, b, passes):
    if passes == 1:
        return _dot(a.astype(BF16), b.astype(BF16))
    return _dot3(_split(a.astype(F32)), _split(b.astype(F32)))


def _nm_body(x_ref, g_ref, w_ref, o_ref, *, tn, act):
    h = _rms(x_ref[...], g_ref[...]).astype(BF16)
    for j in range(w_ref.shape[1] // tn):
        y = _dot(h, w_ref[:, j * tn:(j + 1) * tn])
        if act:
            y = jnp.square(jnp.maximum(y, 0.0))
        o_ref[:, j * tn:(j + 1) * tn] = y.astype(o_ref.dtype)


def _norm_matmul(x, g, w, *, tm, tn, act=False):
    T, D = x.shape
    N = w.shape[1]
    return pl.pallas_call(
        functools.partial(_nm_body, tn=tn, act=act),
        grid=(T // tm,),
        in_specs=[pl.BlockSpec((tm, D), lambda i: (i, 0)),
                  pl.BlockSpec((1, D), lambda i: (0, 0)),
                  pl.BlockSpec((D, N), lambda i: (0, 0), pipeline_mode=pl.Buffered(1))],
        out_specs=pl.BlockSpec((tm, N), lambda i: (i, 0)),
        out_shape=jax.ShapeDtypeStruct((T, N), BF16),
        compiler_params=_params(("parallel",)),
        name="norm_matmul",
    )(x, g.reshape(1, D), w)


def _rotate_half(x, cos, sin):
    return x * cos + pltpu.roll(x, x.shape[-1] // 2, 1) * sin


def _axial_rope(x, gain, cos, sin, scale):
    x = _rms(x, gain)
    lane = lax.broadcasted_iota(jnp.int32, x.shape, 1)
    lower = (lane % (ATT_HD // 2)) < (ATT_HD // 4)
    swapped = jnp.where(lower, pltpu.roll(x, LANES - ATT_HD // 4, 1), pltpu.roll(x, ATT_HD // 4, 1))
    return (x * cos + swapped * sin) * scale


_WIN_SEGMENTS = (
    [(COL_HY + i * HY_WIDTH, HY_WIDTH, None) for i in range(3)]
    + [(COL_RQ, RET_WIDTH, "ret_q"), (COL_RK, RET_WIDTH, "ret_k"), (COL_RV, RET_WIDTH, None),
       (COL_RG, RET_WIDTH, None), (COL_AQ, ATT_WIDTH, "att_q"), (COL_AK, COL_GATE - COL_AK, "att_kv")]
    + [(COL_GATE + i * D_MODEL, D_MODEL, None) for i in range(N_BRANCH)])


def _win_body(x_ref, g_ref, w_ref, qn_ref, kn_ref, rc_ref, rs_ref, ac_ref, as_ref, o_ref):
    h = _rms(x_ref[...], g_ref[...]).astype(BF16)
    for start, width, kind in _WIN_SEGMENTS:
        y = _dot(h, w_ref[:, start:start + width])
        if kind is None:
            o_ref[:, start:start + width] = y.astype(o_ref.dtype)
            continue
        for hh in range(width // LANES):
            yh = y[:, hh * LANES:(hh + 1) * LANES]
            if kind == "ret_q":
                yh = _rotate_half(yh, rc_ref[...], rs_ref[...])
            elif kind == "ret_k":
                yh = _rotate_half(yh, rc_ref[...], rs_ref[...]) * (RET_DK ** -0.5)
            elif kind == "att_q":
                yh = _axial_rope(yh, qn_ref[...], ac_ref[...], as_ref[...], ATT_HD ** -0.5 * LOG2E)
            elif hh < ATT_KV_HEADS:
                yh = _axial_rope(yh, kn_ref[...], ac_ref[...], as_ref[...], 1.0)
            o_ref[:, start + hh * LANES:start + (hh + 1) * LANES] = yh.astype(o_ref.dtype)


def _input_projection(x, g, w, qn, kn, pt, L, *, tm):
    T, D = x.shape
    N = w.shape[1]
    per_seq = L // tm
    vec = lambda n: pl.BlockSpec((1, n), lambda i: (0, 0))
    tab = pl.BlockSpec((tm, LANES), lambda i: (i % per_seq, 0))
    return pl.pallas_call(
        _win_body,
        grid=(T // tm,),
        in_specs=[pl.BlockSpec((tm, D), lambda i: (i, 0)), vec(D),
                  pl.BlockSpec((D, N), lambda i: (0, 0), pipeline_mode=pl.Buffered(1)),
                  vec(ATT_HD), vec(ATT_HD), tab, tab, tab, tab],
        out_specs=pl.BlockSpec((tm, N), lambda i: (i, 0)),
        out_shape=jax.ShapeDtypeStruct((T, N), BF16),
        compiler_params=_params(("parallel",)),
        name="input_projection",
    )(x, g.reshape(1, D), w, qn.reshape(1, ATT_HD), kn.reshape(1, ATT_HD),
      pt["ret_cos"], pt["ret_sin"], pt["att_cos"], pt["att_sin"])


def _mnr_body(a_ref, w_ref, g_ref, x_ref, o_ref):
    y = _dot(a_ref[...], w_ref[...])
    o_ref[...] = x_ref[...] + _rms(y, g_ref[...])


def _matmul_norm_residual(a, w, g, x, *, tm):
    T, K = a.shape
    D = w.shape[1]
    return pl.pallas_call(
        _mnr_body,
        grid=(T // tm,),
        in_specs=[pl.BlockSpec((tm, K), lambda i: (i, 0)),
                  pl.BlockSpec((K, D), lambda i: (0, 0)),
                  pl.BlockSpec((1, D), lambda i: (0, 0)),
                  pl.BlockSpec((tm, D), lambda i: (i, 0))],
        out_specs=pl.BlockSpec((tm, D), lambda i: (i, 0)),
        out_shape=jax.ShapeDtypeStruct((T, D), F32),
        compiler_params=_params(("parallel",)),
        name="matmul_norm_residual",
    )(a, w, g.reshape(1, D), x)


def _hyprep_body(x0_ref, x1_ref, v_ref, w_ref, zf_ref, x0c_ref):
    L = x0_ref.shape[1]
    row = lax.broadcasted_iota(jnp.int32, (L, LANES), 0)

    def conv(p_ref, grp):
        p = p_ref[0].astype(F32)
        w = w_ref[:, grp, :]
        prev = jnp.where(row == 0, 0.0, pltpu.roll(p, 1, 0))
        nxt = jnp.where(row == L - 1, 0.0, pltpu.roll(p, L - 1, 0))
        return prev * w[0:1] + p * w[1:2] + nxt * w[2:3]

    x0c_ref[0] = conv(x0_ref, 0).astype(x0c_ref.dtype)
    zf_ref[0] = (conv(v_ref, 2) * conv(x1_ref, 1)).astype(zf_ref.dtype)


def _hyena_prep(proj, conv_w):
    B, L, _ = proj.shape
    nb = HY_WIDTH // LANES
    w = conv_w.reshape(HY_SHORT, 3, HY_WIDTH)
    blk = lambda off: pl.BlockSpec((1, L, LANES), lambda b, c, off=off: (b, 0, off + c))
    out = pl.BlockSpec((1, L, LANES), lambda b, c: (b, 0, c))
    return pl.pallas_call(
        _hyprep_body,
        grid=(B, nb),
        in_specs=[blk(0), blk(nb), blk(2 * nb),
                  pl.BlockSpec((HY_SHORT, 3, LANES), lambda b, c: (0, 0, c))],
        out_specs=[out, out],
        out_shape=[jax.ShapeDtypeStruct((B, L, HY_WIDTH), BF16)] * 2,
        compiler_params=_params(("parallel", "parallel")),
        name="hyena_prep",
    )(proj, proj, proj, w)


def _filt_body(f_ref, w1_ref, b1_ref, w2_ref, b2_ref, w3_ref, dl_ref, o_ref, *, L):
    tl = f_ref.shape[0]
    hp = functools.partial(jnp.dot, preferred_element_type=F32, precision=lax.Precision.HIGHEST)
    f = f_ref[...]
    h = jnp.sin(HY_SIN_FREQ * (hp(f, w1_ref[...]) + b1_ref[...]))
    h = jnp.sin(HY_SIN_FREQ * (hp(h, w2_ref[...]) + b2_ref[...]))
    filt = hp(h, w3_ref[...])
    mod = jnp.exp(-f[:, 0:1] * dl_ref[...]) + HY_SHIFT
    row = pl.program_id(0) * tl + lax.broadcasted_iota(jnp.int32, (tl, HY_WIDTH), 0)
    sel = jnp.where(row < L, filt[:, :HY_WIDTH], filt[:, HY_WIDTH:])
    o_ref[...] = jnp.where(row == L, 0.0, sel * mod)


def _hyena_features(L):
    t = jnp.linspace(0.0, 1.0, L, dtype=F32)[:, None]
    bands = (HY_EMB - 1) // 2
    w = 2.0 * math.pi * jnp.arange(L, dtype=F32)[:, None] / L
    f = jnp.linspace(1e-4, bands - 1, bands, dtype=F32)[None, :]
    z = jnp.concatenate([t, jnp.cos(f * w), -jnp.sin(f * w)], axis=-1)
    zc = jnp.concatenate([z, z[:1], jnp.flip(z[1:], axis=0)], axis=0)
    return jnp.pad(zc, ((0, 0), (0, LANES - HY_EMB)))


def _hyena_filter(feats, w1, b1, w2, b2, w3, L):
    tl = 512
    pad = LANES - HY_FH
    w1p = jnp.pad(w1, ((0, LANES - HY_EMB), (0, pad)))
    b1p = jnp.pad(b1, (0, pad)).reshape(1, LANES)
    w2p = jnp.pad(w2, ((0, pad), (0, pad)))
    b2p = jnp.pad(b2, (0, pad)).reshape(1, LANES)
    w3p = jnp.pad(w3, ((0, pad), (0, 0)))
    min_decay = math.log(HY_TARGET) / HY_SLOW
    max_decay = math.log(HY_TARGET) / HY_FAST
    deltas = jnp.abs(jnp.linspace(min_decay, max_decay, HY_WIDTH, dtype=F32)).reshape(1, HY_WIDTH)
    full = lambda shape: pl.BlockSpec(shape, lambda i: (0,) * len(shape))
    return pl.pallas_call(
        functools.partial(_filt_body, L=L),
        grid=(2 * L // tl,),
        in_specs=[pl.BlockSpec((tl, LANES), lambda i: (i, 0)),
                  full((LANES, LANES)), full((1, LANES)), full((LANES, LANES)), full((1, LANES)),
                  full((LANES, 2 * HY_WIDTH)), full((1, HY_WIDTH))],
        out_specs=pl.BlockSpec((tl, HY_WIDTH), lambda i: (i, 0)),
        out_shape=jax.ShapeDtypeStruct((2 * L, HY_WIDTH), F32),
        compiler_params=_params(("parallel",)),
        name="hyena_filter",
    )(feats, w1p, b1p, w2p, b2p, w3p, deltas)


def _expanded(fn, rows, cols):
    ri = lax.broadcasted_iota(jnp.int32, (rows * DFT_NB, cols * DFT_NB), 0)
    ci = lax.broadcasted_iota(jnp.int32, (rows * DFT_NB, cols * DFT_NB), 1)
    return jnp.where(ri % DFT_NB == ci % DFT_NB, fn(ri // DFT_NB, ci // DFT_NB), 0.0)


def _dft_tables(L):
    N = 2 * L
    N2 = DFT_N2
    N1 = N // N2
    N1h = N1 // 2

    def cs1(k1, n1):
        ang = ((k1 * n1) % N1).astype(F32) * (2.0 * math.pi / N1)
        return jnp.cos(ang), jnp.sin(ang)

    def f1_pair(r, c):
        cs, sn = cs1(r // 2, c % N1h)
        re_row, first = (r % 2) == 0, c < N1h
        return jnp.where(re_row, jnp.where(first, cs, sn), jnp.where(first, -sn, cs))

    def f1_real(r, c):
        cs, sn = cs1(r // 2, c)
        return jnp.where((r % 2) == 0, cs, -sn)

    def f3_pair(r, c):
        cs, sn = cs1(c // 2, r % N1h)
        re_out, re_in = r < N1h, (c % 2) == 0
        return jnp.where(re_out, jnp.where(re_in, cs, -sn), jnp.where(re_in, sn, cs)) * (1.0 / N)

    kk = (jnp.arange(N1, dtype=jnp.int32)[:, None, None]
          + N1 * jnp.arange(N2, dtype=jnp.int32)[None, :, None])
    n2 = jnp.arange(N2, dtype=jnp.int32)[None, None, :]
    a2 = ((kk * n2) % N).astype(F32) * (2.0 * math.pi / N)
    gr, gi = jnp.cos(a2), -jnp.sin(a2)
    gtr, gti = jnp.swapaxes(gr, 1, 2), jnp.swapaxes(gi, 1, 2)
    g_fwd = jnp.concatenate([jnp.concatenate([gr, -gi], axis=2),
                             jnp.concatenate([gi, gr], axis=2)], axis=1)
    g_inv = jnp.concatenate([jnp.concatenate([gtr, gti], axis=2),
                             jnp.concatenate([-gti, gtr], axis=2)], axis=1)
    return dict(N1=N1, N1h=N1h, N2=N2,
                f1_pair=_expanded(f1_pair, 2 * N1, 2 * N1h), f1_real=_expanded(f1_real, 2 * N1, N1),
                f3_pair=_expanded(f3_pair, 2 * N1h, 2 * N1), g_fwd=g_fwd, g_inv=g_inv)


def _stage_dtype(passes):
    return BF16 if passes == 1 else F32


def _dft1_body(x_ref, f_ref, o_ref, *, passes):
    P, K, NB, C = x_ref.shape
    y = _mm(f_ref[...], x_ref[...].reshape(P * K * NB, C), passes)
    o_ref[0] = y.astype(o_ref.dtype).reshape(o_ref.shape[1:])


def _dft_stage1(x, f, *, P, passes, out_dtype):
    Bx, K, N2, C = x.shape
    N1 = f.shape[0] // (2 * DFT_NB)
    return pl.pallas_call(
        functools.partial(_dft1_body, passes=passes),
        grid=(Bx // P, N2 // DFT_NB),
        in_specs=[pl.BlockSpec((P, K, DFT_NB, C), lambda b, j: (b, 0, j, 0)),
                  pl.BlockSpec(f.shape, lambda b, j: (0, 0), pipeline_mode=pl.Buffered(1))],
        out_specs=pl.BlockSpec((1, N1, 2, DFT_NB, C), lambda b, j: (b, 0, 0, j, 0)),
        out_shape=jax.ShapeDtypeStruct((Bx // P, N1, 2, N2, C), out_dtype),
        compiler_params=_params(("parallel", "parallel")),
        name="dft_stage1",
    )(x, f)


def _spec_body(a_ref, g_ref, o_ref, *, passes):
    def step(k, carry):
        o_ref[k] = _mm(g_ref[k], a_ref[0, k], passes)
        return carry

    lax.fori_loop(0, a_ref.shape[1], step, 0)


def _dft_spectrum(a, g, *, kb, passes):
    _, N1, R, C = a.shape
    return pl.pallas_call(
        functools.partial(_spec_body, passes=passes),
        grid=(N1 // kb,),
        in_specs=[pl.BlockSpec((1, kb, R, C), lambda i: (0, i, 0, 0)),
                  pl.BlockSpec((kb, R, R), lambda i: (i, 0, 0))],
        out_specs=pl.BlockSpec((kb, R, C), lambda i: (i, 0, 0)),
        out_shape=jax.ShapeDtypeStruct((N1, R, C), F32),
        compiler_params=_params(("parallel",)),
        name="dft_spectrum",
    )(a, g)


def _dft2_body(a_ref, gf_ref, gi_ref, ks_ref, o_ref, *, passes_fwd, passes_inv):
    H = a_ref.shape[2] // 2

    def step(k, carry):
        y = _mm(gf_ref[k], a_ref[0, k], passes_fwd)
        yr, yi = y[:H], y[H:]
        kr, ki = ks_ref[k, :H, :], ks_ref[k, H:, :]
        z = jnp.concatenate([yr * kr - yi * ki, yr * ki + yi * kr], axis=0)
        o_ref[0, k] = _mm(gi_ref[k], z, passes_inv).astype(o_ref.dtype)
        return carry

    lax.fori_loop(0, a_ref.shape[1], step, 0)


def _dft_stage2(a, gf, gi, ks, *, kb, passes_fwd, passes_inv, out_dtype):
    Bp, N1, R, C = a.shape
    dat = pl.BlockSpec((1, kb, R, C), lambda i, b: (b, i, 0, 0))
    mat = pl.BlockSpec((kb, R, R), lambda i, b: (i, 0, 0))
    return pl.pallas_call(
        functools.partial(_dft2_body, passes_fwd=passes_fwd, passes_inv=passes_inv),
        grid=(N1 // kb, Bp),
        in_specs=[dat, mat, mat, pl.BlockSpec((kb, R, C), lambda i, b: (i, 0, 0))],
        out_specs=dat,
        out_shape=jax.ShapeDtypeStruct((Bp, N1, R, C), out_dtype),
        compiler_params=_params(("parallel", "parallel")),
        name="dft_stage2",
    )(a, gf, gi, ks)


def _dft3_body(t_ref, f_ref, zf_ref, x0_ref, bias_ref, o_ref, *, passes):
    _, N1, _, NB, C = t_ref.shape
    y = _mm(f_ref[...], t_ref[0].reshape(N1 * 2 * NB, C), passes).reshape(zf_ref.shape)
    o_ref[...] = ((y + zf_ref[...].astype(F32) * bias_ref[...]) * x0_ref[...].astype(F32)).astype(o_ref.dtype)


def _dft_stage3(t, f, zf, x0, bias, *, passes):
    Bp, N1, _, N2, C = t.shape
    B, N1h = zf.shape[:2]
    pair = pl.BlockSpec((2, N1h, DFT_NB, C), lambda b, j: (b, 0, j, 0))
    return pl.pallas_call(
        functools.partial(_dft3_body, passes=passes),
        grid=(Bp, N2 // DFT_NB),
        in_specs=[pl.BlockSpec((1, N1, 2, DFT_NB, C), lambda b, j: (b, 0, 0, j, 0)),
                  pl.BlockSpec(f.shape, lambda b, j: (0, 0), pipeline_mode=pl.Buffered(1)),
                  pair, pair,
                  pl.BlockSpec((1, C), lambda b, j: (0, 0))],
        out_specs=pair,
        out_shape=jax.ShapeDtypeStruct((B, N1h, N2, C), BF16),
        compiler_params=_params(("parallel", "parallel")),
        name="dft_stage3",
    )(t, f, zf, x0, bias)


def _hyena_mixer(proj, conv_w, kspec, bias, tabs):
    B, L, _ = proj.shape
    N1, N1h, N2 = tabs["N1"], tabs["N1h"], tabs["N2"]
    C = HY_WIDTH
    ps = DFT_PASSES
    zf, x0c = _hyena_prep(proj, conv_w)
    z4 = zf.reshape(B, N1h, N2, C)
    f1 = tabs["f1_pair"].astype(_stage_dtype(ps["s1"]))
    a = _dft_stage1(z4, f1, P=2, passes=ps["s1"], out_dtype=_stage_dtype(ps["s2f"]))
    gdt = _stage_dtype(min(ps["s2f"], ps["s2i"]))
    t = _dft_stage2(a.reshape(B // 2, N1, 2 * N2, C), tabs["g_fwd"].astype(gdt), tabs["g_inv"].astype(gdt),
                    kspec, kb=DFT_KB, passes_fwd=ps["s2f"], passes_inv=ps["s2i"],
                    out_dtype=_stage_dtype(ps["s3"]))
    f3 = tabs["f3_pair"].astype(_stage_dtype(ps["s3"]))
    y = _dft_stage3(t.reshape(B // 2, N1, 2, N2, C), f3, z4, x0c.reshape(B, N1h, N2, C),
                    bias.astype(F32).reshape(1, C), passes=ps["s3"])
    return y.reshape(B * L, C)


def _hyena_spectrum(feats, w1, b1, w2, b2, w3, L, tabs):
    N1, N2 = tabs["N1"], tabs["N2"]
    C = HY_WIDTH
    ps = DFT_PASSES
    kfull = _hyena_filter(feats, w1, b1, w2, b2, w3, L)
    a = _dft_stage1(kfull.reshape(1, N1, N2, C), tabs["f1_real"].astype(_stage_dtype(ps["k1"])), P=1, passes=ps["k1"],
                    out_dtype=_stage_dtype(ps["k2"]))
    return _dft_spectrum(a.reshape(1, N1, 2 * N2, C), tabs["g_fwd"].astype(_stage_dtype(ps["k2"])), kb=DFT_KB,
                         passes=ps["k2"])


def _ret_body(lg_ref, q_ref, k_ref, v_ref, g_ref, o_ref, st_ref):
    C = RET_BLOCK
    L = q_ref.shape[1]
    nblk = L // C
    dk = RET_DK
    h = pl.program_id(1)
    lgf = lg_ref[0, h]
    lgb = lg_ref[1, h]
    ii = lax.broadcasted_iota(jnp.int32, (C, C), 0)
    jj = lax.broadcasted_iota(jnp.int32, (C, C), 1)
    diff = (ii - jj).astype(F32)
    dm = jnp.where(diff >= 0, jnp.exp(jnp.maximum(diff, 0.0) * lgf), jnp.exp(jnp.maximum(-diff, 0.0) * lgb))
    pos = lax.broadcasted_iota(jnp.int32, (C, dk), 0).astype(F32)
    qw_f = jnp.exp((pos + 1.0) * lgf)
    kw_f = jnp.exp((C - 1.0 - pos) * lgf)
    qw_b = jnp.exp((C - pos) * lgb)
    kw_b = jnp.exp(pos * lgb)
    cd_f = jnp.exp(C * lgf)
    cd_b = jnp.exp(C * lgb)

    def block_kv(n, carry):
        sl = pl.ds(pl.multiple_of(n * C, C), C)
        k = k_ref[0, sl, :].astype(F32)
        kk = jnp.concatenate([k * kw_f, k * kw_b], axis=1).astype(BF16)
        st_ref[n] = _dot_tn(kk, v_ref[0, sl, :].astype(BF16))
        return carry

    lax.fori_loop(0, nblk, block_kv, 0, unroll=RET_UNROLL)

    def scan_f(n, S):
        kv = st_ref[n, :dk, :]
        st_ref[n, :dk, :] = S
        return S * cd_f + kv

    lax.fori_loop(0, nblk, scan_f, jnp.zeros((dk, RET_DV), F32))

    def scan_b(m, S):
        n = nblk - 1 - m
        kv = st_ref[n, dk:, :]
        st_ref[n, dk:, :] = S
        return S * cd_b + kv

    lax.fori_loop(0, nblk, scan_b, jnp.zeros((dk, RET_DV), F32))

    def block_out(n, carry):
        sl = pl.ds(pl.multiple_of(n * C, C), C)
        q = q_ref[0, sl, :].astype(BF16)
        scores = _dot_nt(q, k_ref[0, sl, :].astype(BF16)) * dm
        qf = q.astype(F32)
        qq = jnp.concatenate([qf * qw_f, qf * qw_b], axis=1).astype(BF16)
        o = _dot(scores.astype(BF16), v_ref[0, sl, :].astype(BF16)) + _dot(qq, st_ref[n].astype(BF16))
        mu = jnp.mean(o, axis=-1, keepdims=True)
        oc = o - mu
        on = oc * lax.rsqrt(jnp.mean(oc * oc, axis=-1, keepdims=True) + EPS)
        g = g_ref[0, sl, :].astype(F32)
        o_ref[0, sl, :] = (g * _sigmoid(g) * on).astype(o_ref.dtype)
        return carry

    lax.fori_loop(0, nblk, block_out, 0, unroll=RET_UNROLL)


def _retention_mixer(proj, log_gamma):
    B, L, _ = proj.shape
    blk = lambda col: pl.BlockSpec((1, L, LANES), lambda b, h, c=col // LANES: (b, 0, c + h))
    y = pl.pallas_call(
        _ret_body,
        grid=(B, RET_HEADS),
        in_specs=[pl.BlockSpec(memory_space=pltpu.SMEM),
                  blk(COL_RQ), blk(COL_RK), blk(COL_RV), blk(COL_RG)],
        out_specs=pl.BlockSpec((1, L, LANES), lambda b, h: (b, 0, h)),
        out_shape=jax.ShapeDtypeStruct((B, L, RET_WIDTH), BF16),
        scratch_shapes=[pltpu.VMEM((L // RET_BLOCK, 2 * RET_DK, RET_DV), F32)],
        compiler_params=_params(("parallel", "parallel")),
        name="retention",
    )(log_gamma, proj, proj, proj, proj)
    return y.reshape(B * L, RET_WIDTH)


def _att_body(q_ref, k_ref, v_ref, o_ref):
    G = ATT_HEADS // ATT_KV_HEADS
    tq = q_ref.shape[1]
    kc = min(ATT_KEY_CHUNK, k_ref.shape[1])
    nchunk = k_ref.shape[1] // kc
    q = q_ref[0].astype(BF16)
    qs = jnp.concatenate([q[:, g * ATT_HD:(g + 1) * ATT_HD] for g in range(G)], axis=0)
    m = jnp.full((G * tq, 1), -jnp.inf, F32)
    l = jnp.zeros((G * tq, 1), F32)
    acc = jnp.zeros((G * tq, ATT_HD), F32)
    s_next = _dot_nt(qs, k_ref[0, 0:kc, :].astype(BF16))
    for c in range(nchunk):
        s = s_next
        if c + 1 < nchunk:
            s_next = _dot_nt(qs, k_ref[0, (c + 1) * kc:(c + 2) * kc, :].astype(BF16))
        m_new = jnp.maximum(m, jnp.max(s, axis=-1, keepdims=True))
        alpha = jnp.exp2(m - m_new)
        p = jnp.exp2(s - m_new)
        l = alpha * l + jnp.sum(p, axis=-1, keepdims=True)
        acc = alpha * acc + _dot(p.astype(BF16), v_ref[0, c * kc:(c + 1) * kc, :].astype(BF16))
        m = m_new
    o = acc * (1.0 / l)
    for g in range(G):
        o_ref[0, :, g * ATT_HD:(g + 1) * ATT_HD] = o[g * tq:(g + 1) * tq].astype(o_ref.dtype)


def _attention_mixer(proj, *, tq):
    B, L, _ = proj.shape
    G = ATT_HEADS // ATT_KV_HEADS
    y = pl.pallas_call(
        _att_body,
        grid=(B, ATT_KV_HEADS, L // tq),
        in_specs=[pl.BlockSpec((1, tq, G * ATT_HD), lambda b, h, i: (b, i, COL_AQ // (G * ATT_HD) + h)),
                  pl.BlockSpec((1, L, ATT_HD), lambda b, h, i: (b, 0, COL_AK // ATT_HD + h)),
                  pl.BlockSpec((1, L, ATT_HD), lambda b, h, i: (b, 0, COL_AV // ATT_HD + h))],
        out_specs=pl.BlockSpec((1, tq, G * ATT_HD), lambda b, h, i: (b, i, h)),
        out_shape=jax.ShapeDtypeStruct((B, L, ATT_WIDTH), BF16),
        compiler_params=_params(("parallel", "parallel", "arbitrary")),
        name="attention",
    )(proj, proj, proj)
    return y.reshape(B * L, ATT_WIDTH)


def _merge_body(yh_ref, yr_ref, ya_ref, ga_ref, gb_ref, wb_ref, wo_ref, g_ref, x_ref, o_ref):
    D = D_MODEL
    ph = _dot(yh_ref[...], wb_ref[0:HY_WIDTH, :])
    pr = _dot(yr_ref[...], wb_ref[HY_WIDTH:HY_WIDTH + RET_WIDTH, :])
    pa = _dot(ya_ref[...], wb_ref[HY_WIDTH + RET_WIDTH:, :])
    ga = ga_ref[...].astype(F32)
    gb = gb_ref[...].astype(F32)
    hd = D // 2
    g0 = _sigmoid(ga[:, :D])
    g1 = _sigmoid(jnp.concatenate([ga[:, D:], gb[:, :hd]], axis=1))
    g2 = _sigmoid(gb[:, hd:])
    merged = g0 * ph + g1 * pr + g2 * pa
    y = _dot(merged.astype(BF16), wo_ref[...])
    o_ref[...] = x_ref[...] + _rms(y, g_ref[...])


def _merge(yh, yr, ya, proj2d, wb, wo, g, x, *, tm):
    T = x.shape[0]
    D = D_MODEL
    gw = N_BRANCH * D // 2
    row = lambda w: pl.BlockSpec((tm, w), lambda i: (i, 0))
    return pl.pallas_call(
        _merge_body,
        grid=(T // tm,),
        in_specs=[row(HY_WIDTH), row(RET_WIDTH), row(ATT_WIDTH),
                  pl.BlockSpec((tm, gw), lambda i: (i, COL_GATE // gw)),
                  pl.BlockSpec((tm, gw), lambda i: (i, COL_GATE // gw + 1)),
                  pl.BlockSpec((MIX_WIDTH, D), lambda i: (0, 0)),
                  pl.BlockSpec((D, D), lambda i: (0, 0)),
                  pl.BlockSpec((1, D), lambda i: (0, 0)),
                  row(D)],
        out_specs=row(D),
        out_shape=jax.ShapeDtypeStruct((T, D), F32),
        compiler_params=_params(("parallel",)),
        name="branch_merge",
    )(yh, yr, ya, proj2d, proj2d, wb, wo, g.reshape(1, D), x)


def _xattn_body(x_ref, kv_ref, gpre_ref, wq_ref, wo_ref, gpost_ref, o_ref):
    x = x_ref[0]
    h = _rms(x, gpre_ref[...]).astype(BF16)
    q = (_dot(h, wq_ref[...]) * (X_HD ** -0.5)).astype(BF16)
    kv = kv_ref[0]
    outs = []
    for hh in range(X_HEADS):
        sl = slice(hh * X_HD, (hh + 1) * X_HD)
        s = _dot_nt(q[:, sl], kv[:, sl])
        p = jnp.exp(s - jnp.max(s, axis=-1, keepdims=True))
        l = jnp.sum(p, axis=-1, keepdims=True)
        vh = kv[:, D_MODEL + hh * X_HD:D_MODEL + (hh + 1) * X_HD]
        outs.append(_dot(p.astype(BF16), vh) * (1.0 / l))
    o = jnp.concatenate(outs, axis=1).astype(BF16)
    y = _dot(o, wo_ref[...])
    o_ref[0] = x + _rms(y, gpost_ref[...])


def _cross_attention(x, kv, gpre, wq, wo, gpost, *, tm):
    B, L, D = x.shape
    M = kv.shape[1]
    vec = pl.BlockSpec((1, D), lambda b, i: (0, 0))
    mat = pl.BlockSpec((D, D), lambda b, i: (0, 0))
    return pl.pallas_call(
        _xattn_body,
        grid=(B, L // tm),
        in_specs=[pl.BlockSpec((1, tm, D), lambda b, i: (b, i, 0)),
                  pl.BlockSpec((1, M, 2 * D), lambda b, i: (b, 0, 0)),
                  vec, mat, mat, vec],
        out_specs=pl.BlockSpec((1, tm, D), lambda b, i: (b, i, 0)),
        out_shape=jax.ShapeDtypeStruct((B, L, D), F32),
        compiler_params=_params(("parallel", "parallel")),
        name="cross_attention",
    )(x, kv, gpre.reshape(1, D), wq, wo, gpost.reshape(1, D))


def _rotary_tables(pos, dim, theta):
    inv = theta ** (-jnp.arange(0, dim, 2, dtype=F32) / dim)
    ang = pos.astype(F32)[:, None] * inv[None, :]
    return jnp.cos(ang), jnp.sin(ang)


def _position_tables(L):
    pos = jnp.arange(L)
    rows = L // GRID_W
    row_ids = jnp.repeat(jnp.arange(rows), GRID_W)
    col_ids = jnp.tile(jnp.arange(GRID_W), rows)
    rcos, rsin = _rotary_tables(pos, RET_DK, RET_THETA)
    rc, rs = _rotary_tables(row_ids, ATT_HD // 2, ROPE_THETA)
    cc, cs = _rotary_tables(col_ids, ATT_HD // 2, ROPE_THETA)
    return dict(
        ret_cos=jnp.concatenate([rcos, rcos], axis=-1),
        ret_sin=jnp.concatenate([-rsin, rsin], axis=-1),
        att_cos=jnp.concatenate([rc, rc, cc, cc], axis=-1),
        att_sin=jnp.concatenate([-rs, rs, -cs, cs], axis=-1),
        feats=_hyena_features(L),
        dft=_dft_tables(L),
    )


def _trunk(x, mem, w, pt):
    B, L, D = x.shape
    M = mem.shape[1]
    T = B * L
    x2 = x.reshape(T, D)
    mem2 = mem.reshape(B * M, D)
    for l in range(DEPTH):
        proj = _input_projection(x2, w["g_mix_pre"][l], w["w_in"][l], w["att_qnorm"][l], w["att_knorm"][l],
                                 pt, L, tm=512)
        proj3 = proj.reshape(B, L, IN_COLS)
        kspec = _hyena_spectrum(pt["feats"], w["hy_fw1"][l], w["hy_fb1"][l], w["hy_fw2"][l],
                                w["hy_fb2"][l], w["hy_fw3"][l], L, pt["dft"])
        y_h = _hyena_mixer(proj3, w["hy_conv"][l], kspec, w["hy_bias"][l], pt["dft"])
        y_r = _retention_mixer(proj3, w["ret_lg"][l])
        y_a = _attention_mixer(proj3, tq=512)
        x2 = _merge(y_h, y_r, y_a, proj, w["w_branch"][l], w["w_out"][l], w["g_mix_post"][l], x2, tm=512)
        kv = _norm_matmul(mem2, w["g_mem"][l], w["w_xkv"][l], tm=512, tn=1024)
        x3 = _cross_attention(x2.reshape(B, L, D), kv.reshape(B, M, 2 * D), w["g_x_pre"][l],
                              w["w_xq"][l], w["w_xo"][l], w["g_x_post"][l], tm=512)
        x2 = x3.reshape(T, D)
        u = _norm_matmul(x2, w["g_ff_pre"][l], w["w_ff1"][l], tm=512, tn=1024, act=True)
        x2 = _matmul_norm_residual(u, w["w_ff2"][l], w["g_ff_post"][l], x2, tm=512)
    return x2.reshape(B, L, D)


def kernel(x_prompt, x_sample, mem_prompt, mem_sample, g_mix_pre, g_mix_post, w_in, hy_conv, hy_fw1, hy_fb1,
           hy_fw2, hy_fb2, hy_fw3, hy_bias, ret_decay, att_qnorm, att_knorm, w_branch, w_out, g_x_pre, g_x_post,
           g_mem, w_xq, w_xkv, w_xo, g_ff_pre, g_ff_post, w_ff1, w_ff2):
    w = dict(
        g_mix_pre=g_mix_pre, g_mix_post=g_mix_post, w_in=w_in.astype(BF16), hy_conv=hy_conv,
        hy_fw1=hy_fw1, hy_fb1=hy_fb1, hy_fw2=hy_fw2, hy_fb2=hy_fb2, hy_fw3=hy_fw3, hy_bias=hy_bias,
        ret_lg=jax.nn.log_sigmoid(ret_decay.astype(F32)), att_qnorm=att_qnorm, att_knorm=att_knorm,
        w_branch=w_branch.astype(BF16), w_out=w_out.astype(BF16), g_x_pre=g_x_pre, g_x_post=g_x_post,
        g_mem=g_mem, w_xq=w_xq.astype(BF16), w_xkv=w_xkv.astype(BF16), w_xo=w_xo.astype(BF16),
        g_ff_pre=g_ff_pre, g_ff_post=g_ff_post, w_ff1=w_ff1.astype(BF16), w_ff2=w_ff2.astype(BF16),
    )
    y_prompt = _trunk(x_prompt, mem_prompt, w, _position_tables(x_prompt.shape[1]))
    y_sample = _trunk(x_sample, mem_sample, w, _position_tables(x_sample.shape[1]))
    return (y_prompt, y_sample)
```

```python
import functools
import math

import jax
import jax.numpy as jnp
from jax import lax
from jax.experimental import pallas as pl
from jax.experimental.pallas import tpu as pltpu

F32 = jnp.float32
BF16 = jnp.bfloat16

D_MODEL = 1024
DEPTH = 4
GRID_W = 64
EPS = 1e-6
HY_WIDTH = 512
HY_SHORT = 3
HY_EMB = 33
HY_FH = 64
HY_SIN_FREQ = 1.0
HY_TARGET = 1e-2
HY_FAST = 0.3
HY_SLOW = 1.5
HY_SHIFT = 0.0
RET_HEADS = 4
RET_DK = 128
RET_DV = 128
RET_WIDTH = RET_HEADS * RET_DV
RET_THETA = 10000.0
ATT_HEADS = 4
ATT_KV_HEADS = 2
ATT_HD = 128
ATT_WIDTH = ATT_HEADS * ATT_HD
ROPE_THETA = 10000.0
X_HEADS = 4
X_HD = D_MODEL // X_HEADS
D_FF = 4 * D_MODEL
N_BRANCH = 3
MIX_WIDTH = HY_WIDTH + RET_WIDTH + ATT_WIDTH

COL_HY = 0
COL_RQ = 3 * HY_WIDTH
COL_RK = COL_RQ + RET_HEADS * RET_DK
COL_RV = COL_RK + RET_HEADS * RET_DK
COL_RG = COL_RV + RET_WIDTH
COL_AQ = COL_RG + RET_WIDTH
COL_AK = COL_AQ + ATT_WIDTH
COL_AV = COL_AK + ATT_KV_HEADS * ATT_HD
COL_GATE = COL_AV + ATT_KV_HEADS * ATT_HD
IN_COLS = COL_GATE + N_BRANCH * D_MODEL

LANES = 128
VMEM_LIMIT = 56 * 1024 * 1024
DFT_N2 = 128
DFT_NB = 16
DFT_KB = 8
DFT_PASSES = dict(s1=1, s2f=1, s2i=1, s3=1, k1=1, k2=1)
RET_BLOCK = 256
RET_UNROLL = 8
ATT_KEY_CHUNK = 1024
FILTER_PASSES = 1
FF_CHUNK = 1024
TAIL_TM = 512
LOG2E = 1.4426950408889634


def _params(sem):
    return pltpu.CompilerParams(dimension_semantics=sem, vmem_limit_bytes=VMEM_LIMIT)


def _dot(a, b):
    return jnp.dot(a, b, preferred_element_type=F32)


def _dot_nt(a, b):
    return lax.dot_general(a, b, (((1,), (1,)), ((), ())), preferred_element_type=F32)


def _dot_tn(a, b):
    return lax.dot_general(a, b, (((0,), (0,)), ((), ())), preferred_element_type=F32)


def _rms(x, g):
    ms = jnp.mean(x * x, axis=-1, keepdims=True)
    return x * lax.rsqrt(ms + EPS) * g


def _sigmoid(x):
    return 1.0 / (1.0 + jnp.exp(-x))


def _split(x):
    hi = x.astype(BF16)
    lo = (x - hi.astype(F32)).astype(BF16)
    return hi, lo


def _dot3(a, b):
    return _dot(a[0], b[0]) + (_dot(a[0], b[1]) + _dot(a[1], b[0]))


def _mm(a, b, passes):
    if passes == 1:
        return _dot(a.astype(BF16), b.astype(BF16))
    return _dot3(_split(a.astype(F32)), _split(b.astype(F32)))


def _nm_body(x_ref, g_ref, w_ref, o_ref, *, tn, act):
    h = _rms(x_ref[...], g_ref[...]).astype(BF16)
    for j in range(w_ref.shape[1] // tn):
        y = _dot(h, w_ref[:, j * tn:(j + 1) * tn])
        if act:
            y = jnp.square(jnp.maximum(y, 0.0))
        o_ref[:, j * tn:(j + 1) * tn] = y.astype(o_ref.dtype)


def _norm_matmul(x, g, w, *, tm, tn, act=False):
    T, D = x.shape
    N = w.shape[1]
    return pl.pallas_call(
        functools.partial(_nm_body, tn=tn, act=act),
        grid=(T // tm,),
        in_specs=[pl.BlockSpec((tm, D), lambda i: (i, 0)),
                  pl.BlockSpec((1, D), lambda i: (0, 0)),
                  pl.BlockSpec((D, N), lambda i: (0, 0), pipeline_mode=pl.Buffered(1))],
        out_specs=pl.BlockSpec((tm, N), lambda i: (i, 0)),
        out_shape=jax.ShapeDtypeStruct((T, N), BF16),
        compiler_params=_params(("parallel",)),
        name="norm_matmul",
    )(x, g.reshape(1, D), w)


def _rotate_half(x, cos, sin):
    return x * cos + pltpu.roll(x, x.shape[-1] // 2, 1) * sin


def _axial_rope(x, gain, cos, sin, scale):
    x = _rms(x, gain)
    lane = lax.broadcasted_iota(jnp.int32, x.shape, 1)
    lower = (lane % (ATT_HD // 2)) < (ATT_HD // 4)
    swapped = jnp.where(lower, pltpu.roll(x, LANES - ATT_HD // 4, 1), pltpu.roll(x, ATT_HD // 4, 1))
    return (x * cos + swapped * sin) * scale


_WIN_SEGMENTS = (
    [(COL_HY + i * HY_WIDTH, HY_WIDTH, None) for i in range(3)]
    + [(COL_RQ, RET_WIDTH, "ret_q"), (COL_RK, RET_WIDTH, "ret_k"), (COL_RV, RET_WIDTH, None),
       (COL_RG, RET_WIDTH, None), (COL_AQ, ATT_WIDTH, "att_q"), (COL_AK, COL_GATE - COL_AK, "att_kv")]
    + [(COL_GATE + i * D_MODEL, D_MODEL, None) for i in range(N_BRANCH)])


def _win_body(x_ref, g_ref, w_ref, qn_ref, kn_ref, rc_ref, rs_ref, ac_ref, as_ref, o_ref):
    h = _rms(x_ref[...], g_ref[...]).astype(BF16)
    for start, width, kind in _WIN_SEGMENTS:
        y = _dot(h, w_ref[:, start:start + width])
        if kind is None:
            o_ref[:, start:start + width] = y.astype(o_ref.dtype)
            continue
        for hh in range(width // LANES):
            yh = y[:, hh * LANES:(hh + 1) * LANES]
            if kind == "ret_q":
                yh = _rotate_half(yh, rc_ref[...], rs_ref[...])
            elif kind == "ret_k":
                yh = _rotate_half(yh, rc_ref[...], rs_ref[...]) * (RET_DK ** -0.5)
            elif kind == "att_q":
                yh = _axial_rope(yh, qn_ref[...], ac_ref[...], as_ref[...], ATT_HD ** -0.5 * LOG2E)
            elif hh < ATT_KV_HEADS:
                yh = _axial_rope(yh, kn_ref[...], ac_ref[...], as_ref[...], 1.0)
            o_ref[:, start + hh * LANES:start + (hh + 1) * LANES] = yh.astype(o_ref.dtype)


def _input_projection(x, g, w, qn, kn, pt, L, *, tm):
    T, D = x.shape
    N = w.shape[1]
    per_seq = L // tm
    vec = lambda n: pl.BlockSpec((1, n), lambda i: (0, 0))
    tab = pl.BlockSpec((tm, LANES), lambda i: (i % per_seq, 0))
    return pl.pallas_call(
        _win_body,
        grid=(T // tm,),
        in_specs=[pl.BlockSpec((tm, D), lambda i: (i, 0)), vec(D),
                  pl.BlockSpec((D, N), lambda i: (0, 0), pipeline_mode=pl.Buffered(1)),
                  vec(ATT_HD), vec(ATT_HD), tab, tab, tab, tab],
        out_specs=pl.BlockSpec((tm, N), lambda i: (i, 0)),
        out_shape=jax.ShapeDtypeStruct((T, N), BF16),
        compiler_params=_params(("parallel",)),
        name="input_projection",
    )(x, g.reshape(1, D), w, qn.reshape(1, ATT_HD), kn.reshape(1, ATT_HD),
      pt["ret_cos"], pt["ret_sin"], pt["att_cos"], pt["att_sin"])


def _hyprep_body(x0_ref, x1_ref, v_ref, w_ref, zf_ref, x0c_ref):
    L = x0_ref.shape[1]
    row = lax.broadcasted_iota(jnp.int32, (L, LANES), 0)

    def conv(p_ref, grp):
        p = p_ref[0].astype(F32)
        w = w_ref[:, grp, :]
        prev = jnp.where(row == 0, 0.0, pltpu.roll(p, 1, 0))
        nxt = jnp.where(row == L - 1, 0.0, pltpu.roll(p, L - 1, 0))
        return prev * w[0:1] + p * w[1:2] + nxt * w[2:3]

    x0c_ref[0] = conv(x0_ref, 0).astype(x0c_ref.dtype)
    zf_ref[0] = (conv(v_ref, 2) * conv(x1_ref, 1)).astype(zf_ref.dtype)


def _hyena_prep(proj, conv_w):
    B, L, _ = proj.shape
    nb = HY_WIDTH // LANES
    w = conv_w.reshape(HY_SHORT, 3, HY_WIDTH)
    blk = lambda off: pl.BlockSpec((1, L, LANES), lambda b, c, off=off: (b, 0, off + c))
    out = pl.BlockSpec((1, L, LANES), lambda b, c: (b, 0, c))
    return pl.pallas_call(
        _hyprep_body,
        grid=(B, nb),
        in_specs=[blk(0), blk(nb), blk(2 * nb),
                  pl.BlockSpec((HY_SHORT, 3, LANES), lambda b, c: (0, 0, c))],
        out_specs=[out, out],
        out_shape=[jax.ShapeDtypeStruct((B, L, HY_WIDTH), BF16)] * 2,
        compiler_params=_params(("parallel", "parallel")),
        name="hyena_prep",
    )(proj, proj, proj, w)


def _filt_body(f_ref, w1_ref, b1_ref, w2_ref, b2_ref, w3_ref, dl_ref, o_ref, *, L):
    tl = f_ref.shape[0]
    f = f_ref[...]
    h = jnp.sin(HY_SIN_FREQ * (_mm(f, w1_ref[...], 3) + b1_ref[...]))
    h = jnp.sin(HY_SIN_FREQ * (_mm(h, w2_ref[...], FILTER_PASSES) + b2_ref[...]))
    filt = _mm(h, w3_ref[...], FILTER_PASSES)
    mod = jnp.exp(-f[:, 0:1] * dl_ref[...]) + HY_SHIFT
    row = pl.program_id(0) * tl + lax.broadcasted_iota(jnp.int32, (tl, HY_WIDTH), 0)
    sel = jnp.where(row < L, filt[:, :HY_WIDTH], filt[:, HY_WIDTH:])
    o_ref[...] = jnp.where(row == L, 0.0, sel * mod)


def _hyena_features(L):
    t = jnp.linspace(0.0, 1.0, L, dtype=F32)[:, None]
    bands = (HY_EMB - 1) // 2
    w = 2.0 * math.pi * jnp.arange(L, dtype=F32)[:, None] / L
    f = jnp.linspace(1e-4, bands - 1, bands, dtype=F32)[None, :]
    z = jnp.concatenate([t, jnp.cos(f * w), -jnp.sin(f * w)], axis=-1)
    zc = jnp.concatenate([z, z[:1], jnp.flip(z[1:], axis=0)], axis=0)
    return jnp.pad(zc, ((0, 0), (0, LANES - HY_EMB)))


def _hyena_filter(feats, w1, b1, w2, b2, w3, L):
    tl = 512
    pad = LANES - HY_FH
    w1p = jnp.pad(w1, ((0, LANES - HY_EMB), (0, pad)))
    b1p = jnp.pad(b1, (0, pad)).reshape(1, LANES)
    w2p = jnp.pad(w2, ((0, pad), (0, pad)))
    b2p = jnp.pad(b2, (0, pad)).reshape(1, LANES)
    w3p = jnp.pad(w3, ((0, pad), (0, 0)))
    min_decay = math.log(HY_TARGET) / HY_SLOW
    max_decay = math.log(HY_TARGET) / HY_FAST
    deltas = jnp.abs(jnp.linspace(min_decay, max_decay, HY_WIDTH, dtype=F32)).reshape(1, HY_WIDTH)
    full = lambda shape: pl.BlockSpec(shape, lambda i: (0,) * len(shape))
    return pl.pallas_call(
        functools.partial(_filt_body, L=L),
        grid=(2 * L // tl,),
        in_specs=[pl.BlockSpec((tl, LANES), lambda i: (i, 0)),
                  full((LANES, LANES)), full((1, LANES)), full((LANES, LANES)), full((1, LANES)),
                  full((LANES, 2 * HY_WIDTH)), full((1, HY_WIDTH))],
        out_specs=pl.BlockSpec((tl, HY_WIDTH), lambda i: (i, 0)),
        out_shape=jax.ShapeDtypeStruct((2 * L, HY_WIDTH), F32),
        compiler_params=_params(("parallel",)),
        name="hyena_filter",
    )(feats, w1p, b1p, w2p, b2p, w3p, deltas)


def _expanded(fn, rows, cols):
    small = fn(lax.broadcasted_iota(jnp.int32, (rows, cols), 0), lax.broadcasted_iota(jnp.int32, (rows, cols), 1))
    hp = functools.partial(jnp.dot, precision=lax.Precision.HIGHEST)
    pick_r = (lax.broadcasted_iota(jnp.int32, (rows * DFT_NB, rows), 0) // DFT_NB
              == lax.broadcasted_iota(jnp.int32, (rows * DFT_NB, rows), 1)).astype(F32)
    pick_c = (lax.broadcasted_iota(jnp.int32, (cols, cols * DFT_NB), 1) // DFT_NB
              == lax.broadcasted_iota(jnp.int32, (cols, cols * DFT_NB), 0)).astype(F32)
    ri = lax.broadcasted_iota(jnp.int32, (rows * DFT_NB, cols * DFT_NB), 0)
    ci = lax.broadcasted_iota(jnp.int32, (rows * DFT_NB, cols * DFT_NB), 1)
    return jnp.where(ri % DFT_NB == ci % DFT_NB, hp(hp(pick_r, small), pick_c), 0.0)


def _dft_tables(L):
    N = 2 * L
    N2 = DFT_N2
    N1 = N // N2
    N1h = N1 // 2

    def cs1(k1, n1):
        ang = ((k1 * n1) % N1).astype(F32) * (2.0 * math.pi / N1)
        return jnp.cos(ang), jnp.sin(ang)

    def f1_pair(r, c):
        cs, sn = cs1(r // 2, c % N1h)
        re_row, first = (r % 2) == 0, c < N1h
        return jnp.where(re_row, jnp.where(first, cs, sn), jnp.where(first, -sn, cs))

    def f1_real(r, c):
        cs, sn = cs1(r // 2, c)
        return jnp.where((r % 2) == 0, cs, -sn)

    def f3_pair(r, c):
        cs, sn = cs1(c // 2, r % N1h)
        re_out, re_in = r < N1h, (c % 2) == 0
        return jnp.where(re_out, jnp.where(re_in, cs, -sn), jnp.where(re_in, sn, cs)) * (1.0 / N)

    kk = (jnp.arange(N1, dtype=jnp.int32)[:, None, None]
          + N1 * jnp.arange(N2, dtype=jnp.int32)[None, :, None])
    n2 = jnp.arange(N2, dtype=jnp.int32)[None, None, :]
    a2 = ((kk * n2) % N).astype(F32) * (2.0 * math.pi / N)
    gr, gi = jnp.cos(a2), -jnp.sin(a2)
    gtr, gti = jnp.swapaxes(gr, 1, 2), jnp.swapaxes(gi, 1, 2)
    g_fwd = jnp.concatenate([jnp.concatenate([gr, -gi], axis=2),
                             jnp.concatenate([gi, gr], axis=2)], axis=1)
    g_inv = jnp.concatenate([jnp.concatenate([gtr, gti], axis=2),
                             jnp.concatenate([-gti, gtr], axis=2)], axis=1)
    return dict(N1=N1, N1h=N1h, N2=N2,
                f1_pair=_expanded(f1_pair, 2 * N1, 2 * N1h), f1_real=_expanded(f1_real, 2 * N1, N1),
                f3_pair=_expanded(f3_pair, 2 * N1h, 2 * N1), g_fwd=g_fwd, g_inv=g_inv)


def _stage_dtype(passes):
    return BF16 if passes == 1 else F32


def _dft1_body(x_ref, f_ref, o_ref, *, passes):
    P, K, NB, C = x_ref.shape
    y = _mm(f_ref[...], x_ref[...].reshape(P * K * NB, C), passes)
    o_ref[0] = y.astype(o_ref.dtype).reshape(o_ref.shape[1:])


def _dft_stage1(x, f, *, P, passes, out_dtype):
    Bx, K, N2, C = x.shape
    N1 = f.shape[0] // (2 * DFT_NB)
    return pl.pallas_call(
        functools.partial(_dft1_body, passes=passes),
        grid=(Bx // P, N2 // DFT_NB),
        in_specs=[pl.BlockSpec((P, K, DFT_NB, C), lambda b, j: (b, 0, j, 0)),
                  pl.BlockSpec(f.shape, lambda b, j: (0, 0), pipeline_mode=pl.Buffered(1))],
        out_specs=pl.BlockSpec((1, N1, 2, DFT_NB, C), lambda b, j: (b, 0, 0, j, 0)),
        out_shape=jax.ShapeDtypeStruct((Bx // P, N1, 2, N2, C), out_dtype),
        compiler_params=_params(("parallel", "parallel")),
        name="dft_stage1",
    )(x, f)


def _spec_body(a_ref, g_ref, o_ref, *, passes):
    def step(k, carry):
        o_ref[k] = _mm(g_ref[k], a_ref[0, k], passes)
        return carry

    lax.fori_loop(0, a_ref.shape[1], step, 0)


def _dft_spectrum(a, g, *, kb, passes):
    _, N1, R, C = a.shape
    return pl.pallas_call(
        functools.partial(_spec_body, passes=passes),
        grid=(N1 // kb,),
        in_specs=[pl.BlockSpec((1, kb, R, C), lambda i: (0, i, 0, 0)),
                  pl.BlockSpec((kb, R, R), lambda i: (i, 0, 0))],
        out_specs=pl.BlockSpec((kb, R, C), lambda i: (i, 0, 0)),
        out_shape=jax.ShapeDtypeStruct((N1, R, C), F32),
        compiler_params=_params(("parallel",)),
        name="dft_spectrum",
    )(a, g)


def _dft2_body(a_ref, gf_ref, gi_ref, ks_ref, o_ref, *, passes_fwd, passes_inv):
    H = a_ref.shape[2] // 2

    def step(k, carry):
        y = _mm(gf_ref[k], a_ref[0, k], passes_fwd)
        yr, yi = y[:H], y[H:]
        kr, ki = ks_ref[k, :H, :], ks_ref[k, H:, :]
        z = jnp.concatenate([yr * kr - yi * ki, yr * ki + yi * kr], axis=0)
        o_ref[0, k] = _mm(gi_ref[k], z, passes_inv).astype(o_ref.dtype)
        return carry

    lax.fori_loop(0, a_ref.shape[1], step, 0)


def _dft_stage2(a, gf, gi, ks, *, kb, passes_fwd, passes_inv, out_dtype):
    Bp, N1, R, C = a.shape
    dat = pl.BlockSpec((1, kb, R, C), lambda i, b: (b, i, 0, 0))
    mat = pl.BlockSpec((kb, R, R), lambda i, b: (i, 0, 0))
    return pl.pallas_call(
        functools.partial(_dft2_body, passes_fwd=passes_fwd, passes_inv=passes_inv),
        grid=(N1 // kb, Bp),
        in_specs=[dat, mat, mat, pl.BlockSpec((kb, R, C), lambda i, b: (i, 0, 0))],
        out_specs=dat,
        out_shape=jax.ShapeDtypeStruct((Bp, N1, R, C), out_dtype),
        compiler_params=_params(("parallel", "parallel")),
        name="dft_stage2",
    )(a, gf, gi, ks)


def _dft3_body(t_ref, f_ref, zf_ref, x0_ref, bias_ref, o_ref, *, passes):
    _, N1, _, NB, C = t_ref.shape
    y = _mm(f_ref[...], t_ref[0].reshape(N1 * 2 * NB, C), passes).reshape(zf_ref.shape)
    o_ref[...] = ((y + zf_ref[...].astype(F32) * bias_ref[...]) * x0_ref[...].astype(F32)).astype(o_ref.dtype)


def _dft_stage3(t, f, zf, x0, bias, *, passes):
    Bp, N1, _, N2, C = t.shape
    B, N1h = zf.shape[:2]
    pair = pl.BlockSpec((2, N1h, DFT_NB, C), lambda b, j: (b, 0, j, 0))
    return pl.pallas_call(
        functools.partial(_dft3_body, passes=passes),
        grid=(Bp, N2 // DFT_NB),
        in_specs=[pl.BlockSpec((1, N1, 2, DFT_NB, C), lambda b, j: (b, 0, 0, j, 0)),
                  pl.BlockSpec(f.shape, lambda b, j: (0, 0), pipeline_mode=pl.Buffered(1)),
                  pair, pair,
                  pl.BlockSpec((1, C), lambda b, j: (0, 0))],
        out_specs=pair,
        out_shape=jax.ShapeDtypeStruct((B, N1h, N2, C), BF16),
        compiler_params=_params(("parallel", "parallel")),
        name="dft_stage3",
    )(t, f, zf, x0, bias)


def _hyena_mixer(proj, conv_w, kspec, bias, tabs):
    B, L, _ = proj.shape
    N1, N1h, N2 = tabs["N1"], tabs["N1h"], tabs["N2"]
    C = HY_WIDTH
    ps = DFT_PASSES
    zf, x0c = _hyena_prep(proj, conv_w)
    z4 = zf.reshape(B, N1h, N2, C)
    f1 = tabs["f1_pair"].astype(_stage_dtype(ps["s1"]))
    a = _dft_stage1(z4, f1, P=2, passes=ps["s1"], out_dtype=_stage_dtype(ps["s2f"]))
    gdt = _stage_dtype(min(ps["s2f"], ps["s2i"]))
    t = _dft_stage2(a.reshape(B // 2, N1, 2 * N2, C), tabs["g_fwd"].astype(gdt), tabs["g_inv"].astype(gdt),
                    kspec, kb=DFT_KB, passes_fwd=ps["s2f"], passes_inv=ps["s2i"],
                    out_dtype=_stage_dtype(ps["s3"]))
    f3 = tabs["f3_pair"].astype(_stage_dtype(ps["s3"]))
    y = _dft_stage3(t.reshape(B // 2, N1, 2, N2, C), f3, z4, x0c.reshape(B, N1h, N2, C),
                    bias.astype(F32).reshape(1, C), passes=ps["s3"])
    return y.reshape(B * L, C)


def _hyena_spectrum(feats, w1, b1, w2, b2, w3, L, tabs):
    N1, N2 = tabs["N1"], tabs["N2"]
    C = HY_WIDTH
    ps = DFT_PASSES
    kfull = _hyena_filter(feats, w1, b1, w2, b2, w3, L)
    a = _dft_stage1(kfull.reshape(1, N1, N2, C), tabs["f1_real"].astype(_stage_dtype(ps["k1"])), P=1, passes=ps["k1"],
                    out_dtype=_stage_dtype(ps["k2"]))
    return _dft_spectrum(a.reshape(1, N1, 2 * N2, C), tabs["g_fwd"].astype(_stage_dtype(ps["k2"])), kb=DFT_KB,
                         passes=ps["k2"])


def _ret_body(lg_ref, q_ref, k_ref, v_ref, g_ref, o_ref, st_ref):
    C = RET_BLOCK
    L = q_ref.shape[1]
    nblk = L // C
    dk = RET_DK
    h = pl.program_id(1)
    lgf = lg_ref[0, h]
    lgb = lg_ref[1, h]
    ii = lax.broadcasted_iota(jnp.int32, (C, C), 0)
    jj = lax.broadcasted_iota(jnp.int32, (C, C), 1)
    diff = (ii - jj).astype(F32)
    dm = jnp.where(diff >= 0, jnp.exp(jnp.maximum(diff, 0.0) * lgf), jnp.exp(jnp.maximum(-diff, 0.0) * lgb))
    pos = lax.broadcasted_iota(jnp.int32, (C, dk), 0).astype(F32)
    qw_f = jnp.exp((pos + 1.0) * lgf)
    kw_f = jnp.exp((C - 1.0 - pos) * lgf)
    qw_b = jnp.exp((C - pos) * lgb)
    kw_b = jnp.exp(pos * lgb)
    cd_f = jnp.exp(C * lgf)
    cd_b = jnp.exp(C * lgb)

    def block_kv(n, carry):
        sl = pl.ds(pl.multiple_of(n * C, C), C)
        k = k_ref[0, sl, :].astype(F32)
        kk = jnp.concatenate([k * kw_f, k * kw_b], axis=1).astype(BF16)
        st_ref[n] = _dot_tn(kk, v_ref[0, sl, :].astype(BF16))
        return carry

    lax.fori_loop(0, nblk, block_kv, 0, unroll=RET_UNROLL)

    def scan_f(n, S):
        kv = st_ref[n, :dk, :]
        st_ref[n, :dk, :] = S
        return S * cd_f + kv

    lax.fori_loop(0, nblk, scan_f, jnp.zeros((dk, RET_DV), F32))

    def scan_b(m, S):
        n = nblk - 1 - m
        kv = st_ref[n, dk:, :]
        st_ref[n, dk:, :] = S
        return S * cd_b + kv

    lax.fori_loop(0, nblk, scan_b, jnp.zeros((dk, RET_DV), F32))

    def block_out(n, carry):
        sl = pl.ds(pl.multiple_of(n * C, C), C)
        q = q_ref[0, sl, :].astype(BF16)
        scores = _dot_nt(q, k_ref[0, sl, :].astype(BF16)) * dm
        qf = q.astype(F32)
        qq = jnp.concatenate([qf * qw_f, qf * qw_b], axis=1).astype(BF16)
        o = _dot(scores.astype(BF16), v_ref[0, sl, :].astype(BF16)) + _dot(qq, st_ref[n].astype(BF16))
        mu = jnp.mean(o, axis=-1, keepdims=True)
        oc = o - mu
        on = oc * lax.rsqrt(jnp.mean(oc * oc, axis=-1, keepdims=True) + EPS)
        g = g_ref[0, sl, :].astype(F32)
        o_ref[0, sl, :] = (g * _sigmoid(g) * on).astype(o_ref.dtype)
        return carry

    lax.fori_loop(0, nblk, block_out, 0, unroll=RET_UNROLL)


def _retention_mixer(proj, log_gamma):
    B, L, _ = proj.shape
    blk = lambda col: pl.BlockSpec((1, L, LANES), lambda b, h, c=col // LANES: (b, 0, c + h))
    y = pl.pallas_call(
        _ret_body,
        grid=(B, RET_HEADS),
        in_specs=[pl.BlockSpec(memory_space=pltpu.SMEM),
                  blk(COL_RQ), blk(COL_RK), blk(COL_RV), blk(COL_RG)],
        out_specs=pl.BlockSpec((1, L, LANES), lambda b, h: (b, 0, h)),
        out_shape=jax.ShapeDtypeStruct((B, L, RET_WIDTH), BF16),
        scratch_shapes=[pltpu.VMEM((L // RET_BLOCK, 2 * RET_DK, RET_DV), F32)],
        compiler_params=_params(("parallel", "parallel")),
        name="retention",
    )(log_gamma, proj, proj, proj, proj)
    return y.reshape(B * L, RET_WIDTH)


def _att_body(q_ref, k_ref, v_ref, o_ref):
    G = ATT_HEADS // ATT_KV_HEADS
    tq = q_ref.shape[1]
    kc = min(ATT_KEY_CHUNK, k_ref.shape[1])
    nchunk = k_ref.shape[1] // kc
    q = q_ref[0].astype(BF16)
    qs = jnp.concatenate([q[:, g * ATT_HD:(g + 1) * ATT_HD] for g in range(G)], axis=0)
    m = jnp.full((G * tq, 1), -jnp.inf, F32)
    l = jnp.zeros((G * tq, 1), F32)
    acc = jnp.zeros((G * tq, ATT_HD), F32)
    s_next = _dot_nt(qs, k_ref[0, 0:kc, :].astype(BF16))
    for c in range(nchunk):
        s = s_next
        if c + 1 < nchunk:
            s_next = _dot_nt(qs, k_ref[0, (c + 1) * kc:(c + 2) * kc, :].astype(BF16))
        m_new = jnp.maximum(m, jnp.max(s, axis=-1, keepdims=True))
        alpha = jnp.exp2(m - m_new)
        p = jnp.exp2(s - m_new)
        l = alpha * l + jnp.sum(p, axis=-1, keepdims=True)
        acc = alpha * acc + _dot(p.astype(BF16), v_ref[0, c * kc:(c + 1) * kc, :].astype(BF16))
        m = m_new
    o = acc * (1.0 / l)
    for g in range(G):
        o_ref[0, :, g * ATT_HD:(g + 1) * ATT_HD] = o[g * tq:(g + 1) * tq].astype(o_ref.dtype)


def _attention_mixer(proj, *, tq):
    B, L, _ = proj.shape
    G = ATT_HEADS // ATT_KV_HEADS
    y = pl.pallas_call(
        _att_body,
        grid=(B, ATT_KV_HEADS, L // tq),
        in_specs=[pl.BlockSpec((1, tq, G * ATT_HD), lambda b, h, i: (b, i, COL_AQ // (G * ATT_HD) + h)),
                  pl.BlockSpec((1, L, ATT_HD), lambda b, h, i: (b, 0, COL_AK // ATT_HD + h)),
                  pl.BlockSpec((1, L, ATT_HD), lambda b, h, i: (b, 0, COL_AV // ATT_HD + h))],
        out_specs=pl.BlockSpec((1, tq, G * ATT_HD), lambda b, h, i: (b, i, h)),
        out_shape=jax.ShapeDtypeStruct((B, L, ATT_WIDTH), BF16),
        compiler_params=_params(("parallel", "parallel", "arbitrary")),
        name="attention",
    )(proj, proj, proj)
    return y.reshape(B * L, ATT_WIDTH)


def _tail_body(yh_ref, yr_ref, ya_ref, ga_ref, gb_ref, x_ref, kv_ref, wb_ref, wo_ref, gmix_ref,
               gxpre_ref, wq_ref, wxo_ref, gxpost_ref, gfpre_ref, w1_ref, w2_ref, gfpost_ref, o_ref, u_ref):
    D = D_MODEL
    ph = _dot(yh_ref[0], wb_ref[0:HY_WIDTH, :])
    pr = _dot(yr_ref[0], wb_ref[HY_WIDTH:HY_WIDTH + RET_WIDTH, :])
    pa = _dot(ya_ref[0], wb_ref[HY_WIDTH + RET_WIDTH:, :])
    ga = ga_ref[0].astype(F32)
    gb = gb_ref[0].astype(F32)
    hd = D // 2
    g0 = _sigmoid(ga[:, :D])
    g1 = _sigmoid(jnp.concatenate([ga[:, D:], gb[:, :hd]], axis=1))
    g2 = _sigmoid(gb[:, hd:])
    merged = g0 * ph + g1 * pr + g2 * pa
    x = x_ref[0] + _rms(_dot(merged.astype(BF16), wo_ref[...]), gmix_ref[...])
    h = _rms(x, gxpre_ref[...]).astype(BF16)
    q = (_dot(h, wq_ref[...]) * (X_HD ** -0.5)).astype(BF16)
    kv = kv_ref[0]
    outs = []
    for hh in range(X_HEADS):
        sl = slice(hh * X_HD, (hh + 1) * X_HD)
        s = _dot_nt(q[:, sl], kv[:, sl])
        p = jnp.exp(s - jnp.max(s, axis=-1, keepdims=True))
        l = jnp.sum(p, axis=-1, keepdims=True)
        vh = kv[:, D + hh * X_HD:D + (hh + 1) * X_HD]
        outs.append(_dot(p.astype(BF16), vh) * (1.0 / l))
    o = jnp.concatenate(outs, axis=1).astype(BF16)
    x = x + _rms(_dot(o, wxo_ref[...]), gxpost_ref[...])
    h = _rms(x, gfpre_ref[...]).astype(BF16)
    for c in range(D_FF // FF_CHUNK):
        sl = slice(c * FF_CHUNK, (c + 1) * FF_CHUNK)
        u_ref[:, sl] = jnp.square(jnp.maximum(_dot(h, w1_ref[:, sl]), 0.0)).astype(BF16)
    o_ref[0] = x + _rms(_dot(u_ref[...], w2_ref[...]), gfpost_ref[...])


def _layer_tail(yh, yr, ya, proj, x, kv, w, l, *, tm):
    B, L, D = x.shape
    M = kv.shape[1]
    gw = N_BRANCH * D // 2
    tok = lambda width, col=0: pl.BlockSpec((1, tm, width), lambda b, i, col=col: (b, i, col))
    vec = pl.BlockSpec((1, D), lambda b, i: (0, 0))
    res = lambda shape: pl.BlockSpec(shape, lambda b, i: (0, 0), pipeline_mode=pl.Buffered(1))
    g = lambda name: w[name][l].reshape(1, D)
    return pl.pallas_call(
        _tail_body,
        grid=(B, L // tm),
        in_specs=[tok(HY_WIDTH), tok(RET_WIDTH), tok(ATT_WIDTH), tok(gw, COL_GATE // gw), tok(gw, COL_GATE // gw + 1),
                  tok(D), pl.BlockSpec((1, M, 2 * D), lambda b, i: (b, 0, 0)),
                  res((MIX_WIDTH, D)), res((D, D)), vec,
                  vec, res((D, D)), res((D, D)), vec,
                  vec, res((D, D_FF)), res((D_FF, D)), vec],
        out_specs=tok(D),
        out_shape=jax.ShapeDtypeStruct((B, L, D), F32),
        scratch_shapes=[pltpu.VMEM((tm, D_FF), BF16)],
        compiler_params=_params(("parallel", "parallel")),
        name="layer_tail",
    )(yh, yr, ya, proj, proj, x, kv, w["w_branch"][l], w["w_out"][l], g("g_mix_post"),
      g("g_x_pre"), w["w_xq"][l], w["w_xo"][l], g("g_x_post"),
      g("g_ff_pre"), w["w_ff1"][l], w["w_ff2"][l], g("g_ff_post"))


def _rotary_tables(pos, dim, theta):
    inv = theta ** (-jnp.arange(0, dim, 2, dtype=F32) / dim)
    ang = pos.astype(F32)[:, None] * inv[None, :]
    return jnp.cos(ang), jnp.sin(ang)


def _position_tables(L):
    pos = jnp.arange(L)
    rows = L // GRID_W
    row_ids = jnp.repeat(jnp.arange(rows), GRID_W)
    col_ids = jnp.tile(jnp.arange(GRID_W), rows)
    rcos, rsin = _rotary_tables(pos, RET_DK, RET_THETA)
    rc, rs = _rotary_tables(row_ids, ATT_HD // 2, ROPE_THETA)
    cc, cs = _rotary_tables(col_ids, ATT_HD // 2, ROPE_THETA)
    return dict(
        ret_cos=jnp.concatenate([rcos, rcos], axis=-1),
        ret_sin=jnp.concatenate([-rsin, rsin], axis=-1),
        att_cos=jnp.concatenate([rc, rc, cc, cc], axis=-1),
        att_sin=jnp.concatenate([-rs, rs, -cs, cs], axis=-1),
        feats=_hyena_features(L),
        dft=_dft_tables(L),
    )


def _trunk(x, mem, w, pt):
    B, L, D = x.shape
    M = mem.shape[1]
    mem2 = mem.reshape(B * M, D)
    for l in range(DEPTH):
        proj = _input_projection(x.reshape(B * L, D), w["g_mix_pre"][l], w["w_in"][l], w["att_qnorm"][l],
                                 w["att_knorm"][l], pt, L, tm=512).reshape(B, L, IN_COLS)
        kspec = _hyena_spectrum(pt["feats"], w["hy_fw1"][l], w["hy_fb1"][l], w["hy_fw2"][l],
                                w["hy_fb2"][l], w["hy_fw3"][l], L, pt["dft"])
        y_h = _hyena_mixer(proj, w["hy_conv"][l], kspec, w["hy_bias"][l], pt["dft"])
        y_r = _retention_mixer(proj, w["ret_lg"][l])
        y_a = _attention_mixer(proj, tq=512)
        kv = _norm_matmul(mem2, w["g_mem"][l], w["w_xkv"][l], tm=512, tn=1024).reshape(B, M, 2 * D)
        x = _layer_tail(y_h.reshape(B, L, HY_WIDTH), y_r.reshape(B, L, RET_WIDTH), y_a.reshape(B, L, ATT_WIDTH),
                        proj, x, kv, w, l, tm=TAIL_TM)
    return x


def kernel(x_prompt, x_sample, mem_prompt, mem_sample, g_mix_pre, g_mix_post, w_in, hy_conv, hy_fw1, hy_fb1,
           hy_fw2, hy_fb2, hy_fw3, hy_bias, ret_decay, att_qnorm, att_knorm, w_branch, w_out, g_x_pre, g_x_post,
           g_mem, w_xq, w_xkv, w_xo, g_ff_pre, g_ff_post, w_ff1, w_ff2):
    w = dict(
        g_mix_pre=g_mix_pre, g_mix_post=g_mix_post, w_in=w_in.astype(BF16), hy_conv=hy_conv,
        hy_fw1=hy_fw1, hy_fb1=hy_fb1, hy_fw2=hy_fw2, hy_fb2=hy_fb2, hy_fw3=hy_fw3, hy_bias=hy_bias,
        ret_lg=jax.nn.log_sigmoid(ret_decay.astype(F32)), att_qnorm=att_qnorm, att_knorm=att_knorm,
        w_branch=w_branch.astype(BF16), w_out=w_out.astype(BF16), g_x_pre=g_x_pre, g_x_post=g_x_post,
        g_mem=g_mem, w_xq=w_xq.astype(BF16), w_xkv=w_xkv.astype(BF16), w_xo=w_xo.astype(BF16),
        g_ff_pre=g_ff_pre, g_ff_post=g_ff_post, w_ff1=w_ff1.astype(BF16), w_ff2=w_ff2.astype(BF16),
    )
    y_prompt = _trunk(x_prompt, mem_prompt, w, _position_tables(x_prompt.shape[1]))
    y_sample = _trunk(x_sample, mem_sample, w, _position_tables(x_sample.shape[1]))
    return (y_prompt, y_sample)
```

```python
import functools
import math

import jax
import jax.numpy as jnp
from jax import lax
from jax.experimental import pallas as pl
from jax.experimental.pallas import tpu as pltpu

F32 = jnp.float32
BF16 = jnp.bfloat16

D_MODEL = 1024
DEPTH = 4
GRID_W = 64
EPS = 1e-6
HY_WIDTH = 512
HY_SHORT = 3
HY_EMB = 33
HY_FH = 64
HY_SIN_FREQ = 1.0
HY_TARGET = 1e-2
HY_FAST = 0.3
HY_SLOW = 1.5
HY_SHIFT = 0.0
RET_HEADS = 4
RET_DK = 128
RET_DV = 128
RET_WIDTH = RET_HEADS * RET_DV
RET_THETA = 10000.0
ATT_HEADS = 4
ATT_KV_HEADS = 2
ATT_HD = 128
ATT_WIDTH = ATT_HEADS * ATT_HD
ROPE_THETA = 10000.0
X_HEADS = 4
X_HD = D_MODEL // X_HEADS
D_FF = 4 * D_MODEL
N_BRANCH = 3
MIX_WIDTH = HY_WIDTH + RET_WIDTH + ATT_WIDTH

COL_HY = 0
COL_X0C = 0
COL_ZF = HY_WIDTH
COL_RQ = 3 * HY_WIDTH
COL_RK = COL_RQ + RET_HEADS * RET_DK
COL_RV = COL_RK + RET_HEADS * RET_DK
COL_RG = COL_RV + RET_WIDTH
COL_AQ = COL_RG + RET_WIDTH
COL_AK = COL_AQ + ATT_WIDTH
COL_AV = COL_AK + ATT_KV_HEADS * ATT_HD
COL_GATE = COL_AV + ATT_KV_HEADS * ATT_HD
IN_COLS = COL_GATE + N_BRANCH * D_MODEL

LANES = 128
HALO = 8
VMEM_LIMIT = 56 * 1024 * 1024
DFT_N2 = 128
DFT_NB = 16
DFT_KB = 8
DFT_UNROLL = 4
DFT_PASSES = dict(s1=1, s2f=1, s2i=1, s3=1, k1=1, k2=1)
RET_BLOCK = 256
RET_UNROLL = 8
ATT_KEY_CHUNK = 1024
FILTER_PASSES = 1
FF_CHUNK = 1024
TAIL_TM = 512
LOG2E = 1.4426950408889634


def _params(sem):
    return pltpu.CompilerParams(dimension_semantics=sem, vmem_limit_bytes=VMEM_LIMIT)


def _dot(a, b):
    return jnp.dot(a, b, preferred_element_type=F32)


def _dot_nt(a, b):
    return lax.dot_general(a, b, (((1,), (1,)), ((), ())), preferred_element_type=F32)


def _dot_tn(a, b):
    return lax.dot_general(a, b, (((0,), (0,)), ((), ())), preferred_element_type=F32)


def _rms(x, g):
    ms = jnp.mean(x * x, axis=-1, keepdims=True)
    return x * lax.rsqrt(ms + EPS) * g


def _sigmoid(x):
    return 1.0 / (1.0 + jnp.exp(-x))


def _split(x):
    hi = x.astype(BF16)
    lo = (x - hi.astype(F32)).astype(BF16)
    return hi, lo


def _dot3(a, b):
    return _dot(a[0], b[0]) + (_dot(a[0], b[1]) + _dot(a[1], b[0]))


def _mm(a, b, passes):
    if passes == 1:
        return _dot(a.astype(BF16), b.astype(BF16))
    return _dot3(_split(a.astype(F32)), _split(b.astype(F32)))


def _nm_body(x_ref, g_ref, w_ref, o_ref, *, tn, act):
    h = _rms(x_ref[...], g_ref[...]).astype(BF16)
    for j in range(w_ref.shape[1] // tn):
        y = _dot(h, w_ref[:, j * tn:(j + 1) * tn])
        if act:
            y = jnp.square(jnp.maximum(y, 0.0))
        o_ref[:, j * tn:(j + 1) * tn] = y.astype(o_ref.dtype)


def _norm_matmul(x, g, w, *, tm, tn, act=False):
    T, D = x.shape
    N = w.shape[1]
    return pl.pallas_call(
        functools.partial(_nm_body, tn=tn, act=act),
        grid=(T // tm,),
        in_specs=[pl.BlockSpec((tm, D), lambda i: (i, 0)),
                  pl.BlockSpec((1, D), lambda i: (0, 0)),
                  pl.BlockSpec((D, N), lambda i: (0, 0), pipeline_mode=pl.Buffered(1))],
        out_specs=pl.BlockSpec((tm, N), lambda i: (i, 0)),
        out_shape=jax.ShapeDtypeStruct((T, N), BF16),
        compiler_params=_params(("parallel",)),
        name="norm_matmul",
    )(x, g.reshape(1, D), w)


def _rotate_half(x, cos, sin):
    return x * cos + pltpu.roll(x, x.shape[-1] // 2, 1) * sin


def _axial_rope(x, gain, cos, sin, scale):
    x = _rms(x, gain)
    lane = lax.broadcasted_iota(jnp.int32, x.shape, 1)
    lower = (lane % (ATT_HD // 2)) < (ATT_HD // 4)
    swapped = jnp.where(lower, pltpu.roll(x, LANES - ATT_HD // 4, 1), pltpu.roll(x, ATT_HD // 4, 1))
    return (x * cos + swapped * sin) * scale


_WIN_SEGMENTS = (
    [(COL_RQ, RET_WIDTH, "ret_q"), (COL_RK, RET_WIDTH, "ret_k"), (COL_RV, RET_WIDTH, None),
       (COL_RG, RET_WIDTH, None), (COL_AQ, ATT_WIDTH, "att_q"), (COL_AK, COL_GATE - COL_AK, "att_kv")]
    + [(COL_GATE + i * D_MODEL, D_MODEL, None) for i in range(N_BRANCH)])


def _win_body(x_ref, xp_ref, xn_ref, g_ref, w_ref, cw_ref, qn_ref, kn_ref, rc_ref, rs_ref, ac_ref, as_ref, o_ref,
              *, per_seq):
    tm = x_ref.shape[0]
    pos = pl.program_id(0) % per_seq
    x_ext = jnp.concatenate([xp_ref[...], x_ref[...], xn_ref[...]], axis=0)
    r_ext = lax.broadcasted_iota(jnp.int32, (tm + 2 * HALO, 1), 0)
    pad = ((r_ext < HALO) & (pos == 0)) | ((r_ext >= tm + HALO) & (pos == per_seq - 1))
    hf = jnp.where(pad, 0.0, _rms(x_ext, g_ref[...]))
    h_ext = hf.astype(BF16)
    h = hf[HALO:HALO + tm].astype(BF16)
    row = lax.broadcasted_iota(jnp.int32, (tm, HY_WIDTH), 0)
    conv = []
    for grp in range(3):
        cols = slice(COL_HY + grp * HY_WIDTH, COL_HY + (grp + 1) * HY_WIDTH)
        y_ext = _dot(h_ext, w_ref[:, cols])
        y = y_ext[HALO:HALO + tm]
        y_prev = y_ext[HALO - 1:HALO]
        y_next = y_ext[HALO + tm:HALO + tm + 1]
        prev = jnp.where(row == 0, y_prev, pltpu.roll(y, 1, 0))
        nxt = jnp.where(row == tm - 1, y_next, pltpu.roll(y, tm - 1, 0))
        cw = cw_ref[:, cols]
        conv.append(prev * cw[0:1] + y * cw[1:2] + nxt * cw[2:3])
    x0c, x1c, vc = conv
    o_ref[:, COL_X0C:COL_X0C + HY_WIDTH] = x0c.astype(o_ref.dtype)
    o_ref[:, COL_ZF:COL_ZF + HY_WIDTH] = (vc * x1c).astype(o_ref.dtype)
    o_ref[:, COL_ZF + HY_WIDTH:COL_RQ] = vc.astype(o_ref.dtype)
    for start, width, kind in _WIN_SEGMENTS:
        y = _dot(h, w_ref[:, start:start + width])
        if kind is None:
            o_ref[:, start:start + width] = y.astype(o_ref.dtype)
            continue
        for hh in range(width // LANES):
            yh = y[:, hh * LANES:(hh + 1) * LANES]
            if kind == "ret_q":
                yh = _rotate_half(yh, rc_ref[...], rs_ref[...])
            elif kind == "ret_k":
                yh = _rotate_half(yh, rc_ref[...], rs_ref[...]) * (RET_DK ** -0.5)
            elif kind == "att_q":
                yh = _axial_rope(yh, qn_ref[...], ac_ref[...], as_ref[...], ATT_HD ** -0.5 * LOG2E)
            elif hh < ATT_KV_HEADS:
                yh = _axial_rope(yh, kn_ref[...], ac_ref[...], as_ref[...], 1.0)
            o_ref[:, start + hh * LANES:start + (hh + 1) * LANES] = yh.astype(o_ref.dtype)


def _input_projection(x, g, w, conv_w, qn, kn, pt, L, *, tm):
    T, D = x.shape
    N = w.shape[1]
    per_seq = L // tm
    halo_per_tile = tm // HALO
    vec = lambda n: pl.BlockSpec((1, n), lambda i: (0, 0))
    tab = pl.BlockSpec((tm, LANES), lambda i: (i % per_seq, 0))
    return pl.pallas_call(
        functools.partial(_win_body, per_seq=per_seq),
        grid=(T // tm,),
        in_specs=[pl.BlockSpec((tm, D), lambda i: (i, 0)),
                  pl.BlockSpec((HALO, D), lambda i: (jnp.maximum(i * halo_per_tile - 1, 0), 0)),
                  pl.BlockSpec((HALO, D), lambda i: (jnp.minimum((i + 1) * halo_per_tile, T // HALO - 1), 0)),
                  vec(D),
                  pl.BlockSpec((D, N), lambda i: (0, 0), pipeline_mode=pl.Buffered(1)),
                  pl.BlockSpec((HY_SHORT, 3 * HY_WIDTH), lambda i: (0, 0)),
                  vec(ATT_HD), vec(ATT_HD), tab, tab, tab, tab],
        out_specs=pl.BlockSpec((tm, N), lambda i: (i, 0)),
        out_shape=jax.ShapeDtypeStruct((T, N), BF16),
        compiler_params=_params(("parallel",)),
        name="input_projection",
    )(x, x, x, g.reshape(1, D), w, conv_w, qn.reshape(1, ATT_HD), kn.reshape(1, ATT_HD),
      pt["ret_cos"], pt["ret_sin"], pt["att_cos"], pt["att_sin"])


def _filt_body(f_ref, w1_ref, b1_ref, w2_ref, b2_ref, w3_ref, dl_ref, o_ref, *, L):
    tl = f_ref.shape[0]
    f = f_ref[...]
    h = jnp.sin(HY_SIN_FREQ * (_mm(f, w1_ref[...], 3) + b1_ref[...]))
    h = jnp.sin(HY_SIN_FREQ * (_mm(h, w2_ref[...], FILTER_PASSES) + b2_ref[...]))
    filt = _mm(h, w3_ref[...], FILTER_PASSES)
    mod = jnp.exp(-f[:, 0:1] * dl_ref[...]) + HY_SHIFT
    row = pl.program_id(0) * tl + lax.broadcasted_iota(jnp.int32, (tl, HY_WIDTH), 0)
    sel = jnp.where(row < L, filt[:, :HY_WIDTH], filt[:, HY_WIDTH:])
    o_ref[...] = jnp.where(row == L, 0.0, sel * mod)


def _hyena_features(L):
    t = jnp.linspace(0.0, 1.0, L, dtype=F32)[:, None]
    bands = (HY_EMB - 1) // 2
    w = 2.0 * math.pi * jnp.arange(L, dtype=F32)[:, None] / L
    f = jnp.linspace(1e-4, bands - 1, bands, dtype=F32)[None, :]
    z = jnp.concatenate([t, jnp.cos(f * w), -jnp.sin(f * w)], axis=-1)
    zc = jnp.concatenate([z, z[:1], jnp.flip(z[1:], axis=0)], axis=0)
    return jnp.pad(zc, ((0, 0), (0, LANES - HY_EMB)))


def _hyena_filter(feats, w1, b1, w2, b2, w3, L):
    tl = 512
    pad = LANES - HY_FH
    w1p = jnp.pad(w1, ((0, LANES - HY_EMB), (0, pad)))
    b1p = jnp.pad(b1, (0, pad)).reshape(1, LANES)
    w2p = jnp.pad(w2, ((0, pad), (0, pad)))
    b2p = jnp.pad(b2, (0, pad)).reshape(1, LANES)
    w3p = jnp.pad(w3, ((0, pad), (0, 0)))
    min_decay = math.log(HY_TARGET) / HY_SLOW
    max_decay = math.log(HY_TARGET) / HY_FAST
    deltas = jnp.abs(jnp.linspace(min_decay, max_decay, HY_WIDTH, dtype=F32)).reshape(1, HY_WIDTH)
    full = lambda shape: pl.BlockSpec(shape, lambda i: (0,) * len(shape))
    return pl.pallas_call(
        functools.partial(_filt_body, L=L),
        grid=(2 * L // tl,),
        in_specs=[pl.BlockSpec((tl, LANES), lambda i: (i, 0)),
                  full((LANES, LANES)), full((1, LANES)), full((LANES, LANES)), full((1, LANES)),
                  full((LANES, 2 * HY_WIDTH)), full((1, HY_WIDTH))],
        out_specs=pl.BlockSpec((tl, HY_WIDTH), lambda i: (i, 0)),
        out_shape=jax.ShapeDtypeStruct((2 * L, HY_WIDTH), F32),
        compiler_params=_params(("parallel",)),
        name="hyena_filter",
    )(feats, w1p, b1p, w2p, b2p, w3p, deltas)


def _expanded(fn, rows, cols):
    small = fn(lax.broadcasted_iota(jnp.int32, (rows, cols), 0), lax.broadcasted_iota(jnp.int32, (rows, cols), 1))
    hp = functools.partial(jnp.dot, precision=lax.Precision.HIGHEST)
    pick_r = (lax.broadcasted_iota(jnp.int32, (rows * DFT_NB, rows), 0) // DFT_NB
              == lax.broadcasted_iota(jnp.int32, (rows * DFT_NB, rows), 1)).astype(F32)
    pick_c = (lax.broadcasted_iota(jnp.int32, (cols, cols * DFT_NB), 1) // DFT_NB
              == lax.broadcasted_iota(jnp.int32, (cols, cols * DFT_NB), 0)).astype(F32)
    ri = lax.broadcasted_iota(jnp.int32, (rows * DFT_NB, cols * DFT_NB), 0)
    ci = lax.broadcasted_iota(jnp.int32, (rows * DFT_NB, cols * DFT_NB), 1)
    return jnp.where(ri % DFT_NB == ci % DFT_NB, hp(hp(pick_r, small), pick_c), 0.0)


def _dft_tables(L):
    N = 2 * L
    N2 = DFT_N2
    N1 = N // N2
    N1h = N1 // 2

    def cs1(k1, n1):
        ang = ((k1 * n1) % N1).astype(F32) * (2.0 * math.pi / N1)
        return jnp.cos(ang), jnp.sin(ang)

    def f1_pair(r, c):
        cs, sn = cs1(r // 2, c % N1h)
        re_row, first = (r % 2) == 0, c < N1h
        return jnp.where(re_row, jnp.where(first, cs, sn), jnp.where(first, -sn, cs))

    def f1_real(r, c):
        cs, sn = cs1(r // 2, c)
        return jnp.where((r % 2) == 0, cs, -sn)

    def f3_pair(r, c):
        cs, sn = cs1(c // 2, r % N1h)
        re_out, re_in = r < N1h, (c % 2) == 0
        return jnp.where(re_out, jnp.where(re_in, cs, -sn), jnp.where(re_in, sn, cs)) * (1.0 / N)

    kk = (jnp.arange(N1, dtype=jnp.int32)[:, None, None]
          + N1 * jnp.arange(N2, dtype=jnp.int32)[None, :, None])
    n2 = jnp.arange(N2, dtype=jnp.int32)[None, None, :]
    a2 = ((kk * n2) % N).astype(F32) * (2.0 * math.pi / N)
    gr, gi = jnp.cos(a2), -jnp.sin(a2)
    gtr, gti = jnp.swapaxes(gr, 1, 2), jnp.swapaxes(gi, 1, 2)
    g_fwd = jnp.concatenate([jnp.concatenate([gr, -gi], axis=2),
                             jnp.concatenate([gi, gr], axis=2)], axis=1)
    g_inv = jnp.concatenate([jnp.concatenate([gtr, gti], axis=2),
                             jnp.concatenate([-gti, gtr], axis=2)], axis=1)
    return dict(N1=N1, N1h=N1h, N2=N2,
                f1_pair=_expanded(f1_pair, 2 * N1, 2 * N1h), f1_real=_expanded(f1_real, 2 * N1, N1),
                f3_pair=_expanded(f3_pair, 2 * N1h, 2 * N1), g_fwd=g_fwd, g_inv=g_inv)


def _stage_dtype(passes):
    return BF16 if passes == 1 else F32


def _dft1_body(x_ref, f_ref, o_ref, *, passes):
    P, K, NB, C = x_ref.shape
    y = _mm(f_ref[...], x_ref[...].reshape(P * K * NB, C), passes)
    o_ref[0] = y.astype(o_ref.dtype).reshape(o_ref.shape[1:])


def _dft_stage1(x, f, *, P, passes, out_dtype, col=0):
    Bx, K, N2, _ = x.shape
    C = HY_WIDTH
    N1 = f.shape[0] // (2 * DFT_NB)
    return pl.pallas_call(
        functools.partial(_dft1_body, passes=passes),
        grid=(Bx // P, N2 // DFT_NB),
        in_specs=[pl.BlockSpec((P, K, DFT_NB, C), lambda b, j: (b, 0, j, col)),
                  pl.BlockSpec(f.shape, lambda b, j: (0, 0), pipeline_mode=pl.Buffered(1))],
        out_specs=pl.BlockSpec((1, N1, 2, DFT_NB, C), lambda b, j: (b, 0, 0, j, 0)),
        out_shape=jax.ShapeDtypeStruct((Bx // P, N1, 2, N2, C), out_dtype),
        compiler_params=_params(("parallel", "parallel")),
        name="dft_stage1",
    )(x, f)


def _spec_body(a_ref, g_ref, o_ref, *, passes):
    def step(k, carry):
        o_ref[k] = _mm(g_ref[k], a_ref[0, k], passes)
        return carry

    lax.fori_loop(0, a_ref.shape[1], step, 0, unroll=DFT_UNROLL)


def _dft_spectrum(a, g, *, kb, passes):
    _, N1, R, C = a.shape
    return pl.pallas_call(
        functools.partial(_spec_body, passes=passes),
        grid=(N1 // kb,),
        in_specs=[pl.BlockSpec((1, kb, R, C), lambda i: (0, i, 0, 0)),
                  pl.BlockSpec((kb, R, R), lambda i: (i, 0, 0))],
        out_specs=pl.BlockSpec((kb, R, C), lambda i: (i, 0, 0)),
        out_shape=jax.ShapeDtypeStruct((N1, R, C), F32),
        compiler_params=_params(("parallel",)),
        name="dft_spectrum",
    )(a, g)


def _dft2_body(a_ref, gf_ref, gi_ref, ks_ref, o_ref, *, passes_fwd, passes_inv):
    H = a_ref.shape[2] // 2

    def step(k, carry):
        y = _mm(gf_ref[k], a_ref[0, k], passes_fwd)
        yr, yi = y[:H], y[H:]
        kr, ki = ks_ref[k, :H, :], ks_ref[k, H:, :]
        z = jnp.concatenate([yr * kr - yi * ki, yr * ki + yi * kr], axis=0)
        o_ref[0, k] = _mm(gi_ref[k], z, passes_inv).astype(o_ref.dtype)
        return carry

    lax.fori_loop(0, a_ref.shape[1], step, 0, unroll=DFT_UNROLL)


def _dft_stage2(a, gf, gi, ks, *, kb, passes_fwd, passes_inv, out_dtype):
    Bp, N1, R, C = a.shape
    dat = pl.BlockSpec((1, kb, R, C), lambda i, b: (b, i, 0, 0))
    mat = pl.BlockSpec((kb, R, R), lambda i, b: (i, 0, 0))
    return pl.pallas_call(
        functools.partial(_dft2_body, passes_fwd=passes_fwd, passes_inv=passes_inv),
        grid=(N1 // kb, Bp),
        in_specs=[dat, mat, mat, pl.BlockSpec((kb, R, C), lambda i, b: (i, 0, 0))],
        out_specs=dat,
        out_shape=jax.ShapeDtypeStruct((Bp, N1, R, C), out_dtype),
        compiler_params=_params(("parallel", "parallel")),
        name="dft_stage2",
    )(a, gf, gi, ks)


def _dft3_body(t_ref, f_ref, zf_ref, x0_ref, bias_ref, o_ref, *, passes):
    _, N1, _, NB, C = t_ref.shape
    y = _mm(f_ref[...], t_ref[0].reshape(N1 * 2 * NB, C), passes).reshape(zf_ref.shape)
    o_ref[...] = ((y + zf_ref[...].astype(F32) * bias_ref[...]) * x0_ref[...].astype(F32)).astype(o_ref.dtype)


def _dft_stage3(t, f, proj4, bias, *, passes):
    Bp, N1, _, N2, C = t.shape
    B, N1h = proj4.shape[:2]
    pair = lambda col=0: pl.BlockSpec((2, N1h, DFT_NB, C), lambda b, j: (b, 0, j, col))
    return pl.pallas_call(
        functools.partial(_dft3_body, passes=passes),
        grid=(Bp, N2 // DFT_NB),
        in_specs=[pl.BlockSpec((1, N1, 2, DFT_NB, C), lambda b, j: (b, 0, 0, j, 0)),
                  pl.BlockSpec(f.shape, lambda b, j: (0, 0), pipeline_mode=pl.Buffered(1)),
                  pair(COL_ZF // C), pair(COL_X0C // C),
                  pl.BlockSpec((1, C), lambda b, j: (0, 0))],
        out_specs=pair(),
        out_shape=jax.ShapeDtypeStruct((B, N1h, N2, C), BF16),
        compiler_params=_params(("parallel", "parallel")),
        name="dft_stage3",
    )(t, f, proj4, proj4, bias)


def _hyena_mixer(proj, kspec, bias, tabs):
    B, L, _ = proj.shape
    N1, N1h, N2 = tabs["N1"], tabs["N1h"], tabs["N2"]
    C = HY_WIDTH
    ps = DFT_PASSES
    proj4 = proj.reshape(B, N1h, N2, IN_COLS)
    f1 = tabs["f1_pair"].astype(_stage_dtype(ps["s1"]))
    a = _dft_stage1(proj4, f1, P=2, passes=ps["s1"], out_dtype=_stage_dtype(ps["s2f"]), col=COL_ZF // C)
    gdt = _stage_dtype(min(ps["s2f"], ps["s2i"]))
    t = _dft_stage2(a.reshape(B // 2, N1, 2 * N2, C), tabs["g_fwd"].astype(gdt), tabs["g_inv"].astype(gdt),
                    kspec, kb=DFT_KB, passes_fwd=ps["s2f"], passes_inv=ps["s2i"],
                    out_dtype=_stage_dtype(ps["s3"]))
    f3 = tabs["f3_pair"].astype(_stage_dtype(ps["s3"]))
    y = _dft_stage3(t.reshape(B // 2, N1, 2, N2, C), f3, proj4, bias.astype(F32).reshape(1, C), passes=ps["s3"])
    return y.reshape(B * L, C)


def _hyena_spectrum(feats, w1, b1, w2, b2, w3, L, tabs):
    N1, N2 = tabs["N1"], tabs["N2"]
    C = HY_WIDTH
    ps = DFT_PASSES
    kfull = _hyena_filter(feats, w1, b1, w2, b2, w3, L)
    a = _dft_stage1(kfull.reshape(1, N1, N2, C), tabs["f1_real"].astype(_stage_dtype(ps["k1"])), P=1, passes=ps["k1"],
                    out_dtype=_stage_dtype(ps["k2"]))
    return _dft_spectrum(a.reshape(1, N1, 2 * N2, C), tabs["g_fwd"].astype(_stage_dtype(ps["k2"])), kb=DFT_KB,
                         passes=ps["k2"])


def _ret_body(lg_ref, q_ref, k_ref, v_ref, g_ref, o_ref, st_ref):
    C = RET_BLOCK
    L = q_ref.shape[1]
    nblk = L // C
    dk = RET_DK
    h = pl.program_id(1)
    lgf = lg_ref[0, h]
    lgb = lg_ref[1, h]
    ii = lax.broadcasted_iota(jnp.int32, (C, C), 0)
    jj = lax.broadcasted_iota(jnp.int32, (C, C), 1)
    diff = (ii - jj).astype(F32)
    dm = jnp.where(diff >= 0, jnp.exp(jnp.maximum(diff, 0.0) * lgf), jnp.exp(jnp.maximum(-diff, 0.0) * lgb))
    pos = lax.broadcasted_iota(jnp.int32, (C, dk), 0).astype(F32)
    qw_f = jnp.exp((pos + 1.0) * lgf)
    kw_f = jnp.exp((C - 1.0 - pos) * lgf)
    qw_b = jnp.exp((C - pos) * lgb)
    kw_b = jnp.exp(pos * lgb)
    cd_f = jnp.exp(C * lgf)
    cd_b = jnp.exp(C * lgb)

    def block_kv(n, carry):
        sl = pl.ds(pl.multiple_of(n * C, C), C)
        k = k_ref[0, sl, :].astype(F32)
        kk = jnp.concatenate([k * kw_f, k * kw_b], axis=1).astype(BF16)
        st_ref[n] = _dot_tn(kk, v_ref[0, sl, :].astype(BF16))
        return carry

    lax.fori_loop(0, nblk, block_kv, 0, unroll=RET_UNROLL)

    def scan_f(n, S):
        kv = st_ref[n, :dk, :]
        st_ref[n, :dk, :] = S
        return S * cd_f + kv

    lax.fori_loop(0, nblk, scan_f, jnp.zeros((dk, RET_DV), F32))

    def scan_b(m, S):
        n = nblk - 1 - m
        kv = st_ref[n, dk:, :]
        st_ref[n, dk:, :] = S
        return S * cd_b + kv

    lax.fori_loop(0, nblk, scan_b, jnp.zeros((dk, RET_DV), F32))

    def block_out(n, carry):
        sl = pl.ds(pl.multiple_of(n * C, C), C)
        q = q_ref[0, sl, :].astype(BF16)
        scores = _dot_nt(q, k_ref[0, sl, :].astype(BF16)) * dm
        qf = q.astype(F32)
        qq = jnp.concatenate([qf * qw_f, qf * qw_b], axis=1).astype(BF16)
        o = _dot(scores.astype(BF16), v_ref[0, sl, :].astype(BF16)) + _dot(qq, st_ref[n].astype(BF16))
        mu = jnp.mean(o, axis=-1, keepdims=True)
        oc = o - mu
        on = oc * lax.rsqrt(jnp.mean(oc * oc, axis=-1, keepdims=True) + EPS)
        g = g_ref[0, sl, :].astype(F32)
        o_ref[0, sl, :] = (g * _sigmoid(g) * on).astype(o_ref.dtype)
        return carry

    lax.fori_loop(0, nblk, block_out, 0, unroll=RET_UNROLL)


def _retention_mixer(proj, log_gamma):
    B, L, _ = proj.shape
    blk = lambda col: pl.BlockSpec((1, L, LANES), lambda b, h, c=col // LANES: (b, 0, c + h))
    y = pl.pallas_call(
        _ret_body,
        grid=(B, RET_HEADS),
        in_specs=[pl.BlockSpec(memory_space=pltpu.SMEM),
                  blk(COL_RQ), blk(COL_RK), blk(COL_RV), blk(COL_RG)],
        out_specs=pl.BlockSpec((1, L, LANES), lambda b, h: (b, 0, h)),
        out_shape=jax.ShapeDtypeStruct((B, L, RET_WIDTH), BF16),
        scratch_shapes=[pltpu.VMEM((L // RET_BLOCK, 2 * RET_DK, RET_DV), F32)],
        compiler_params=_params(("parallel", "parallel")),
        name="retention",
    )(log_gamma, proj, proj, proj, proj)
    return y.reshape(B * L, RET_WIDTH)


def _att_body(q_ref, k_ref, v_ref, o_ref):
    G = ATT_HEADS // ATT_KV_HEADS
    tq = q_ref.shape[1]
    kc = min(ATT_KEY_CHUNK, k_ref.shape[1])
    nchunk = k_ref.shape[1] // kc
    q = q_ref[0].astype(BF16)
    qs = jnp.concatenate([q[:, g * ATT_HD:(g + 1) * ATT_HD] for g in range(G)], axis=0)
    m = jnp.full((G * tq, 1), -jnp.inf, F32)
    l = jnp.zeros((G * tq, 1), F32)
    acc = jnp.zeros((G * tq, ATT_HD), F32)
    s_next = _dot_nt(qs, k_ref[0, 0:kc, :].astype(BF16))
    for c in range(nchunk):
        s = s_next
        if c + 1 < nchunk:
            s_next = _dot_nt(qs, k_ref[0, (c + 1) * kc:(c + 2) * kc, :].astype(BF16))
        m_new = jnp.maximum(m, jnp.max(s, axis=-1, keepdims=True))
        alpha = jnp.exp2(m - m_new)
        p = jnp.exp2(s - m_new)
        l = alpha * l + jnp.sum(p, axis=-1, keepdims=True)
        acc = alpha * acc + _dot(p.astype(BF16), v_ref[0, c * kc:(c + 1) * kc, :].astype(BF16))
        m = m_new
    o = acc * (1.0 / l)
    for g in range(G):
        o_ref[0, :, g * ATT_HD:(g + 1) * ATT_HD] = o[g * tq:(g + 1) * tq].astype(o_ref.dtype)


def _attention_mixer(proj, *, tq):
    B, L, _ = proj.shape
    G = ATT_HEADS // ATT_KV_HEADS
    y = pl.pallas_call(
        _att_body,
        grid=(B, ATT_KV_HEADS, L // tq),
        in_specs=[pl.BlockSpec((1, tq, G * ATT_HD), lambda b, h, i: (b, i, COL_AQ // (G * ATT_HD) + h)),
                  pl.BlockSpec((1, L, ATT_HD), lambda b, h, i: (b, 0, COL_AK // ATT_HD + h)),
                  pl.BlockSpec((1, L, ATT_HD), lambda b, h, i: (b, 0, COL_AV // ATT_HD + h))],
        out_specs=pl.BlockSpec((1, tq, G * ATT_HD), lambda b, h, i: (b, i, h)),
        out_shape=jax.ShapeDtypeStruct((B, L, ATT_WIDTH), BF16),
        compiler_params=_params(("parallel", "parallel", "arbitrary")),
        name="attention",
    )(proj, proj, proj)
    return y.reshape(B * L, ATT_WIDTH)


def _tail_body(yh_ref, yr_ref, ya_ref, ga_ref, gb_ref, x_ref, kv_ref, wb_ref, wo_ref, gmix_ref,
               gxpre_ref, wq_ref, wxo_ref, gxpost_ref, gfpre_ref, w1_ref, w2_ref, gfpost_ref, o_ref, u_ref):
    D = D_MODEL
    ph = _dot(yh_ref[0], wb_ref[0:HY_WIDTH, :])
    pr = _dot(yr_ref[0], wb_ref[HY_WIDTH:HY_WIDTH + RET_WIDTH, :])
    pa = _dot(ya_ref[0], wb_ref[HY_WIDTH + RET_WIDTH:, :])
    ga = ga_ref[0].astype(F32)
    gb = gb_ref[0].astype(F32)
    hd = D // 2
    g0 = _sigmoid(ga[:, :D])
    g1 = _sigmoid(jnp.concatenate([ga[:, D:], gb[:, :hd]], axis=1))
    g2 = _sigmoid(gb[:, hd:])
    merged = g0 * ph + g1 * pr + g2 * pa
    x = x_ref[0] + _rms(_dot(merged.astype(BF16), wo_ref[...]), gmix_ref[...])
    h = _rms(x, gxpre_ref[...]).astype(BF16)
    q = (_dot(h, wq_ref[...]) * (X_HD ** -0.5)).astype(BF16)
    kv = kv_ref[0]
    outs = []
    for hh in range(X_HEADS):
        sl = slice(hh * X_HD, (hh + 1) * X_HD)
        s = _dot_nt(q[:, sl], kv[:, sl])
        p = jnp.exp(s - jnp.max(s, axis=-1, keepdims=True))
        l = jnp.sum(p, axis=-1, keepdims=True)
        vh = kv[:, D + hh * X_HD:D + (hh + 1) * X_HD]
        outs.append(_dot(p.astype(BF16), vh) * (1.0 / l))
    o = jnp.concatenate(outs, axis=1).astype(BF16)
    x = x + _rms(_dot(o, wxo_ref[...]), gxpost_ref[...])
    h = _rms(x, gfpre_ref[...]).astype(BF16)
    for c in range(D_FF // FF_CHUNK):
        sl = slice(c * FF_CHUNK, (c + 1) * FF_CHUNK)
        u_ref[:, sl] = jnp.square(jnp.maximum(_dot(h, w1_ref[:, sl]), 0.0)).astype(BF16)
    o_ref[0] = x + _rms(_dot(u_ref[...], w2_ref[...]), gfpost_ref[...])


def _layer_tail(yh, yr, ya, proj, x, kv, w, l, *, tm):
    B, L, D = x.shape
    M = kv.shape[1]
    gw = N_BRANCH * D // 2
    tok = lambda width, col=0: pl.BlockSpec((1, tm, width), lambda b, i, col=col: (b, i, col))
    vec = pl.BlockSpec((1, D), lambda b, i: (0, 0))
    res = lambda shape: pl.BlockSpec(shape, lambda b, i: (0, 0), pipeline_mode=pl.Buffered(1))
    g = lambda name: w[name][l].reshape(1, D)
    return pl.pallas_call(
        _tail_body,
        grid=(B, L // tm),
        in_specs=[tok(HY_WIDTH), tok(RET_WIDTH), tok(ATT_WIDTH), tok(gw, COL_GATE // gw), tok(gw, COL_GATE // gw + 1),
                  tok(D), pl.BlockSpec((1, M, 2 * D), lambda b, i: (b, 0, 0)),
                  res((MIX_WIDTH, D)), res((D, D)), vec,
                  vec, res((D, D)), res((D, D)), vec,
                  vec, res((D, D_FF)), res((D_FF, D)), vec],
        out_specs=tok(D),
        out_shape=jax.ShapeDtypeStruct((B, L, D), F32),
        scratch_shapes=[pltpu.VMEM((tm, D_FF), BF16)],
        compiler_params=_params(("parallel", "parallel")),
        name="layer_tail",
    )(yh, yr, ya, proj, proj, x, kv, w["w_branch"][l], w["w_out"][l], g("g_mix_post"),
      g("g_x_pre"), w["w_xq"][l], w["w_xo"][l], g("g_x_post"),
      g("g_ff_pre"), w["w_ff1"][l], w["w_ff2"][l], g("g_ff_post"))


def _rotary_tables(pos, dim, theta):
    inv = theta ** (-jnp.arange(0, dim, 2, dtype=F32) / dim)
    ang = pos.astype(F32)[:, None] * inv[None, :]
    return jnp.cos(ang), jnp.sin(ang)


def _position_tables(L):
    pos = jnp.arange(L)
    rows = L // GRID_W
    row_ids = jnp.repeat(jnp.arange(rows), GRID_W)
    col_ids = jnp.tile(jnp.arange(GRID_W), rows)
    rcos, rsin = _rotary_tables(pos, RET_DK, RET_THETA)
    rc, rs = _rotary_tables(row_ids, ATT_HD // 2, ROPE_THETA)
    cc, cs = _rotary_tables(col_ids, ATT_HD // 2, ROPE_THETA)
    return dict(
        ret_cos=jnp.concatenate([rcos, rcos], axis=-1),
        ret_sin=jnp.concatenate([-rsin, rsin], axis=-1),
        att_cos=jnp.concatenate([rc, rc, cc, cc], axis=-1),
        att_sin=jnp.concatenate([-rs, rs, -cs, cs], axis=-1),
        feats=_hyena_features(L),
        dft=_dft_tables(L),
    )


def _trunk(x, mem, w, pt):
    B, L, D = x.shape
    M = mem.shape[1]
    mem2 = mem.reshape(B * M, D)
    for l in range(DEPTH):
        proj = _input_projection(x.reshape(B * L, D), w["g_mix_pre"][l], w["w_in"][l], w["hy_conv"][l],
                                 w["att_qnorm"][l], w["att_knorm"][l], pt, L, tm=512).reshape(B, L, IN_COLS)
        kspec = _hyena_spectrum(pt["feats"], w["hy_fw1"][l], w["hy_fb1"][l], w["hy_fw2"][l],
                                w["hy_fb2"][l], w["hy_fw3"][l], L, pt["dft"])
        y_h = _hyena_mixer(proj, kspec, w["hy_bias"][l], pt["dft"])
        y_r = _retention_mixer(proj, w["ret_lg"][l])
        y_a = _attention_mixer(proj, tq=512)
        kv = _norm_matmul(mem2, w["g_mem"][l], w["w_xkv"][l], tm=512, tn=1024).reshape(B, M, 2 * D)
        x = _layer_tail(y_h.reshape(B, L, HY_WIDTH), y_r.reshape(B, L, RET_WIDTH), y_a.reshape(B, L, ATT_WIDTH),
                        proj, x, kv, w, l, tm=TAIL_TM)
    return x


def kernel(x_prompt, x_sample, mem_prompt, mem_sample, g_mix_pre, g_mix_post, w_in, hy_conv, hy_fw1, hy_fb1,
           hy_fw2, hy_fb2, hy_fw3, hy_bias, ret_decay, att_qnorm, att_knorm, w_branch, w_out, g_x_pre, g_x_post,
           g_mem, w_xq, w_xkv, w_xo, g_ff_pre, g_ff_post, w_ff1, w_ff2):
    w = dict(
        g_mix_pre=g_mix_pre, g_mix_post=g_mix_post, w_in=w_in.astype(BF16), hy_conv=hy_conv,
        hy_fw1=hy_fw1, hy_fb1=hy_fb1, hy_fw2=hy_fw2, hy_fb2=hy_fb2, hy_fw3=hy_fw3, hy_bias=hy_bias,
        ret_lg=jax.nn.log_sigmoid(ret_decay.astype(F32)), att_qnorm=att_qnorm, att_knorm=att_knorm,
        w_branch=w_branch.astype(BF16), w_out=w_out.astype(BF16), g_x_pre=g_x_pre, g_x_post=g_x_post,
        g_mem=g_mem, w_xq=w_xq.astype(BF16), w_xkv=w_xkv.astype(BF16), w_xo=w_xo.astype(BF16),
        g_ff_pre=g_ff_pre, g_ff_post=g_ff_post, w_ff1=w_ff1.astype(BF16), w_ff2=w_ff2.astype(BF16),
    )
    y_prompt = _trunk(x_prompt, mem_prompt, w, _position_tables(x_prompt.shape[1]))
    y_sample = _trunk(x_sample, mem_sample, w, _position_tables(x_sample.shape[1]))
    return (y_prompt, y_sample)
```

```python
import functools
import math

import jax
import jax.numpy as jnp
from jax import lax
from jax.experimental import pallas as pl
from jax.experimental.pallas import tpu as pltpu

F32 = jnp.float32
BF16 = jnp.bfloat16

D_MODEL = 1024
DEPTH = 4
GRID_W = 64
EPS = 1e-6
HY_WIDTH = 512
HY_SHORT = 3
HY_EMB = 33
HY_FH = 64
HY_SIN_FREQ = 1.0
HY_TARGET = 1e-2
HY_FAST = 0.3
HY_SLOW = 1.5
HY_SHIFT = 0.0
RET_HEADS = 4
RET_DK = 128
RET_DV = 128
RET_WIDTH = RET_HEADS * RET_DV
RET_THETA = 10000.0
ATT_HEADS = 4
ATT_KV_HEADS = 2
ATT_HD = 128
ATT_WIDTH = ATT_HEADS * ATT_HD
ROPE_THETA = 10000.0
X_HEADS = 4
X_HD = D_MODEL // X_HEADS
D_FF = 4 * D_MODEL
N_BRANCH = 3
MIX_WIDTH = HY_WIDTH + RET_WIDTH + ATT_WIDTH

COL_HY = 0
COL_X0C = 0
COL_ZF = HY_WIDTH
COL_RQ = 3 * HY_WIDTH
COL_RK = COL_RQ + RET_HEADS * RET_DK
COL_RV = COL_RK + RET_HEADS * RET_DK
COL_RG = COL_RV + RET_WIDTH
COL_AQ = COL_RG + RET_WIDTH
COL_AK = COL_AQ + ATT_WIDTH
COL_AV = COL_AK + ATT_KV_HEADS * ATT_HD
COL_GATE = COL_AV + ATT_KV_HEADS * ATT_HD
IN_COLS = COL_GATE + N_BRANCH * D_MODEL

LANES = 128
HALO = 8
VMEM_LIMIT = 56 * 1024 * 1024
DFT_N2 = 128
DFT_NB = 8
DFT_KB = 8
DFT_UNROLL = 4
DFT_PASSES = dict(s1=1, s2f=1, s2i=1, s3=1, k1=1, k2=1)
RET_BLOCK = 256
RET_UNROLL = 8
ATT_KEY_CHUNK = 1024
FILTER_PASSES = 1
FF_CHUNK = 1024
TAIL_TM = 512
LOG2E = 1.4426950408889634


def _params(sem):
    return pltpu.CompilerParams(dimension_semantics=sem, vmem_limit_bytes=VMEM_LIMIT)


def _dot(a, b):
    return jnp.dot(a, b, preferred_element_type=F32)


def _dot_nt(a, b):
    return lax.dot_general(a, b, (((1,), (1,)), ((), ())), preferred_element_type=F32)


def _dot_tn(a, b):
    return lax.dot_general(a, b, (((0,), (0,)), ((), ())), preferred_element_type=F32)


def _rms(x, g):
    ms = jnp.mean(x * x, axis=-1, keepdims=True)
    return x * lax.rsqrt(ms + EPS) * g


def _sigmoid(x):
    return 1.0 / (1.0 + jnp.exp(-x))


def _split(x):
    hi = x.astype(BF16)
    lo = (x - hi.astype(F32)).astype(BF16)
    return hi, lo


def _dot3(a, b):
    return _dot(a[0], b[0]) + (_dot(a[0], b[1]) + _dot(a[1], b[0]))


def _mm(a, b, passes):
    if passes == 1:
        return _dot(a.astype(BF16), b.astype(BF16))
    return _dot3(_split(a.astype(F32)), _split(b.astype(F32)))


def _nm_body(x_ref, g_ref, w_ref, o_ref, *, tn, act):
    h = _rms(x_ref[...], g_ref[...]).astype(BF16)
    for j in range(w_ref.shape[1] // tn):
        y = _dot(h, w_ref[:, j * tn:(j + 1) * tn])
        if act:
            y = jnp.square(jnp.maximum(y, 0.0))
        o_ref[:, j * tn:(j + 1) * tn] = y.astype(o_ref.dtype)


def _norm_matmul(x, g, w, *, tm, tn, act=False):
    T, D = x.shape
    N = w.shape[1]
    return pl.pallas_call(
        functools.partial(_nm_body, tn=tn, act=act),
        grid=(T // tm,),
        in_specs=[pl.BlockSpec((tm, D), lambda i: (i, 0)),
                  pl.BlockSpec((1, D), lambda i: (0, 0)),
                  pl.BlockSpec((D, N), lambda i: (0, 0), pipeline_mode=pl.Buffered(1))],
        out_specs=pl.BlockSpec((tm, N), lambda i: (i, 0)),
        out_shape=jax.ShapeDtypeStruct((T, N), BF16),
        compiler_params=_params(("parallel",)),
        name="norm_matmul",
    )(x, g.reshape(1, D), w)


def _rotate_half(x, cos, sin):
    return x * cos + pltpu.roll(x, x.shape[-1] // 2, 1) * sin


def _axial_rope(x, gain, cos, sin, scale):
    x = _rms(x, gain)
    lane = lax.broadcasted_iota(jnp.int32, x.shape, 1)
    lower = (lane % (ATT_HD // 2)) < (ATT_HD // 4)
    swapped = jnp.where(lower, pltpu.roll(x, LANES - ATT_HD // 4, 1), pltpu.roll(x, ATT_HD // 4, 1))
    return (x * cos + swapped * sin) * scale


_WIN_SEGMENTS = (
    [(COL_RQ, RET_WIDTH, "ret_q"), (COL_RK, RET_WIDTH, "ret_k"), (COL_RV, RET_WIDTH, None),
       (COL_RG, RET_WIDTH, None), (COL_AQ, ATT_WIDTH, "att_q"), (COL_AK, COL_GATE - COL_AK, "att_kv")]
    + [(COL_GATE + i * D_MODEL, D_MODEL, None) for i in range(N_BRANCH)])


def _win_body(x_ref, xp_ref, xn_ref, g_ref, w_ref, cw_ref, qn_ref, kn_ref, rc_ref, rs_ref, ac_ref, as_ref, o_ref,
              *, per_seq):
    tm = x_ref.shape[0]
    pos = pl.program_id(0) % per_seq
    x_ext = jnp.concatenate([xp_ref[...], x_ref[...], xn_ref[...]], axis=0)
    r_ext = lax.broadcasted_iota(jnp.int32, (tm + 2 * HALO, 1), 0)
    pad = ((r_ext < HALO) & (pos == 0)) | ((r_ext >= tm + HALO) & (pos == per_seq - 1))
    hf = jnp.where(pad, 0.0, _rms(x_ext, g_ref[...]))
    h_ext = hf.astype(BF16)
    h = hf[HALO:HALO + tm].astype(BF16)
    row = lax.broadcasted_iota(jnp.int32, (tm, HY_WIDTH), 0)
    conv = []
    for grp in range(3):
        cols = slice(COL_HY + grp * HY_WIDTH, COL_HY + (grp + 1) * HY_WIDTH)
        y_ext = _dot(h_ext, w_ref[:, cols])
        y = y_ext[HALO:HALO + tm]
        y_prev = y_ext[HALO - 1:HALO]
        y_next = y_ext[HALO + tm:HALO + tm + 1]
        prev = jnp.where(row == 0, y_prev, pltpu.roll(y, 1, 0))
        nxt = jnp.where(row == tm - 1, y_next, pltpu.roll(y, tm - 1, 0))
        cw = cw_ref[:, cols]
        conv.append(prev * cw[0:1] + y * cw[1:2] + nxt * cw[2:3])
    x0c, x1c, vc = conv
    o_ref[:, COL_X0C:COL_X0C + HY_WIDTH] = x0c.astype(o_ref.dtype)
    o_ref[:, COL_ZF:COL_ZF + HY_WIDTH] = (vc * x1c).astype(o_ref.dtype)
    o_ref[:, COL_ZF + HY_WIDTH:COL_RQ] = vc.astype(o_ref.dtype)
    for start, width, kind in _WIN_SEGMENTS:
        y = _dot(h, w_ref[:, start:start + width])
        if kind is None:
            o_ref[:, start:start + width] = y.astype(o_ref.dtype)
            continue
        for hh in range(width // LANES):
            yh = y[:, hh * LANES:(hh + 1) * LANES]
            if kind == "ret_q":
                yh = _rotate_half(yh, rc_ref[...], rs_ref[...])
            elif kind == "ret_k":
                yh = _rotate_half(yh, rc_ref[...], rs_ref[...]) * (RET_DK ** -0.5)
            elif kind == "att_q":
                yh = _axial_rope(yh, qn_ref[...], ac_ref[...], as_ref[...], ATT_HD ** -0.5 * LOG2E)
            elif hh < ATT_KV_HEADS:
                yh = _axial_rope(yh, kn_ref[...], ac_ref[...], as_ref[...], 1.0)
            o_ref[:, start + hh * LANES:start + (hh + 1) * LANES] = yh.astype(o_ref.dtype)


def _input_projection(x, g, w, conv_w, qn, kn, pt, L, *, tm):
    T, D = x.shape
    N = w.shape[1]
    per_seq = L // tm
    halo_per_tile = tm // HALO
    vec = lambda n: pl.BlockSpec((1, n), lambda i: (0, 0))
    tab = pl.BlockSpec((tm, LANES), lambda i: (i % per_seq, 0))
    return pl.pallas_call(
        functools.partial(_win_body, per_seq=per_seq),
        grid=(T // tm,),
        in_specs=[pl.BlockSpec((tm, D), lambda i: (i, 0)),
                  pl.BlockSpec((HALO, D), lambda i: (jnp.maximum(i * halo_per_tile - 1, 0), 0)),
                  pl.BlockSpec((HALO, D), lambda i: (jnp.minimum((i + 1) * halo_per_tile, T // HALO - 1), 0)),
                  vec(D),
                  pl.BlockSpec((D, N), lambda i: (0, 0), pipeline_mode=pl.Buffered(1)),
                  pl.BlockSpec((HY_SHORT, 3 * HY_WIDTH), lambda i: (0, 0)),
                  vec(ATT_HD), vec(ATT_HD), tab, tab, tab, tab],
        out_specs=pl.BlockSpec((tm, N), lambda i: (i, 0)),
        out_shape=jax.ShapeDtypeStruct((T, N), BF16),
        compiler_params=_params(("parallel",)),
        name="input_projection",
    )(x, x, x, g.reshape(1, D), w, conv_w, qn.reshape(1, ATT_HD), kn.reshape(1, ATT_HD),
      pt["ret_cos"], pt["ret_sin"], pt["att_cos"], pt["att_sin"])


def _filt_body(f_ref, w1_ref, b1_ref, w2_ref, b2_ref, w3_ref, dl_ref, o_ref, *, L):
    tl = f_ref.shape[0]
    f = f_ref[...]
    h = jnp.sin(HY_SIN_FREQ * (_mm(f, w1_ref[...], 3) + b1_ref[...]))
    h = jnp.sin(HY_SIN_FREQ * (_mm(h, w2_ref[...], FILTER_PASSES) + b2_ref[...]))
    filt = _mm(h, w3_ref[...], FILTER_PASSES)
    mod = jnp.exp(-f[:, 0:1] * dl_ref[...]) + HY_SHIFT
    row = pl.program_id(0) * tl + lax.broadcasted_iota(jnp.int32, (tl, HY_WIDTH), 0)
    sel = jnp.where(row < L, filt[:, :HY_WIDTH], filt[:, HY_WIDTH:])
    o_ref[...] = jnp.where(row == L, 0.0, sel * mod)


def _hyena_features(L):
    t = jnp.linspace(0.0, 1.0, L, dtype=F32)[:, None]
    bands = (HY_EMB - 1) // 2
    w = 2.0 * math.pi * jnp.arange(L, dtype=F32)[:, None] / L
    f = jnp.linspace(1e-4, bands - 1, bands, dtype=F32)[None, :]
    z = jnp.concatenate([t, jnp.cos(f * w), -jnp.sin(f * w)], axis=-1)
    zc = jnp.concatenate([z, z[:1], jnp.flip(z[1:], axis=0)], axis=0)
    return jnp.pad(zc, ((0, 0), (0, LANES - HY_EMB)))


def _hyena_filter(feats, w1, b1, w2, b2, w3, L):
    tl = 512
    pad = LANES - HY_FH
    w1p = jnp.pad(w1, ((0, LANES - HY_EMB), (0, pad)))
    b1p = jnp.pad(b1, (0, pad)).reshape(1, LANES)
    w2p = jnp.pad(w2, ((0, pad), (0, pad)))
    b2p = jnp.pad(b2, (0, pad)).reshape(1, LANES)
    w3p = jnp.pad(w3, ((0, pad), (0, 0)))
    min_decay = math.log(HY_TARGET) / HY_SLOW
    max_decay = math.log(HY_TARGET) / HY_FAST
    deltas = jnp.abs(jnp.linspace(min_decay, max_decay, HY_WIDTH, dtype=F32)).reshape(1, HY_WIDTH)
    full = lambda shape: pl.BlockSpec(shape, lambda i: (0,) * len(shape))
    return pl.pallas_call(
        functools.partial(_filt_body, L=L),
        grid=(2 * L // tl,),
        in_specs=[pl.BlockSpec((tl, LANES), lambda i: (i, 0)),
                  full((LANES, LANES)), full((1, LANES)), full((LANES, LANES)), full((1, LANES)),
                  full((LANES, 2 * HY_WIDTH)), full((1, HY_WIDTH))],
        out_specs=pl.BlockSpec((tl, HY_WIDTH), lambda i: (i, 0)),
        out_shape=jax.ShapeDtypeStruct((2 * L, HY_WIDTH), F32),
        compiler_params=_params(("parallel",)),
        name="hyena_filter",
    )(feats, w1p, b1p, w2p, b2p, w3p, deltas)


def _expanded(fn, rows, cols):
    small = fn(lax.broadcasted_iota(jnp.int32, (rows, cols), 0), lax.broadcasted_iota(jnp.int32, (rows, cols), 1))
    hp = functools.partial(jnp.dot, precision=lax.Precision.HIGHEST)
    pick_r = (lax.broadcasted_iota(jnp.int32, (rows * DFT_NB, rows), 0) // DFT_NB
              == lax.broadcasted_iota(jnp.int32, (rows * DFT_NB, rows), 1)).astype(F32)
    pick_c = (lax.broadcasted_iota(jnp.int32, (cols, cols * DFT_NB), 1) // DFT_NB
              == lax.broadcasted_iota(jnp.int32, (cols, cols * DFT_NB), 0)).astype(F32)
    ri = lax.broadcasted_iota(jnp.int32, (rows * DFT_NB, cols * DFT_NB), 0)
    ci = lax.broadcasted_iota(jnp.int32, (rows * DFT_NB, cols * DFT_NB), 1)
    return jnp.where(ri % DFT_NB == ci % DFT_NB, hp(hp(pick_r, small), pick_c), 0.0)


def _dft_tables(L):
    N = 2 * L
    N2 = DFT_N2
    N1 = N // N2
    N1h = N1 // 2

    def cs1(k1, n1):
        ang = ((k1 * n1) % N1).astype(F32) * (2.0 * math.pi / N1)
        return jnp.cos(ang), jnp.sin(ang)

    def f1_pair(r, c):
        cs, sn = cs1(r // 2, c % N1h)
        re_row, first = (r % 2) == 0, c < N1h
        return jnp.where(re_row, jnp.where(first, cs, sn), jnp.where(first, -sn, cs))

    def f1_real(r, c):
        cs, sn = cs1(r // 2, c)
        return jnp.where((r % 2) == 0, cs, -sn)

    def f3_pair(r, c):
        cs, sn = cs1(c // 2, r % N1h)
        re_out, re_in = r < N1h, (c % 2) == 0
        return jnp.where(re_out, jnp.where(re_in, cs, -sn), jnp.where(re_in, sn, cs)) * (1.0 / N)

    kk = (jnp.arange(N1, dtype=jnp.int32)[:, None, None]
          + N1 * jnp.arange(N2, dtype=jnp.int32)[None, :, None])
    n2 = jnp.arange(N2, dtype=jnp.int32)[None, None, :]
    a2 = ((kk * n2) % N).astype(F32) * (2.0 * math.pi / N)
    gr, gi = jnp.cos(a2), -jnp.sin(a2)
    gtr, gti = jnp.swapaxes(gr, 1, 2), jnp.swapaxes(gi, 1, 2)
    g_fwd = jnp.concatenate([jnp.concatenate([gr, -gi], axis=2),
                             jnp.concatenate([gi, gr], axis=2)], axis=1)
    g_inv = jnp.concatenate([jnp.concatenate([gtr, gti], axis=2),
                             jnp.concatenate([-gti, gtr], axis=2)], axis=1)
    return dict(N1=N1, N1h=N1h, N2=N2,
                f1_pair=_expanded(f1_pair, 2 * N1, 2 * N1h), f1_real=_expanded(f1_real, 2 * N1, N1),
                f3_pair=_expanded(f3_pair, 2 * N1h, 2 * N1), g_fwd=g_fwd, g_inv=g_inv)


def _stage_dtype(passes):
    return BF16 if passes == 1 else F32


def _dft1_body(x_ref, f_ref, o_ref, *, passes):
    P, K, NB, C = x_ref.shape
    y = _mm(f_ref[...], x_ref[...].reshape(P * K * NB, C), passes)
    o_ref[0] = y.astype(o_ref.dtype).reshape(o_ref.shape[1:])


def _dft_stage1(x, f, *, P, passes, out_dtype, col=0):
    Bx, K, N2, _ = x.shape
    C = HY_WIDTH
    N1 = f.shape[0] // (2 * DFT_NB)
    return pl.pallas_call(
        functools.partial(_dft1_body, passes=passes),
        grid=(Bx // P, N2 // DFT_NB),
        in_specs=[pl.BlockSpec((P, K, DFT_NB, C), lambda b, j: (b, 0, j, col)),
                  pl.BlockSpec(f.shape, lambda b, j: (0, 0), pipeline_mode=pl.Buffered(1))],
        out_specs=pl.BlockSpec((1, N1, 2, DFT_NB, C), lambda b, j: (b, 0, 0, j, 0)),
        out_shape=jax.ShapeDtypeStruct((Bx // P, N1, 2, N2, C), out_dtype),
        compiler_params=_params(("parallel", "parallel")),
        name="dft_stage1",
    )(x, f)


def _spec_body(a_ref, g_ref, o_ref, *, passes):
    def step(k, carry):
        o_ref[k] = _mm(g_ref[k], a_ref[0, k], passes)
        return carry

    lax.fori_loop(0, a_ref.shape[1], step, 0, unroll=DFT_UNROLL)


def _dft_spectrum(a, g, *, kb, passes):
    _, N1, R, C = a.shape
    return pl.pallas_call(
        functools.partial(_spec_body, passes=passes),
        grid=(N1 // kb,),
        in_specs=[pl.BlockSpec((1, kb, R, C), lambda i: (0, i, 0, 0)),
                  pl.BlockSpec((kb, R, R), lambda i: (i, 0, 0))],
        out_specs=pl.BlockSpec((kb, R, C), lambda i: (i, 0, 0)),
        out_shape=jax.ShapeDtypeStruct((N1, R, C), F32),
        compiler_params=_params(("parallel",)),
        name="dft_spectrum",
    )(a, g)


def _dft2_body(a_ref, gf_ref, gi_ref, ks_ref, o_ref, *, passes_fwd, passes_inv):
    H = a_ref.shape[2] // 2

    def step(k, carry):
        y = _mm(gf_ref[k], a_ref[0, k], passes_fwd)
        yr, yi = y[:H], y[H:]
        kr, ki = ks_ref[k, :H, :], ks_ref[k, H:, :]
        z = jnp.concatenate([yr * kr - yi * ki, yr * ki + yi * kr], axis=0)
        o_ref[0, k] = _mm(gi_ref[k], z, passes_inv).astype(o_ref.dtype)
        return carry

    lax.fori_loop(0, a_ref.shape[1], step, 0, unroll=DFT_UNROLL)


def _dft_stage2(a, gf, gi, ks, *, kb, passes_fwd, passes_inv, out_dtype):
    Bp, N1, R, C = a.shape
    dat = pl.BlockSpec((1, kb, R, C), lambda i, b: (b, i, 0, 0))
    mat = pl.BlockSpec((kb, R, R), lambda i, b: (i, 0, 0))
    return pl.pallas_call(
        functools.partial(_dft2_body, passes_fwd=passes_fwd, passes_inv=passes_inv),
        grid=(N1 // kb, Bp),
        in_specs=[dat, mat, mat, pl.BlockSpec((kb, R, C), lambda i, b: (i, 0, 0))],
        out_specs=dat,
        out_shape=jax.ShapeDtypeStruct((Bp, N1, R, C), out_dtype),
        compiler_params=_params(("parallel", "parallel")),
        name="dft_stage2",
    )(a, gf, gi, ks)


def _dft3_body(t_ref, f_ref, zf_ref, x0_ref, bias_ref, o_ref, *, passes):
    _, N1, _, NB, C = t_ref.shape
    y = _mm(f_ref[...], t_ref[0].reshape(N1 * 2 * NB, C), passes).reshape(zf_ref.shape)
    o_ref[...] = ((y + zf_ref[...].astype(F32) * bias_ref[...]) * x0_ref[...].astype(F32)).astype(o_ref.dtype)


def _dft_stage3(t, f, proj4, bias, *, passes):
    Bp, N1, _, N2, C = t.shape
    B, N1h = proj4.shape[:2]
    pair = lambda col=0: pl.BlockSpec((2, N1h, DFT_NB, C), lambda b, j: (b, 0, j, col))
    return pl.pallas_call(
        functools.partial(_dft3_body, passes=passes),
        grid=(Bp, N2 // DFT_NB),
        in_specs=[pl.BlockSpec((1, N1, 2, DFT_NB, C), lambda b, j: (b, 0, 0, j, 0)),
                  pl.BlockSpec(f.shape, lambda b, j: (0, 0), pipeline_mode=pl.Buffered(1)),
                  pair(COL_ZF // C), pair(COL_X0C // C),
                  pl.BlockSpec((1, C), lambda b, j: (0, 0))],
        out_specs=pair(),
        out_shape=jax.ShapeDtypeStruct((B, N1h, N2, C), BF16),
        compiler_params=_params(("parallel", "parallel")),
        name="dft_stage3",
    )(t, f, proj4, proj4, bias)


def _hyena_mixer(proj, kspec, bias, tabs):
    B, L, _ = proj.shape
    N1, N1h, N2 = tabs["N1"], tabs["N1h"], tabs["N2"]
    C = HY_WIDTH
    ps = DFT_PASSES
    proj4 = proj.reshape(B, N1h, N2, IN_COLS)
    f1 = tabs["f1_pair"].astype(_stage_dtype(ps["s1"]))
    a = _dft_stage1(proj4, f1, P=2, passes=ps["s1"], out_dtype=_stage_dtype(ps["s2f"]), col=COL_ZF // C)
    gdt = _stage_dtype(min(ps["s2f"], ps["s2i"]))
    t = _dft_stage2(a.reshape(B // 2, N1, 2 * N2, C), tabs["g_fwd"].astype(gdt), tabs["g_inv"].astype(gdt),
                    kspec, kb=DFT_KB, passes_fwd=ps["s2f"], passes_inv=ps["s2i"],
                    out_dtype=_stage_dtype(ps["s3"]))
    f3 = tabs["f3_pair"].astype(_stage_dtype(ps["s3"]))
    y = _dft_stage3(t.reshape(B // 2, N1, 2, N2, C), f3, proj4, bias.astype(F32).reshape(1, C), passes=ps["s3"])
    return y.reshape(B * L, C)


def _hyena_spectrum(feats, w1, b1, w2, b2, w3, L, tabs):
    N1, N2 = tabs["N1"], tabs["N2"]
    C = HY_WIDTH
    ps = DFT_PASSES
    kfull = _hyena_filter(feats, w1, b1, w2, b2, w3, L)
    a = _dft_stage1(kfull.reshape(1, N1, N2, C), tabs["f1_real"].astype(_stage_dtype(ps["k1"])), P=1, passes=ps["k1"],
                    out_dtype=_stage_dtype(ps["k2"]))
    return _dft_spectrum(a.reshape(1, N1, 2 * N2, C), tabs["g_fwd"].astype(_stage_dtype(ps["k2"])), kb=DFT_KB,
                         passes=ps["k2"])


def _ret_body(lg_ref, q_ref, k_ref, v_ref, g_ref, o_ref, st_ref):
    C = RET_BLOCK
    L = q_ref.shape[1]
    nblk = L // C
    dk = RET_DK
    h = pl.program_id(1)
    lgf = lg_ref[0, h]
    lgb = lg_ref[1, h]
    ii = lax.broadcasted_iota(jnp.int32, (C, C), 0)
    jj = lax.broadcasted_iota(jnp.int32, (C, C), 1)
    diff = (ii - jj).astype(F32)
    dm = jnp.where(diff >= 0, jnp.exp(jnp.maximum(diff, 0.0) * lgf), jnp.exp(jnp.maximum(-diff, 0.0) * lgb))
    pos = lax.broadcasted_iota(jnp.int32, (C, dk), 0).astype(F32)
    qw_f = jnp.exp((pos + 1.0) * lgf)
    kw_f = jnp.exp((C - 1.0 - pos) * lgf)
    qw_b = jnp.exp((C - pos) * lgb)
    kw_b = jnp.exp(pos * lgb)
    cd_f = jnp.exp(C * lgf)
    cd_b = jnp.exp(C * lgb)

    def block_kv(n, carry):
        sl = pl.ds(pl.multiple_of(n * C, C), C)
        k = k_ref[0, sl, :].astype(F32)
        kk = jnp.concatenate([k * kw_f, k * kw_b], axis=1).astype(BF16)
        st_ref[n] = _dot_tn(kk, v_ref[0, sl, :].astype(BF16))
        return carry

    lax.fori_loop(0, nblk, block_kv, 0, unroll=RET_UNROLL)

    def scan_f(n, S):
        kv = st_ref[n, :dk, :]
        st_ref[n, :dk, :] = S
        return S * cd_f + kv

    lax.fori_loop(0, nblk, scan_f, jnp.zeros((dk, RET_DV), F32))

    def scan_b(m, S):
        n = nblk - 1 - m
        kv = st_ref[n, dk:, :]
        st_ref[n, dk:, :] = S
        return S * cd_b + kv

    lax.fori_loop(0, nblk, scan_b, jnp.zeros((dk, RET_DV), F32))

    def block_out(n, carry):
        sl = pl.ds(pl.multiple_of(n * C, C), C)
        q = q_ref[0, sl, :].astype(BF16)
        scores = _dot_nt(q, k_ref[0, sl, :].astype(BF16)) * dm
        qf = q.astype(F32)
        qq = jnp.concatenate([qf * qw_f, qf * qw_b], axis=1).astype(BF16)
        o = _dot(scores.astype(BF16), v_ref[0, sl, :].astype(BF16)) + _dot(qq, st_ref[n].astype(BF16))
        mu = jnp.mean(o, axis=-1, keepdims=True)
        oc = o - mu
        on = oc * lax.rsqrt(jnp.mean(oc * oc, axis=-1, keepdims=True) + EPS)
        g = g_ref[0, sl, :].astype(F32)
        o_ref[0, sl, :] = (g * _sigmoid(g) * on).astype(o_ref.dtype)
        return carry

    lax.fori_loop(0, nblk, block_out, 0, unroll=RET_UNROLL)


def _retention_mixer(proj, log_gamma):
    B, L, _ = proj.shape
    blk = lambda col: pl.BlockSpec((1, L, LANES), lambda b, h, c=col // LANES: (b, 0, c + h))
    y = pl.pallas_call(
        _ret_body,
        grid=(B, RET_HEADS),
        in_specs=[pl.BlockSpec(memory_space=pltpu.SMEM),
                  blk(COL_RQ), blk(COL_RK), blk(COL_RV), blk(COL_RG)],
        out_specs=pl.BlockSpec((1, L, LANES), lambda b, h: (b, 0, h)),
        out_shape=jax.ShapeDtypeStruct((B, L, RET_WIDTH), BF16),
        scratch_shapes=[pltpu.VMEM((L // RET_BLOCK, 2 * RET_DK, RET_DV), F32)],
        compiler_params=_params(("parallel", "parallel")),
        name="retention",
    )(log_gamma, proj, proj, proj, proj)
    return y.reshape(B * L, RET_WIDTH)


def _att_body(q_ref, k_ref, v_ref, o_ref):
    G = ATT_HEADS // ATT_KV_HEADS
    tq = q_ref.shape[1]
    kc = min(ATT_KEY_CHUNK, k_ref.shape[1])
    nchunk = k_ref.shape[1] // kc
    q = q_ref[0].astype(BF16)
    qs = jnp.concatenate([q[:, g * ATT_HD:(g + 1) * ATT_HD] for g in range(G)], axis=0)
    m = jnp.full((G * tq, 1), -jnp.inf, F32)
    l = jnp.zeros((G * tq, 1), F32)
    acc = jnp.zeros((G * tq, ATT_HD), F32)
    s_next = _dot_nt(qs, k_ref[0, 0:kc, :].astype(BF16))
    for c in range(nchunk):
        s = s_next
        if c + 1 < nchunk:
            s_next = _dot_nt(qs, k_ref[0, (c + 1) * kc:(c + 2) * kc, :].astype(BF16))
        m_new = jnp.maximum(m, jnp.max(s, axis=-1, keepdims=True))
        alpha = jnp.exp2(m - m_new)
        p = jnp.exp2(s - m_new)
        l = alpha * l + jnp.sum(p, axis=-1, keepdims=True)
        acc = alpha * acc + _dot(p.astype(BF16), v_ref[0, c * kc:(c + 1) * kc, :].astype(BF16))
        m = m_new
    o = acc * (1.0 / l)
    for g in range(G):
        o_ref[0, :, g * ATT_HD:(g + 1) * ATT_HD] = o[g * tq:(g + 1) * tq].astype(o_ref.dtype)


def _attention_mixer(proj, *, tq):
    B, L, _ = proj.shape
    G = ATT_HEADS // ATT_KV_HEADS
    y = pl.pallas_call(
        _att_body,
        grid=(B, ATT_KV_HEADS, L // tq),
        in_specs=[pl.BlockSpec((1, tq, G * ATT_HD), lambda b, h, i: (b, i, COL_AQ // (G * ATT_HD) + h)),
                  pl.BlockSpec((1, L, ATT_HD), lambda b, h, i: (b, 0, COL_AK // ATT_HD + h)),
                  pl.BlockSpec((1, L, ATT_HD), lambda b, h, i: (b, 0, COL_AV // ATT_HD + h))],
        out_specs=pl.BlockSpec((1, tq, G * ATT_HD), lambda b, h, i: (b, i, h)),
        out_shape=jax.ShapeDtypeStruct((B, L, ATT_WIDTH), BF16),
        compiler_params=_params(("parallel", "parallel", "arbitrary")),
        name="attention",
    )(proj, proj, proj)
    return y.reshape(B * L, ATT_WIDTH)


def _tail_body(yh_ref, yr_ref, ya_ref, ga_ref, gb_ref, x_ref, kv_ref, wb_ref, wo_ref, gmix_ref,
               gxpre_ref, wq_ref, wxo_ref, gxpost_ref, gfpre_ref, w1_ref, w2_ref, gfpost_ref, o_ref, u_ref):
    D = D_MODEL
    ph = _dot(yh_ref[0], wb_ref[0:HY_WIDTH, :])
    pr = _dot(yr_ref[0], wb_ref[HY_WIDTH:HY_WIDTH + RET_WIDTH, :])
    pa = _dot(ya_ref[0], wb_ref[HY_WIDTH + RET_WIDTH:, :])
    ga = ga_ref[0].astype(F32)
    gb = gb_ref[0].astype(F32)
    hd = D // 2
    g0 = _sigmoid(ga[:, :D])
    g1 = _sigmoid(jnp.concatenate([ga[:, D:], gb[:, :hd]], axis=1))
    g2 = _sigmoid(gb[:, hd:])
    merged = g0 * ph + g1 * pr + g2 * pa
    x = x_ref[0] + _rms(_dot(merged.astype(BF16), wo_ref[...]), gmix_ref[...])
    h = _rms(x, gxpre_ref[...]).astype(BF16)
    q = (_dot(h, wq_ref[...]) * (X_HD ** -0.5)).astype(BF16)
    kv = kv_ref[0]
    outs = []
    for hh in range(X_HEADS):
        sl = slice(hh * X_HD, (hh + 1) * X_HD)
        s = _dot_nt(q[:, sl], kv[:, sl])
        p = jnp.exp(s - jnp.max(s, axis=-1, keepdims=True))
        l = jnp.sum(p, axis=-1, keepdims=True)
        vh = kv[:, D + hh * X_HD:D + (hh + 1) * X_HD]
        outs.append(_dot(p.astype(BF16), vh) * (1.0 / l))
    o = jnp.concatenate(outs, axis=1).astype(BF16)
    x = x + _rms(_dot(o, wxo_ref[...]), gxpost_ref[...])
    h = _rms(x, gfpre_ref[...]).astype(BF16)
    for c in range(D_FF // FF_CHUNK):
        sl = slice(c * FF_CHUNK, (c + 1) * FF_CHUNK)
        u_ref[:, sl] = jnp.square(jnp.maximum(_dot(h, w1_ref[:, sl]), 0.0)).astype(BF16)
    o_ref[0] = x + _rms(_dot(u_ref[...], w2_ref[...]), gfpost_ref[...])


def _layer_tail(yh, yr, ya, proj, x, kv, w, l, *, tm):
    B, L, D = x.shape
    M = kv.shape[1]
    gw = N_BRANCH * D // 2
    tok = lambda width, col=0: pl.BlockSpec((1, tm, width), lambda b, i, col=col: (b, i, col))
    vec = pl.BlockSpec((1, D), lambda b, i: (0, 0))
    res = lambda shape: pl.BlockSpec(shape, lambda b, i: (0, 0), pipeline_mode=pl.Buffered(1))
    g = lambda name: w[name][l].reshape(1, D)
    return pl.pallas_call(
        _tail_body,
        grid=(B, L // tm),
        in_specs=[tok(HY_WIDTH), tok(RET_WIDTH), tok(ATT_WIDTH), tok(gw, COL_GATE // gw), tok(gw, COL_GATE // gw + 1),
                  tok(D), pl.BlockSpec((1, M, 2 * D), lambda b, i: (b, 0, 0)),
                  res((MIX_WIDTH, D)), res((D, D)), vec,
                  vec, res((D, D)), res((D, D)), vec,
                  vec, res((D, D_FF)), res((D_FF, D)), vec],
        out_specs=tok(D),
        out_shape=jax.ShapeDtypeStruct((B, L, D), F32),
        scratch_shapes=[pltpu.VMEM((tm, D_FF), BF16)],
        compiler_params=_params(("parallel", "parallel")),
        name="layer_tail",
    )(yh, yr, ya, proj, proj, x, kv, w["w_branch"][l], w["w_out"][l], g("g_mix_post"),
      g("g_x_pre"), w["w_xq"][l], w["w_xo"][l], g("g_x_post"),
      g("g_ff_pre"), w["w_ff1"][l], w["w_ff2"][l], g("g_ff_post"))


def _rotary_tables(pos, dim, theta):
    inv = theta ** (-jnp.arange(0, dim, 2, dtype=F32) / dim)
    ang = pos.astype(F32)[:, None] * inv[None, :]
    return jnp.cos(ang), jnp.sin(ang)


def _position_tables(L):
    pos = jnp.arange(L)
    rows = L // GRID_W
    row_ids = jnp.repeat(jnp.arange(rows), GRID_W)
    col_ids = jnp.tile(jnp.arange(GRID_W), rows)
    rcos, rsin = _rotary_tables(pos, RET_DK, RET_THETA)
    rc, rs = _rotary_tables(row_ids, ATT_HD // 2, ROPE_THETA)
    cc, cs = _rotary_tables(col_ids, ATT_HD // 2, ROPE_THETA)
    return dict(
        ret_cos=jnp.concatenate([rcos, rcos], axis=-1),
        ret_sin=jnp.concatenate([-rsin, rsin], axis=-1),
        att_cos=jnp.concatenate([rc, rc, cc, cc], axis=-1),
        att_sin=jnp.concatenate([-rs, rs, -cs, cs], axis=-1),
        feats=_hyena_features(L),
        dft=_dft_tables(L),
    )


def _trunk(x, mem, w, pt):
    B, L, D = x.shape
    M = mem.shape[1]
    mem2 = mem.reshape(B * M, D)
    for l in range(DEPTH):
        proj = _input_projection(x.reshape(B * L, D), w["g_mix_pre"][l], w["w_in"][l], w["hy_conv"][l],
                                 w["att_qnorm"][l], w["att_knorm"][l], pt, L, tm=512).reshape(B, L, IN_COLS)
        kspec = _hyena_spectrum(pt["feats"], w["hy_fw1"][l], w["hy_fb1"][l], w["hy_fw2"][l],
                                w["hy_fb2"][l], w["hy_fw3"][l], L, pt["dft"])
        y_h = _hyena_mixer(proj, kspec, w["hy_bias"][l], pt["dft"])
        y_r = _retention_mixer(proj, w["ret_lg"][l])
        y_a = _attention_mixer(proj, tq=512)
        kv = _norm_matmul(mem2, w["g_mem"][l], w["w_xkv"][l], tm=512, tn=1024).reshape(B, M, 2 * D)
        x = _layer_tail(y_h.reshape(B, L, HY_WIDTH), y_r.reshape(B, L, RET_WIDTH), y_a.reshape(B, L, ATT_WIDTH),
                        proj, x, kv, w, l, tm=TAIL_TM)
    return x


def kernel(x_prompt, x_sample, mem_prompt, mem_sample, g_mix_pre, g_mix_post, w_in, hy_conv, hy_fw1, hy_fb1,
           hy_fw2, hy_fb2, hy_fw3, hy_bias, ret_decay, att_qnorm, att_knorm, w_branch, w_out, g_x_pre, g_x_post,
           g_mem, w_xq, w_xkv, w_xo, g_ff_pre, g_ff_post, w_ff1, w_ff2):
    w = dict(
        g_mix_pre=g_mix_pre, g_mix_post=g_mix_post, w_in=w_in.astype(BF16), hy_conv=hy_conv,
        hy_fw1=hy_fw1, hy_fb1=hy_fb1, hy_fw2=hy_fw2, hy_fb2=hy_fb2, hy_fw3=hy_fw3, hy_bias=hy_bias,
        ret_lg=jax.nn.log_sigmoid(ret_decay.astype(F32)), att_qnorm=att_qnorm, att_knorm=att_knorm,
        w_branch=w_branch.astype(BF16), w_out=w_out.astype(BF16), g_x_pre=g_x_pre, g_x_post=g_x_post,
        g_mem=g_mem, w_xq=w_xq.astype(BF16), w_xkv=w_xkv.astype(BF16), w_xo=w_xo.astype(BF16),
        g_ff_pre=g_ff_pre, g_ff_post=g_ff_post, w_ff1=w_ff1.astype(BF16), w_ff2=w_ff2.astype(BF16),
    )
    y_prompt = _trunk(x_prompt, mem_prompt, w, _position_tables(x_prompt.shape[1]))
    y_sample = _trunk(x_sample, mem_sample, w, _position_tables(x_sample.shape[1]))
    return (y_prompt, y_sample)
```

```python
import functools
import math

import jax
import jax.numpy as jnp
from jax import lax
from jax.experimental import pallas as pl
from jax.experimental.pallas import tpu as pltpu

F32 = jnp.float32
BF16 = jnp.bfloat16

D_MODEL = 1024
DEPTH = 4
GRID_W = 64
EPS = 1e-6
HY_WIDTH = 512
HY_SHORT = 3
HY_EMB = 33
HY_FH = 64
HY_SIN_FREQ = 1.0
HY_TARGET = 1e-2
HY_FAST = 0.3
HY_SLOW = 1.5
HY_SHIFT = 0.0
RET_HEADS = 4
RET_DK = 128
RET_DV = 128
RET_WIDTH = RET_HEADS * RET_DV
RET_THETA = 10000.0
ATT_HEADS = 4
ATT_KV_HEADS = 2
ATT_HD = 128
ATT_WIDTH = ATT_HEADS * ATT_HD
ROPE_THETA = 10000.0
X_HEADS = 4
X_HD = D_MODEL // X_HEADS
D_FF = 4 * D_MODEL
N_BRANCH = 3
MIX_WIDTH = HY_WIDTH + RET_WIDTH + ATT_WIDTH

COL_HY = 0
COL_X0C = 0
COL_ZF = HY_WIDTH
COL_RQ = 3 * HY_WIDTH
COL_RK = COL_RQ + RET_HEADS * RET_DK
COL_RV = COL_RK + RET_HEADS * RET_DK
COL_RG = COL_RV + RET_WIDTH
COL_AQ = COL_RG + RET_WIDTH
COL_AK = COL_AQ + ATT_WIDTH
COL_AV = COL_AK + ATT_KV_HEADS * ATT_HD
COL_GATE = COL_AV + ATT_KV_HEADS * ATT_HD
IN_COLS = COL_GATE + N_BRANCH * D_MODEL

LANES = 128
HALO = 8
VMEM_LIMIT = 56 * 1024 * 1024
DFT_N2 = 128
DFT_NB = 16
DFT_KB = 8
DFT_UNROLL = 4
DFT_PASSES = dict(s1=1, s2f=1, s2i=1, s3=1, k1=1, k2=1)
RET_BLOCK = 256
RET_UNROLL = 8
ATT_KEY_CHUNK = 1024
FILTER_PASSES = 1
FF_CHUNK = 1024
TOKEN_TILE = 512
LOG2E = 1.4426950408889634


def _params(sem):
    return pltpu.CompilerParams(dimension_semantics=sem, vmem_limit_bytes=VMEM_LIMIT)


def _dot(a, b):
    return jnp.dot(a, b, preferred_element_type=F32)


def _dot_nt(a, b):
    return lax.dot_general(a, b, (((1,), (1,)), ((), ())), preferred_element_type=F32)


def _dot_tn(a, b):
    return lax.dot_general(a, b, (((0,), (0,)), ((), ())), preferred_element_type=F32)


def _rms(x, g):
    ms = jnp.mean(x * x, axis=-1, keepdims=True)
    return x * lax.rsqrt(ms + EPS) * g


def _sigmoid(x):
    return 1.0 / (1.0 + jnp.exp(-x))


def _split(x):
    hi = x.astype(BF16)
    lo = (x - hi.astype(F32)).astype(BF16)
    return hi, lo


def _dot3(a, b):
    return _dot(a[0], b[0]) + (_dot(a[0], b[1]) + _dot(a[1], b[0]))


def _mm(a, b, passes):
    if passes == 1:
        return _dot(a.astype(BF16), b.astype(BF16))
    return _dot3(_split(a.astype(F32)), _split(b.astype(F32)))


def _nm_body(x_ref, g_ref, w_ref, o_ref, *, tn):
    h = _rms(x_ref[...], g_ref[...]).astype(BF16)
    for j in range(w_ref.shape[1] // tn):
        o_ref[:, j * tn:(j + 1) * tn] = _dot(h, w_ref[:, j * tn:(j + 1) * tn]).astype(o_ref.dtype)


def _norm_matmul(x, g, w, *, tm, tn):
    T, D = x.shape
    N = w.shape[1]
    return pl.pallas_call(
        functools.partial(_nm_body, tn=tn),
        grid=(T // tm,),
        in_specs=[pl.BlockSpec((tm, D), lambda i: (i, 0)),
                  pl.BlockSpec((1, D), lambda i: (0, 0)),
                  pl.BlockSpec((D, N), lambda i: (0, 0), pipeline_mode=pl.Buffered(1))],
        out_specs=pl.BlockSpec((tm, N), lambda i: (i, 0)),
        out_shape=jax.ShapeDtypeStruct((T, N), BF16),
        compiler_params=_params(("parallel",)),
        name="norm_matmul",
    )(x, g.reshape(1, D), w)


def _rotate_half(x, cos, sin):
    return x * cos + pltpu.roll(x, x.shape[-1] // 2, 1) * sin


def _axial_rope(x, gain, cos, sin, scale):
    x = _rms(x, gain)
    lane = lax.broadcasted_iota(jnp.int32, x.shape, 1)
    lower = (lane % (ATT_HD // 2)) < (ATT_HD // 4)
    swapped = jnp.where(lower, pltpu.roll(x, LANES - ATT_HD // 4, 1), pltpu.roll(x, ATT_HD // 4, 1))
    return (x * cos + swapped * sin) * scale


_WIN_SEGMENTS = (
    [(COL_RQ, RET_WIDTH, "ret_q"), (COL_RK, RET_WIDTH, "ret_k"), (COL_RV, RET_WIDTH, None),
       (COL_RG, RET_WIDTH, None), (COL_AQ, ATT_WIDTH, "att_q"), (COL_AK, COL_GATE - COL_AK, "att_kv")]
    + [(COL_GATE + i * D_MODEL, D_MODEL, None) for i in range(N_BRANCH)])


def _win_body(x_ref, xp_ref, xn_ref, g_ref, w_ref, cw_ref, qn_ref, kn_ref, rc_ref, rs_ref, ac_ref, as_ref, o_ref,
              *, per_seq):
    tm = x_ref.shape[0]
    pos = pl.program_id(0) % per_seq
    x_ext = jnp.concatenate([xp_ref[...], x_ref[...], xn_ref[...]], axis=0)
    r_ext = lax.broadcasted_iota(jnp.int32, (tm + 2 * HALO, 1), 0)
    pad = ((r_ext < HALO) & (pos == 0)) | ((r_ext >= tm + HALO) & (pos == per_seq - 1))
    hf = jnp.where(pad, 0.0, _rms(x_ext, g_ref[...]))
    h_ext = hf.astype(BF16)
    h = hf[HALO:HALO + tm].astype(BF16)
    row = lax.broadcasted_iota(jnp.int32, (tm, HY_WIDTH), 0)
    conv = []
    for grp in range(3):
        cols = slice(COL_HY + grp * HY_WIDTH, COL_HY + (grp + 1) * HY_WIDTH)
        y_ext = _dot(h_ext, w_ref[:, cols])
        y = y_ext[HALO:HALO + tm]
        y_prev = y_ext[HALO - 1:HALO]
        y_next = y_ext[HALO + tm:HALO + tm + 1]
        prev = jnp.where(row == 0, y_prev, pltpu.roll(y, 1, 0))
        nxt = jnp.where(row == tm - 1, y_next, pltpu.roll(y, tm - 1, 0))
        cw = cw_ref[:, cols]
        conv.append(prev * cw[0:1] + y * cw[1:2] + nxt * cw[2:3])
    x0c, x1c, vc = conv
    o_ref[:, COL_X0C:COL_X0C + HY_WIDTH] = x0c.astype(o_ref.dtype)
    o_ref[:, COL_ZF:COL_ZF + HY_WIDTH] = (vc * x1c).astype(o_ref.dtype)
    o_ref[:, COL_ZF + HY_WIDTH:COL_RQ] = vc.astype(o_ref.dtype)
    for start, width, kind in _WIN_SEGMENTS:
        y = _dot(h, w_ref[:, start:start + width])
        if kind is None:
            o_ref[:, start:start + width] = y.astype(o_ref.dtype)
            continue
        for hh in range(width // LANES):
            yh = y[:, hh * LANES:(hh + 1) * LANES]
            if kind == "ret_q":
                yh = _rotate_half(yh, rc_ref[...], rs_ref[...])
            elif kind == "ret_k":
                yh = _rotate_half(yh, rc_ref[...], rs_ref[...]) * (RET_DK ** -0.5)
            elif kind == "att_q":
                yh = _axial_rope(yh, qn_ref[...], ac_ref[...], as_ref[...], ATT_HD ** -0.5 * LOG2E)
            elif hh < ATT_KV_HEADS:
                yh = _axial_rope(yh, kn_ref[...], ac_ref[...], as_ref[...], 1.0)
            o_ref[:, start + hh * LANES:start + (hh + 1) * LANES] = yh.astype(o_ref.dtype)


def _input_projection(x, g, w_stack, l, conv_w, qn, kn, pt, L, *, tm):
    T, D = x.shape
    N = w_stack.shape[2]
    per_seq = L // tm
    halo_per_tile = tm // HALO
    vec = lambda n: pl.BlockSpec((1, n), lambda i: (0, 0))
    tab = pl.BlockSpec((tm, LANES), lambda i: (i % per_seq, 0))
    return pl.pallas_call(
        functools.partial(_win_body, per_seq=per_seq),
        grid=(T // tm,),
        in_specs=[pl.BlockSpec((tm, D), lambda i: (i, 0)),
                  pl.BlockSpec((HALO, D), lambda i: (jnp.maximum(i * halo_per_tile - 1, 0), 0)),
                  pl.BlockSpec((HALO, D), lambda i: (jnp.minimum((i + 1) * halo_per_tile, T // HALO - 1), 0)),
                  vec(D),
                  pl.BlockSpec((None, D, N), lambda i: (l, 0, 0), pipeline_mode=pl.Buffered(1)),
                  pl.BlockSpec((HY_SHORT, 3 * HY_WIDTH), lambda i: (0, 0)),
                  vec(ATT_HD), vec(ATT_HD), tab, tab, tab, tab],
        out_specs=pl.BlockSpec((tm, N), lambda i: (i, 0)),
        out_shape=jax.ShapeDtypeStruct((T, N), BF16),
        compiler_params=_params(("parallel",)),
        name="input_projection",
    )(x, x, x, g.reshape(1, D), w_stack, conv_w, qn.reshape(1, ATT_HD), kn.reshape(1, ATT_HD),
      pt["ret_cos"], pt["ret_sin"], pt["att_cos"], pt["att_sin"])


def _filt_body(f_ref, w1_ref, b1_ref, w2_ref, b2_ref, w3_ref, dl_ref, o_ref, *, L):
    tl = f_ref.shape[0]
    f = f_ref[...]
    h = jnp.sin(HY_SIN_FREQ * (_mm(f, w1_ref[...], 3) + b1_ref[...]))
    h = jnp.sin(HY_SIN_FREQ * (_mm(h, w2_ref[...], FILTER_PASSES) + b2_ref[...]))
    filt = _mm(h, w3_ref[...], FILTER_PASSES)
    mod = jnp.exp(-f[:, 0:1] * dl_ref[...]) + HY_SHIFT
    row = pl.program_id(0) * tl + lax.broadcasted_iota(jnp.int32, (tl, HY_WIDTH), 0)
    sel = jnp.where(row < L, filt[:, :HY_WIDTH], filt[:, HY_WIDTH:])
    o_ref[...] = jnp.where(row == L, 0.0, sel * mod)


def _hyena_features(L):
    t = jnp.linspace(0.0, 1.0, L, dtype=F32)[:, None]
    bands = (HY_EMB - 1) // 2
    w = 2.0 * math.pi * jnp.arange(L, dtype=F32)[:, None] / L
    f = jnp.linspace(1e-4, bands - 1, bands, dtype=F32)[None, :]
    z = jnp.concatenate([t, jnp.cos(f * w), -jnp.sin(f * w)], axis=-1)
    zc = jnp.concatenate([z, z[:1], jnp.flip(z[1:], axis=0)], axis=0)
    return jnp.pad(zc, ((0, 0), (0, LANES - HY_EMB)))


def _hyena_filter(feats, w1, b1, w2, b2, w3, L):
    tl = TOKEN_TILE
    pad = LANES - HY_FH
    w1p = jnp.pad(w1, ((0, LANES - HY_EMB), (0, pad)))
    b1p = jnp.pad(b1, (0, pad)).reshape(1, LANES)
    w2p = jnp.pad(w2, ((0, pad), (0, pad)))
    b2p = jnp.pad(b2, (0, pad)).reshape(1, LANES)
    w3p = jnp.pad(w3, ((0, pad), (0, 0)))
    min_decay = math.log(HY_TARGET) / HY_SLOW
    max_decay = math.log(HY_TARGET) / HY_FAST
    deltas = jnp.abs(jnp.linspace(min_decay, max_decay, HY_WIDTH, dtype=F32)).reshape(1, HY_WIDTH)
    full = lambda shape: pl.BlockSpec(shape, lambda i: (0,) * len(shape))
    return pl.pallas_call(
        functools.partial(_filt_body, L=L),
        grid=(2 * L // tl,),
        in_specs=[pl.BlockSpec((tl, LANES), lambda i: (i, 0)),
                  full((LANES, LANES)), full((1, LANES)), full((LANES, LANES)), full((1, LANES)),
                  full((LANES, 2 * HY_WIDTH)), full((1, HY_WIDTH))],
        out_specs=pl.BlockSpec((tl, HY_WIDTH), lambda i: (i, 0)),
        out_shape=jax.ShapeDtypeStruct((2 * L, HY_WIDTH), F32),
        compiler_params=_params(("parallel",)),
        name="hyena_filter",
    )(feats, w1p, b1p, w2p, b2p, w3p, deltas)


def _expanded(fn, rows, cols):
    small = fn(lax.broadcasted_iota(jnp.int32, (rows, cols), 0), lax.broadcasted_iota(jnp.int32, (rows, cols), 1))
    hp = functools.partial(jnp.dot, precision=lax.Precision.HIGHEST)
    pick_r = (lax.broadcasted_iota(jnp.int32, (rows * DFT_NB, rows), 0) // DFT_NB
              == lax.broadcasted_iota(jnp.int32, (rows * DFT_NB, rows), 1)).astype(F32)
    pick_c = (lax.broadcasted_iota(jnp.int32, (cols, cols * DFT_NB), 1) // DFT_NB
              == lax.broadcasted_iota(jnp.int32, (cols, cols * DFT_NB), 0)).astype(F32)
    ri = lax.broadcasted_iota(jnp.int32, (rows * DFT_NB, cols * DFT_NB), 0)
    ci = lax.broadcasted_iota(jnp.int32, (rows * DFT_NB, cols * DFT_NB), 1)
    return jnp.where(ri % DFT_NB == ci % DFT_NB, hp(hp(pick_r, small), pick_c), 0.0)


def _dft_tables(L):
    N = 2 * L
    N2 = DFT_N2
    N1 = N // N2
    N1h = N1 // 2

    def cs1(k1, n1):
        ang = ((k1 * n1) % N1).astype(F32) * (2.0 * math.pi / N1)
        return jnp.cos(ang), jnp.sin(ang)

    def f1_pair(r, c):
        cs, sn = cs1(r // 2, c % N1h)
        re_row, first = (r % 2) == 0, c < N1h
        return jnp.where(re_row, jnp.where(first, cs, sn), jnp.where(first, -sn, cs))

    def f1_real(r, c):
        cs, sn = cs1(r // 2, c)
        return jnp.where((r % 2) == 0, cs, -sn)

    def f3_pair(r, c):
        cs, sn = cs1(c // 2, r % N1h)
        re_out, re_in = r < N1h, (c % 2) == 0
        return jnp.where(re_out, jnp.where(re_in, cs, -sn), jnp.where(re_in, sn, cs)) * (1.0 / N)

    kk = (jnp.arange(N1, dtype=jnp.int32)[:, None, None]
          + N1 * jnp.arange(N2, dtype=jnp.int32)[None, :, None])
    n2 = jnp.arange(N2, dtype=jnp.int32)[None, None, :]
    a2 = ((kk * n2) % N).astype(F32) * (2.0 * math.pi / N)
    gr, gi = jnp.cos(a2), -jnp.sin(a2)
    gtr, gti = jnp.swapaxes(gr, 1, 2), jnp.swapaxes(gi, 1, 2)
    g_fwd = jnp.concatenate([jnp.concatenate([gr, -gi], axis=2),
                             jnp.concatenate([gi, gr], axis=2)], axis=1)
    g_inv = jnp.concatenate([jnp.concatenate([gtr, gti], axis=2),
                             jnp.concatenate([-gti, gtr], axis=2)], axis=1)
    return dict(N1=N1, N1h=N1h, N2=N2,
                f1_pair=_expanded(f1_pair, 2 * N1, 2 * N1h), f1_real=_expanded(f1_real, 2 * N1, N1),
                f3_pair=_expanded(f3_pair, 2 * N1h, 2 * N1), g_fwd=g_fwd, g_inv=g_inv)


def _stage_dtype(passes):
    return BF16 if passes == 1 else F32


def _dft1_body(x_ref, f_ref, o_ref, *, passes):
    P, K, NB, C = x_ref.shape
    y = _mm(f_ref[...], x_ref[...].reshape(P * K * NB, C), passes)
    o_ref[0] = y.astype(o_ref.dtype).reshape(o_ref.shape[1:])


def _dft_stage1(x, f, *, P, passes, out_dtype, col=0):
    Bx, K, N2, _ = x.shape
    C = HY_WIDTH
    N1 = f.shape[0] // (2 * DFT_NB)
    return pl.pallas_call(
        functools.partial(_dft1_body, passes=passes),
        grid=(Bx // P, N2 // DFT_NB),
        in_specs=[pl.BlockSpec((P, K, DFT_NB, C), lambda b, j: (b, 0, j, col)),
                  pl.BlockSpec(f.shape, lambda b, j: (0, 0), pipeline_mode=pl.Buffered(1))],
        out_specs=pl.BlockSpec((1, N1, 2, DFT_NB, C), lambda b, j: (b, 0, 0, j, 0)),
        out_shape=jax.ShapeDtypeStruct((Bx // P, N1, 2, N2, C), out_dtype),
        compiler_params=_params(("parallel", "parallel")),
        name="dft_stage1",
    )(x, f)


def _spec_body(a_ref, g_ref, o_ref, *, passes):
    def step(k, carry):
        o_ref[k] = _mm(g_ref[k], a_ref[0, k], passes)
        return carry

    lax.fori_loop(0, a_ref.shape[1], step, 0, unroll=DFT_UNROLL)


def _dft_spectrum(a, g, *, kb, passes):
    _, N1, R, C = a.shape
    return pl.pallas_call(
        functools.partial(_spec_body, passes=passes),
        grid=(N1 // kb,),
        in_specs=[pl.BlockSpec((1, kb, R, C), lambda i: (0, i, 0, 0)),
                  pl.BlockSpec((kb, R, R), lambda i: (i, 0, 0))],
        out_specs=pl.BlockSpec((kb, R, C), lambda i: (i, 0, 0)),
        out_shape=jax.ShapeDtypeStruct((N1, R, C), F32),
        compiler_params=_params(("parallel",)),
        name="dft_spectrum",
    )(a, g)


def _dft2_body(a_ref, gf_ref, gi_ref, ks_ref, o_ref, *, passes_fwd, passes_inv):
    H = a_ref.shape[2] // 2

    def step(k, carry):
        y = _mm(gf_ref[k], a_ref[0, k], passes_fwd)
        yr, yi = y[:H], y[H:]
        kr, ki = ks_ref[k, :H, :], ks_ref[k, H:, :]
        z = jnp.concatenate([yr * kr - yi * ki, yr * ki + yi * kr], axis=0)
        o_ref[0, k] = _mm(gi_ref[k], z, passes_inv).astype(o_ref.dtype)
        return carry

    lax.fori_loop(0, a_ref.shape[1], step, 0, unroll=DFT_UNROLL)


def _dft_stage2(a, gf, gi, ks, *, kb, passes_fwd, passes_inv, out_dtype):
    Bp, N1, R, C = a.shape
    dat = pl.BlockSpec((1, kb, R, C), lambda i, b: (b, i, 0, 0))
    mat = pl.BlockSpec((kb, R, R), lambda i, b: (i, 0, 0))
    return pl.pallas_call(
        functools.partial(_dft2_body, passes_fwd=passes_fwd, passes_inv=passes_inv),
        grid=(N1 // kb, Bp),
        in_specs=[dat, mat, mat, pl.BlockSpec((kb, R, C), lambda i, b: (i, 0, 0))],
        out_specs=dat,
        out_shape=jax.ShapeDtypeStruct((Bp, N1, R, C), out_dtype),
        compiler_params=_params(("parallel", "parallel")),
        name="dft_stage2",
    )(a, gf, gi, ks)


def _dft3_body(t_ref, f_ref, zf_ref, x0_ref, bias_ref, o_ref, *, passes):
    _, N1, _, NB, C = t_ref.shape
    y = _mm(f_ref[...], t_ref[0].reshape(N1 * 2 * NB, C), passes).reshape(zf_ref.shape)
    o_ref[...] = ((y + zf_ref[...].astype(F32) * bias_ref[...]) * x0_ref[...].astype(F32)).astype(o_ref.dtype)


def _dft_stage3(t, f, proj4, bias, *, passes):
    Bp, N1, _, N2, C = t.shape
    B, N1h = proj4.shape[:2]
    pair = lambda col=0: pl.BlockSpec((2, N1h, DFT_NB, C), lambda b, j: (b, 0, j, col))
    return pl.pallas_call(
        functools.partial(_dft3_body, passes=passes),
        grid=(Bp, N2 // DFT_NB),
        in_specs=[pl.BlockSpec((1, N1, 2, DFT_NB, C), lambda b, j: (b, 0, 0, j, 0)),
                  pl.BlockSpec(f.shape, lambda b, j: (0, 0), pipeline_mode=pl.Buffered(1)),
                  pair(COL_ZF // C), pair(COL_X0C // C),
                  pl.BlockSpec((1, C), lambda b, j: (0, 0))],
        out_specs=pair(),
        out_shape=jax.ShapeDtypeStruct((B, N1h, N2, C), BF16),
        compiler_params=_params(("parallel", "parallel")),
        name="dft_stage3",
    )(t, f, proj4, proj4, bias)


def _hyena_mixer(proj, kspec, bias, tabs):
    B, L, _ = proj.shape
    N1, N1h, N2 = tabs["N1"], tabs["N1h"], tabs["N2"]
    C = HY_WIDTH
    ps = DFT_PASSES
    proj4 = proj.reshape(B, N1h, N2, IN_COLS)
    f1 = tabs["f1_pair"].astype(_stage_dtype(ps["s1"]))
    a = _dft_stage1(proj4, f1, P=2, passes=ps["s1"], out_dtype=_stage_dtype(ps["s2f"]), col=COL_ZF // C)
    gdt = _stage_dtype(min(ps["s2f"], ps["s2i"]))
    t = _dft_stage2(a.reshape(B // 2, N1, 2 * N2, C), tabs["g_fwd"].astype(gdt), tabs["g_inv"].astype(gdt),
                    kspec, kb=DFT_KB, passes_fwd=ps["s2f"], passes_inv=ps["s2i"],
                    out_dtype=_stage_dtype(ps["s3"]))
    f3 = tabs["f3_pair"].astype(_stage_dtype(ps["s3"]))
    y = _dft_stage3(t.reshape(B // 2, N1, 2, N2, C), f3, proj4, bias.astype(F32).reshape(1, C), passes=ps["s3"])
    return y.reshape(B * L, C)


def _hyena_spectrum(feats, w1, b1, w2, b2, w3, L, tabs):
    N1, N2 = tabs["N1"], tabs["N2"]
    C = HY_WIDTH
    ps = DFT_PASSES
    kfull = _hyena_filter(feats, w1, b1, w2, b2, w3, L)
    a = _dft_stage1(kfull.reshape(1, N1, N2, C), tabs["f1_real"].astype(_stage_dtype(ps["k1"])), P=1, passes=ps["k1"],
                    out_dtype=_stage_dtype(ps["k2"]))
    return _dft_spectrum(a.reshape(1, N1, 2 * N2, C), tabs["g_fwd"].astype(_stage_dtype(ps["k2"])), kb=DFT_KB,
                         passes=ps["k2"])


def _ret_body(lg_ref, q_ref, k_ref, v_ref, g_ref, o_ref, st_ref):
    C = RET_BLOCK
    L = q_ref.shape[1]
    nblk = L // C
    dk = RET_DK
    h = pl.program_id(1)
    lgf = lg_ref[0, h]
    lgb = lg_ref[1, h]
    ii = lax.broadcasted_iota(jnp.int32, (C, C), 0)
    jj = lax.broadcasted_iota(jnp.int32, (C, C), 1)
    diff = (ii - jj).astype(F32)
    dm = jnp.where(diff >= 0, jnp.exp(jnp.maximum(diff, 0.0) * lgf), jnp.exp(jnp.maximum(-diff, 0.0) * lgb))
    pos = lax.broadcasted_iota(jnp.int32, (C, dk), 0).astype(F32)
    qw_f = jnp.exp((pos + 1.0) * lgf)
    kw_f = jnp.exp((C - 1.0 - pos) * lgf)
    qw_b = jnp.exp((C - pos) * lgb)
    kw_b = jnp.exp(pos * lgb)
    cd_f = jnp.exp(C * lgf)
    cd_b = jnp.exp(C * lgb)

    def block_kv(n, carry):
        sl = pl.ds(pl.multiple_of(n * C, C), C)
        k = k_ref[0, sl, :].astype(F32)
        kk = jnp.concatenate([k * kw_f, k * kw_b], axis=1).astype(BF16)
        st_ref[n] = _dot_tn(kk, v_ref[0, sl, :].astype(BF16))
        return carry

    lax.fori_loop(0, nblk, block_kv, 0, unroll=RET_UNROLL)

    def scan_f(n, S):
        kv = st_ref[n, :dk, :]
        st_ref[n, :dk, :] = S
        return S * cd_f + kv

    lax.fori_loop(0, nblk, scan_f, jnp.zeros((dk, RET_DV), F32))

    def scan_b(m, S):
        n = nblk - 1 - m
        kv = st_ref[n, dk:, :]
        st_ref[n, dk:, :] = S
        return S * cd_b + kv

    lax.fori_loop(0, nblk, scan_b, jnp.zeros((dk, RET_DV), F32))

    def block_out(n, carry):
        sl = pl.ds(pl.multiple_of(n * C, C), C)
        q = q_ref[0, sl, :].astype(BF16)
        scores = _dot_nt(q, k_ref[0, sl, :].astype(BF16)) * dm
        qf = q.astype(F32)
        qq = jnp.concatenate([qf * qw_f, qf * qw_b], axis=1).astype(BF16)
        o = _dot(scores.astype(BF16), v_ref[0, sl, :].astype(BF16)) + _dot(qq, st_ref[n].astype(BF16))
        mu = jnp.mean(o, axis=-1, keepdims=True)
        oc = o - mu
        on = oc * lax.rsqrt(jnp.mean(oc * oc, axis=-1, keepdims=True) + EPS)
        g = g_ref[0, sl, :].astype(F32)
        o_ref[0, sl, :] = (g * _sigmoid(g) * on).astype(o_ref.dtype)
        return carry

    lax.fori_loop(0, nblk, block_out, 0, unroll=RET_UNROLL)


def _retention_mixer(proj, log_gamma):
    B, L, _ = proj.shape
    blk = lambda col: pl.BlockSpec((1, L, LANES), lambda b, h, c=col // LANES: (b, 0, c + h))
    y = pl.pallas_call(
        _ret_body,
        grid=(B, RET_HEADS),
        in_specs=[pl.BlockSpec(memory_space=pltpu.SMEM),
                  blk(COL_RQ), blk(COL_RK), blk(COL_RV), blk(COL_RG)],
        out_specs=pl.BlockSpec((1, L, LANES), lambda b, h: (b, 0, h)),
        out_shape=jax.ShapeDtypeStruct((B, L, RET_WIDTH), BF16),
        scratch_shapes=[pltpu.VMEM((L // RET_BLOCK, 2 * RET_DK, RET_DV), F32)],
        compiler_params=_params(("parallel", "parallel")),
        name="retention",
    )(log_gamma, proj, proj, proj, proj)
    return y.reshape(B * L, RET_WIDTH)


def _att_body(q_ref, k_ref, v_ref, o_ref):
    G = ATT_HEADS // ATT_KV_HEADS
    tq = q_ref.shape[1]
    kc = min(ATT_KEY_CHUNK, k_ref.shape[1])
    nchunk = k_ref.shape[1] // kc
    q = q_ref[0].astype(BF16)
    qs = jnp.concatenate([q[:, g * ATT_HD:(g + 1) * ATT_HD] for g in range(G)], axis=0)
    m = jnp.full((G * tq, 1), -jnp.inf, F32)
    l = jnp.zeros((G * tq, 1), F32)
    acc = jnp.zeros((G * tq, ATT_HD), F32)
    s_next = _dot_nt(qs, k_ref[0, 0:kc, :].astype(BF16))
    for c in range(nchunk):
        s = s_next
        if c + 1 < nchunk:
            s_next = _dot_nt(qs, k_ref[0, (c + 1) * kc:(c + 2) * kc, :].astype(BF16))
        m_new = jnp.maximum(m, jnp.max(s, axis=-1, keepdims=True))
        alpha = jnp.exp2(m - m_new)
        p = jnp.exp2(s - m_new)
        l = alpha * l + jnp.sum(p, axis=-1, keepdims=True)
        acc = alpha * acc + _dot(p.astype(BF16), v_ref[0, c * kc:(c + 1) * kc, :].astype(BF16))
        m = m_new
    o = acc * (1.0 / l)
    for g in range(G):
        o_ref[0, :, g * ATT_HD:(g + 1) * ATT_HD] = o[g * tq:(g + 1) * tq].astype(o_ref.dtype)


def _attention_mixer(proj, *, tq):
    B, L, _ = proj.shape
    G = ATT_HEADS // ATT_KV_HEADS
    y = pl.pallas_call(
        _att_body,
        grid=(B, ATT_KV_HEADS, L // tq),
        in_specs=[pl.BlockSpec((1, tq, G * ATT_HD), lambda b, h, i: (b, i, COL_AQ // (G * ATT_HD) + h)),
                  pl.BlockSpec((1, L, ATT_HD), lambda b, h, i: (b, 0, COL_AK // ATT_HD + h)),
                  pl.BlockSpec((1, L, ATT_HD), lambda b, h, i: (b, 0, COL_AV // ATT_HD + h))],
        out_specs=pl.BlockSpec((1, tq, G * ATT_HD), lambda b, h, i: (b, i, h)),
        out_shape=jax.ShapeDtypeStruct((B, L, ATT_WIDTH), BF16),
        compiler_params=_params(("parallel", "parallel", "arbitrary")),
        name="attention",
    )(proj, proj, proj)
    return y.reshape(B * L, ATT_WIDTH)


def _tail_body(yh_ref, yr_ref, ya_ref, ga_ref, gb_ref, x_ref, kv_ref, wb_ref, wo_ref, gmix_ref,
               gxpre_ref, wq_ref, wxo_ref, gxpost_ref, gfpre_ref, w1_ref, w2_ref, gfpost_ref, o_ref, u_ref):
    D = D_MODEL
    ph = _dot(yh_ref[0], wb_ref[0:HY_WIDTH, :])
    pr = _dot(yr_ref[0], wb_ref[HY_WIDTH:HY_WIDTH + RET_WIDTH, :])
    pa = _dot(ya_ref[0], wb_ref[HY_WIDTH + RET_WIDTH:, :])
    ga = ga_ref[0].astype(F32)
    gb = gb_ref[0].astype(F32)
    hd = D // 2
    g0 = _sigmoid(ga[:, :D])
    g1 = _sigmoid(jnp.concatenate([ga[:, D:], gb[:, :hd]], axis=1))
    g2 = _sigmoid(gb[:, hd:])
    merged = g0 * ph + g1 * pr + g2 * pa
    x = x_ref[0] + _rms(_dot(merged.astype(BF16), wo_ref[...]), gmix_ref[...])
    h = _rms(x, gxpre_ref[...]).astype(BF16)
    q = (_dot(h, wq_ref[...]) * (X_HD ** -0.5)).astype(BF16)
    kv = kv_ref[0]
    outs = []
    for hh in range(X_HEADS):
        sl = slice(hh * X_HD, (hh + 1) * X_HD)
        s = _dot_nt(q[:, sl], kv[:, sl])
        p = jnp.exp(s - jnp.max(s, axis=-1, keepdims=True))
        l = jnp.sum(p, axis=-1, keepdims=True)
        vh = kv[:, D + hh * X_HD:D + (hh + 1) * X_HD]
        outs.append(_dot(p.astype(BF16), vh) * (1.0 / l))
    o = jnp.concatenate(outs, axis=1).astype(BF16)
    x = x + _rms(_dot(o, wxo_ref[...]), gxpost_ref[...])
    h = _rms(x, gfpre_ref[...]).astype(BF16)
    for c in range(D_FF // FF_CHUNK):
        sl = slice(c * FF_CHUNK, (c + 1) * FF_CHUNK)
        u_ref[:, sl] = jnp.square(jnp.maximum(_dot(h, w1_ref[:, sl]), 0.0)).astype(BF16)
    o_ref[0] = x + _rms(_dot(u_ref[...], w2_ref[...]), gfpost_ref[...])


def _layer_tail(yh, yr, ya, proj, x, kv, w, l, *, tm):
    B, L, D = x.shape
    M = kv.shape[1]
    gw = N_BRANCH * D // 2
    tok = lambda width, col=0: pl.BlockSpec((1, tm, width), lambda b, i, col=col: (b, i, col))
    vec = pl.BlockSpec((1, D), lambda b, i: (0, 0))
    res = lambda shape: pl.BlockSpec((None,) + shape, lambda b, i: (l, 0, 0), pipeline_mode=pl.Buffered(1))
    g = lambda name: w[name][l].reshape(1, D)
    return pl.pallas_call(
        _tail_body,
        grid=(B, L // tm),
        in_specs=[tok(HY_WIDTH), tok(RET_WIDTH), tok(ATT_WIDTH), tok(gw, COL_GATE // gw), tok(gw, COL_GATE // gw + 1),
                  tok(D), pl.BlockSpec((1, M, 2 * D), lambda b, i: (b, 0, 0)),
                  res((MIX_WIDTH, D)), res((D, D)), vec,
                  vec, res((D, D)), res((D, D)), vec,
                  vec, res((D, D_FF)), res((D_FF, D)), vec],
        out_specs=tok(D),
        out_shape=jax.ShapeDtypeStruct((B, L, D), F32),
        scratch_shapes=[pltpu.VMEM((tm, D_FF), BF16)],
        compiler_params=_params(("parallel", "parallel")),
        name="layer_tail",
    )(yh, yr, ya, proj, proj, x, kv, w["w_branch"], w["w_out"], g("g_mix_post"),
      g("g_x_pre"), w["w_xq"], w["w_xo"], g("g_x_post"),
      g("g_ff_pre"), w["w_ff1"], w["w_ff2"], g("g_ff_post"))


def _rotary_tables(pos, dim, theta):
    inv = theta ** (-jnp.arange(0, dim, 2, dtype=F32) / dim)
    ang = pos.astype(F32)[:, None] * inv[None, :]
    return jnp.cos(ang), jnp.sin(ang)


def _position_tables(L):
    pos = jnp.arange(L)
    rows = L // GRID_W
    row_ids = jnp.repeat(jnp.arange(rows), GRID_W)
    col_ids = jnp.tile(jnp.arange(GRID_W), rows)
    rcos, rsin = _rotary_tables(pos, RET_DK, RET_THETA)
    rc, rs = _rotary_tables(row_ids, ATT_HD // 2, ROPE_THETA)
    cc, cs = _rotary_tables(col_ids, ATT_HD // 2, ROPE_THETA)
    return dict(
        ret_cos=jnp.concatenate([rcos, rcos], axis=-1),
        ret_sin=jnp.concatenate([-rsin, rsin], axis=-1),
        att_cos=jnp.concatenate([rc, rc, cc, cc], axis=-1),
        att_sin=jnp.concatenate([-rs, rs, -cs, cs], axis=-1),
        feats=_hyena_features(L),
        dft=_dft_tables(L),
    )


def _trunk(x, mem, w, pt):
    B, L, D = x.shape
    M = mem.shape[1]
    mem2 = mem.reshape(B * M, D)
    for l in range(DEPTH):
        proj = _input_projection(x.reshape(B * L, D), w["g_mix_pre"][l], w["w_in"], l, w["hy_conv"][l],
                                 w["att_qnorm"][l], w["att_knorm"][l], pt, L, tm=TOKEN_TILE).reshape(B, L, IN_COLS)
        kspec = _hyena_spectrum(pt["feats"], w["hy_fw1"][l], w["hy_fb1"][l], w["hy_fw2"][l],
                                w["hy_fb2"][l], w["hy_fw3"][l], L, pt["dft"])
        y_h = _hyena_mixer(proj, kspec, w["hy_bias"][l], pt["dft"])
        y_r = _retention_mixer(proj, w["ret_lg"][l])
        y_a = _attention_mixer(proj, tq=TOKEN_TILE)
        kv = _norm_matmul(mem2, w["g_mem"][l], w["w_xkv"][l], tm=TOKEN_TILE, tn=FF_CHUNK).reshape(B, M, 2 * D)
        x = _layer_tail(y_h.reshape(B, L, HY_WIDTH), y_r.reshape(B, L, RET_WIDTH), y_a.reshape(B, L, ATT_WIDTH),
                        proj, x, kv, w, l, tm=TOKEN_TILE)
    return x


def kernel(x_prompt, x_sample, mem_prompt, mem_sample, g_mix_pre, g_mix_post, w_in, hy_conv, hy_fw1, hy_fb1,
           hy_fw2, hy_fb2, hy_fw3, hy_bias, ret_decay, att_qnorm, att_knorm, w_branch, w_out, g_x_pre, g_x_post,
           g_mem, w_xq, w_xkv, w_xo, g_ff_pre, g_ff_post, w_ff1, w_ff2):
    w = dict(
        g_mix_pre=g_mix_pre, g_mix_post=g_mix_post, w_in=w_in.astype(BF16), hy_conv=hy_conv,
        hy_fw1=hy_fw1, hy_fb1=hy_fb1, hy_fw2=hy_fw2, hy_fb2=hy_fb2, hy_fw3=hy_fw3, hy_bias=hy_bias,
        ret_lg=jax.nn.log_sigmoid(ret_decay.astype(F32)), att_qnorm=att_qnorm, att_knorm=att_knorm,
        w_branch=w_branch.astype(BF16), w_out=w_out.astype(BF16), g_x_pre=g_x_pre, g_x_post=g_x_post,
        g_mem=g_mem, w_xq=w_xq.astype(BF16), w_xkv=w_xkv.astype(BF16), w_xo=w_xo.astype(BF16),
        g_ff_pre=g_ff_pre, g_ff_post=g_ff_post, w_ff1=w_ff1.astype(BF16), w_ff2=w_ff2.astype(BF16),
    )
    y_prompt = _trunk(x_prompt, mem_prompt, w, _position_tables(x_prompt.shape[1]))
    y_sample = _trunk(x_sample, mem_sample, w, _position_tables(x_sample.shape[1]))
    return (y_prompt, y_sample)
```

```python
import functools
import math

import jax
import jax.numpy as jnp
from jax import lax
from jax.experimental import pallas as pl
from jax.experimental.pallas import tpu as pltpu

F32 = jnp.float32
BF16 = jnp.bfloat16

D_MODEL = 1024
DEPTH = 4
GRID_W = 64
EPS = 1e-6
HY_WIDTH = 512
HY_SHORT = 3
HY_EMB = 33
HY_FH = 64
HY_SIN_FREQ = 1.0
HY_TARGET = 1e-2
HY_FAST = 0.3
HY_SLOW = 1.5
HY_SHIFT = 0.0
RET_HEADS = 4
RET_DK = 128
RET_DV = 128
RET_WIDTH = RET_HEADS * RET_DV
RET_THETA = 10000.0
ATT_HEADS = 4
ATT_KV_HEADS = 2
ATT_HD = 128
ATT_WIDTH = ATT_HEADS * ATT_HD
ROPE_THETA = 10000.0
X_HEADS = 4
X_HD = D_MODEL // X_HEADS
D_FF = 4 * D_MODEL
N_BRANCH = 3
MIX_WIDTH = HY_WIDTH + RET_WIDTH + ATT_WIDTH

COL_HY = 0
COL_X0C = 0
COL_ZF = HY_WIDTH
COL_RQ = 3 * HY_WIDTH
COL_RK = COL_RQ + RET_HEADS * RET_DK
COL_RV = COL_RK + RET_HEADS * RET_DK
COL_RG = COL_RV + RET_WIDTH
COL_AQ = COL_RG + RET_WIDTH
COL_AK = COL_AQ + ATT_WIDTH
COL_AV = COL_AK + ATT_KV_HEADS * ATT_HD
COL_GATE = COL_AV + ATT_KV_HEADS * ATT_HD
IN_COLS = COL_GATE + N_BRANCH * D_MODEL

LANES = 128
HALO = 8
VMEM_LIMIT = 56 * 1024 * 1024
DFT_N2 = 128
DFT_NB = 16
DFT_KB = 8
DFT_UNROLL = 4
DFT_PASSES = dict(s1=1, s2f=1, s2i=1, s3=1, k1=1, k2=1)
RET_BLOCK = 256
RET_UNROLL = 8
ATT_KEY_CHUNK = 1024
ATT_QUERY_TILE = 1024
FILTER_PASSES = 1
FF_CHUNK = 1024
TOKEN_TILE = 512
LOG2E = 1.4426950408889634


def _params(sem):
    return pltpu.CompilerParams(dimension_semantics=sem, vmem_limit_bytes=VMEM_LIMIT)


def _dot(a, b):
    return jnp.dot(a, b, preferred_element_type=F32)


def _dot_nt(a, b):
    return lax.dot_general(a, b, (((1,), (1,)), ((), ())), preferred_element_type=F32)


def _dot_tn(a, b):
    return lax.dot_general(a, b, (((0,), (0,)), ((), ())), preferred_element_type=F32)


def _rms(x, g):
    ms = jnp.mean(x * x, axis=-1, keepdims=True)
    return x * lax.rsqrt(ms + EPS) * g


def _sigmoid(x):
    return 1.0 / (1.0 + jnp.exp(-x))


def _split(x):
    hi = x.astype(BF16)
    lo = (x - hi.astype(F32)).astype(BF16)
    return hi, lo


def _dot3(a, b):
    return _dot(a[0], b[0]) + (_dot(a[0], b[1]) + _dot(a[1], b[0]))


def _mm(a, b, passes):
    if passes == 1:
        return _dot(a.astype(BF16), b.astype(BF16))
    return _dot3(_split(a.astype(F32)), _split(b.astype(F32)))


def _nm_body(x_ref, g_ref, w_ref, o_ref, *, tn):
    h = _rms(x_ref[...], g_ref[...]).astype(BF16)
    for j in range(w_ref.shape[1] // tn):
        o_ref[:, j * tn:(j + 1) * tn] = _dot(h, w_ref[:, j * tn:(j + 1) * tn]).astype(o_ref.dtype)


def _norm_matmul(x, g, w, *, tm, tn):
    T, D = x.shape
    N = w.shape[1]
    return pl.pallas_call(
        functools.partial(_nm_body, tn=tn),
        grid=(T // tm,),
        in_specs=[pl.BlockSpec((tm, D), lambda i: (i, 0)),
                  pl.BlockSpec((1, D), lambda i: (0, 0)),
                  pl.BlockSpec((D, N), lambda i: (0, 0), pipeline_mode=pl.Buffered(1))],
        out_specs=pl.BlockSpec((tm, N), lambda i: (i, 0)),
        out_shape=jax.ShapeDtypeStruct((T, N), BF16),
        compiler_params=_params(("parallel",)),
        name="norm_matmul",
    )(x, g.reshape(1, D), w)


def _rotate_half(x, cos, sin):
    return x * cos + pltpu.roll(x, x.shape[-1] // 2, 1) * sin


def _axial_rope(x, gain, cos, sin, scale):
    x = _rms(x, gain)
    lane = lax.broadcasted_iota(jnp.int32, x.shape, 1)
    lower = (lane % (ATT_HD // 2)) < (ATT_HD // 4)
    swapped = jnp.where(lower, pltpu.roll(x, LANES - ATT_HD // 4, 1), pltpu.roll(x, ATT_HD // 4, 1))
    return (x * cos + swapped * sin) * scale


_WIN_SEGMENTS = (
    [(COL_RQ, RET_WIDTH, "ret_q"), (COL_RK, RET_WIDTH, "ret_k"), (COL_RV, RET_WIDTH, None),
       (COL_RG, RET_WIDTH, None), (COL_AQ, ATT_WIDTH, "att_q"), (COL_AK, COL_GATE - COL_AK, "att_kv")]
    + [(COL_GATE + i * D_MODEL, D_MODEL, None) for i in range(N_BRANCH)])


def _win_body(x_ref, xp_ref, xn_ref, g_ref, w_ref, cw_ref, qn_ref, kn_ref, rc_ref, rs_ref, ac_ref, as_ref, o_ref,
              *, per_seq):
    tm = x_ref.shape[0]
    pos = pl.program_id(0) % per_seq
    x_ext = jnp.concatenate([xp_ref[...], x_ref[...], xn_ref[...]], axis=0)
    r_ext = lax.broadcasted_iota(jnp.int32, (tm + 2 * HALO, 1), 0)
    pad = ((r_ext < HALO) & (pos == 0)) | ((r_ext >= tm + HALO) & (pos == per_seq - 1))
    hf = jnp.where(pad, 0.0, _rms(x_ext, g_ref[...]))
    h_ext = hf.astype(BF16)
    h = hf[HALO:HALO + tm].astype(BF16)
    row = lax.broadcasted_iota(jnp.int32, (tm, HY_WIDTH), 0)
    conv = []
    for grp in range(3):
        cols = slice(COL_HY + grp * HY_WIDTH, COL_HY + (grp + 1) * HY_WIDTH)
        y_ext = _dot(h_ext, w_ref[:, cols])
        y = y_ext[HALO:HALO + tm]
        y_prev = y_ext[HALO - 1:HALO]
        y_next = y_ext[HALO + tm:HALO + tm + 1]
        prev = jnp.where(row == 0, y_prev, pltpu.roll(y, 1, 0))
        nxt = jnp.where(row == tm - 1, y_next, pltpu.roll(y, tm - 1, 0))
        cw = cw_ref[:, cols]
        conv.append(prev * cw[0:1] + y * cw[1:2] + nxt * cw[2:3])
    x0c, x1c, vc = conv
    o_ref[:, COL_X0C:COL_X0C + HY_WIDTH] = x0c.astype(o_ref.dtype)
    o_ref[:, COL_ZF:COL_ZF + HY_WIDTH] = (vc * x1c).astype(o_ref.dtype)
    o_ref[:, COL_ZF + HY_WIDTH:COL_RQ] = vc.astype(o_ref.dtype)
    for start, width, kind in _WIN_SEGMENTS:
        y = _dot(h, w_ref[:, start:start + width])
        if kind is None:
            o_ref[:, start:start + width] = y.astype(o_ref.dtype)
            continue
        for hh in range(width // LANES):
            yh = y[:, hh * LANES:(hh + 1) * LANES]
            if kind == "ret_q":
                yh = _rotate_half(yh, rc_ref[...], rs_ref[...])
            elif kind == "ret_k":
                yh = _rotate_half(yh, rc_ref[...], rs_ref[...]) * (RET_DK ** -0.5)
            elif kind == "att_q":
                yh = _axial_rope(yh, qn_ref[...], ac_ref[...], as_ref[...], ATT_HD ** -0.5 * LOG2E)
            elif hh < ATT_KV_HEADS:
                yh = _axial_rope(yh, kn_ref[...], ac_ref[...], as_ref[...], 1.0)
            o_ref[:, start + hh * LANES:start + (hh + 1) * LANES] = yh.astype(o_ref.dtype)


def _input_projection(x, g, w_stack, l, conv_w, qn, kn, pt, L, *, tm):
    T, D = x.shape
    N = w_stack.shape[2]
    per_seq = L // tm
    halo_per_tile = tm // HALO
    vec = lambda n: pl.BlockSpec((1, n), lambda i: (0, 0))
    tab = pl.BlockSpec((tm, LANES), lambda i: (i % per_seq, 0))
    return pl.pallas_call(
        functools.partial(_win_body, per_seq=per_seq),
        grid=(T // tm,),
        in_specs=[pl.BlockSpec((tm, D), lambda i: (i, 0)),
                  pl.BlockSpec((HALO, D), lambda i: (jnp.maximum(i * halo_per_tile - 1, 0), 0)),
                  pl.BlockSpec((HALO, D), lambda i: (jnp.minimum((i + 1) * halo_per_tile, T // HALO - 1), 0)),
                  vec(D),
                  pl.BlockSpec((None, D, N), lambda i: (l, 0, 0), pipeline_mode=pl.Buffered(1)),
                  pl.BlockSpec((HY_SHORT, 3 * HY_WIDTH), lambda i: (0, 0)),
                  vec(ATT_HD), vec(ATT_HD), tab, tab, tab, tab],
        out_specs=pl.BlockSpec((tm, N), lambda i: (i, 0)),
        out_shape=jax.ShapeDtypeStruct((T, N), BF16),
        compiler_params=_params(("parallel",)),
        name="input_projection",
    )(x, x, x, g.reshape(1, D), w_stack, conv_w, qn.reshape(1, ATT_HD), kn.reshape(1, ATT_HD),
      pt["ret_cos"], pt["ret_sin"], pt["att_cos"], pt["att_sin"])


def _filt_body(f_ref, w1_ref, b1_ref, w2_ref, b2_ref, w3_ref, dl_ref, o_ref, *, L):
    tl = f_ref.shape[0]
    f = f_ref[...]
    h = jnp.sin(HY_SIN_FREQ * (_mm(f, w1_ref[...], 3) + b1_ref[...]))
    h = jnp.sin(HY_SIN_FREQ * (_mm(h, w2_ref[...], FILTER_PASSES) + b2_ref[...]))
    filt = _mm(h, w3_ref[...], FILTER_PASSES)
    mod = jnp.exp(-f[:, 0:1] * dl_ref[...]) + HY_SHIFT
    row = pl.program_id(0) * tl + lax.broadcasted_iota(jnp.int32, (tl, HY_WIDTH), 0)
    sel = jnp.where(row < L, filt[:, :HY_WIDTH], filt[:, HY_WIDTH:])
    o_ref[...] = jnp.where(row == L, 0.0, sel * mod)


def _hyena_features(L):
    t = jnp.linspace(0.0, 1.0, L, dtype=F32)[:, None]
    bands = (HY_EMB - 1) // 2
    w = 2.0 * math.pi * jnp.arange(L, dtype=F32)[:, None] / L
    f = jnp.linspace(1e-4, bands - 1, bands, dtype=F32)[None, :]
    z = jnp.concatenate([t, jnp.cos(f * w), -jnp.sin(f * w)], axis=-1)
    zc = jnp.concatenate([z, z[:1], jnp.flip(z[1:], axis=0)], axis=0)
    return jnp.pad(zc, ((0, 0), (0, LANES - HY_EMB)))


def _hyena_filter(feats, w1, b1, w2, b2, w3, L):
    tl = TOKEN_TILE
    pad = LANES - HY_FH
    w1p = jnp.pad(w1, ((0, LANES - HY_EMB), (0, pad)))
    b1p = jnp.pad(b1, (0, pad)).reshape(1, LANES)
    w2p = jnp.pad(w2, ((0, pad), (0, pad)))
    b2p = jnp.pad(b2, (0, pad)).reshape(1, LANES)
    w3p = jnp.pad(w3, ((0, pad), (0, 0)))
    min_decay = math.log(HY_TARGET) / HY_SLOW
    max_decay = math.log(HY_TARGET) / HY_FAST
    deltas = jnp.abs(jnp.linspace(min_decay, max_decay, HY_WIDTH, dtype=F32)).reshape(1, HY_WIDTH)
    full = lambda shape: pl.BlockSpec(shape, lambda i: (0,) * len(shape))
    return pl.pallas_call(
        functools.partial(_filt_body, L=L),
        grid=(2 * L // tl,),
        in_specs=[pl.BlockSpec((tl, LANES), lambda i: (i, 0)),
                  full((LANES, LANES)), full((1, LANES)), full((LANES, LANES)), full((1, LANES)),
                  full((LANES, 2 * HY_WIDTH)), full((1, HY_WIDTH))],
        out_specs=pl.BlockSpec((tl, HY_WIDTH), lambda i: (i, 0)),
        out_shape=jax.ShapeDtypeStruct((2 * L, HY_WIDTH), F32),
        compiler_params=_params(("parallel",)),
        name="hyena_filter",
    )(feats, w1p, b1p, w2p, b2p, w3p, deltas)


def _expanded(fn, rows, cols):
    small = fn(lax.broadcasted_iota(jnp.int32, (rows, cols), 0), lax.broadcasted_iota(jnp.int32, (rows, cols), 1))
    hp = functools.partial(jnp.dot, precision=lax.Precision.HIGHEST)
    pick_r = (lax.broadcasted_iota(jnp.int32, (rows * DFT_NB, rows), 0) // DFT_NB
              == lax.broadcasted_iota(jnp.int32, (rows * DFT_NB, rows), 1)).astype(F32)
    pick_c = (lax.broadcasted_iota(jnp.int32, (cols, cols * DFT_NB), 1) // DFT_NB
              == lax.broadcasted_iota(jnp.int32, (cols, cols * DFT_NB), 0)).astype(F32)
    ri = lax.broadcasted_iota(jnp.int32, (rows * DFT_NB, cols * DFT_NB), 0)
    ci = lax.broadcasted_iota(jnp.int32, (rows * DFT_NB, cols * DFT_NB), 1)
    return jnp.where(ri % DFT_NB == ci % DFT_NB, hp(hp(pick_r, small), pick_c), 0.0)


def _dft_tables(L):
    N = 2 * L
    N2 = DFT_N2
    N1 = N // N2
    N1h = N1 // 2

    def cs1(k1, n1):
        ang = ((k1 * n1) % N1).astype(F32) * (2.0 * math.pi / N1)
        return jnp.cos(ang), jnp.sin(ang)

    def f1_pair(r, c):
        cs, sn = cs1(r // 2, c % N1h)
        re_row, first = (r % 2) == 0, c < N1h
        return jnp.where(re_row, jnp.where(first, cs, sn), jnp.where(first, -sn, cs))

    def f1_real(r, c):
        cs, sn = cs1(r // 2, c)
        return jnp.where((r % 2) == 0, cs, -sn)

    def f3_pair(r, c):
        cs, sn = cs1(c // 2, r % N1h)
        re_out, re_in = r < N1h, (c % 2) == 0
        return jnp.where(re_out, jnp.where(re_in, cs, -sn), jnp.where(re_in, sn, cs)) * (1.0 / N)

    shape = (N1, 2 * N2, 2 * N2)
    k1g = lax.broadcasted_iota(jnp.int32, shape, 0)
    rg = lax.broadcasted_iota(jnp.int32, shape, 1)
    cg = lax.broadcasted_iota(jnp.int32, shape, 2)
    lower_left = (rg >= N2) & (cg < N2)
    on_diag = (rg >= N2) == (cg >= N2)

    def stacked(k2, n2, sign):
        ang = ((n2 * (k1g + N1 * k2)) % N).astype(F32) * (2.0 * math.pi / N)
        return jnp.where(on_diag, jnp.cos(ang), jnp.where(lower_left, -sign, sign) * jnp.sin(ang))

    g_fwd = stacked(rg % N2, cg % N2, 1.0)
    g_inv = stacked(cg % N2, rg % N2, -1.0)
    return dict(N1=N1, N1h=N1h, N2=N2,
                f1_pair=_expanded(f1_pair, 2 * N1, 2 * N1h), f1_real=_expanded(f1_real, 2 * N1, N1),
                f3_pair=_expanded(f3_pair, 2 * N1h, 2 * N1), g_fwd=g_fwd, g_inv=g_inv)


def _stage_dtype(passes):
    return BF16 if passes == 1 else F32


def _dft1_body(x_ref, f_ref, o_ref, *, passes):
    P, K, NB, C = x_ref.shape
    y = _mm(f_ref[...], x_ref[...].reshape(P * K * NB, C), passes)
    o_ref[0] = y.astype(o_ref.dtype).reshape(o_ref.shape[1:])


def _dft_stage1(x, f, *, P, passes, out_dtype, col=0):
    Bx, K, N2, _ = x.shape
    C = HY_WIDTH
    N1 = f.shape[0] // (2 * DFT_NB)
    return pl.pallas_call(
        functools.partial(_dft1_body, passes=passes),
        grid=(Bx // P, N2 // DFT_NB),
        in_specs=[pl.BlockSpec((P, K, DFT_NB, C), lambda b, j: (b, 0, j, col)),
                  pl.BlockSpec(f.shape, lambda b, j: (0, 0), pipeline_mode=pl.Buffered(1))],
        out_specs=pl.BlockSpec((1, N1, 2, DFT_NB, C), lambda b, j: (b, 0, 0, j, 0)),
        out_shape=jax.ShapeDtypeStruct((Bx // P, N1, 2, N2, C), out_dtype),
        compiler_params=_params(("parallel", "parallel")),
        name="dft_stage1",
    )(x, f)


def _spec_body(a_ref, g_ref, o_ref, *, passes):
    def step(k, carry):
        o_ref[k] = _mm(g_ref[k], a_ref[0, k], passes)
        return carry

    lax.fori_loop(0, a_ref.shape[1], step, 0, unroll=DFT_UNROLL)


def _dft_spectrum(a, g, *, kb, passes):
    _, N1, R, C = a.shape
    return pl.pallas_call(
        functools.partial(_spec_body, passes=passes),
        grid=(N1 // kb,),
        in_specs=[pl.BlockSpec((1, kb, R, C), lambda i: (0, i, 0, 0)),
                  pl.BlockSpec((kb, R, R), lambda i: (i, 0, 0))],
        out_specs=pl.BlockSpec((kb, R, C), lambda i: (i, 0, 0)),
        out_shape=jax.ShapeDtypeStruct((N1, R, C), F32),
        compiler_params=_params(("parallel",)),
        name="dft_spectrum",
    )(a, g)


def _dft2_body(a_ref, gf_ref, gi_ref, ks_ref, o_ref, *, passes_fwd, passes_inv):
    H = a_ref.shape[2] // 2

    def step(k, carry):
        y = _mm(gf_ref[k], a_ref[0, k], passes_fwd)
        yr, yi = y[:H], y[H:]
        kr, ki = ks_ref[k, :H, :], ks_ref[k, H:, :]
        z = jnp.concatenate([yr * kr - yi * ki, yr * ki + yi * kr], axis=0)
        o_ref[0, k] = _mm(gi_ref[k], z, passes_inv).astype(o_ref.dtype)
        return carry

    lax.fori_loop(0, a_ref.shape[1], step, 0, unroll=DFT_UNROLL)


def _dft_stage2(a, gf, gi, ks, *, kb, passes_fwd, passes_inv, out_dtype):
    Bp, N1, R, C = a.shape
    dat = pl.BlockSpec((1, kb, R, C), lambda i, b: (b, i, 0, 0))
    mat = pl.BlockSpec((kb, R, R), lambda i, b: (i, 0, 0))
    return pl.pallas_call(
        functools.partial(_dft2_body, passes_fwd=passes_fwd, passes_inv=passes_inv),
        grid=(N1 // kb, Bp),
        in_specs=[dat, mat, mat, pl.BlockSpec((kb, R, C), lambda i, b: (i, 0, 0))],
        out_specs=dat,
        out_shape=jax.ShapeDtypeStruct((Bp, N1, R, C), out_dtype),
        compiler_params=_params(("parallel", "parallel")),
        name="dft_stage2",
    )(a, gf, gi, ks)


def _dft3_body(t_ref, f_ref, zf_ref, x0_ref, bias_ref, o_ref, *, passes):
    _, N1, _, NB, C = t_ref.shape
    y = _mm(f_ref[...], t_ref[0].reshape(N1 * 2 * NB, C), passes).reshape(zf_ref.shape)
    o_ref[...] = ((y + zf_ref[...].astype(F32) * bias_ref[...]) * x0_ref[...].astype(F32)).astype(o_ref.dtype)


def _dft_stage3(t, f, proj4, bias, *, passes):
    Bp, N1, _, N2, C = t.shape
    B, N1h = proj4.shape[:2]
    pair = lambda col=0: pl.BlockSpec((2, N1h, DFT_NB, C), lambda b, j: (b, 0, j, col))
    return pl.pallas_call(
        functools.partial(_dft3_body, passes=passes),
        grid=(Bp, N2 // DFT_NB),
        in_specs=[pl.BlockSpec((1, N1, 2, DFT_NB, C), lambda b, j: (b, 0, 0, j, 0)),
                  pl.BlockSpec(f.shape, lambda b, j: (0, 0), pipeline_mode=pl.Buffered(1)),
                  pair(COL_ZF // C), pair(COL_X0C // C),
                  pl.BlockSpec((1, C), lambda b, j: (0, 0))],
        out_specs=pair(),
        out_shape=jax.ShapeDtypeStruct((B, N1h, N2, C), BF16),
        compiler_params=_params(("parallel", "parallel")),
        name="dft_stage3",
    )(t, f, proj4, proj4, bias)


def _hyena_mixer(proj, kspec, bias, tabs):
    B, L, _ = proj.shape
    N1, N1h, N2 = tabs["N1"], tabs["N1h"], tabs["N2"]
    C = HY_WIDTH
    ps = DFT_PASSES
    proj4 = proj.reshape(B, N1h, N2, IN_COLS)
    f1 = tabs["f1_pair"].astype(_stage_dtype(ps["s1"]))
    a = _dft_stage1(proj4, f1, P=2, passes=ps["s1"], out_dtype=_stage_dtype(ps["s2f"]), col=COL_ZF // C)
    gdt = _stage_dtype(min(ps["s2f"], ps["s2i"]))
    t = _dft_stage2(a.reshape(B // 2, N1, 2 * N2, C), tabs["g_fwd"].astype(gdt), tabs["g_inv"].astype(gdt),
                    kspec, kb=DFT_KB, passes_fwd=ps["s2f"], passes_inv=ps["s2i"],
                    out_dtype=_stage_dtype(ps["s3"]))
    f3 = tabs["f3_pair"].astype(_stage_dtype(ps["s3"]))
    y = _dft_stage3(t.reshape(B // 2, N1, 2, N2, C), f3, proj4, bias.astype(F32).reshape(1, C), passes=ps["s3"])
    return y.reshape(B * L, C)


def _hyena_spectrum(feats, w1, b1, w2, b2, w3, L, tabs):
    N1, N2 = tabs["N1"], tabs["N2"]
    C = HY_WIDTH
    ps = DFT_PASSES
    kfull = _hyena_filter(feats, w1, b1, w2, b2, w3, L)
    a = _dft_stage1(kfull.reshape(1, N1, N2, C), tabs["f1_real"].astype(_stage_dtype(ps["k1"])), P=1, passes=ps["k1"],
                    out_dtype=_stage_dtype(ps["k2"]))
    return _dft_spectrum(a.reshape(1, N1, 2 * N2, C), tabs["g_fwd"].astype(_stage_dtype(ps["k2"])), kb=DFT_KB,
                         passes=ps["k2"])


def _ret_body(lg_ref, q_ref, k_ref, v_ref, g_ref, o_ref, st_ref):
    C = RET_BLOCK
    L = q_ref.shape[1]
    nblk = L // C
    dk = RET_DK
    h = pl.program_id(1)
    lgf = lg_ref[0, h]
    lgb = lg_ref[1, h]
    ii = lax.broadcasted_iota(jnp.int32, (C, C), 0)
    jj = lax.broadcasted_iota(jnp.int32, (C, C), 1)
    diff = (ii - jj).astype(F32)
    dm = jnp.where(diff >= 0, jnp.exp(jnp.maximum(diff, 0.0) * lgf), jnp.exp(jnp.maximum(-diff, 0.0) * lgb))
    pos = lax.broadcasted_iota(jnp.int32, (C, dk), 0).astype(F32)
    qw_f = jnp.exp((pos + 1.0) * lgf)
    kw_f = jnp.exp((C - 1.0 - pos) * lgf)
    qw_b = jnp.exp((C - pos) * lgb)
    kw_b = jnp.exp(pos * lgb)
    cd_f = jnp.exp(C * lgf)
    cd_b = jnp.exp(C * lgb)

    def block_kv(n, carry):
        sl = pl.ds(pl.multiple_of(n * C, C), C)
        k = k_ref[0, sl, :].astype(F32)
        kk = jnp.concatenate([k * kw_f, k * kw_b], axis=1).astype(BF16)
        st_ref[n] = _dot_tn(kk, v_ref[0, sl, :].astype(BF16))
        return carry

    lax.fori_loop(0, nblk, block_kv, 0, unroll=RET_UNROLL)

    def scan_f(n, S):
        kv = st_ref[n, :dk, :]
        st_ref[n, :dk, :] = S
        return S * cd_f + kv

    lax.fori_loop(0, nblk, scan_f, jnp.zeros((dk, RET_DV), F32))

    def scan_b(m, S):
        n = nblk - 1 - m
        kv = st_ref[n, dk:, :]
        st_ref[n, dk:, :] = S
        return S * cd_b + kv

    lax.fori_loop(0, nblk, scan_b, jnp.zeros((dk, RET_DV), F32))

    def block_out(n, carry):
        sl = pl.ds(pl.multiple_of(n * C, C), C)
        q = q_ref[0, sl, :].astype(BF16)
        scores = _dot_nt(q, k_ref[0, sl, :].astype(BF16)) * dm
        qf = q.astype(F32)
        qq = jnp.concatenate([qf * qw_f, qf * qw_b], axis=1).astype(BF16)
        o = _dot(scores.astype(BF16), v_ref[0, sl, :].astype(BF16)) + _dot(qq, st_ref[n].astype(BF16))
        mu = jnp.mean(o, axis=-1, keepdims=True)
        oc = o - mu
        on = oc * lax.rsqrt(jnp.mean(oc * oc, axis=-1, keepdims=True) + EPS)
        g = g_ref[0, sl, :].astype(F32)
        o_ref[0, sl, :] = (g * _sigmoid(g) * on).astype(o_ref.dtype)
        return carry

    lax.fori_loop(0, nblk, block_out, 0, unroll=RET_UNROLL)


def _retention_mixer(proj, log_gamma):
    B, L, _ = proj.shape
    blk = lambda col: pl.BlockSpec((1, L, LANES), lambda b, h, c=col // LANES: (b, 0, c + h))
    y = pl.pallas_call(
        _ret_body,
        grid=(B, RET_HEADS),
        in_specs=[pl.BlockSpec(memory_space=pltpu.SMEM),
                  blk(COL_RQ), blk(COL_RK), blk(COL_RV), blk(COL_RG)],
        out_specs=pl.BlockSpec((1, L, LANES), lambda b, h: (b, 0, h)),
        out_shape=jax.ShapeDtypeStruct((B, L, RET_WIDTH), BF16),
        scratch_shapes=[pltpu.VMEM((L // RET_BLOCK, 2 * RET_DK, RET_DV), F32)],
        compiler_params=_params(("parallel", "parallel")),
        name="retention",
    )(log_gamma, proj, proj, proj, proj)
    return y.reshape(B * L, RET_WIDTH)


def _att_body(q_ref, k_ref, v_ref, o_ref):
    G = ATT_HEADS // ATT_KV_HEADS
    tq = q_ref.shape[1]
    kc = min(ATT_KEY_CHUNK, k_ref.shape[1])
    nchunk = k_ref.shape[1] // kc
    q = q_ref[0].astype(BF16)
    qs = jnp.concatenate([q[:, g * ATT_HD:(g + 1) * ATT_HD] for g in range(G)], axis=0)
    m = jnp.full((G * tq, 1), -jnp.inf, F32)
    l = jnp.zeros((G * tq, 1), F32)
    acc = jnp.zeros((G * tq, ATT_HD), F32)
    s_next = _dot_nt(qs, k_ref[0, 0:kc, :].astype(BF16))
    for c in range(nchunk):
        s = s_next
        if c + 1 < nchunk:
            s_next = _dot_nt(qs, k_ref[0, (c + 1) * kc:(c + 2) * kc, :].astype(BF16))
        m_new = jnp.maximum(m, jnp.max(s, axis=-1, keepdims=True))
        alpha = jnp.exp2(m - m_new)
        p = jnp.exp2(s - m_new)
        l = alpha * l + jnp.sum(p, axis=-1, keepdims=True)
        acc = alpha * acc + _dot(p.astype(BF16), v_ref[0, c * kc:(c + 1) * kc, :].astype(BF16))
        m = m_new
    o = acc * (1.0 / l)
    for g in range(G):
        o_ref[0, :, g * ATT_HD:(g + 1) * ATT_HD] = o[g * tq:(g + 1) * tq].astype(o_ref.dtype)


def _attention_mixer(proj, *, tq):
    B, L, _ = proj.shape
    G = ATT_HEADS // ATT_KV_HEADS
    y = pl.pallas_call(
        _att_body,
        grid=(B, ATT_KV_HEADS, L // tq),
        in_specs=[pl.BlockSpec((1, tq, G * ATT_HD), lambda b, h, i: (b, i, COL_AQ // (G * ATT_HD) + h)),
                  pl.BlockSpec((1, L, ATT_HD), lambda b, h, i: (b, 0, COL_AK // ATT_HD + h)),
                  pl.BlockSpec((1, L, ATT_HD), lambda b, h, i: (b, 0, COL_AV // ATT_HD + h))],
        out_specs=pl.BlockSpec((1, tq, G * ATT_HD), lambda b, h, i: (b, i, h)),
        out_shape=jax.ShapeDtypeStruct((B, L, ATT_WIDTH), BF16),
        compiler_params=_params(("parallel", "parallel", "arbitrary")),
        name="attention",
    )(proj, proj, proj)
    return y.reshape(B * L, ATT_WIDTH)


def _tail_body(yh_ref, yr_ref, ya_ref, ga_ref, gb_ref, x_ref, kv_ref, wb_ref, wo_ref, gmix_ref,
               gxpre_ref, wq_ref, wxo_ref, gxpost_ref, gfpre_ref, w1_ref, w2_ref, gfpost_ref, o_ref, u_ref):
    D = D_MODEL
    ph = _dot(yh_ref[0], wb_ref[0:HY_WIDTH, :])
    pr = _dot(yr_ref[0], wb_ref[HY_WIDTH:HY_WIDTH + RET_WIDTH, :])
    pa = _dot(ya_ref[0], wb_ref[HY_WIDTH + RET_WIDTH:, :])
    ga = ga_ref[0].astype(F32)
    gb = gb_ref[0].astype(F32)
    hd = D // 2
    g0 = _sigmoid(ga[:, :D])
    g1 = _sigmoid(jnp.concatenate([ga[:, D:], gb[:, :hd]], axis=1))
    g2 = _sigmoid(gb[:, hd:])
    merged = g0 * ph + g1 * pr + g2 * pa
    x = x_ref[0] + _rms(_dot(merged.astype(BF16), wo_ref[...]), gmix_ref[...])
    h = _rms(x, gxpre_ref[...]).astype(BF16)
    q = (_dot(h, wq_ref[...]) * (X_HD ** -0.5)).astype(BF16)
    kv = kv_ref[0]
    outs = []
    for hh in range(X_HEADS):
        sl = slice(hh * X_HD, (hh + 1) * X_HD)
        s = _dot_nt(q[:, sl], kv[:, sl])
        p = jnp.exp(s - jnp.max(s, axis=-1, keepdims=True))
        l = jnp.sum(p, axis=-1, keepdims=True)
        vh = kv[:, D + hh * X_HD:D + (hh + 1) * X_HD]
        outs.append(_dot(p.astype(BF16), vh) * (1.0 / l))
    o = jnp.concatenate(outs, axis=1).astype(BF16)
    x = x + _rms(_dot(o, wxo_ref[...]), gxpost_ref[...])
    h = _rms(x, gfpre_ref[...]).astype(BF16)
    for c in range(D_FF // FF_CHUNK):
        sl = slice(c * FF_CHUNK, (c + 1) * FF_CHUNK)
        u_ref[:, sl] = jnp.square(jnp.maximum(_dot(h, w1_ref[:, sl]), 0.0)).astype(BF16)
    o_ref[0] = x + _rms(_dot(u_ref[...], w2_ref[...]), gfpost_ref[...])


def _layer_tail(yh, yr, ya, proj, x, kv, w, l, *, tm):
    B, L, D = x.shape
    M = kv.shape[1]
    gw = N_BRANCH * D // 2
    tok = lambda width, col=0: pl.BlockSpec((1, tm, width), lambda b, i, col=col: (b, i, col))
    vec = pl.BlockSpec((1, D), lambda b, i: (0, 0))
    res = lambda shape: pl.BlockSpec((None,) + shape, lambda b, i: (l, 0, 0), pipeline_mode=pl.Buffered(1))
    g = lambda name: w[name][l].reshape(1, D)
    return pl.pallas_call(
        _tail_body,
        grid=(B, L // tm),
        in_specs=[tok(HY_WIDTH), tok(RET_WIDTH), tok(ATT_WIDTH), tok(gw, COL_GATE // gw), tok(gw, COL_GATE // gw + 1),
                  tok(D), pl.BlockSpec((1, M, 2 * D), lambda b, i: (b, 0, 0)),
                  res((MIX_WIDTH, D)), res((D, D)), vec,
                  vec, res((D, D)), res((D, D)), vec,
                  vec, res((D, D_FF)), res((D_FF, D)), vec],
        out_specs=tok(D),
        out_shape=jax.ShapeDtypeStruct((B, L, D), F32),
        scratch_shapes=[pltpu.VMEM((tm, D_FF), BF16)],
        compiler_params=_params(("parallel", "parallel")),
        name="layer_tail",
    )(yh, yr, ya, proj, proj, x, kv, w["w_branch"], w["w_out"], g("g_mix_post"),
      g("g_x_pre"), w["w_xq"], w["w_xo"], g("g_x_post"),
      g("g_ff_pre"), w["w_ff1"], w["w_ff2"], g("g_ff_post"))


def _rotary_tables(pos, dim, theta):
    inv = theta ** (-jnp.arange(0, dim, 2, dtype=F32) / dim)
    ang = pos.astype(F32)[:, None] * inv[None, :]
    return jnp.cos(ang), jnp.sin(ang)


def _position_tables(L):
    pos = jnp.arange(L)
    rows = L // GRID_W
    row_ids = jnp.repeat(jnp.arange(rows), GRID_W)
    col_ids = jnp.tile(jnp.arange(GRID_W), rows)
    rcos, rsin = _rotary_tables(pos, RET_DK, RET_THETA)
    rc, rs = _rotary_tables(row_ids, ATT_HD // 2, ROPE_THETA)
    cc, cs = _rotary_tables(col_ids, ATT_HD // 2, ROPE_THETA)
    return dict(
        ret_cos=jnp.concatenate([rcos, rcos], axis=-1),
        ret_sin=jnp.concatenate([-rsin, rsin], axis=-1),
        att_cos=jnp.concatenate([rc, rc, cc, cc], axis=-1),
        att_sin=jnp.concatenate([-rs, rs, -cs, cs], axis=-1),
        feats=_hyena_features(L),
        dft=_dft_tables(L),
    )


def _trunk(x, mem, w, pt):
    B, L, D = x.shape
    M = mem.shape[1]
    mem2 = mem.reshape(B * M, D)
    for l in range(DEPTH):
        proj = _input_projection(x.reshape(B * L, D), w["g_mix_pre"][l], w["w_in"], l, w["hy_conv"][l],
                                 w["att_qnorm"][l], w["att_knorm"][l], pt, L, tm=TOKEN_TILE).reshape(B, L, IN_COLS)
        kspec = _hyena_spectrum(pt["feats"], w["hy_fw1"][l], w["hy_fb1"][l], w["hy_fw2"][l],
                                w["hy_fb2"][l], w["hy_fw3"][l], L, pt["dft"])
        y_h = _hyena_mixer(proj, kspec, w["hy_bias"][l], pt["dft"])
        y_r = _retention_mixer(proj, w["ret_lg"][l])
        y_a = _attention_mixer(proj, tq=ATT_QUERY_TILE)
        kv = _norm_matmul(mem2, w["g_mem"][l], w["w_xkv"][l], tm=TOKEN_TILE, tn=FF_CHUNK).reshape(B, M, 2 * D)
        x = _layer_tail(y_h.reshape(B, L, HY_WIDTH), y_r.reshape(B, L, RET_WIDTH), y_a.reshape(B, L, ATT_WIDTH),
                        proj, x, kv, w, l, tm=TOKEN_TILE)
    return x


def kernel(x_prompt, x_sample, mem_prompt, mem_sample, g_mix_pre, g_mix_post, w_in, hy_conv, hy_fw1, hy_fb1,
           hy_fw2, hy_fb2, hy_fw3, hy_bias, ret_decay, att_qnorm, att_knorm, w_branch, w_out, g_x_pre, g_x_post,
           g_mem, w_xq, w_xkv, w_xo, g_ff_pre, g_ff_post, w_ff1, w_ff2):
    w = dict(
        g_mix_pre=g_mix_pre, g_mix_post=g_mix_post, w_in=w_in.astype(BF16), hy_conv=hy_conv,
        hy_fw1=hy_fw1, hy_fb1=hy_fb1, hy_fw2=hy_fw2, hy_fb2=hy_fb2, hy_fw3=hy_fw3, hy_bias=hy_bias,
        ret_lg=jax.nn.log_sigmoid(ret_decay.astype(F32)), att_qnorm=att_qnorm, att_knorm=att_knorm,
        w_branch=w_branch.astype(BF16), w_out=w_out.astype(BF16), g_x_pre=g_x_pre, g_x_post=g_x_post,
        g_mem=g_mem, w_xq=w_xq.astype(BF16), w_xkv=w_xkv.astype(BF16), w_xo=w_xo.astype(BF16),
        g_ff_pre=g_ff_pre, g_ff_post=g_ff_post, w_ff1=w_ff1.astype(BF16), w_ff2=w_ff2.astype(BF16),
    )
    y_prompt = _trunk(x_prompt, mem_prompt, w, _position_tables(x_prompt.shape[1]))
    y_sample = _trunk(x_sample, mem_sample, w, _position_tables(x_sample.shape[1]))
    return (y_prompt, y_sample)
```

```python
import functools
import math

import jax
import jax.numpy as jnp
from jax import lax
from jax.experimental import pallas as pl
from jax.experimental.pallas import tpu as pltpu

F32 = jnp.float32
BF16 = jnp.bfloat16

D_MODEL = 1024
DEPTH = 4
GRID_W = 64
EPS = 1e-6
HY_WIDTH = 512
HY_SHORT = 3
HY_EMB = 33
HY_FH = 64
HY_SIN_FREQ = 1.0
HY_TARGET = 1e-2
HY_FAST = 0.3
HY_SLOW = 1.5
HY_SHIFT = 0.0
RET_HEADS = 4
RET_DK = 128
RET_DV = 128
RET_WIDTH = RET_HEADS * RET_DV
RET_THETA = 10000.0
ATT_HEADS = 4
ATT_KV_HEADS = 2
ATT_HD = 128
ATT_WIDTH = ATT_HEADS * ATT_HD
ROPE_THETA = 10000.0
X_HEADS = 4
X_HD = D_MODEL // X_HEADS
D_FF = 4 * D_MODEL
N_BRANCH = 3
MIX_WIDTH = HY_WIDTH + RET_WIDTH + ATT_WIDTH

COL_HY = 0
COL_X0C = 0
COL_ZF = HY_WIDTH
COL_RQ = 3 * HY_WIDTH
COL_RK = COL_RQ + RET_HEADS * RET_DK
COL_RV = COL_RK + RET_HEADS * RET_DK
COL_RG = COL_RV + RET_WIDTH
COL_AQ = COL_RG + RET_WIDTH
COL_AK = COL_AQ + ATT_WIDTH
COL_AV = COL_AK + ATT_KV_HEADS * ATT_HD
COL_GATE = COL_AV + ATT_KV_HEADS * ATT_HD
IN_COLS = COL_GATE + N_BRANCH * D_MODEL

LANES = 128
HALO = 8
VMEM_LIMIT = 56 * 1024 * 1024
DFT_N2 = 128
DFT_NB = 16
DFT_KB = 8
DFT_UNROLL = 4
DFT_PASSES = dict(s1=1, s2f=1, s2i=1, s3=1, k1=1, k2=1)
RET_BLOCK = 256
RET_UNROLL = 8
ATT_KEY_CHUNK = 1024
ATT_QUERY_TILE = 1024
FILTER_PASSES = 1
FF_CHUNK = 1024
TOKEN_TILE = 512
LOG2E = 1.4426950408889634


def _params(sem):
    return pltpu.CompilerParams(dimension_semantics=sem, vmem_limit_bytes=VMEM_LIMIT)


def _dot(a, b):
    return jnp.dot(a, b, preferred_element_type=F32)


def _dot_nt(a, b):
    return lax.dot_general(a, b, (((1,), (1,)), ((), ())), preferred_element_type=F32)


def _dot_tn(a, b):
    return lax.dot_general(a, b, (((0,), (0,)), ((), ())), preferred_element_type=F32)


def _rms(x, g):
    ms = jnp.mean(x * x, axis=-1, keepdims=True)
    return x * lax.rsqrt(ms + EPS) * g


def _sigmoid(x):
    return 1.0 / (1.0 + jnp.exp(-x))


def _split(x):
    hi = x.astype(BF16)
    lo = (x - hi.astype(F32)).astype(BF16)
    return hi, lo


def _dot3(a, b):
    return _dot(a[0], b[0]) + (_dot(a[0], b[1]) + _dot(a[1], b[0]))


def _mm(a, b, passes):
    if passes == 1:
        return _dot(a.astype(BF16), b.astype(BF16))
    return _dot3(_split(a.astype(F32)), _split(b.astype(F32)))


def _nm_body(x_ref, g_ref, w_ref, o_ref, *, tn):
    h = _rms(x_ref[...], g_ref[...]).astype(BF16)
    for j in range(w_ref.shape[1] // tn):
        o_ref[:, j * tn:(j + 1) * tn] = _dot(h, w_ref[:, j * tn:(j + 1) * tn]).astype(o_ref.dtype)


def _norm_matmul(x, g, w, *, tm, tn):
    T, D = x.shape
    N = w.shape[1]
    return pl.pallas_call(
        functools.partial(_nm_body, tn=tn),
        grid=(T // tm,),
        in_specs=[pl.BlockSpec((tm, D), lambda i: (i, 0)),
                  pl.BlockSpec((1, D), lambda i: (0, 0)),
                  pl.BlockSpec((D, N), lambda i: (0, 0), pipeline_mode=pl.Buffered(1))],
        out_specs=pl.BlockSpec((tm, N), lambda i: (i, 0)),
        out_shape=jax.ShapeDtypeStruct((T, N), BF16),
        compiler_params=_params(("parallel",)),
        name="norm_matmul",
    )(x, g.reshape(1, D), w)


def _rotate_half(x, cos, sin):
    return x * cos + pltpu.roll(x, x.shape[-1] // 2, 1) * sin


def _axial_rope(x, gain, cos, sin, scale):
    x = _rms(x, gain)
    lane = lax.broadcasted_iota(jnp.int32, x.shape, 1)
    lower = (lane % (ATT_HD // 2)) < (ATT_HD // 4)
    swapped = jnp.where(lower, pltpu.roll(x, LANES - ATT_HD // 4, 1), pltpu.roll(x, ATT_HD // 4, 1))
    return (x * cos + swapped * sin) * scale


_WIN_SEGMENTS = (
    [(COL_RQ, RET_WIDTH, "ret_q"), (COL_RK, RET_WIDTH, "ret_k"), (COL_RV, RET_WIDTH, None),
       (COL_RG, RET_WIDTH, None), (COL_AQ, ATT_WIDTH, "att_q"), (COL_AK, COL_GATE - COL_AK, "att_kv")]
    + [(COL_GATE + i * D_MODEL, D_MODEL, None) for i in range(N_BRANCH)])


def _win_body(x_ref, xp_ref, xn_ref, g_ref, w_ref, cw_ref, qn_ref, kn_ref, rc_ref, rs_ref, ac_ref, as_ref, o_ref,
              *, per_seq):
    tm = x_ref.shape[0]
    pos = pl.program_id(0) % per_seq
    x_ext = jnp.concatenate([xp_ref[...], x_ref[...], xn_ref[...]], axis=0)
    r_ext = lax.broadcasted_iota(jnp.int32, (tm + 2 * HALO, 1), 0)
    pad = ((r_ext < HALO) & (pos == 0)) | ((r_ext >= tm + HALO) & (pos == per_seq - 1))
    hf = jnp.where(pad, 0.0, _rms(x_ext, g_ref[...]))
    h_ext = hf.astype(BF16)
    h = hf[HALO:HALO + tm].astype(BF16)
    row = lax.broadcasted_iota(jnp.int32, (tm, HY_WIDTH), 0)
    conv = []
    for grp in range(3):
        cols = slice(COL_HY + grp * HY_WIDTH, COL_HY + (grp + 1) * HY_WIDTH)
        y_ext = _dot(h_ext, w_ref[:, cols])
        y = y_ext[HALO:HALO + tm]
        y_prev = y_ext[HALO - 1:HALO]
        y_next = y_ext[HALO + tm:HALO + tm + 1]
        prev = jnp.where(row == 0, y_prev, pltpu.roll(y, 1, 0))
        nxt = jnp.where(row == tm - 1, y_next, pltpu.roll(y, tm - 1, 0))
        cw = cw_ref[:, cols]
        conv.append(prev * cw[0:1] + y * cw[1:2] + nxt * cw[2:3])
    x0c, x1c, vc = conv
    o_ref[:, COL_X0C:COL_X0C + HY_WIDTH] = x0c.astype(o_ref.dtype)
    o_ref[:, COL_ZF:COL_ZF + HY_WIDTH] = (vc * x1c).astype(o_ref.dtype)
    o_ref[:, COL_ZF + HY_WIDTH:COL_RQ] = vc.astype(o_ref.dtype)
    for start, width, kind in _WIN_SEGMENTS:
        y = _dot(h, w_ref[:, start:start + width])
        if kind is None:
            o_ref[:, start:start + width] = y.astype(o_ref.dtype)
            continue
        for hh in range(width // LANES):
            yh = y[:, hh * LANES:(hh + 1) * LANES]
            if kind == "ret_q":
                yh = _rotate_half(yh, rc_ref[...], rs_ref[...])
            elif kind == "ret_k":
                yh = _rotate_half(yh, rc_ref[...], rs_ref[...]) * (RET_DK ** -0.5)
            elif kind == "att_q":
                yh = _axial_rope(yh, qn_ref[...], ac_ref[...], as_ref[...], ATT_HD ** -0.5 * LOG2E)
            elif hh < ATT_KV_HEADS:
                yh = _axial_rope(yh, kn_ref[...], ac_ref[...], as_ref[...], 1.0)
            o_ref[:, start + hh * LANES:start + (hh + 1) * LANES] = yh.astype(o_ref.dtype)


def _input_projection(x, g, w_stack, l, conv_w, qn, kn, pt, L, *, tm):
    T, D = x.shape
    N = w_stack.shape[2]
    per_seq = L // tm
    halo_per_tile = tm // HALO
    vec = lambda n: pl.BlockSpec((1, n), lambda i: (0, 0))
    tab = pl.BlockSpec((tm, LANES), lambda i: (i % per_seq, 0))
    return pl.pallas_call(
        functools.partial(_win_body, per_seq=per_seq),
        grid=(T // tm,),
        in_specs=[pl.BlockSpec((tm, D), lambda i: (i, 0)),
                  pl.BlockSpec((HALO, D), lambda i: (jnp.maximum(i * halo_per_tile - 1, 0), 0)),
                  pl.BlockSpec((HALO, D), lambda i: (jnp.minimum((i + 1) * halo_per_tile, T // HALO - 1), 0)),
                  vec(D),
                  pl.BlockSpec((None, D, N), lambda i: (l, 0, 0), pipeline_mode=pl.Buffered(1)),
                  pl.BlockSpec((HY_SHORT, 3 * HY_WIDTH), lambda i: (0, 0)),
                  vec(ATT_HD), vec(ATT_HD), tab, tab, tab, tab],
        out_specs=pl.BlockSpec((tm, N), lambda i: (i, 0)),
        out_shape=jax.ShapeDtypeStruct((T, N), BF16),
        compiler_params=_params(("parallel",)),
        name="input_projection",
    )(x, x, x, g.reshape(1, D), w_stack, conv_w, qn.reshape(1, ATT_HD), kn.reshape(1, ATT_HD),
      pt["ret_cos"], pt["ret_sin"], pt["att_cos"], pt["att_sin"])


def _filt_body(f_ref, w1_ref, b1_ref, w2_ref, b2_ref, w3_ref, dl_ref, o_ref, *, L):
    tl = f_ref.shape[0]
    f = f_ref[...]
    h = jnp.sin(HY_SIN_FREQ * (_mm(f, w1_ref[...], 3) + b1_ref[...]))
    h = jnp.sin(HY_SIN_FREQ * (_mm(h, w2_ref[...], FILTER_PASSES) + b2_ref[...]))
    filt = _mm(h, w3_ref[...], FILTER_PASSES)
    mod = jnp.exp(-f[:, 0:1] * dl_ref[...]) + HY_SHIFT
    row = pl.program_id(0) * tl + lax.broadcasted_iota(jnp.int32, (tl, HY_WIDTH), 0)
    sel = jnp.where(row < L, filt[:, :HY_WIDTH], filt[:, HY_WIDTH:])
    o_ref[...] = jnp.where(row == L, 0.0, sel * mod)


def _hyena_features(L):
    t = jnp.linspace(0.0, 1.0, L, dtype=F32)[:, None]
    bands = (HY_EMB - 1) // 2
    w = 2.0 * math.pi * jnp.arange(L, dtype=F32)[:, None] / L
    f = jnp.linspace(1e-4, bands - 1, bands, dtype=F32)[None, :]
    z = jnp.concatenate([t, jnp.cos(f * w), -jnp.sin(f * w)], axis=-1)
    zc = jnp.concatenate([z, z[:1], jnp.flip(z[1:], axis=0)], axis=0)
    return jnp.pad(zc, ((0, 0), (0, LANES - HY_EMB)))


def _hyena_filter(feats, w1, b1, w2, b2, w3, L):
    tl = TOKEN_TILE
    pad = LANES - HY_FH
    w1p = jnp.pad(w1, ((0, LANES - HY_EMB), (0, pad)))
    b1p = jnp.pad(b1, (0, pad)).reshape(1, LANES)
    w2p = jnp.pad(w2, ((0, pad), (0, pad)))
    b2p = jnp.pad(b2, (0, pad)).reshape(1, LANES)
    w3p = jnp.pad(w3, ((0, pad), (0, 0)))
    min_decay = math.log(HY_TARGET) / HY_SLOW
    max_decay = math.log(HY_TARGET) / HY_FAST
    deltas = jnp.abs(jnp.linspace(min_decay, max_decay, HY_WIDTH, dtype=F32)).reshape(1, HY_WIDTH)
    full = lambda shape: pl.BlockSpec(shape, lambda i: (0,) * len(shape))
    return pl.pallas_call(
        functools.partial(_filt_body, L=L),
        grid=(2 * L // tl,),
        in_specs=[pl.BlockSpec((tl, LANES), lambda i: (i, 0)),
                  full((LANES, LANES)), full((1, LANES)), full((LANES, LANES)), full((1, LANES)),
                  full((LANES, 2 * HY_WIDTH)), full((1, HY_WIDTH))],
        out_specs=pl.BlockSpec((tl, HY_WIDTH), lambda i: (i, 0)),
        out_shape=jax.ShapeDtypeStruct((2 * L, HY_WIDTH), F32),
        compiler_params=_params(("parallel",)),
        name="hyena_filter",
    )(feats, w1p, b1p, w2p, b2p, w3p, deltas)


def _expanded(fn, rows, cols):
    small = fn(lax.broadcasted_iota(jnp.int32, (rows, cols), 0), lax.broadcasted_iota(jnp.int32, (rows, cols), 1))
    hp = functools.partial(jnp.dot, precision=lax.Precision.HIGHEST)
    pick_r = (lax.broadcasted_iota(jnp.int32, (rows * DFT_NB, rows), 0) // DFT_NB
              == lax.broadcasted_iota(jnp.int32, (rows * DFT_NB, rows), 1)).astype(F32)
    pick_c = (lax.broadcasted_iota(jnp.int32, (cols, cols * DFT_NB), 1) // DFT_NB
              == lax.broadcasted_iota(jnp.int32, (cols, cols * DFT_NB), 0)).astype(F32)
    ri = lax.broadcasted_iota(jnp.int32, (rows * DFT_NB, cols * DFT_NB), 0)
    ci = lax.broadcasted_iota(jnp.int32, (rows * DFT_NB, cols * DFT_NB), 1)
    return jnp.where(ri % DFT_NB == ci % DFT_NB, hp(hp(pick_r, small), pick_c), 0.0)


def _dft_tables(L):
    N = 2 * L
    N2 = DFT_N2
    N1 = N // N2
    N1h = N1 // 2

    def cs1(k1, n1):
        ang = ((k1 * n1) % N1).astype(F32) * (2.0 * math.pi / N1)
        return jnp.cos(ang), jnp.sin(ang)

    def f1_pair(r, c):
        cs, sn = cs1(r // 2, c % N1h)
        re_row, first = (r % 2) == 0, c < N1h
        return jnp.where(re_row, jnp.where(first, cs, sn), jnp.where(first, -sn, cs))

    def f1_real(r, c):
        cs, sn = cs1(r // 2, c)
        return jnp.where((r % 2) == 0, cs, -sn)

    def f3_pair(r, c):
        cs, sn = cs1(c // 2, r % N1h)
        re_out, re_in = r < N1h, (c % 2) == 0
        return jnp.where(re_out, jnp.where(re_in, cs, -sn), jnp.where(re_in, sn, cs)) * (1.0 / N)

    kk = (jnp.arange(N1, dtype=jnp.int32)[:, None, None]
          + N1 * jnp.arange(N2, dtype=jnp.int32)[None, :, None])
    n2 = jnp.arange(N2, dtype=jnp.int32)[None, None, :]
    a2 = ((kk * n2) % N).astype(F32) * (2.0 * math.pi / N)
    gr, gi = jnp.cos(a2), -jnp.sin(a2)
    gtr, gti = jnp.swapaxes(gr, 1, 2), jnp.swapaxes(gi, 1, 2)
    g_fwd = jnp.concatenate([jnp.concatenate([gr, -gi], axis=2),
                             jnp.concatenate([gi, gr], axis=2)], axis=1)
    g_inv = jnp.concatenate([jnp.concatenate([gtr, gti], axis=2),
                             jnp.concatenate([-gti, gtr], axis=2)], axis=1)
    return dict(N1=N1, N1h=N1h, N2=N2,
                f1_pair=_expanded(f1_pair, 2 * N1, 2 * N1h), f1_real=_expanded(f1_real, 2 * N1, N1),
                f3_pair=_expanded(f3_pair, 2 * N1h, 2 * N1), g_fwd=g_fwd, g_inv=g_inv)


def _stage_dtype(passes):
    return BF16 if passes == 1 else F32


def _dft1_body(x_ref, f_ref, o_ref, *, passes):
    P, K, NB, C = x_ref.shape
    y = _mm(f_ref[...], x_ref[...].reshape(P * K * NB, C), passes)
    o_ref[0] = y.astype(o_ref.dtype).reshape(o_ref.shape[1:])


def _dft_stage1(x, f, *, P, passes, out_dtype, col=0):
    Bx, K, N2, _ = x.shape
    C = HY_WIDTH
    N1 = f.shape[0] // (2 * DFT_NB)
    return pl.pallas_call(
        functools.partial(_dft1_body, passes=passes),
        grid=(Bx // P, N2 // DFT_NB),
        in_specs=[pl.BlockSpec((P, K, DFT_NB, C), lambda b, j: (b, 0, j, col)),
                  pl.BlockSpec(f.shape, lambda b, j: (0, 0), pipeline_mode=pl.Buffered(1))],
        out_specs=pl.BlockSpec((1, N1, 2, DFT_NB, C), lambda b, j: (b, 0, 0, j, 0)),
        out_shape=jax.ShapeDtypeStruct((Bx // P, N1, 2, N2, C), out_dtype),
        compiler_params=_params(("parallel", "parallel")),
        name="dft_stage1",
    )(x, f)


def _spec_body(a_ref, g_ref, o_ref, *, passes):
    def step(k, carry):
        o_ref[k] = _mm(g_ref[k], a_ref[0, k], passes)
        return carry

    lax.fori_loop(0, a_ref.shape[1], step, 0, unroll=DFT_UNROLL)


def _dft_spectrum(a, g, *, kb, passes):
    _, N1, R, C = a.shape
    return pl.pallas_call(
        functools.partial(_spec_body, passes=passes),
        grid=(N1 // kb,),
        in_specs=[pl.BlockSpec((1, kb, R, C), lambda i: (0, i, 0, 0)),
                  pl.BlockSpec((kb, R, R), lambda i: (i, 0, 0))],
        out_specs=pl.BlockSpec((kb, R, C), lambda i: (i, 0, 0)),
        out_shape=jax.ShapeDtypeStruct((N1, R, C), F32),
        compiler_params=_params(("parallel",)),
        name="dft_spectrum",
    )(a, g)


def _dft2_body(a_ref, gf_ref, gi_ref, ks_ref, o_ref, *, passes_fwd, passes_inv):
    H = a_ref.shape[2] // 2

    def step(k, carry):
        y = _mm(gf_ref[k], a_ref[0, k], passes_fwd)
        yr, yi = y[:H], y[H:]
        kr, ki = ks_ref[k, :H, :], ks_ref[k, H:, :]
        z = jnp.concatenate([yr * kr - yi * ki, yr * ki + yi * kr], axis=0)
        o_ref[0, k] = _mm(gi_ref[k], z, passes_inv).astype(o_ref.dtype)
        return carry

    lax.fori_loop(0, a_ref.shape[1], step, 0, unroll=DFT_UNROLL)


def _dft_stage2(a, gf, gi, ks, *, kb, passes_fwd, passes_inv, out_dtype):
    Bp, N1, R, C = a.shape
    dat = pl.BlockSpec((1, kb, R, C), lambda i, b: (b, i, 0, 0))
    mat = pl.BlockSpec((kb, R, R), lambda i, b: (i, 0, 0))
    return pl.pallas_call(
        functools.partial(_dft2_body, passes_fwd=passes_fwd, passes_inv=passes_inv),
        grid=(N1 // kb, Bp),
        in_specs=[dat, mat, mat, pl.BlockSpec((kb, R, C), lambda i, b: (i, 0, 0))],
        out_specs=dat,
        out_shape=jax.ShapeDtypeStruct((Bp, N1, R, C), out_dtype),
        compiler_params=_params(("parallel", "parallel")),
        name="dft_stage2",
    )(a, gf, gi, ks)


def _dft3_body(t_ref, f_ref, zf_ref, x0_ref, bias_ref, o_ref, *, passes):
    _, N1, _, NB, C = t_ref.shape
    y = _mm(f_ref[...], t_ref[0].reshape(N1 * 2 * NB, C), passes).reshape(zf_ref.shape)
    o_ref[...] = ((y + zf_ref[...].astype(F32) * bias_ref[...]) * x0_ref[...].astype(F32)).astype(o_ref.dtype)


def _dft_stage3(t, f, proj4, bias, *, passes):
    Bp, N1, _, N2, C = t.shape
    B, N1h = proj4.shape[:2]
    pair = lambda col=0: pl.BlockSpec((2, N1h, DFT_NB, C), lambda b, j: (b, 0, j, col))
    return pl.pallas_call(
        functools.partial(_dft3_body, passes=passes),
        grid=(Bp, N2 // DFT_NB),
        in_specs=[pl.BlockSpec((1, N1, 2, DFT_NB, C), lambda b, j: (b, 0, 0, j, 0)),
                  pl.BlockSpec(f.shape, lambda b, j: (0, 0), pipeline_mode=pl.Buffered(1)),
                  pair(COL_ZF // C), pair(COL_X0C // C),
                  pl.BlockSpec((1, C), lambda b, j: (0, 0))],
        out_specs=pair(),
        out_shape=jax.ShapeDtypeStruct((B, N1h, N2, C), BF16),
        compiler_params=_params(("parallel", "parallel")),
        name="dft_stage3",
    )(t, f, proj4, proj4, bias)


def _hyena_mixer(proj, kspec, bias, tabs):
    B, L, _ = proj.shape
    N1, N1h, N2 = tabs["N1"], tabs["N1h"], tabs["N2"]
    C = HY_WIDTH
    ps = DFT_PASSES
    proj4 = proj.reshape(B, N1h, N2, IN_COLS)
    f1 = tabs["f1_pair"].astype(_stage_dtype(ps["s1"]))
    a = _dft_stage1(proj4, f1, P=2, passes=ps["s1"], out_dtype=_stage_dtype(ps["s2f"]), col=COL_ZF // C)
    gdt = _stage_dtype(min(ps["s2f"], ps["s2i"]))
    t = _dft_stage2(a.reshape(B // 2, N1, 2 * N2, C), tabs["g_fwd"].astype(gdt), tabs["g_inv"].astype(gdt),
                    kspec, kb=DFT_KB, passes_fwd=ps["s2f"], passes_inv=ps["s2i"],
                    out_dtype=_stage_dtype(ps["s3"]))
    f3 = tabs["f3_pair"].astype(_stage_dtype(ps["s3"]))
    y = _dft_stage3(t.reshape(B // 2, N1, 2, N2, C), f3, proj4, bias.astype(F32).reshape(1, C), passes=ps["s3"])
    return y.reshape(B * L, C)


def _hyena_spectrum(feats, w1, b1, w2, b2, w3, L, tabs):
    N1, N2 = tabs["N1"], tabs["N2"]
    C = HY_WIDTH
    ps = DFT_PASSES
    kfull = _hyena_filter(feats, w1, b1, w2, b2, w3, L)
    a = _dft_stage1(kfull.reshape(1, N1, N2, C), tabs["f1_real"].astype(_stage_dtype(ps["k1"])), P=1, passes=ps["k1"],
                    out_dtype=_stage_dtype(ps["k2"]))
    return _dft_spectrum(a.reshape(1, N1, 2 * N2, C), tabs["g_fwd"].astype(_stage_dtype(ps["k2"])), kb=DFT_KB,
                         passes=ps["k2"])


def _ret_body(lg_ref, q_ref, k_ref, v_ref, g_ref, o_ref, st_ref):
    C = RET_BLOCK
    L = q_ref.shape[1]
    nblk = L // C
    dk = RET_DK
    h = pl.program_id(1)
    lgf = lg_ref[0, h]
    lgb = lg_ref[1, h]
    ii = lax.broadcasted_iota(jnp.int32, (C, C), 0)
    jj = lax.broadcasted_iota(jnp.int32, (C, C), 1)
    diff = (ii - jj).astype(F32)
    dm = jnp.where(diff >= 0, jnp.exp(jnp.maximum(diff, 0.0) * lgf), jnp.exp(jnp.maximum(-diff, 0.0) * lgb))
    pos = lax.broadcasted_iota(jnp.int32, (C, dk), 0).astype(F32)
    qw_f = jnp.exp((pos + 1.0) * lgf)
    kw_f = jnp.exp((C - 1.0 - pos) * lgf)
    qw_b = jnp.exp((C - pos) * lgb)
    kw_b = jnp.exp(pos * lgb)
    cd_f = jnp.exp(C * lgf)
    cd_b = jnp.exp(C * lgb)

    def block_kv(n, carry):
        sl = pl.ds(pl.multiple_of(n * C, C), C)
        k = k_ref[0, sl, :].astype(F32)
        kk = jnp.concatenate([k * kw_f, k * kw_b], axis=1).astype(BF16)
        st_ref[n] = _dot_tn(kk, v_ref[0, sl, :].astype(BF16))
        return carry

    lax.fori_loop(0, nblk, block_kv, 0, unroll=RET_UNROLL)

    def scan_f(n, S):
        kv = st_ref[n, :dk, :]
        st_ref[n, :dk, :] = S
        return S * cd_f + kv

    lax.fori_loop(0, nblk, scan_f, jnp.zeros((dk, RET_DV), F32))

    def scan_b(m, S):
        n = nblk - 1 - m
        kv = st_ref[n, dk:, :]
        st_ref[n, dk:, :] = S
        return S * cd_b + kv

    lax.fori_loop(0, nblk, scan_b, jnp.zeros((dk, RET_DV), F32))

    def block_out(n, carry):
        sl = pl.ds(pl.multiple_of(n * C, C), C)
        q = q_ref[0, sl, :].astype(BF16)
        scores = _dot_nt(q, k_ref[0, sl, :].astype(BF16)) * dm
        qf = q.astype(F32)
        qq = jnp.concatenate([qf * qw_f, qf * qw_b], axis=1).astype(BF16)
        o = _dot(scores.astype(BF16), v_ref[0, sl, :].astype(BF16)) + _dot(qq, st_ref[n].astype(BF16))
        mu = jnp.mean(o, axis=-1, keepdims=True)
        oc = o - mu
        on = oc * lax.rsqrt(jnp.mean(oc * oc, axis=-1, keepdims=True) + EPS)
        g = g_ref[0, sl, :].astype(F32)
        o_ref[0, sl, :] = (g * _sigmoid(g) * on).astype(o_ref.dtype)
        return carry

    lax.fori_loop(0, nblk, block_out, 0, unroll=RET_UNROLL)


def _retention_mixer(proj, log_gamma):
    B, L, _ = proj.shape
    blk = lambda col: pl.BlockSpec((1, L, LANES), lambda b, h, c=col // LANES: (b, 0, c + h))
    y = pl.pallas_call(
        _ret_body,
        grid=(B, RET_HEADS),
        in_specs=[pl.BlockSpec(memory_space=pltpu.SMEM),
                  blk(COL_RQ), blk(COL_RK), blk(COL_RV), blk(COL_RG)],
        out_specs=pl.BlockSpec((1, L, LANES), lambda b, h: (b, 0, h)),
        out_shape=jax.ShapeDtypeStruct((B, L, RET_WIDTH), BF16),
        scratch_shapes=[pltpu.VMEM((L // RET_BLOCK, 2 * RET_DK, RET_DV), F32)],
        compiler_params=_params(("parallel", "parallel")),
        name="retention",
    )(log_gamma, proj, proj, proj, proj)
    return y.reshape(B * L, RET_WIDTH)


def _att_body(q_ref, k_ref, v_ref, o_ref):
    G = ATT_HEADS // ATT_KV_HEADS
    tq = q_ref.shape[1]
    kc = min(ATT_KEY_CHUNK, k_ref.shape[1])
    nchunk = k_ref.shape[1] // kc
    q = q_ref[0].astype(BF16)
    qs = jnp.concatenate([q[:, g * ATT_HD:(g + 1) * ATT_HD] for g in range(G)], axis=0)
    m = jnp.full((G * tq, 1), -jnp.inf, F32)
    l = jnp.zeros((G * tq, 1), F32)
    acc = jnp.zeros((G * tq, ATT_HD), F32)
    s_next = _dot_nt(qs, k_ref[0, 0:kc, :].astype(BF16))
    for c in range(nchunk):
        s = s_next
        if c + 1 < nchunk:
            s_next = _dot_nt(qs, k_ref[0, (c + 1) * kc:(c + 2) * kc, :].astype(BF16))
        m_new = jnp.maximum(m, jnp.max(s, axis=-1, keepdims=True))
        alpha = jnp.exp2(m - m_new)
        p = jnp.exp2(s - m_new)
        l = alpha * l + jnp.sum(p, axis=-1, keepdims=True)
        acc = alpha * acc + _dot(p.astype(BF16), v_ref[0, c * kc:(c + 1) * kc, :].astype(BF16))
        m = m_new
    o = acc * (1.0 / l)
    for g in range(G):
        o_ref[0, :, g * ATT_HD:(g + 1) * ATT_HD] = o[g * tq:(g + 1) * tq].astype(o_ref.dtype)


def _attention_mixer(proj, *, tq):
    B, L, _ = proj.shape
    G = ATT_HEADS // ATT_KV_HEADS
    y = pl.pallas_call(
        _att_body,
        grid=(B, ATT_KV_HEADS, L // tq),
        in_specs=[pl.BlockSpec((1, tq, G * ATT_HD), lambda b, h, i: (b, i, COL_AQ // (G * ATT_HD) + h)),
                  pl.BlockSpec((1, L, ATT_HD), lambda b, h, i: (b, 0, COL_AK // ATT_HD + h)),
                  pl.BlockSpec((1, L, ATT_HD), lambda b, h, i: (b, 0, COL_AV // ATT_HD + h))],
        out_specs=pl.BlockSpec((1, tq, G * ATT_HD), lambda b, h, i: (b, i, h)),
        out_shape=jax.ShapeDtypeStruct((B, L, ATT_WIDTH), BF16),
        compiler_params=_params(("parallel", "parallel", "arbitrary")),
        name="attention",
    )(proj, proj, proj)
    return y.reshape(B * L, ATT_WIDTH)


def _tail_body(yh_ref, yr_ref, ya_ref, ga_ref, gb_ref, x_ref, kv_ref, wb_ref, wo_ref, gmix_ref,
               gxpre_ref, wq_ref, wxo_ref, gxpost_ref, gfpre_ref, w1_ref, w2_ref, gfpost_ref, o_ref, u_ref):
    D = D_MODEL
    ph = _dot(yh_ref[0], wb_ref[0:HY_WIDTH, :])
    pr = _dot(yr_ref[0], wb_ref[HY_WIDTH:HY_WIDTH + RET_WIDTH, :])
    pa = _dot(ya_ref[0], wb_ref[HY_WIDTH + RET_WIDTH:, :])
    ga = ga_ref[0].astype(F32)
    gb = gb_ref[0].astype(F32)
    hd = D // 2
    g0 = _sigmoid(ga[:, :D])
    g1 = _sigmoid(jnp.concatenate([ga[:, D:], gb[:, :hd]], axis=1))
    g2 = _sigmoid(gb[:, hd:])
    merged = g0 * ph + g1 * pr + g2 * pa
    x = x_ref[0] + _rms(_dot(merged.astype(BF16), wo_ref[...]), gmix_ref[...])
    h = _rms(x, gxpre_ref[...]).astype(BF16)
    q = (_dot(h, wq_ref[...]) * (X_HD ** -0.5)).astype(BF16)
    kv = kv_ref[0]
    outs = []
    for hh in range(X_HEADS):
        sl = slice(hh * X_HD, (hh + 1) * X_HD)
        s = _dot_nt(q[:, sl], kv[:, sl])
        p = jnp.exp(s - jnp.max(s, axis=-1, keepdims=True))
        l = jnp.sum(p, axis=-1, keepdims=True)
        vh = kv[:, D + hh * X_HD:D + (hh + 1) * X_HD]
        outs.append(_dot(p.astype(BF16), vh) * (1.0 / l))
    o = jnp.concatenate(outs, axis=1).astype(BF16)
    x = x + _rms(_dot(o, wxo_ref[...]), gxpost_ref[...])
    h = _rms(x, gfpre_ref[...]).astype(BF16)
    for c in range(D_FF // FF_CHUNK):
        sl = slice(c * FF_CHUNK, (c + 1) * FF_CHUNK)
        u_ref[:, sl] = jnp.square(jnp.maximum(_dot(h, w1_ref[:, sl]), 0.0)).astype(BF16)
    o_ref[0] = x + _rms(_dot(u_ref[...], w2_ref[...]), gfpost_ref[...])


def _layer_tail(yh, yr, ya, proj, x, kv, w, l, *, tm):
    B, L, D = x.shape
    M = kv.shape[1]
    gw = N_BRANCH * D // 2
    tok = lambda width, col=0: pl.BlockSpec((1, tm, width), lambda b, i, col=col: (b, i, col))
    vec = pl.BlockSpec((1, D), lambda b, i: (0, 0))
    res = lambda shape: pl.BlockSpec((None,) + shape, lambda b, i: (l, 0, 0), pipeline_mode=pl.Buffered(1))
    g = lambda name: w[name][l].reshape(1, D)
    return pl.pallas_call(
        _tail_body,
        grid=(B, L // tm),
        in_specs=[tok(HY_WIDTH), tok(RET_WIDTH), tok(ATT_WIDTH), tok(gw, COL_GATE // gw), tok(gw, COL_GATE // gw + 1),
                  tok(D), pl.BlockSpec((1, M, 2 * D), lambda b, i: (b, 0, 0)),
                  res((MIX_WIDTH, D)), res((D, D)), vec,
                  vec, res((D, D)), res((D, D)), vec,
                  vec, res((D, D_FF)), res((D_FF, D)), vec],
        out_specs=tok(D),
        out_shape=jax.ShapeDtypeStruct((B, L, D), F32),
        scratch_shapes=[pltpu.VMEM((tm, D_FF), BF16)],
        compiler_params=_params(("parallel", "parallel")),
        name="layer_tail",
    )(yh, yr, ya, proj, proj, x, kv, w["w_branch"], w["w_out"], g("g_mix_post"),
      g("g_x_pre"), w["w_xq"], w["w_xo"], g("g_x_post"),
      g("g_ff_pre"), w["w_ff1"], w["w_ff2"], g("g_ff_post"))


def _rotary_tables(pos, dim, theta):
    inv = theta ** (-jnp.arange(0, dim, 2, dtype=F32) / dim)
    ang = pos.astype(F32)[:, None] * inv[None, :]
    return jnp.cos(ang), jnp.sin(ang)


def _position_tables(L):
    pos = jnp.arange(L)
    rows = L // GRID_W
    row_ids = jnp.repeat(jnp.arange(rows), GRID_W)
    col_ids = jnp.tile(jnp.arange(GRID_W), rows)
    rcos, rsin = _rotary_tables(pos, RET_DK, RET_THETA)
    rc, rs = _rotary_tables(row_ids, ATT_HD // 2, ROPE_THETA)
    cc, cs = _rotary_tables(col_ids, ATT_HD // 2, ROPE_THETA)
    return dict(
        ret_cos=jnp.concatenate([rcos, rcos], axis=-1),
        ret_sin=jnp.concatenate([-rsin, rsin], axis=-1),
        att_cos=jnp.concatenate([rc, rc, cc, cc], axis=-1),
        att_sin=jnp.concatenate([-rs, rs, -cs, cs], axis=-1),
        feats=_hyena_features(L),
        dft=_dft_tables(L),
    )


def _trunk(x, mem, w, pt):
    B, L, D = x.shape
    M = mem.shape[1]
    mem2 = mem.reshape(B * M, D)
    for l in range(DEPTH):
        proj = _input_projection(x.reshape(B * L, D), w["g_mix_pre"][l], w["w_in"], l, w["hy_conv"][l],
                                 w["att_qnorm"][l], w["att_knorm"][l], pt, L, tm=TOKEN_TILE).reshape(B, L, IN_COLS)
        kspec = _hyena_spectrum(pt["feats"], w["hy_fw1"][l], w["hy_fb1"][l], w["hy_fw2"][l],
                                w["hy_fb2"][l], w["hy_fw3"][l], L, pt["dft"])
        y_h = _hyena_mixer(proj, kspec, w["hy_bias"][l], pt["dft"])
        y_r = _retention_mixer(proj, w["ret_lg"][l])
        y_a = _attention_mixer(proj, tq=ATT_QUERY_TILE)
        kv = _norm_matmul(mem2, w["g_mem"][l], w["w_xkv"][l], tm=TOKEN_TILE, tn=FF_CHUNK).reshape(B, M, 2 * D)
        x = _layer_tail(y_h.reshape(B, L, HY_WIDTH), y_r.reshape(B, L, RET_WIDTH), y_a.reshape(B, L, ATT_WIDTH),
                        proj, x, kv, w, l, tm=TOKEN_TILE)
    return x


def kernel(x_prompt, x_sample, mem_prompt, mem_sample, g_mix_pre, g_mix_post, w_in, hy_conv, hy_fw1, hy_fb1,
           hy_fw2, hy_fb2, hy_fw3, hy_bias, ret_decay, att_qnorm, att_knorm, w_branch, w_out, g_x_pre, g_x_post,
           g_mem, w_xq, w_xkv, w_xo, g_ff_pre, g_ff_post, w_ff1, w_ff2):
    w = dict(
        g_mix_pre=g_mix_pre, g_mix_post=g_mix_post, w_in=w_in.astype(BF16), hy_conv=hy_conv,
        hy_fw1=hy_fw1, hy_fb1=hy_fb1, hy_fw2=hy_fw2, hy_fb2=hy_fb2, hy_fw3=hy_fw3, hy_bias=hy_bias,
        ret_lg=jax.nn.log_sigmoid(ret_decay.astype(F32)), att_qnorm=att_qnorm, att_knorm=att_knorm,
        w_branch=w_branch.astype(BF16), w_out=w_out.astype(BF16), g_x_pre=g_x_pre, g_x_post=g_x_post,
        g_mem=g_mem, w_xq=w_xq.astype(BF16), w_xkv=w_xkv.astype(BF16), w_xo=w_xo.astype(BF16),
        g_ff_pre=g_ff_pre, g_ff_post=g_ff_post, w_ff1=w_ff1.astype(BF16), w_ff2=w_ff2.astype(BF16),
    )
    y_prompt = _trunk(x_prompt, mem_prompt, w, _position_tables(x_prompt.shape[1]))
    y_sample = _trunk(x_sample, mem_sample, w, _position_tables(x_sample.shape[1]))
    return (y_prompt, y_sample)
```

```python
import functools
import math

import jax
import jax.numpy as jnp
from jax import lax
from jax.experimental import pallas as pl
from jax.experimental.pallas import tpu as pltpu

F32 = jnp.float32
BF16 = jnp.bfloat16

D_MODEL = 1024
DEPTH = 4
GRID_W = 64
EPS = 1e-6
HY_WIDTH = 512
HY_SHORT = 3
HY_EMB = 33
HY_FH = 64
HY_SIN_FREQ = 1.0
HY_TARGET = 1e-2
HY_FAST = 0.3
HY_SLOW = 1.5
HY_SHIFT = 0.0
RET_HEADS = 4
RET_DK = 128
RET_DV = 128
RET_WIDTH = RET_HEADS * RET_DV
RET_THETA = 10000.0
ATT_HEADS = 4
ATT_KV_HEADS = 2
ATT_HD = 128
ATT_WIDTH = ATT_HEADS * ATT_HD
ROPE_THETA = 10000.0
X_HEADS = 4
X_HD = D_MODEL // X_HEADS
D_FF = 4 * D_MODEL
N_BRANCH = 3
MIX_WIDTH = HY_WIDTH + RET_WIDTH + ATT_WIDTH

COL_HY = 0
COL_X0C = 0
COL_ZF = HY_WIDTH
COL_RQ = 3 * HY_WIDTH
COL_RK = COL_RQ + RET_HEADS * RET_DK
COL_RV = COL_RK + RET_HEADS * RET_DK
COL_RG = COL_RV + RET_WIDTH
COL_AQ = COL_RG + RET_WIDTH
COL_AK = COL_AQ + ATT_WIDTH
COL_AV = COL_AK + ATT_KV_HEADS * ATT_HD
COL_GATE = COL_AV + ATT_KV_HEADS * ATT_HD
IN_COLS = COL_GATE + N_BRANCH * D_MODEL

LANES = 128
HALO = 8
VMEM_LIMIT = 56 * 1024 * 1024
DFT_N2 = 128
DFT_NB = 16
DFT_KB = 8
DFT_UNROLL = 4
DFT_PASSES = dict(s1=1, s2f=1, s2i=1, s3=1, k1=1, k2=1)
RET_BLOCK = 256
RET_UNROLL = 8
ATT_KEY_CHUNK = 1024
ATT_QUERY_TILE = 1024
FILTER_PASSES = 1
FF_CHUNK = 1024
TOKEN_TILE = 512
LOG2E = 1.4426950408889634


def _params(sem):
    return pltpu.CompilerParams(dimension_semantics=sem, vmem_limit_bytes=VMEM_LIMIT)


def _dot(a, b):
    return jnp.dot(a, b, preferred_element_type=F32)


def _dot_nt(a, b):
    return lax.dot_general(a, b, (((1,), (1,)), ((), ())), preferred_element_type=F32)


def _dot_tn(a, b):
    return lax.dot_general(a, b, (((0,), (0,)), ((), ())), preferred_element_type=F32)


def _rms(x, g):
    ms = jnp.mean(x * x, axis=-1, keepdims=True)
    return x * lax.rsqrt(ms + EPS) * g


def _sigmoid(x):
    return 1.0 / (1.0 + jnp.exp(-x))


def _split(x):
    hi = x.astype(BF16)
    lo = (x - hi.astype(F32)).astype(BF16)
    return hi, lo


def _dot3(a, b):
    return _dot(a[0], b[0]) + (_dot(a[0], b[1]) + _dot(a[1], b[0]))


def _mm(a, b, passes):
    if passes == 1:
        return _dot(a.astype(BF16), b.astype(BF16))
    return _dot3(_split(a.astype(F32)), _split(b.astype(F32)))


def _nm_body(x_ref, g_ref, w_ref, o_ref, *, tn):
    h = _rms(x_ref[...], g_ref[...]).astype(BF16)
    for j in range(w_ref.shape[1] // tn):
        o_ref[:, j * tn:(j + 1) * tn] = _dot(h, w_ref[:, j * tn:(j + 1) * tn]).astype(o_ref.dtype)


def _norm_matmul(x, g, w, *, tm, tn):
    T, D = x.shape
    NL, _, N = w.shape
    return pl.pallas_call(
        functools.partial(_nm_body, tn=tn),
        grid=(NL, T // tm),
        in_specs=[pl.BlockSpec((tm, D), lambda l, i: (i, 0)),
                  pl.BlockSpec((None, 1, D), lambda l, i: (l, 0, 0)),
                  pl.BlockSpec((None, D, N), lambda l, i: (l, 0, 0))],
        out_specs=pl.BlockSpec((None, tm, N), lambda l, i: (l, i, 0)),
        out_shape=jax.ShapeDtypeStruct((NL, T, N), BF16),
        compiler_params=_params(("parallel", "parallel")),
        name="norm_matmul",
    )(x, g.reshape(NL, 1, D), w)


def _rotate_half(x, cos, sin):
    return x * cos + pltpu.roll(x, x.shape[-1] // 2, 1) * sin


def _axial_rope(x, gain, cos, sin, scale):
    x = _rms(x, gain)
    lane = lax.broadcasted_iota(jnp.int32, x.shape, 1)
    lower = (lane % (ATT_HD // 2)) < (ATT_HD // 4)
    swapped = jnp.where(lower, pltpu.roll(x, LANES - ATT_HD // 4, 1), pltpu.roll(x, ATT_HD // 4, 1))
    return (x * cos + swapped * sin) * scale


_WIN_SEGMENTS = (
    [(COL_RQ, RET_WIDTH, "ret_q"), (COL_RK, RET_WIDTH, "ret_k"), (COL_RV, RET_WIDTH, None),
       (COL_RG, RET_WIDTH, None), (COL_AQ, ATT_WIDTH, "att_q"), (COL_AK, COL_GATE - COL_AK, "att_kv")]
    + [(COL_GATE + i * D_MODEL, D_MODEL, None) for i in range(N_BRANCH)])


def _win_body(x_ref, xp_ref, xn_ref, g_ref, w_ref, cw_ref, qn_ref, kn_ref, rc_ref, rs_ref, ac_ref, as_ref, o_ref,
              *, per_seq):
    tm = x_ref.shape[0]
    pos = pl.program_id(0) % per_seq
    x_ext = jnp.concatenate([xp_ref[...], x_ref[...], xn_ref[...]], axis=0)
    r_ext = lax.broadcasted_iota(jnp.int32, (tm + 2 * HALO, 1), 0)
    pad = ((r_ext < HALO) & (pos == 0)) | ((r_ext >= tm + HALO) & (pos == per_seq - 1))
    hf = jnp.where(pad, 0.0, _rms(x_ext, g_ref[...]))
    h_ext = hf.astype(BF16)
    h = hf[HALO:HALO + tm].astype(BF16)
    row = lax.broadcasted_iota(jnp.int32, (tm, HY_WIDTH), 0)
    conv = []
    for grp in range(3):
        cols = slice(COL_HY + grp * HY_WIDTH, COL_HY + (grp + 1) * HY_WIDTH)
        y_ext = _dot(h_ext, w_ref[:, cols])
        y = y_ext[HALO:HALO + tm]
        y_prev = y_ext[HALO - 1:HALO]
        y_next = y_ext[HALO + tm:HALO + tm + 1]
        prev = jnp.where(row == 0, y_prev, pltpu.roll(y, 1, 0))
        nxt = jnp.where(row == tm - 1, y_next, pltpu.roll(y, tm - 1, 0))
        cw = cw_ref[:, cols]
        conv.append(prev * cw[0:1] + y * cw[1:2] + nxt * cw[2:3])
    x0c, x1c, vc = conv
    o_ref[:, COL_X0C:COL_X0C + HY_WIDTH] = x0c.astype(o_ref.dtype)
    o_ref[:, COL_ZF:COL_ZF + HY_WIDTH] = (vc * x1c).astype(o_ref.dtype)
    o_ref[:, COL_ZF + HY_WIDTH:COL_RQ] = vc.astype(o_ref.dtype)
    for start, width, kind in _WIN_SEGMENTS:
        y = _dot(h, w_ref[:, start:start + width])
        if kind is None:
            o_ref[:, start:start + width] = y.astype(o_ref.dtype)
            continue
        for hh in range(width // LANES):
            yh = y[:, hh * LANES:(hh + 1) * LANES]
            if kind == "ret_q":
                yh = _rotate_half(yh, rc_ref[...], rs_ref[...])
            elif kind == "ret_k":
                yh = _rotate_half(yh, rc_ref[...], rs_ref[...]) * (RET_DK ** -0.5)
            elif kind == "att_q":
                yh = _axial_rope(yh, qn_ref[...], ac_ref[...], as_ref[...], ATT_HD ** -0.5 * LOG2E)
            elif hh < ATT_KV_HEADS:
                yh = _axial_rope(yh, kn_ref[...], ac_ref[...], as_ref[...], 1.0)
            o_ref[:, start + hh * LANES:start + (hh + 1) * LANES] = yh.astype(o_ref.dtype)


def _input_projection(x, g, w_stack, l, conv_w, qn, kn, pt, L, *, tm):
    T, D = x.shape
    N = w_stack.shape[2]
    per_seq = L // tm
    halo_per_tile = tm // HALO
    vec = lambda n: pl.BlockSpec((1, n), lambda i: (0, 0))
    tab = pl.BlockSpec((tm, LANES), lambda i: (i % per_seq, 0))
    return pl.pallas_call(
        functools.partial(_win_body, per_seq=per_seq),
        grid=(T // tm,),
        in_specs=[pl.BlockSpec((tm, D), lambda i: (i, 0)),
                  pl.BlockSpec((HALO, D), lambda i: (jnp.maximum(i * halo_per_tile - 1, 0), 0)),
                  pl.BlockSpec((HALO, D), lambda i: (jnp.minimum((i + 1) * halo_per_tile, T // HALO - 1), 0)),
                  vec(D),
                  pl.BlockSpec((None, D, N), lambda i: (l, 0, 0), pipeline_mode=pl.Buffered(1)),
                  pl.BlockSpec((HY_SHORT, 3 * HY_WIDTH), lambda i: (0, 0)),
                  vec(ATT_HD), vec(ATT_HD), tab, tab, tab, tab],
        out_specs=pl.BlockSpec((tm, N), lambda i: (i, 0)),
        out_shape=jax.ShapeDtypeStruct((T, N), BF16),
        compiler_params=_params(("parallel",)),
        name="input_projection",
    )(x, x, x, g.reshape(1, D), w_stack, conv_w, qn.reshape(1, ATT_HD), kn.reshape(1, ATT_HD),
      pt["ret_cos"], pt["ret_sin"], pt["att_cos"], pt["att_sin"])


def _filt_body(f_ref, w1_ref, b1_ref, w2_ref, b2_ref, w3_ref, dl_ref, o_ref, *, L):
    tl = f_ref.shape[0]
    f = f_ref[...]
    h = jnp.sin(HY_SIN_FREQ * (_mm(f, w1_ref[...], 3) + b1_ref[...]))
    h = jnp.sin(HY_SIN_FREQ * (_mm(h, w2_ref[...], FILTER_PASSES) + b2_ref[...]))
    filt = _mm(h, w3_ref[...], FILTER_PASSES)
    mod = jnp.exp(-f[:, 0:1] * dl_ref[...]) + HY_SHIFT
    row = pl.program_id(1) * tl + lax.broadcasted_iota(jnp.int32, (tl, HY_WIDTH), 0)
    sel = jnp.where(row < L, filt[:, :HY_WIDTH], filt[:, HY_WIDTH:])
    o_ref[...] = jnp.where(row == L, 0.0, sel * mod)


def _hyena_features(L):
    t = jnp.linspace(0.0, 1.0, L, dtype=F32)[:, None]
    bands = (HY_EMB - 1) // 2
    w = 2.0 * math.pi * jnp.arange(L, dtype=F32)[:, None] / L
    f = jnp.linspace(1e-4, bands - 1, bands, dtype=F32)[None, :]
    z = jnp.concatenate([t, jnp.cos(f * w), -jnp.sin(f * w)], axis=-1)
    zc = jnp.concatenate([z, z[:1], jnp.flip(z[1:], axis=0)], axis=0)
    return jnp.pad(zc, ((0, 0), (0, LANES - HY_EMB)))


def _hyena_filter(feats, w1, b1, w2, b2, w3, L):
    tl = TOKEN_TILE
    NL = w1.shape[0]
    pad = LANES - HY_FH
    w1p = jnp.pad(w1, ((0, 0), (0, LANES - HY_EMB), (0, pad)))
    b1p = jnp.pad(b1, ((0, 0), (0, pad))).reshape(NL, 1, LANES)
    w2p = jnp.pad(w2, ((0, 0), (0, pad), (0, pad)))
    b2p = jnp.pad(b2, ((0, 0), (0, pad))).reshape(NL, 1, LANES)
    w3p = jnp.pad(w3, ((0, 0), (0, pad), (0, 0)))
    min_decay = math.log(HY_TARGET) / HY_SLOW
    max_decay = math.log(HY_TARGET) / HY_FAST
    deltas = jnp.abs(jnp.linspace(min_decay, max_decay, HY_WIDTH, dtype=F32)).reshape(1, HY_WIDTH)
    layer = lambda r, c: pl.BlockSpec((None, r, c), lambda l, i: (l, 0, 0))
    return pl.pallas_call(
        functools.partial(_filt_body, L=L),
        grid=(NL, 2 * L // tl),
        in_specs=[pl.BlockSpec((tl, LANES), lambda l, i: (i, 0)),
                  layer(LANES, LANES), layer(1, LANES), layer(LANES, LANES), layer(1, LANES),
                  layer(LANES, 2 * HY_WIDTH), pl.BlockSpec((1, HY_WIDTH), lambda l, i: (0, 0))],
        out_specs=pl.BlockSpec((None, tl, HY_WIDTH), lambda l, i: (l, i, 0)),
        out_shape=jax.ShapeDtypeStruct((NL, 2 * L, HY_WIDTH), F32),
        compiler_params=_params(("parallel", "parallel")),
        name="hyena_filter",
    )(feats, w1p, b1p, w2p, b2p, w3p, deltas)


def _expanded(fn, rows, cols):
    small = fn(lax.broadcasted_iota(jnp.int32, (rows, cols), 0), lax.broadcasted_iota(jnp.int32, (rows, cols), 1))
    hp = functools.partial(jnp.dot, precision=lax.Precision.HIGHEST)
    pick_r = (lax.broadcasted_iota(jnp.int32, (rows * DFT_NB, rows), 0) // DFT_NB
              == lax.broadcasted_iota(jnp.int32, (rows * DFT_NB, rows), 1)).astype(F32)
    pick_c = (lax.broadcasted_iota(jnp.int32, (cols, cols * DFT_NB), 1) // DFT_NB
              == lax.broadcasted_iota(jnp.int32, (cols, cols * DFT_NB), 0)).astype(F32)
    ri = lax.broadcasted_iota(jnp.int32, (rows * DFT_NB, cols * DFT_NB), 0)
    ci = lax.broadcasted_iota(jnp.int32, (rows * DFT_NB, cols * DFT_NB), 1)
    return jnp.where(ri % DFT_NB == ci % DFT_NB, hp(hp(pick_r, small), pick_c), 0.0)


def _dft_tables(L):
    N = 2 * L
    N2 = DFT_N2
    N1 = N // N2
    N1h = N1 // 2

    def cs1(k1, n1):
        ang = ((k1 * n1) % N1).astype(F32) * (2.0 * math.pi / N1)
        return jnp.cos(ang), jnp.sin(ang)

    def f1_pair(r, c):
        cs, sn = cs1(r // 2, c % N1h)
        re_row, first = (r % 2) == 0, c < N1h
        return jnp.where(re_row, jnp.where(first, cs, sn), jnp.where(first, -sn, cs))

    def f1_real(r, c):
        cs, sn = cs1(r // 2, c)
        return jnp.where((r % 2) == 0, cs, -sn)

    def f3_pair(r, c):
        cs, sn = cs1(c // 2, r % N1h)
        re_out, re_in = r < N1h, (c % 2) == 0
        return jnp.where(re_out, jnp.where(re_in, cs, -sn), jnp.where(re_in, sn, cs)) * (1.0 / N)

    kk = (jnp.arange(N1, dtype=jnp.int32)[:, None, None]
          + N1 * jnp.arange(N2, dtype=jnp.int32)[None, :, None])
    n2 = jnp.arange(N2, dtype=jnp.int32)[None, None, :]
    a2 = ((kk * n2) % N).astype(F32) * (2.0 * math.pi / N)
    gr, gi = jnp.cos(a2), -jnp.sin(a2)
    gtr, gti = jnp.swapaxes(gr, 1, 2), jnp.swapaxes(gi, 1, 2)
    g_fwd = jnp.concatenate([jnp.concatenate([gr, -gi], axis=2),
                             jnp.concatenate([gi, gr], axis=2)], axis=1)
    g_inv = jnp.concatenate([jnp.concatenate([gtr, gti], axis=2),
                             jnp.concatenate([-gti, gtr], axis=2)], axis=1)
    return dict(N1=N1, N1h=N1h, N2=N2,
                f1_pair=_expanded(f1_pair, 2 * N1, 2 * N1h), f1_real=_expanded(f1_real, 2 * N1, N1),
                f3_pair=_expanded(f3_pair, 2 * N1h, 2 * N1), g_fwd=g_fwd, g_inv=g_inv)


def _stage_dtype(passes):
    return BF16 if passes == 1 else F32


def _dft1_body(x_ref, f_ref, o_ref, *, passes):
    P, K, NB, C = x_ref.shape
    y = _mm(f_ref[...], x_ref[...].reshape(P * K * NB, C), passes)
    o_ref[0] = y.astype(o_ref.dtype).reshape(o_ref.shape[1:])


def _dft_stage1(x, f, *, P, passes, out_dtype, col=0):
    Bx, K, N2, _ = x.shape
    C = HY_WIDTH
    N1 = f.shape[0] // (2 * DFT_NB)
    return pl.pallas_call(
        functools.partial(_dft1_body, passes=passes),
        grid=(Bx // P, N2 // DFT_NB),
        in_specs=[pl.BlockSpec((P, K, DFT_NB, C), lambda b, j: (b, 0, j, col)),
                  pl.BlockSpec(f.shape, lambda b, j: (0, 0), pipeline_mode=pl.Buffered(1))],
        out_specs=pl.BlockSpec((1, N1, 2, DFT_NB, C), lambda b, j: (b, 0, 0, j, 0)),
        out_shape=jax.ShapeDtypeStruct((Bx // P, N1, 2, N2, C), out_dtype),
        compiler_params=_params(("parallel", "parallel")),
        name="dft_stage1",
    )(x, f)


def _spec_body(a_ref, g_ref, o_ref, *, passes):
    def step(k, carry):
        o_ref[k] = _mm(g_ref[k], a_ref[0, k], passes)
        return carry

    lax.fori_loop(0, a_ref.shape[1], step, 0, unroll=DFT_UNROLL)


def _dft_spectrum(a, g, *, kb, passes):
    NL, N1, R, C = a.shape
    return pl.pallas_call(
        functools.partial(_spec_body, passes=passes),
        grid=(N1 // kb, NL),
        in_specs=[pl.BlockSpec((1, kb, R, C), lambda i, l: (l, i, 0, 0)),
                  pl.BlockSpec((kb, R, R), lambda i, l: (i, 0, 0))],
        out_specs=pl.BlockSpec((None, kb, R, C), lambda i, l: (l, i, 0, 0)),
        out_shape=jax.ShapeDtypeStruct((NL, N1, R, C), F32),
        compiler_params=_params(("parallel", "parallel")),
        name="dft_spectrum",
    )(a, g)


def _dft2_body(a_ref, gf_ref, gi_ref, ks_ref, o_ref, *, passes_fwd, passes_inv):
    H = a_ref.shape[2] // 2

    def step(k, carry):
        y = _mm(gf_ref[k], a_ref[0, k], passes_fwd)
        yr, yi = y[:H], y[H:]
        kr, ki = ks_ref[k, :H, :], ks_ref[k, H:, :]
        z = jnp.concatenate([yr * kr - yi * ki, yr * ki + yi * kr], axis=0)
        o_ref[0, k] = _mm(gi_ref[k], z, passes_inv).astype(o_ref.dtype)
        return carry

    lax.fori_loop(0, a_ref.shape[1], step, 0, unroll=DFT_UNROLL)


def _dft_stage2(a, gf, gi, ks, l, *, kb, passes_fwd, passes_inv, out_dtype):
    Bp, N1, R, C = a.shape
    dat = pl.BlockSpec((1, kb, R, C), lambda i, b: (b, i, 0, 0))
    mat = pl.BlockSpec((kb, R, R), lambda i, b: (i, 0, 0))
    return pl.pallas_call(
        functools.partial(_dft2_body, passes_fwd=passes_fwd, passes_inv=passes_inv),
        grid=(N1 // kb, Bp),
        in_specs=[dat, mat, mat, pl.BlockSpec((None, kb, R, C), lambda i, b: (l, i, 0, 0))],
        out_specs=dat,
        out_shape=jax.ShapeDtypeStruct((Bp, N1, R, C), out_dtype),
        compiler_params=_params(("parallel", "parallel")),
        name="dft_stage2",
    )(a, gf, gi, ks)


def _dft3_body(t_ref, f_ref, zf_ref, x0_ref, bias_ref, o_ref, *, passes):
    _, N1, _, NB, C = t_ref.shape
    y = _mm(f_ref[...], t_ref[0].reshape(N1 * 2 * NB, C), passes).reshape(zf_ref.shape)
    o_ref[...] = ((y + zf_ref[...].astype(F32) * bias_ref[...]) * x0_ref[...].astype(F32)).astype(o_ref.dtype)


def _dft_stage3(t, f, proj4, bias, *, passes):
    Bp, N1, _, N2, C = t.shape
    B, N1h = proj4.shape[:2]
    pair = lambda col=0: pl.BlockSpec((2, N1h, DFT_NB, C), lambda b, j: (b, 0, j, col))
    return pl.pallas_call(
        functools.partial(_dft3_body, passes=passes),
        grid=(Bp, N2 // DFT_NB),
        in_specs=[pl.BlockSpec((1, N1, 2, DFT_NB, C), lambda b, j: (b, 0, 0, j, 0)),
                  pl.BlockSpec(f.shape, lambda b, j: (0, 0), pipeline_mode=pl.Buffered(1)),
                  pair(COL_ZF // C), pair(COL_X0C // C),
                  pl.BlockSpec((1, C), lambda b, j: (0, 0))],
        out_specs=pair(),
        out_shape=jax.ShapeDtypeStruct((B, N1h, N2, C), BF16),
        compiler_params=_params(("parallel", "parallel")),
        name="dft_stage3",
    )(t, f, proj4, proj4, bias)


def _hyena_mixer(proj, kspec, l, bias, tabs):
    B, L, _ = proj.shape
    N1, N1h, N2 = tabs["N1"], tabs["N1h"], tabs["N2"]
    C = HY_WIDTH
    ps = DFT_PASSES
    proj4 = proj.reshape(B, N1h, N2, IN_COLS)
    f1 = tabs["f1_pair"].astype(_stage_dtype(ps["s1"]))
    a = _dft_stage1(proj4, f1, P=2, passes=ps["s1"], out_dtype=_stage_dtype(ps["s2f"]), col=COL_ZF // C)
    gdt = _stage_dtype(min(ps["s2f"], ps["s2i"]))
    t = _dft_stage2(a.reshape(B // 2, N1, 2 * N2, C), tabs["g_fwd"].astype(gdt), tabs["g_inv"].astype(gdt),
                    kspec, l, kb=DFT_KB, passes_fwd=ps["s2f"], passes_inv=ps["s2i"],
                    out_dtype=_stage_dtype(ps["s3"]))
    f3 = tabs["f3_pair"].astype(_stage_dtype(ps["s3"]))
    y = _dft_stage3(t.reshape(B // 2, N1, 2, N2, C), f3, proj4, bias.astype(F32).reshape(1, C), passes=ps["s3"])
    return y.reshape(B * L, C)


def _hyena_spectra(feats, w1, b1, w2, b2, w3, L, tabs):
    N1, N2 = tabs["N1"], tabs["N2"]
    C = HY_WIDTH
    ps = DFT_PASSES
    NL = w1.shape[0]
    kfull = _hyena_filter(feats, w1, b1, w2, b2, w3, L)
    a = _dft_stage1(kfull.reshape(NL, N1, N2, C), tabs["f1_real"].astype(_stage_dtype(ps["k1"])), P=1, passes=ps["k1"],
                    out_dtype=_stage_dtype(ps["k2"]))
    return _dft_spectrum(a.reshape(NL, N1, 2 * N2, C), tabs["g_fwd"].astype(_stage_dtype(ps["k2"])), kb=DFT_KB,
                         passes=ps["k2"])


def _ret_body(lg_ref, q_ref, k_ref, v_ref, g_ref, o_ref, st_ref):
    C = RET_BLOCK
    L = q_ref.shape[1]
    nblk = L // C
    dk = RET_DK
    h = pl.program_id(1)
    lgf = lg_ref[0, h]
    lgb = lg_ref[1, h]
    ii = lax.broadcasted_iota(jnp.int32, (C, C), 0)
    jj = lax.broadcasted_iota(jnp.int32, (C, C), 1)
    diff = (ii - jj).astype(F32)
    dm = jnp.where(diff >= 0, jnp.exp(jnp.maximum(diff, 0.0) * lgf), jnp.exp(jnp.maximum(-diff, 0.0) * lgb))
    pos = lax.broadcasted_iota(jnp.int32, (C, dk), 0).astype(F32)
    qw_f = jnp.exp((pos + 1.0) * lgf)
    kw_f = jnp.exp((C - 1.0 - pos) * lgf)
    qw_b = jnp.exp((C - pos) * lgb)
    kw_b = jnp.exp(pos * lgb)
    cd_f = jnp.exp(C * lgf)
    cd_b = jnp.exp(C * lgb)

    def block_kv(n, carry):
        sl = pl.ds(pl.multiple_of(n * C, C), C)
        k = k_ref[0, sl, :].astype(F32)
        kk = jnp.concatenate([k * kw_f, k * kw_b], axis=1).astype(BF16)
        st_ref[n] = _dot_tn(kk, v_ref[0, sl, :].astype(BF16))
        return carry

    lax.fori_loop(0, nblk, block_kv, 0, unroll=RET_UNROLL)

    def scan_f(n, S):
        kv = st_ref[n, :dk, :]
        st_ref[n, :dk, :] = S
        return S * cd_f + kv

    lax.fori_loop(0, nblk, scan_f, jnp.zeros((dk, RET_DV), F32))

    def scan_b(m, S):
        n = nblk - 1 - m
        kv = st_ref[n, dk:, :]
        st_ref[n, dk:, :] = S
        return S * cd_b + kv

    lax.fori_loop(0, nblk, scan_b, jnp.zeros((dk, RET_DV), F32))

    def block_out(n, carry):
        sl = pl.ds(pl.multiple_of(n * C, C), C)
        q = q_ref[0, sl, :].astype(BF16)
        scores = _dot_nt(q, k_ref[0, sl, :].astype(BF16)) * dm
        qf = q.astype(F32)
        qq = jnp.concatenate([qf * qw_f, qf * qw_b], axis=1).astype(BF16)
        o = _dot(scores.astype(BF16), v_ref[0, sl, :].astype(BF16)) + _dot(qq, st_ref[n].astype(BF16))
        mu = jnp.mean(o, axis=-1, keepdims=True)
        oc = o - mu
        on = oc * lax.rsqrt(jnp.mean(oc * oc, axis=-1, keepdims=True) + EPS)
        g = g_ref[0, sl, :].astype(F32)
        o_ref[0, sl, :] = (g * _sigmoid(g) * on).astype(o_ref.dtype)
        return carry

    lax.fori_loop(0, nblk, block_out, 0, unroll=RET_UNROLL)


def _retention_mixer(proj, log_gamma):
    B, L, _ = proj.shape
    blk = lambda col: pl.BlockSpec((1, L, LANES), lambda b, h, c=col // LANES: (b, 0, c + h))
    y = pl.pallas_call(
        _ret_body,
        grid=(B, RET_HEADS),
        in_specs=[pl.BlockSpec(memory_space=pltpu.SMEM),
                  blk(COL_RQ), blk(COL_RK), blk(COL_RV), blk(COL_RG)],
        out_specs=pl.BlockSpec((1, L, LANES), lambda b, h: (b, 0, h)),
        out_shape=jax.ShapeDtypeStruct((B, L, RET_WIDTH), BF16),
        scratch_shapes=[pltpu.VMEM((L // RET_BLOCK, 2 * RET_DK, RET_DV), F32)],
        compiler_params=_params(("parallel", "parallel")),
        name="retention",
    )(log_gamma, proj, proj, proj, proj)
    return y.reshape(B * L, RET_WIDTH)


def _att_body(q_ref, k_ref, v_ref, o_ref):
    G = ATT_HEADS // ATT_KV_HEADS
    tq = q_ref.shape[1]
    kc = min(ATT_KEY_CHUNK, k_ref.shape[1])
    nchunk = k_ref.shape[1] // kc
    q = q_ref[0].astype(BF16)
    qs = jnp.concatenate([q[:, g * ATT_HD:(g + 1) * ATT_HD] for g in range(G)], axis=0)
    m = jnp.full((G * tq, 1), -jnp.inf, F32)
    l = jnp.zeros((G * tq, 1), F32)
    acc = jnp.zeros((G * tq, ATT_HD), F32)
    s_next = _dot_nt(qs, k_ref[0, 0:kc, :].astype(BF16))
    for c in range(nchunk):
        s = s_next
        if c + 1 < nchunk:
            s_next = _dot_nt(qs, k_ref[0, (c + 1) * kc:(c + 2) * kc, :].astype(BF16))
        m_new = jnp.maximum(m, jnp.max(s, axis=-1, keepdims=True))
        alpha = jnp.exp2(m - m_new)
        p = jnp.exp2(s - m_new)
        l = alpha * l + jnp.sum(p, axis=-1, keepdims=True)
        acc = alpha * acc + _dot(p.astype(BF16), v_ref[0, c * kc:(c + 1) * kc, :].astype(BF16))
        m = m_new
    o = acc * (1.0 / l)
    for g in range(G):
        o_ref[0, :, g * ATT_HD:(g + 1) * ATT_HD] = o[g * tq:(g + 1) * tq].astype(o_ref.dtype)


def _attention_mixer(proj, *, tq):
    B, L, _ = proj.shape
    G = ATT_HEADS // ATT_KV_HEADS
    tq = min(tq, L)
    y = pl.pallas_call(
        _att_body,
        grid=(B, ATT_KV_HEADS, L // tq),
        in_specs=[pl.BlockSpec((1, tq, G * ATT_HD), lambda b, h, i: (b, i, COL_AQ // (G * ATT_HD) + h)),
                  pl.BlockSpec((1, L, ATT_HD), lambda b, h, i: (b, 0, COL_AK // ATT_HD + h)),
                  pl.BlockSpec((1, L, ATT_HD), lambda b, h, i: (b, 0, COL_AV // ATT_HD + h))],
        out_specs=pl.BlockSpec((1, tq, G * ATT_HD), lambda b, h, i: (b, i, h)),
        out_shape=jax.ShapeDtypeStruct((B, L, ATT_WIDTH), BF16),
        compiler_params=_params(("parallel", "parallel", "arbitrary")),
        name="attention",
    )(proj, proj, proj)
    return y.reshape(B * L, ATT_WIDTH)


def _tail_body(yh_ref, yr_ref, ya_ref, ga_ref, gb_ref, x_ref, kv_ref, wb_ref, wo_ref, gmix_ref,
               gxpre_ref, wq_ref, wxo_ref, gxpost_ref, gfpre_ref, w1_ref, w2_ref, gfpost_ref, o_ref, u_ref):
    D = D_MODEL
    ph = _dot(yh_ref[0], wb_ref[0:HY_WIDTH, :])
    pr = _dot(yr_ref[0], wb_ref[HY_WIDTH:HY_WIDTH + RET_WIDTH, :])
    pa = _dot(ya_ref[0], wb_ref[HY_WIDTH + RET_WIDTH:, :])
    ga = ga_ref[0].astype(F32)
    gb = gb_ref[0].astype(F32)
    hd = D // 2
    g0 = _sigmoid(ga[:, :D])
    g1 = _sigmoid(jnp.concatenate([ga[:, D:], gb[:, :hd]], axis=1))
    g2 = _sigmoid(gb[:, hd:])
    merged = g0 * ph + g1 * pr + g2 * pa
    x = x_ref[0] + _rms(_dot(merged.astype(BF16), wo_ref[...]), gmix_ref[...])
    h = _rms(x, gxpre_ref[...]).astype(BF16)
    q = (_dot(h, wq_ref[...]) * (X_HD ** -0.5)).astype(BF16)
    kv = kv_ref[0]
    outs = []
    for hh in range(X_HEADS):
        sl = slice(hh * X_HD, (hh + 1) * X_HD)
        s = _dot_nt(q[:, sl], kv[:, sl])
        p = jnp.exp(s - jnp.max(s, axis=-1, keepdims=True))
        l = jnp.sum(p, axis=-1, keepdims=True)
        vh = kv[:, D + hh * X_HD:D + (hh + 1) * X_HD]
        outs.append(_dot(p.astype(BF16), vh) * (1.0 / l))
    o = jnp.concatenate(outs, axis=1).astype(BF16)
    x = x + _rms(_dot(o, wxo_ref[...]), gxpost_ref[...])
    h = _rms(x, gfpre_ref[...]).astype(BF16)
    for c in range(D_FF // FF_CHUNK):
        sl = slice(c * FF_CHUNK, (c + 1) * FF_CHUNK)
        u_ref[:, sl] = jnp.square(jnp.maximum(_dot(h, w1_ref[:, sl]), 0.0)).astype(BF16)
    o_ref[0] = x + _rms(_dot(u_ref[...], w2_ref[...]), gfpost_ref[...])


def _layer_tail(yh, yr, ya, proj, x, kv, w, l, *, tm):
    B, L, D = x.shape
    M = kv.shape[2]
    gw = N_BRANCH * D // 2
    tok = lambda width, col=0: pl.BlockSpec((1, tm, width), lambda b, i, col=col: (b, i, col))
    vec = pl.BlockSpec((1, D), lambda b, i: (0, 0))
    res = lambda shape: pl.BlockSpec((None,) + shape, lambda b, i: (l, 0, 0), pipeline_mode=pl.Buffered(1))
    g = lambda name: w[name][l].reshape(1, D)
    return pl.pallas_call(
        _tail_body,
        grid=(B, L // tm),
        in_specs=[tok(HY_WIDTH), tok(RET_WIDTH), tok(ATT_WIDTH), tok(gw, COL_GATE // gw), tok(gw, COL_GATE // gw + 1),
                  tok(D), pl.BlockSpec((None, 1, M, 2 * D), lambda b, i: (l, b, 0, 0)),
                  res((MIX_WIDTH, D)), res((D, D)), vec,
                  vec, res((D, D)), res((D, D)), vec,
                  vec, res((D, D_FF)), res((D_FF, D)), vec],
        out_specs=tok(D),
        out_shape=jax.ShapeDtypeStruct((B, L, D), F32),
        scratch_shapes=[pltpu.VMEM((tm, D_FF), BF16)],
        compiler_params=_params(("parallel", "parallel")),
        name="layer_tail",
    )(yh, yr, ya, proj, proj, x, kv, w["w_branch"], w["w_out"], g("g_mix_post"),
      g("g_x_pre"), w["w_xq"], w["w_xo"], g("g_x_post"),
      g("g_ff_pre"), w["w_ff1"], w["w_ff2"], g("g_ff_post"))


def _rotary_tables(pos, dim, theta):
    inv = theta ** (-jnp.arange(0, dim, 2, dtype=F32) / dim)
    ang = pos.astype(F32)[:, None] * inv[None, :]
    return jnp.cos(ang), jnp.sin(ang)


def _position_tables(L):
    pos = jnp.arange(L)
    rows = L // GRID_W
    row_ids = jnp.repeat(jnp.arange(rows), GRID_W)
    col_ids = jnp.tile(jnp.arange(GRID_W), rows)
    rcos, rsin = _rotary_tables(pos, RET_DK, RET_THETA)
    rc, rs = _rotary_tables(row_ids, ATT_HD // 2, ROPE_THETA)
    cc, cs = _rotary_tables(col_ids, ATT_HD // 2, ROPE_THETA)
    return dict(
        ret_cos=jnp.concatenate([rcos, rcos], axis=-1),
        ret_sin=jnp.concatenate([-rsin, rsin], axis=-1),
        att_cos=jnp.concatenate([rc, rc, cc, cc], axis=-1),
        att_sin=jnp.concatenate([-rs, rs, -cs, cs], axis=-1),
        feats=_hyena_features(L),
        dft=_dft_tables(L),
    )


def _trunk(x, mem, w, pt):
    B, L, D = x.shape
    M = mem.shape[1]
    kspec = _hyena_spectra(pt["feats"], w["hy_fw1"], w["hy_fb1"], w["hy_fw2"], w["hy_fb2"], w["hy_fw3"], L, pt["dft"])
    kv = _norm_matmul(mem.reshape(B * M, D), w["g_mem"], w["w_xkv"], tm=TOKEN_TILE, tn=FF_CHUNK)
    kv = kv.reshape(-1, B, M, 2 * D)
    for l in range(DEPTH):
        proj = _input_projection(x.reshape(B * L, D), w["g_mix_pre"][l], w["w_in"], l, w["hy_conv"][l],
                                 w["att_qnorm"][l], w["att_knorm"][l], pt, L, tm=TOKEN_TILE).reshape(B, L, IN_COLS)
        y_h = _hyena_mixer(proj, kspec, l, w["hy_bias"][l], pt["dft"])
        y_r = _retention_mixer(proj, w["ret_lg"][l])
        y_a = _attention_mixer(proj, tq=ATT_QUERY_TILE)
        x = _layer_tail(y_h.reshape(B, L, HY_WIDTH), y_r.reshape(B, L, RET_WIDTH), y_a.reshape(B, L, ATT_WIDTH),
                        proj, x, kv, w, l, tm=TOKEN_TILE)
    return x


def kernel(x_prompt, x_sample, mem_prompt, mem_sample, g_mix_pre, g_mix_post, w_in, hy_conv, hy_fw1, hy_fb1,
           hy_fw2, hy_fb2, hy_fw3, hy_bias, ret_decay, att_qnorm, att_knorm, w_branch, w_out, g_x_pre, g_x_post,
           g_mem, w_xq, w_xkv, w_xo, g_ff_pre, g_ff_post, w_ff1, w_ff2):
    w = dict(
        g_mix_pre=g_mix_pre, g_mix_post=g_mix_post, w_in=w_in.astype(BF16), hy_conv=hy_conv,
        hy_fw1=hy_fw1, hy_fb1=hy_fb1, hy_fw2=hy_fw2, hy_fb2=hy_fb2, hy_fw3=hy_fw3, hy_bias=hy_bias,
        ret_lg=jax.nn.log_sigmoid(ret_decay.astype(F32)), att_qnorm=att_qnorm, att_knorm=att_knorm,
        w_branch=w_branch.astype(BF16), w_out=w_out.astype(BF16), g_x_pre=g_x_pre, g_x_post=g_x_post,
        g_mem=g_mem, w_xq=w_xq.astype(BF16), w_xkv=w_xkv.astype(BF16), w_xo=w_xo.astype(BF16),
        g_ff_pre=g_ff_pre, g_ff_post=g_ff_post, w_ff1=w_ff1.astype(BF16), w_ff2=w_ff2.astype(BF16),
    )
    y_prompt = _trunk(x_prompt, mem_prompt, w, _position_tables(x_prompt.shape[1]))
    y_sample = _trunk(x_sample, mem_sample, w, _position_tables(x_sample.shape[1]))
    return (y_prompt, y_sample)
```

```python
import functools
import math

import jax
import jax.numpy as jnp
from jax import lax
from jax.experimental import pallas as pl
from jax.experimental.pallas import tpu as pltpu

F32 = jnp.float32
BF16 = jnp.bfloat16

D_MODEL = 1024
DEPTH = 4
GRID_W = 64
EPS = 1e-6
HY_WIDTH = 512
HY_SHORT = 3
HY_EMB = 33
HY_FH = 64
HY_SIN_FREQ = 1.0
HY_TARGET = 1e-2
HY_FAST = 0.3
HY_SLOW = 1.5
HY_SHIFT = 0.0
RET_HEADS = 4
RET_DK = 128
RET_DV = 128
RET_WIDTH = RET_HEADS * RET_DV
RET_THETA = 10000.0
ATT_HEADS = 4
ATT_KV_HEADS = 2
ATT_HD = 128
ATT_WIDTH = ATT_HEADS * ATT_HD
ROPE_THETA = 10000.0
X_HEADS = 4
X_HD = D_MODEL // X_HEADS
D_FF = 4 * D_MODEL
N_BRANCH = 3
MIX_WIDTH = HY_WIDTH + RET_WIDTH + ATT_WIDTH

COL_HY = 0
COL_X0C = 0
COL_ZF = HY_WIDTH
COL_RQ = 3 * HY_WIDTH
COL_RK = COL_RQ + RET_HEADS * RET_DK
COL_RV = COL_RK + RET_HEADS * RET_DK
COL_RG = COL_RV + RET_WIDTH
COL_AQ = COL_RG + RET_WIDTH
COL_AK = COL_AQ + ATT_WIDTH
COL_AV = COL_AK + ATT_KV_HEADS * ATT_HD
COL_GATE = COL_AV + ATT_KV_HEADS * ATT_HD
IN_COLS = COL_GATE + N_BRANCH * D_MODEL

LANES = 128
HALO = 8
VMEM_LIMIT = 56 * 1024 * 1024
DFT_N2 = 128
DFT_NB = 16
DFT_KB = 8
DFT_UNROLL = 8
DFT_PASSES = dict(s1=1, s2f=1, s2i=1, s3=1, k1=1, k2=1)
RET_BLOCK = 256
RET_UNROLL = 16
ATT_KEY_CHUNK = 1024
ATT_QUERY_TILE = 1024
FILTER_PASSES = 1
FF_CHUNK = 1024
TOKEN_TILE = 512
LOG2E = 1.4426950408889634


def _params(sem):
    return pltpu.CompilerParams(dimension_semantics=sem, vmem_limit_bytes=VMEM_LIMIT)


def _dot(a, b):
    return jnp.dot(a, b, preferred_element_type=F32)


def _dot_nt(a, b):
    return lax.dot_general(a, b, (((1,), (1,)), ((), ())), preferred_element_type=F32)


def _dot_tn(a, b):
    return lax.dot_general(a, b, (((0,), (0,)), ((), ())), preferred_element_type=F32)


def _rms(x, g):
    ms = jnp.mean(x * x, axis=-1, keepdims=True)
    return x * lax.rsqrt(ms + EPS) * g


def _sigmoid(x):
    return 0.5 * jnp.tanh(0.5 * x) + 0.5


def _split(x):
    hi = x.astype(BF16)
    lo = (x - hi.astype(F32)).astype(BF16)
    return hi, lo


def _dot3(a, b):
    return _dot(a[0], b[0]) + (_dot(a[0], b[1]) + _dot(a[1], b[0]))


def _mm(a, b, passes):
    if passes == 1:
        return _dot(a.astype(BF16), b.astype(BF16))
    return _dot3(_split(a.astype(F32)), _split(b.astype(F32)))


def _nm_body(x_ref, g_ref, w_ref, o_ref, *, tn):
    h = _rms(x_ref[...], g_ref[...]).astype(BF16)
    for j in range(w_ref.shape[1] // tn):
        o_ref[:, j * tn:(j + 1) * tn] = _dot(h, w_ref[:, j * tn:(j + 1) * tn]).astype(o_ref.dtype)


def _norm_matmul(x, g, w, *, tm, tn):
    T, D = x.shape
    NL, _, N = w.shape
    return pl.pallas_call(
        functools.partial(_nm_body, tn=tn),
        grid=(NL, T // tm),
        in_specs=[pl.BlockSpec((tm, D), lambda l, i: (i, 0)),
                  pl.BlockSpec((None, 1, D), lambda l, i: (l, 0, 0)),
                  pl.BlockSpec((None, D, N), lambda l, i: (l, 0, 0))],
        out_specs=pl.BlockSpec((None, tm, N), lambda l, i: (l, i, 0)),
        out_shape=jax.ShapeDtypeStruct((NL, T, N), BF16),
        compiler_params=_params(("parallel", "parallel")),
        name="norm_matmul",
    )(x, g.reshape(NL, 1, D), w)


def _rotate_half(x, cos, sin):
    return x * cos + pltpu.roll(x, x.shape[-1] // 2, 1) * sin


def _axial_rope(x, gain, cos, sin, scale):
    x = _rms(x, gain)
    lane = lax.broadcasted_iota(jnp.int32, x.shape, 1)
    lower = (lane % (ATT_HD // 2)) < (ATT_HD // 4)
    swapped = jnp.where(lower, pltpu.roll(x, LANES - ATT_HD // 4, 1), pltpu.roll(x, ATT_HD // 4, 1))
    return (x * cos + swapped * sin) * scale


_WIN_SEGMENTS = (
    [(COL_RQ, RET_WIDTH, "ret_q"), (COL_RK, RET_WIDTH, "ret_k"), (COL_RV, RET_WIDTH, None),
       (COL_RG, RET_WIDTH, None), (COL_AQ, ATT_WIDTH, "att_q"), (COL_AK, COL_GATE - COL_AK, "att_kv")]
    + [(COL_GATE + i * D_MODEL, D_MODEL, None) for i in range(N_BRANCH)])


def _win_body(x_ref, xp_ref, xn_ref, g_ref, w_ref, cw_ref, qn_ref, kn_ref, rc_ref, rs_ref, ac_ref, as_ref, o_ref,
              *, per_seq):
    tm = x_ref.shape[0]
    pos = pl.program_id(0) % per_seq
    x_ext = jnp.concatenate([xp_ref[...], x_ref[...], xn_ref[...]], axis=0)
    r_ext = lax.broadcasted_iota(jnp.int32, (tm + 2 * HALO, 1), 0)
    pad = ((r_ext < HALO) & (pos == 0)) | ((r_ext >= tm + HALO) & (pos == per_seq - 1))
    hf = jnp.where(pad, 0.0, _rms(x_ext, g_ref[...]))
    h_ext = hf.astype(BF16)
    h = hf[HALO:HALO + tm].astype(BF16)
    row = lax.broadcasted_iota(jnp.int32, (tm, HY_WIDTH), 0)
    conv = []
    for grp in range(3):
        cols = slice(COL_HY + grp * HY_WIDTH, COL_HY + (grp + 1) * HY_WIDTH)
        y_ext = _dot(h_ext, w_ref[:, cols])
        y = y_ext[HALO:HALO + tm]
        y_prev = y_ext[HALO - 1:HALO]
        y_next = y_ext[HALO + tm:HALO + tm + 1]
        prev = jnp.where(row == 0, y_prev, pltpu.roll(y, 1, 0))
        nxt = jnp.where(row == tm - 1, y_next, pltpu.roll(y, tm - 1, 0))
        cw = cw_ref[:, cols]
        conv.append(prev * cw[0:1] + y * cw[1:2] + nxt * cw[2:3])
    x0c, x1c, vc = conv
    o_ref[:, COL_X0C:COL_X0C + HY_WIDTH] = x0c.astype(o_ref.dtype)
    o_ref[:, COL_ZF:COL_ZF + HY_WIDTH] = (vc * x1c).astype(o_ref.dtype)
    o_ref[:, COL_ZF + HY_WIDTH:COL_RQ] = vc.astype(o_ref.dtype)
    for start, width, kind in _WIN_SEGMENTS:
        y = _dot(h, w_ref[:, start:start + width])
        if kind is None:
            o_ref[:, start:start + width] = y.astype(o_ref.dtype)
            continue
        for hh in range(width // LANES):
            yh = y[:, hh * LANES:(hh + 1) * LANES]
            if kind == "ret_q":
                yh = _rotate_half(yh, rc_ref[...], rs_ref[...])
            elif kind == "ret_k":
                yh = _rotate_half(yh, rc_ref[...], rs_ref[...]) * (RET_DK ** -0.5)
            elif kind == "att_q":
                yh = _axial_rope(yh, qn_ref[...], ac_ref[...], as_ref[...], ATT_HD ** -0.5 * LOG2E)
            elif hh < ATT_KV_HEADS:
                yh = _axial_rope(yh, kn_ref[...], ac_ref[...], as_ref[...], 1.0)
            o_ref[:, start + hh * LANES:start + (hh + 1) * LANES] = yh.astype(o_ref.dtype)


def _input_projection(x, g, w_stack, l, conv_w, qn, kn, pt, L, *, tm):
    T, D = x.shape
    N = w_stack.shape[2]
    per_seq = L // tm
    halo_per_tile = tm // HALO
    vec = lambda n: pl.BlockSpec((1, n), lambda i: (0, 0))
    tab = pl.BlockSpec((tm, LANES), lambda i: (i % per_seq, 0))
    return pl.pallas_call(
        functools.partial(_win_body, per_seq=per_seq),
        grid=(T // tm,),
        in_specs=[pl.BlockSpec((tm, D), lambda i: (i, 0)),
                  pl.BlockSpec((HALO, D), lambda i: (jnp.maximum(i * halo_per_tile - 1, 0), 0)),
                  pl.BlockSpec((HALO, D), lambda i: (jnp.minimum((i + 1) * halo_per_tile, T // HALO - 1), 0)),
                  vec(D),
                  pl.BlockSpec((None, D, N), lambda i: (l, 0, 0), pipeline_mode=pl.Buffered(1)),
                  pl.BlockSpec((HY_SHORT, 3 * HY_WIDTH), lambda i: (0, 0)),
                  vec(ATT_HD), vec(ATT_HD), tab, tab, tab, tab],
        out_specs=pl.BlockSpec((tm, N), lambda i: (i, 0)),
        out_shape=jax.ShapeDtypeStruct((T, N), BF16),
        compiler_params=_params(("parallel",)),
        name="input_projection",
    )(x, x, x, g.reshape(1, D), w_stack, conv_w, qn.reshape(1, ATT_HD), kn.reshape(1, ATT_HD),
      pt["ret_cos"], pt["ret_sin"], pt["att_cos"], pt["att_sin"])


def _filt_body(f_ref, w1_ref, b1_ref, w2_ref, b2_ref, w3_ref, dl_ref, o_ref, *, L):
    tl = f_ref.shape[0]
    f = f_ref[...]
    h = jnp.sin(HY_SIN_FREQ * (_mm(f, w1_ref[...], 3) + b1_ref[...]))
    h = jnp.sin(HY_SIN_FREQ * (_mm(h, w2_ref[...], FILTER_PASSES) + b2_ref[...]))
    filt = _mm(h, w3_ref[...], FILTER_PASSES)
    mod = jnp.exp(-f[:, 0:1] * dl_ref[...]) + HY_SHIFT
    row = pl.program_id(1) * tl + lax.broadcasted_iota(jnp.int32, (tl, HY_WIDTH), 0)
    sel = jnp.where(row < L, filt[:, :HY_WIDTH], filt[:, HY_WIDTH:])
    o_ref[...] = jnp.where(row == L, 0.0, sel * mod)


def _hyena_features(L):
    t = jnp.linspace(0.0, 1.0, L, dtype=F32)[:, None]
    bands = (HY_EMB - 1) // 2
    w = 2.0 * math.pi * jnp.arange(L, dtype=F32)[:, None] / L
    f = jnp.linspace(1e-4, bands - 1, bands, dtype=F32)[None, :]
    z = jnp.concatenate([t, jnp.cos(f * w), -jnp.sin(f * w)], axis=-1)
    zc = jnp.concatenate([z, z[:1], jnp.flip(z[1:], axis=0)], axis=0)
    return jnp.pad(zc, ((0, 0), (0, LANES - HY_EMB)))


def _hyena_filter(feats, w1, b1, w2, b2, w3, L):
    tl = TOKEN_TILE
    NL = w1.shape[0]
    pad = LANES - HY_FH
    w1p = jnp.pad(w1, ((0, 0), (0, LANES - HY_EMB), (0, pad)))
    b1p = jnp.pad(b1, ((0, 0), (0, pad))).reshape(NL, 1, LANES)
    w2p = jnp.pad(w2, ((0, 0), (0, pad), (0, pad)))
    b2p = jnp.pad(b2, ((0, 0), (0, pad))).reshape(NL, 1, LANES)
    w3p = jnp.pad(w3, ((0, 0), (0, pad), (0, 0)))
    min_decay = math.log(HY_TARGET) / HY_SLOW
    max_decay = math.log(HY_TARGET) / HY_FAST
    deltas = jnp.abs(jnp.linspace(min_decay, max_decay, HY_WIDTH, dtype=F32)).reshape(1, HY_WIDTH)
    layer = lambda r, c: pl.BlockSpec((None, r, c), lambda l, i: (l, 0, 0))
    return pl.pallas_call(
        functools.partial(_filt_body, L=L),
        grid=(NL, 2 * L // tl),
        in_specs=[pl.BlockSpec((tl, LANES), lambda l, i: (i, 0)),
                  layer(LANES, LANES), layer(1, LANES), layer(LANES, LANES), layer(1, LANES),
                  layer(LANES, 2 * HY_WIDTH), pl.BlockSpec((1, HY_WIDTH), lambda l, i: (0, 0))],
        out_specs=pl.BlockSpec((None, tl, HY_WIDTH), lambda l, i: (l, i, 0)),
        out_shape=jax.ShapeDtypeStruct((NL, 2 * L, HY_WIDTH), F32),
        compiler_params=_params(("parallel", "parallel")),
        name="hyena_filter",
    )(feats, w1p, b1p, w2p, b2p, w3p, deltas)


def _expanded(fn, rows, cols):
    small = fn(lax.broadcasted_iota(jnp.int32, (rows, cols), 0), lax.broadcasted_iota(jnp.int32, (rows, cols), 1))
    hp = functools.partial(jnp.dot, precision=lax.Precision.HIGHEST)
    pick_r = (lax.broadcasted_iota(jnp.int32, (rows * DFT_NB, rows), 0) // DFT_NB
              == lax.broadcasted_iota(jnp.int32, (rows * DFT_NB, rows), 1)).astype(F32)
    pick_c = (lax.broadcasted_iota(jnp.int32, (cols, cols * DFT_NB), 1) // DFT_NB
              == lax.broadcasted_iota(jnp.int32, (cols, cols * DFT_NB), 0)).astype(F32)
    ri = lax.broadcasted_iota(jnp.int32, (rows * DFT_NB, cols * DFT_NB), 0)
    ci = lax.broadcasted_iota(jnp.int32, (rows * DFT_NB, cols * DFT_NB), 1)
    return jnp.where(ri % DFT_NB == ci % DFT_NB, hp(hp(pick_r, small), pick_c), 0.0)


def _dft_tables(L):
    N = 2 * L
    N2 = DFT_N2
    N1 = N // N2
    N1h = N1 // 2

    def cs1(k1, n1):
        ang = ((k1 * n1) % N1).astype(F32) * (2.0 * math.pi / N1)
        return jnp.cos(ang), jnp.sin(ang)

    def f1_pair(r, c):
        cs, sn = cs1(r // 2, c % N1h)
        re_row, first = (r % 2) == 0, c < N1h
        return jnp.where(re_row, jnp.where(first, cs, sn), jnp.where(first, -sn, cs))

    def f1_real(r, c):
        cs, sn = cs1(r // 2, c)
        return jnp.where((r % 2) == 0, cs, -sn)

    def f3_pair(r, c):
        cs, sn = cs1(c // 2, r % N1h)
        re_out, re_in = r < N1h, (c % 2) == 0
        return jnp.where(re_out, jnp.where(re_in, cs, -sn), jnp.where(re_in, sn, cs)) * (1.0 / N)

    kk = (jnp.arange(N1, dtype=jnp.int32)[:, None, None]
          + N1 * jnp.arange(N2, dtype=jnp.int32)[None, :, None])
    n2 = jnp.arange(N2, dtype=jnp.int32)[None, None, :]
    a2 = ((kk * n2) % N).astype(F32) * (2.0 * math.pi / N)
    gr, gi = jnp.cos(a2), -jnp.sin(a2)
    gtr, gti = jnp.swapaxes(gr, 1, 2), jnp.swapaxes(gi, 1, 2)
    g_fwd = jnp.concatenate([jnp.concatenate([gr, -gi], axis=2),
                             jnp.concatenate([gi, gr], axis=2)], axis=1)
    g_inv = jnp.concatenate([jnp.concatenate([gtr, gti], axis=2),
                             jnp.concatenate([-gti, gtr], axis=2)], axis=1)
    return dict(N1=N1, N1h=N1h, N2=N2,
                f1_pair=_expanded(f1_pair, 2 * N1, 2 * N1h), f1_real=_expanded(f1_real, 2 * N1, N1),
                f3_pair=_expanded(f3_pair, 2 * N1h, 2 * N1), g_fwd=g_fwd, g_inv=g_inv)


def _stage_dtype(passes):
    return BF16 if passes == 1 else F32


def _dft1_body(x_ref, f_ref, o_ref, *, passes):
    P, K, NB, C = x_ref.shape
    y = _mm(f_ref[...], x_ref[...].reshape(P * K * NB, C), passes)
    o_ref[0] = y.astype(o_ref.dtype).reshape(o_ref.shape[1:])


def _dft_stage1(x, f, *, P, passes, out_dtype, col=0):
    Bx, K, N2, _ = x.shape
    C = HY_WIDTH
    N1 = f.shape[0] // (2 * DFT_NB)
    return pl.pallas_call(
        functools.partial(_dft1_body, passes=passes),
        grid=(Bx // P, N2 // DFT_NB),
        in_specs=[pl.BlockSpec((P, K, DFT_NB, C), lambda b, j: (b, 0, j, col)),
                  pl.BlockSpec(f.shape, lambda b, j: (0, 0), pipeline_mode=pl.Buffered(1))],
        out_specs=pl.BlockSpec((1, N1, 2, DFT_NB, C), lambda b, j: (b, 0, 0, j, 0)),
        out_shape=jax.ShapeDtypeStruct((Bx // P, N1, 2, N2, C), out_dtype),
        compiler_params=_params(("parallel", "parallel")),
        name="dft_stage1",
    )(x, f)


def _spec_body(a_ref, g_ref, o_ref, *, passes):
    def step(k, carry):
        o_ref[k] = _mm(g_ref[k], a_ref[0, k], passes)
        return carry

    lax.fori_loop(0, a_ref.shape[1], step, 0, unroll=DFT_UNROLL)


def _dft_spectrum(a, g, *, kb, passes):
    NL, N1, R, C = a.shape
    return pl.pallas_call(
        functools.partial(_spec_body, passes=passes),
        grid=(N1 // kb, NL),
        in_specs=[pl.BlockSpec((1, kb, R, C), lambda i, l: (l, i, 0, 0)),
                  pl.BlockSpec((kb, R, R), lambda i, l: (i, 0, 0))],
        out_specs=pl.BlockSpec((None, kb, R, C), lambda i, l: (l, i, 0, 0)),
        out_shape=jax.ShapeDtypeStruct((NL, N1, R, C), F32),
        compiler_params=_params(("parallel", "parallel")),
        name="dft_spectrum",
    )(a, g)


def _dft2_body(a_ref, gf_ref, gi_ref, ks_ref, o_ref, *, passes_fwd, passes_inv):
    H = a_ref.shape[2] // 2

    def step(k, carry):
        y = _mm(gf_ref[k], a_ref[0, k], passes_fwd)
        yr, yi = y[:H], y[H:]
        kr, ki = ks_ref[k, :H, :], ks_ref[k, H:, :]
        z = jnp.concatenate([yr * kr - yi * ki, yr * ki + yi * kr], axis=0)
        o_ref[0, k] = _mm(gi_ref[k], z, passes_inv).astype(o_ref.dtype)
        return carry

    lax.fori_loop(0, a_ref.shape[1], step, 0, unroll=DFT_UNROLL)


def _dft_stage2(a, gf, gi, ks, l, *, kb, passes_fwd, passes_inv, out_dtype):
    Bp, N1, R, C = a.shape
    dat = pl.BlockSpec((1, kb, R, C), lambda i, b: (b, i, 0, 0))
    mat = pl.BlockSpec((kb, R, R), lambda i, b: (i, 0, 0))
    return pl.pallas_call(
        functools.partial(_dft2_body, passes_fwd=passes_fwd, passes_inv=passes_inv),
        grid=(N1 // kb, Bp),
        in_specs=[dat, mat, mat, pl.BlockSpec((None, kb, R, C), lambda i, b: (l, i, 0, 0))],
        out_specs=dat,
        out_shape=jax.ShapeDtypeStruct((Bp, N1, R, C), out_dtype),
        compiler_params=_params(("parallel", "parallel")),
        name="dft_stage2",
    )(a, gf, gi, ks)


def _dft3_body(t_ref, f_ref, zf_ref, x0_ref, bias_ref, o_ref, *, passes):
    _, N1, _, NB, C = t_ref.shape
    y = _mm(f_ref[...], t_ref[0].reshape(N1 * 2 * NB, C), passes).reshape(zf_ref.shape)
    o_ref[...] = ((y + zf_ref[...].astype(F32) * bias_ref[...]) * x0_ref[...].astype(F32)).astype(o_ref.dtype)


def _dft_stage3(t, f, proj4, bias, *, passes):
    Bp, N1, _, N2, C = t.shape
    B, N1h = proj4.shape[:2]
    pair = lambda col=0: pl.BlockSpec((2, N1h, DFT_NB, C), lambda b, j: (b, 0, j, col))
    return pl.pallas_call(
        functools.partial(_dft3_body, passes=passes),
        grid=(Bp, N2 // DFT_NB),
        in_specs=[pl.BlockSpec((1, N1, 2, DFT_NB, C), lambda b, j: (b, 0, 0, j, 0)),
                  pl.BlockSpec(f.shape, lambda b, j: (0, 0), pipeline_mode=pl.Buffered(1)),
                  pair(COL_ZF // C), pair(COL_X0C // C),
                  pl.BlockSpec((1, C), lambda b, j: (0, 0))],
        out_specs=pair(),
        out_shape=jax.ShapeDtypeStruct((B, N1h, N2, C), BF16),
        compiler_params=_params(("parallel", "parallel")),
        name="dft_stage3",
    )(t, f, proj4, proj4, bias)


def _hyena_mixer(proj, kspec, l, bias, tabs):
    B, L, _ = proj.shape
    N1, N1h, N2 = tabs["N1"], tabs["N1h"], tabs["N2"]
    C = HY_WIDTH
    ps = DFT_PASSES
    proj4 = proj.reshape(B, N1h, N2, IN_COLS)
    f1 = tabs["f1_pair"].astype(_stage_dtype(ps["s1"]))
    a = _dft_stage1(proj4, f1, P=2, passes=ps["s1"], out_dtype=_stage_dtype(ps["s2f"]), col=COL_ZF // C)
    gdt = _stage_dtype(min(ps["s2f"], ps["s2i"]))
    t = _dft_stage2(a.reshape(B // 2, N1, 2 * N2, C), tabs["g_fwd"].astype(gdt), tabs["g_inv"].astype(gdt),
                    kspec, l, kb=DFT_KB, passes_fwd=ps["s2f"], passes_inv=ps["s2i"],
                    out_dtype=_stage_dtype(ps["s3"]))
    f3 = tabs["f3_pair"].astype(_stage_dtype(ps["s3"]))
    y = _dft_stage3(t.reshape(B // 2, N1, 2, N2, C), f3, proj4, bias.astype(F32).reshape(1, C), passes=ps["s3"])
    return y.reshape(B * L, C)


def _hyena_spectra(feats, w1, b1, w2, b2, w3, L, tabs):
    N1, N2 = tabs["N1"], tabs["N2"]
    C = HY_WIDTH
    ps = DFT_PASSES
    NL = w1.shape[0]
    kfull = _hyena_filter(feats, w1, b1, w2, b2, w3, L)
    a = _dft_stage1(kfull.reshape(NL, N1, N2, C), tabs["f1_real"].astype(_stage_dtype(ps["k1"])), P=1, passes=ps["k1"],
                    out_dtype=_stage_dtype(ps["k2"]))
    return _dft_spectrum(a.reshape(NL, N1, 2 * N2, C), tabs["g_fwd"].astype(_stage_dtype(ps["k2"])), kb=DFT_KB,
                         passes=ps["k2"])


def _ret_body(lg_ref, q_ref, k_ref, v_ref, g_ref, o_ref, st_ref):
    C = RET_BLOCK
    L = q_ref.shape[1]
    nblk = L // C
    dk = RET_DK
    h = pl.program_id(1)
    lgf = lg_ref[0, h]
    lgb = lg_ref[1, h]
    ii = lax.broadcasted_iota(jnp.int32, (C, C), 0)
    jj = lax.broadcasted_iota(jnp.int32, (C, C), 1)
    diff = (ii - jj).astype(F32)
    dm = jnp.where(diff >= 0, jnp.exp(jnp.maximum(diff, 0.0) * lgf), jnp.exp(jnp.maximum(-diff, 0.0) * lgb))
    pos = lax.broadcasted_iota(jnp.int32, (C, dk), 0).astype(F32)
    qw_f = jnp.exp((pos + 1.0) * lgf)
    kw_f = jnp.exp((C - 1.0 - pos) * lgf)
    qw_b = jnp.exp((C - pos) * lgb)
    kw_b = jnp.exp(pos * lgb)
    cd_f = jnp.exp(C * lgf)
    cd_b = jnp.exp(C * lgb)

    def block_kv(n, carry):
        sl = pl.ds(pl.multiple_of(n * C, C), C)
        k = k_ref[0, sl, :].astype(F32)
        kk = jnp.concatenate([k * kw_f, k * kw_b], axis=1).astype(BF16)
        st_ref[n] = _dot_tn(kk, v_ref[0, sl, :].astype(BF16))
        return carry

    lax.fori_loop(0, nblk, block_kv, 0, unroll=RET_UNROLL)

    def scan_f(n, S):
        kv = st_ref[n, :dk, :]
        st_ref[n, :dk, :] = S
        return S * cd_f + kv

    lax.fori_loop(0, nblk, scan_f, jnp.zeros((dk, RET_DV), F32))

    def scan_b(m, S):
        n = nblk - 1 - m
        kv = st_ref[n, dk:, :]
        st_ref[n, dk:, :] = S
        return S * cd_b + kv

    lax.fori_loop(0, nblk, scan_b, jnp.zeros((dk, RET_DV), F32))

    def block_out(n, carry):
        sl = pl.ds(pl.multiple_of(n * C, C), C)
        q = q_ref[0, sl, :].astype(BF16)
        scores = _dot_nt(q, k_ref[0, sl, :].astype(BF16)) * dm
        qf = q.astype(F32)
        qq = jnp.concatenate([qf * qw_f, qf * qw_b], axis=1).astype(BF16)
        o = _dot(scores.astype(BF16), v_ref[0, sl, :].astype(BF16)) + _dot(qq, st_ref[n].astype(BF16))
        mu = jnp.mean(o, axis=-1, keepdims=True)
        oc = o - mu
        on = oc * lax.rsqrt(jnp.mean(oc * oc, axis=-1, keepdims=True) + EPS)
        g = g_ref[0, sl, :].astype(F32)
        o_ref[0, sl, :] = (g * _sigmoid(g) * on).astype(o_ref.dtype)
        return carry

    lax.fori_loop(0, nblk, block_out, 0, unroll=RET_UNROLL)


def _retention_mixer(proj, log_gamma):
    B, L, _ = proj.shape
    blk = lambda col: pl.BlockSpec((1, L, LANES), lambda b, h, c=col // LANES: (b, 0, c + h))
    y = pl.pallas_call(
        _ret_body,
        grid=(B, RET_HEADS),
        in_specs=[pl.BlockSpec(memory_space=pltpu.SMEM),
                  blk(COL_RQ), blk(COL_RK), blk(COL_RV), blk(COL_RG)],
        out_specs=pl.BlockSpec((1, L, LANES), lambda b, h: (b, 0, h)),
        out_shape=jax.ShapeDtypeStruct((B, L, RET_WIDTH), BF16),
        scratch_shapes=[pltpu.VMEM((L // RET_BLOCK, 2 * RET_DK, RET_DV), F32)],
        compiler_params=_params(("parallel", "parallel")),
        name="retention",
    )(log_gamma, proj, proj, proj, proj)
    return y.reshape(B * L, RET_WIDTH)


def _att_body(q_ref, k_ref, v_ref, o_ref):
    G = ATT_HEADS // ATT_KV_HEADS
    tq = q_ref.shape[1]
    kc = min(ATT_KEY_CHUNK, k_ref.shape[1])
    nchunk = k_ref.shape[1] // kc
    q = q_ref[0].astype(BF16)
    qs = jnp.concatenate([q[:, g * ATT_HD:(g + 1) * ATT_HD] for g in range(G)], axis=0)
    m = jnp.full((G * tq, 1), -jnp.inf, F32)
    l = jnp.zeros((G * tq, 1), F32)
    acc = jnp.zeros((G * tq, ATT_HD), F32)
    s_next = _dot_nt(qs, k_ref[0, 0:kc, :].astype(BF16))
    for c in range(nchunk):
        s = s_next
        if c + 1 < nchunk:
            s_next = _dot_nt(qs, k_ref[0, (c + 1) * kc:(c + 2) * kc, :].astype(BF16))
        m_new = jnp.maximum(m, jnp.max(s, axis=-1, keepdims=True))
        alpha = jnp.exp2(m - m_new)
        p = jnp.exp2(s - m_new)
        l = alpha * l + jnp.sum(p, axis=-1, keepdims=True)
        acc = alpha * acc + _dot(p.astype(BF16), v_ref[0, c * kc:(c + 1) * kc, :].astype(BF16))
        m = m_new
    o = acc * (1.0 / l)
    for g in range(G):
        o_ref[0, :, g * ATT_HD:(g + 1) * ATT_HD] = o[g * tq:(g + 1) * tq].astype(o_ref.dtype)


def _attention_mixer(proj, *, tq):
    B, L, _ = proj.shape
    G = ATT_HEADS // ATT_KV_HEADS
    tq = min(tq, L)
    y = pl.pallas_call(
        _att_body,
        grid=(B, ATT_KV_HEADS, L // tq),
        in_specs=[pl.BlockSpec((1, tq, G * ATT_HD), lambda b, h, i: (b, i, COL_AQ // (G * ATT_HD) + h)),
                  pl.BlockSpec((1, L, ATT_HD), lambda b, h, i: (b, 0, COL_AK // ATT_HD + h)),
                  pl.BlockSpec((1, L, ATT_HD), lambda b, h, i: (b, 0, COL_AV // ATT_HD + h))],
        out_specs=pl.BlockSpec((1, tq, G * ATT_HD), lambda b, h, i: (b, i, h)),
        out_shape=jax.ShapeDtypeStruct((B, L, ATT_WIDTH), BF16),
        compiler_params=_params(("parallel", "parallel", "arbitrary")),
        name="attention",
    )(proj, proj, proj)
    return y.reshape(B * L, ATT_WIDTH)


def _tail_body(yh_ref, yr_ref, ya_ref, ga_ref, gb_ref, x_ref, kv_ref, wb_ref, wo_ref, gmix_ref,
               gxpre_ref, wq_ref, wxo_ref, gxpost_ref, gfpre_ref, w1_ref, w2_ref, gfpost_ref, o_ref, u_ref):
    D = D_MODEL
    ph = _dot(yh_ref[0], wb_ref[0:HY_WIDTH, :])
    pr = _dot(yr_ref[0], wb_ref[HY_WIDTH:HY_WIDTH + RET_WIDTH, :])
    pa = _dot(ya_ref[0], wb_ref[HY_WIDTH + RET_WIDTH:, :])
    ga = ga_ref[0].astype(F32)
    gb = gb_ref[0].astype(F32)
    hd = D // 2
    g0 = _sigmoid(ga[:, :D])
    g1 = _sigmoid(jnp.concatenate([ga[:, D:], gb[:, :hd]], axis=1))
    g2 = _sigmoid(gb[:, hd:])
    merged = g0 * ph + g1 * pr + g2 * pa
    x = x_ref[0] + _rms(_dot(merged.astype(BF16), wo_ref[...]), gmix_ref[...])
    h = _rms(x, gxpre_ref[...]).astype(BF16)
    q = (_dot(h, wq_ref[...]) * (X_HD ** -0.5)).astype(BF16)
    kv = kv_ref[0]
    outs = []
    for hh in range(X_HEADS):
        sl = slice(hh * X_HD, (hh + 1) * X_HD)
        s = _dot_nt(q[:, sl], kv[:, sl])
        p = jnp.exp(s - jnp.max(s, axis=-1, keepdims=True))
        l = jnp.sum(p, axis=-1, keepdims=True)
        vh = kv[:, D + hh * X_HD:D + (hh + 1) * X_HD]
        outs.append(_dot(p.astype(BF16), vh) * (1.0 / l))
    o = jnp.concatenate(outs, axis=1).astype(BF16)
    x = x + _rms(_dot(o, wxo_ref[...]), gxpost_ref[...])
    h = _rms(x, gfpre_ref[...]).astype(BF16)
    for c in range(D_FF // FF_CHUNK):
        sl = slice(c * FF_CHUNK, (c + 1) * FF_CHUNK)
        u_ref[:, sl] = jnp.square(jnp.maximum(_dot(h, w1_ref[:, sl]), 0.0)).astype(BF16)
    o_ref[0] = x + _rms(_dot(u_ref[...], w2_ref[...]), gfpost_ref[...])


def _layer_tail(yh, yr, ya, proj, x, kv, w, l, *, tm):
    B, L, D = x.shape
    M = kv.shape[2]
    gw = N_BRANCH * D // 2
    tok = lambda width, col=0: pl.BlockSpec((1, tm, width), lambda b, i, col=col: (b, i, col))
    vec = pl.BlockSpec((1, D), lambda b, i: (0, 0))
    res = lambda shape: pl.BlockSpec((None,) + shape, lambda b, i: (l, 0, 0), pipeline_mode=pl.Buffered(1))
    g = lambda name: w[name][l].reshape(1, D)
    return pl.pallas_call(
        _tail_body,
        grid=(B, L // tm),
        in_specs=[tok(HY_WIDTH), tok(RET_WIDTH), tok(ATT_WIDTH), tok(gw, COL_GATE // gw), tok(gw, COL_GATE // gw + 1),
                  tok(D), pl.BlockSpec((None, 1, M, 2 * D), lambda b, i: (l, b, 0, 0)),
                  res((MIX_WIDTH, D)), res((D, D)), vec,
                  vec, res((D, D)), res((D, D)), vec,
                  vec, res((D, D_FF)), res((D_FF, D)), vec],
        out_specs=tok(D),
        out_shape=jax.ShapeDtypeStruct((B, L, D), F32),
        scratch_shapes=[pltpu.VMEM((tm, D_FF), BF16)],
        compiler_params=_params(("parallel", "parallel")),
        name="layer_tail",
    )(yh, yr, ya, proj, proj, x, kv, w["w_branch"], w["w_out"], g("g_mix_post"),
      g("g_x_pre"), w["w_xq"], w["w_xo"], g("g_x_post"),
      g("g_ff_pre"), w["w_ff1"], w["w_ff2"], g("g_ff_post"))


def _rotary_tables(pos, dim, theta):
    inv = theta ** (-jnp.arange(0, dim, 2, dtype=F32) / dim)
    ang = pos.astype(F32)[:, None] * inv[None, :]
    return jnp.cos(ang), jnp.sin(ang)


def _position_tables(L):
    pos = jnp.arange(L)
    rows = L // GRID_W
    row_ids = jnp.repeat(jnp.arange(rows), GRID_W)
    col_ids = jnp.tile(jnp.arange(GRID_W), rows)
    rcos, rsin = _rotary_tables(pos, RET_DK, RET_THETA)
    rc, rs = _rotary_tables(row_ids, ATT_HD // 2, ROPE_THETA)
    cc, cs = _rotary_tables(col_ids, ATT_HD // 2, ROPE_THETA)
    return dict(
        ret_cos=jnp.concatenate([rcos, rcos], axis=-1),
        ret_sin=jnp.concatenate([-rsin, rsin], axis=-1),
        att_cos=jnp.concatenate([rc, rc, cc, cc], axis=-1),
        att_sin=jnp.concatenate([-rs, rs, -cs, cs], axis=-1),
        feats=_hyena_features(L),
        dft=_dft_tables(L),
    )


def _trunk(x, mem, w, pt):
    B, L, D = x.shape
    M = mem.shape[1]
    kspec = _hyena_spectra(pt["feats"], w["hy_fw1"], w["hy_fb1"], w["hy_fw2"], w["hy_fb2"], w["hy_fw3"], L, pt["dft"])
    kv = _norm_matmul(mem.reshape(B * M, D), w["g_mem"], w["w_xkv"], tm=TOKEN_TILE, tn=FF_CHUNK)
    kv = kv.reshape(-1, B, M, 2 * D)
    for l in range(DEPTH):
        proj = _input_projection(x.reshape(B * L, D), w["g_mix_pre"][l], w["w_in"], l, w["hy_conv"][l],
                                 w["att_qnorm"][l], w["att_knorm"][l], pt, L, tm=TOKEN_TILE).reshape(B, L, IN_COLS)
        y_h = _hyena_mixer(proj, kspec, l, w["hy_bias"][l], pt["dft"])
        y_r = _retention_mixer(proj, w["ret_lg"][l])
        y_a = _attention_mixer(proj, tq=ATT_QUERY_TILE)
        x = _layer_tail(y_h.reshape(B, L, HY_WIDTH), y_r.reshape(B, L, RET_WIDTH), y_a.reshape(B, L, ATT_WIDTH),
                        proj, x, kv, w, l, tm=TOKEN_TILE)
    return x


def kernel(x_prompt, x_sample, mem_prompt, mem_sample, g_mix_pre, g_mix_post, w_in, hy_conv, hy_fw1, hy_fb1,
           hy_fw2, hy_fb2, hy_fw3, hy_bias, ret_decay, att_qnorm, att_knorm, w_branch, w_out, g_x_pre, g_x_post,
           g_mem, w_xq, w_xkv, w_xo, g_ff_pre, g_ff_post, w_ff1, w_ff2):
    w = dict(
        g_mix_pre=g_mix_pre, g_mix_post=g_mix_post, w_in=w_in.astype(BF16), hy_conv=hy_conv,
        hy_fw1=hy_fw1, hy_fb1=hy_fb1, hy_fw2=hy_fw2, hy_fb2=hy_fb2, hy_fw3=hy_fw3, hy_bias=hy_bias,
        ret_lg=jax.nn.log_sigmoid(ret_decay.astype(F32)), att_qnorm=att_qnorm, att_knorm=att_knorm,
        w_branch=w_branch.astype(BF16), w_out=w_out.astype(BF16), g_x_pre=g_x_pre, g_x_post=g_x_post,
        g_mem=g_mem, w_xq=w_xq.astype(BF16), w_xkv=w_xkv.astype(BF16), w_xo=w_xo.astype(BF16),
        g_ff_pre=g_ff_pre, g_ff_post=g_ff_post, w_ff1=w_ff1.astype(BF16), w_ff2=w_ff2.astype(BF16),
    )
    y_prompt = _trunk(x_prompt, mem_prompt, w, _position_tables(x_prompt.shape[1]))
    y_sample = _trunk(x_sample, mem_sample, w, _position_tables(x_sample.shape[1]))
    return (y_prompt, y_sample)
```

```python
import functools
import math

import jax
import jax.numpy as jnp
from jax import lax
from jax.experimental import pallas as pl
from jax.experimental.pallas import tpu as pltpu

F32 = jnp.float32
BF16 = jnp.bfloat16

D_MODEL = 1024
DEPTH = 4
GRID_W = 64
EPS = 1e-6
HY_WIDTH = 512
HY_SHORT = 3
HY_EMB = 33
HY_FH = 64
HY_SIN_FREQ = 1.0
HY_TARGET = 1e-2
HY_FAST = 0.3
HY_SLOW = 1.5
HY_SHIFT = 0.0
RET_HEADS = 4
RET_DK = 128
RET_DV = 128
RET_WIDTH = RET_HEADS * RET_DV
RET_THETA = 10000.0
ATT_HEADS = 4
ATT_KV_HEADS = 2
ATT_HD = 128
ATT_WIDTH = ATT_HEADS * ATT_HD
ROPE_THETA = 10000.0
X_HEADS = 4
X_HD = D_MODEL // X_HEADS
D_FF = 4 * D_MODEL
N_BRANCH = 3
MIX_WIDTH = HY_WIDTH + RET_WIDTH + ATT_WIDTH

COL_HY = 0
COL_X0C = 0
COL_ZF = HY_WIDTH
COL_RQ = 3 * HY_WIDTH
COL_RK = COL_RQ + RET_HEADS * RET_DK
COL_RV = COL_RK + RET_HEADS * RET_DK
COL_RG = COL_RV + RET_WIDTH
COL_AQ = COL_RG + RET_WIDTH
COL_AK = COL_AQ + ATT_WIDTH
COL_AV = COL_AK + ATT_KV_HEADS * ATT_HD
COL_GATE = COL_AV + ATT_KV_HEADS * ATT_HD
IN_COLS = COL_GATE + N_BRANCH * D_MODEL

LANES = 128
HALO = 8
VMEM_LIMIT = 56 * 1024 * 1024
DFT_N2 = 128
DFT_NB = 16
DFT_KB = 16
DFT_UNROLL = 8
DFT_PASSES = dict(s1=1, s2f=1, s3=1, k1=1, k2=1)
RET_BLOCK = 256
RET_UNROLL = 16
ATT_KEY_CHUNK = 1024
ATT_QUERY_TILE = 1024
FILTER_PASSES = 1
FF_CHUNK = 1024
TOKEN_TILE = 512
LOG2E = 1.4426950408889634


def _params(sem):
    return pltpu.CompilerParams(dimension_semantics=sem, vmem_limit_bytes=VMEM_LIMIT)


def _dot(a, b):
    return jnp.dot(a, b, preferred_element_type=F32)


def _dot_nt(a, b):
    return lax.dot_general(a, b, (((1,), (1,)), ((), ())), preferred_element_type=F32)


def _dot_tn(a, b):
    return lax.dot_general(a, b, (((0,), (0,)), ((), ())), preferred_element_type=F32)


def _rms(x, g):
    ms = jnp.mean(x * x, axis=-1, keepdims=True)
    return x * lax.rsqrt(ms + EPS) * g


def _sigmoid(x):
    return 0.5 * jnp.tanh(0.5 * x) + 0.5


def _split(x):
    hi = x.astype(BF16)
    lo = (x - hi.astype(F32)).astype(BF16)
    return hi, lo


def _dot3(a, b):
    return _dot(a[0], b[0]) + (_dot(a[0], b[1]) + _dot(a[1], b[0]))


def _mm(a, b, passes):
    if passes == 1:
        return _dot(a.astype(BF16), b.astype(BF16))
    return _dot3(_split(a.astype(F32)), _split(b.astype(F32)))


def _nm_body(x_ref, g_ref, w_ref, o_ref, *, tn):
    h = _rms(x_ref[...], g_ref[...]).astype(BF16)
    for j in range(w_ref.shape[1] // tn):
        o_ref[:, j * tn:(j + 1) * tn] = _dot(h, w_ref[:, j * tn:(j + 1) * tn]).astype(o_ref.dtype)


def _norm_matmul(x, g, w, *, tm, tn):
    T, D = x.shape
    NL, _, N = w.shape
    return pl.pallas_call(
        functools.partial(_nm_body, tn=tn),
        grid=(NL, T // tm),
        in_specs=[pl.BlockSpec((tm, D), lambda l, i: (i, 0)),
                  pl.BlockSpec((None, 1, D), lambda l, i: (l, 0, 0)),
                  pl.BlockSpec((None, D, N), lambda l, i: (l, 0, 0))],
        out_specs=pl.BlockSpec((None, tm, N), lambda l, i: (l, i, 0)),
        out_shape=jax.ShapeDtypeStruct((NL, T, N), BF16),
        compiler_params=_params(("parallel", "parallel")),
        name="norm_matmul",
    )(x, g.reshape(NL, 1, D), w)


def _rotate_half(x, cos, sin):
    return x * cos + pltpu.roll(x, x.shape[-1] // 2, 1) * sin


def _axial_rope(x, gain, cos, sin, scale):
    x = _rms(x, gain)
    lane = lax.broadcasted_iota(jnp.int32, x.shape, 1)
    lower = (lane % (ATT_HD // 2)) < (ATT_HD // 4)
    swapped = jnp.where(lower, pltpu.roll(x, LANES - ATT_HD // 4, 1), pltpu.roll(x, ATT_HD // 4, 1))
    return (x * cos + swapped * sin) * scale


_WIN_SEGMENTS = (
    [(COL_RQ, RET_WIDTH, "ret_q"), (COL_RK, RET_WIDTH, "ret_k"), (COL_RV, RET_WIDTH, None),
       (COL_RG, RET_WIDTH, None), (COL_AQ, ATT_WIDTH, "att_q"), (COL_AK, COL_GATE - COL_AK, "att_kv")]
    + [(COL_GATE + i * D_MODEL, D_MODEL, None) for i in range(N_BRANCH)])


def _win_body(x_ref, xp_ref, xn_ref, g_ref, w_ref, cw_ref, qn_ref, kn_ref, rc_ref, rs_ref, ac_ref, as_ref, o_ref,
              *, per_seq):
    tm = x_ref.shape[0]
    pos = pl.program_id(0) % per_seq
    x_ext = jnp.concatenate([xp_ref[...], x_ref[...], xn_ref[...]], axis=0)
    r_ext = lax.broadcasted_iota(jnp.int32, (tm + 2 * HALO, 1), 0)
    pad = ((r_ext < HALO) & (pos == 0)) | ((r_ext >= tm + HALO) & (pos == per_seq - 1))
    hf = jnp.where(pad, 0.0, _rms(x_ext, g_ref[...]))
    h_ext = hf.astype(BF16)
    h = hf[HALO:HALO + tm].astype(BF16)
    row = lax.broadcasted_iota(jnp.int32, (tm, HY_WIDTH), 0)
    conv = []
    for grp in range(3):
        cols = slice(COL_HY + grp * HY_WIDTH, COL_HY + (grp + 1) * HY_WIDTH)
        y_ext = _dot(h_ext, w_ref[:, cols])
        y = y_ext[HALO:HALO + tm]
        y_prev = y_ext[HALO - 1:HALO]
        y_next = y_ext[HALO + tm:HALO + tm + 1]
        prev = jnp.where(row == 0, y_prev, pltpu.roll(y, 1, 0))
        nxt = jnp.where(row == tm - 1, y_next, pltpu.roll(y, tm - 1, 0))
        cw = cw_ref[:, cols]
        conv.append(prev * cw[0:1] + y * cw[1:2] + nxt * cw[2:3])
    x0c, x1c, vc = conv
    o_ref[:, COL_X0C:COL_X0C + HY_WIDTH] = x0c.astype(o_ref.dtype)
    o_ref[:, COL_ZF:COL_ZF + HY_WIDTH] = (vc * x1c).astype(o_ref.dtype)
    o_ref[:, COL_ZF + HY_WIDTH:COL_RQ] = vc.astype(o_ref.dtype)
    for start, width, kind in _WIN_SEGMENTS:
        y = _dot(h, w_ref[:, start:start + width])
        if kind is None:
            o_ref[:, start:start + width] = y.astype(o_ref.dtype)
            continue
        for hh in range(width // LANES):
            yh = y[:, hh * LANES:(hh + 1) * LANES]
            if kind == "ret_q":
                yh = _rotate_half(yh, rc_ref[...], rs_ref[...])
            elif kind == "ret_k":
                yh = _rotate_half(yh, rc_ref[...], rs_ref[...]) * (RET_DK ** -0.5)
            elif kind == "att_q":
                yh = _axial_rope(yh, qn_ref[...], ac_ref[...], as_ref[...], ATT_HD ** -0.5 * LOG2E)
            elif hh < ATT_KV_HEADS:
                yh = _axial_rope(yh, kn_ref[...], ac_ref[...], as_ref[...], 1.0)
            o_ref[:, start + hh * LANES:start + (hh + 1) * LANES] = yh.astype(o_ref.dtype)


def _input_projection(x, g, w_stack, l, conv_w, qn, kn, pt, L, *, tm):
    T, D = x.shape
    N = w_stack.shape[2]
    per_seq = L // tm
    halo_per_tile = tm // HALO
    vec = lambda n: pl.BlockSpec((1, n), lambda i: (0, 0))
    tab = pl.BlockSpec((tm, LANES), lambda i: (i % per_seq, 0))
    return pl.pallas_call(
        functools.partial(_win_body, per_seq=per_seq),
        grid=(T // tm,),
        in_specs=[pl.BlockSpec((tm, D), lambda i: (i, 0)),
                  pl.BlockSpec((HALO, D), lambda i: (jnp.maximum(i * halo_per_tile - 1, 0), 0)),
                  pl.BlockSpec((HALO, D), lambda i: (jnp.minimum((i + 1) * halo_per_tile, T // HALO - 1), 0)),
                  vec(D),
                  pl.BlockSpec((None, D, N), lambda i: (l, 0, 0), pipeline_mode=pl.Buffered(1)),
                  pl.BlockSpec((HY_SHORT, 3 * HY_WIDTH), lambda i: (0, 0)),
                  vec(ATT_HD), vec(ATT_HD), tab, tab, tab, tab],
        out_specs=pl.BlockSpec((tm, N), lambda i: (i, 0)),
        out_shape=jax.ShapeDtypeStruct((T, N), BF16),
        compiler_params=_params(("parallel",)),
        name="input_projection",
    )(x, x, x, g.reshape(1, D), w_stack, conv_w, qn.reshape(1, ATT_HD), kn.reshape(1, ATT_HD),
      pt["ret_cos"], pt["ret_sin"], pt["att_cos"], pt["att_sin"])


def _filt_body(f_ref, w1_ref, b1_ref, w2_ref, b2_ref, w3_ref, dl_ref, o_ref, *, L):
    tl = f_ref.shape[0]
    f = f_ref[...]
    h = jnp.sin(HY_SIN_FREQ * (_mm(f, w1_ref[...], 3) + b1_ref[...]))
    h = jnp.sin(HY_SIN_FREQ * (_mm(h, w2_ref[...], FILTER_PASSES) + b2_ref[...]))
    filt = _mm(h, w3_ref[...], FILTER_PASSES)
    mod = jnp.exp(-f[:, 0:1] * dl_ref[...]) + HY_SHIFT
    row = pl.program_id(1) * tl + lax.broadcasted_iota(jnp.int32, (tl, HY_WIDTH), 0)
    sel = jnp.where(row < L, filt[:, :HY_WIDTH], filt[:, HY_WIDTH:])
    o_ref[...] = jnp.where(row == L, 0.0, sel * mod)


def _hyena_features(L):
    t = jnp.linspace(0.0, 1.0, L, dtype=F32)[:, None]
    bands = (HY_EMB - 1) // 2
    w = 2.0 * math.pi * jnp.arange(L, dtype=F32)[:, None] / L
    f = jnp.linspace(1e-4, bands - 1, bands, dtype=F32)[None, :]
    z = jnp.concatenate([t, jnp.cos(f * w), -jnp.sin(f * w)], axis=-1)
    zc = jnp.concatenate([z, z[:1], jnp.flip(z[1:], axis=0)], axis=0)
    return jnp.pad(zc, ((0, 0), (0, LANES - HY_EMB)))


def _hyena_filter(feats, w1, b1, w2, b2, w3, L):
    tl = TOKEN_TILE
    NL = w1.shape[0]
    pad = LANES - HY_FH
    w1p = jnp.pad(w1, ((0, 0), (0, LANES - HY_EMB), (0, pad)))
    b1p = jnp.pad(b1, ((0, 0), (0, pad))).reshape(NL, 1, LANES)
    w2p = jnp.pad(w2, ((0, 0), (0, pad), (0, pad)))
    b2p = jnp.pad(b2, ((0, 0), (0, pad))).reshape(NL, 1, LANES)
    w3p = jnp.pad(w3, ((0, 0), (0, pad), (0, 0)))
    min_decay = math.log(HY_TARGET) / HY_SLOW
    max_decay = math.log(HY_TARGET) / HY_FAST
    deltas = jnp.abs(jnp.linspace(min_decay, max_decay, HY_WIDTH, dtype=F32)).reshape(1, HY_WIDTH)
    layer = lambda r, c: pl.BlockSpec((None, r, c), lambda l, i: (l, 0, 0))
    return pl.pallas_call(
        functools.partial(_filt_body, L=L),
        grid=(NL, 2 * L // tl),
        in_specs=[pl.BlockSpec((tl, LANES), lambda l, i: (i, 0)),
                  layer(LANES, LANES), layer(1, LANES), layer(LANES, LANES), layer(1, LANES),
                  layer(LANES, 2 * HY_WIDTH), pl.BlockSpec((1, HY_WIDTH), lambda l, i: (0, 0))],
        out_specs=pl.BlockSpec((None, tl, HY_WIDTH), lambda l, i: (l, i, 0)),
        out_shape=jax.ShapeDtypeStruct((NL, 2 * L, HY_WIDTH), F32),
        compiler_params=_params(("parallel", "parallel")),
        name="hyena_filter",
    )(feats, w1p, b1p, w2p, b2p, w3p, deltas)


def _expanded(fn, rows, cols):
    small = fn(lax.broadcasted_iota(jnp.int32, (rows, cols), 0), lax.broadcasted_iota(jnp.int32, (rows, cols), 1))
    hp = functools.partial(jnp.dot, precision=lax.Precision.HIGHEST)
    pick_r = (lax.broadcasted_iota(jnp.int32, (rows * DFT_NB, rows), 0) // DFT_NB
              == lax.broadcasted_iota(jnp.int32, (rows * DFT_NB, rows), 1)).astype(F32)
    pick_c = (lax.broadcasted_iota(jnp.int32, (cols, cols * DFT_NB), 1) // DFT_NB
              == lax.broadcasted_iota(jnp.int32, (cols, cols * DFT_NB), 0)).astype(F32)
    ri = lax.broadcasted_iota(jnp.int32, (rows * DFT_NB, cols * DFT_NB), 0)
    ci = lax.broadcasted_iota(jnp.int32, (rows * DFT_NB, cols * DFT_NB), 1)
    return jnp.where(ri % DFT_NB == ci % DFT_NB, hp(hp(pick_r, small), pick_c), 0.0)


def _dft_tables(L):
    N = 2 * L
    N2 = DFT_N2
    N1 = N // N2
    N1h = N1 // 2

    def cs1(k1, n1):
        ang = ((k1 * n1) % N1).astype(F32) * (2.0 * math.pi / N1)
        return jnp.cos(ang), jnp.sin(ang)

    def f1_pair(r, c):
        cs, sn = cs1(r // 2, c % N1h)
        re_row, first = (r % 2) == 0, c < N1h
        return jnp.where(re_row, jnp.where(first, cs, sn), jnp.where(first, -sn, cs))

    def f1_real(r, c):
        cs, sn = cs1(r // 2, c)
        return jnp.where((r % 2) == 0, cs, -sn)

    def f3_pair(r, c):
        cs, sn = cs1(c // 2, r % N1h)
        re_out, re_in = r < N1h, (c % 2) == 0
        return jnp.where(re_out, jnp.where(re_in, cs, -sn), jnp.where(re_in, sn, cs)) * (1.0 / N)

    kk = (jnp.arange(N1, dtype=jnp.int32)[:, None, None]
          + N1 * jnp.arange(N2, dtype=jnp.int32)[None, :, None])
    n2 = jnp.arange(N2, dtype=jnp.int32)[None, None, :]
    a2 = ((kk * n2) % N).astype(F32) * (2.0 * math.pi / N)
    gr, gi = jnp.cos(a2), -jnp.sin(a2)
    g_fwd = jnp.concatenate([jnp.concatenate([gr, -gi], axis=2),
                             jnp.concatenate([gi, gr], axis=2)], axis=1)
    return dict(N1=N1, N1h=N1h, N2=N2,
                f1_pair=_expanded(f1_pair, 2 * N1, 2 * N1h), f1_real=_expanded(f1_real, 2 * N1, N1),
                f3_pair=_expanded(f3_pair, 2 * N1h, 2 * N1), g_fwd=g_fwd)


def _stage_dtype(passes):
    return BF16 if passes == 1 else F32


def _dft1_body(x_ref, f_ref, o_ref, *, passes):
    P, K, NB, C = x_ref.shape
    y = _mm(f_ref[...], x_ref[...].reshape(P * K * NB, C), passes)
    o_ref[0] = y.astype(o_ref.dtype).reshape(o_ref.shape[1:])


def _dft_stage1(x, f, *, P, passes, out_dtype, col=0):
    Bx, K, N2, _ = x.shape
    C = HY_WIDTH
    N1 = f.shape[0] // (2 * DFT_NB)
    return pl.pallas_call(
        functools.partial(_dft1_body, passes=passes),
        grid=(Bx // P, N2 // DFT_NB),
        in_specs=[pl.BlockSpec((P, K, DFT_NB, C), lambda b, j: (b, 0, j, col)),
                  pl.BlockSpec(f.shape, lambda b, j: (0, 0), pipeline_mode=pl.Buffered(1))],
        out_specs=pl.BlockSpec((1, N1, 2, DFT_NB, C), lambda b, j: (b, 0, 0, j, 0)),
        out_shape=jax.ShapeDtypeStruct((Bx // P, N1, 2, N2, C), out_dtype),
        compiler_params=_params(("parallel", "parallel")),
        name="dft_stage1",
    )(x, f)


def _spec_body(a_ref, g_ref, o_ref, *, passes):
    def step(k, carry):
        o_ref[k] = _mm(g_ref[k], a_ref[0, k], passes)
        return carry

    lax.fori_loop(0, a_ref.shape[1], step, 0, unroll=DFT_UNROLL)


def _dft_spectrum(a, g, *, kb, passes):
    NL, N1, R, C = a.shape
    return pl.pallas_call(
        functools.partial(_spec_body, passes=passes),
        grid=(N1 // kb, NL),
        in_specs=[pl.BlockSpec((1, kb, R, C), lambda i, l: (l, i, 0, 0)),
                  pl.BlockSpec((kb, R, R), lambda i, l: (i, 0, 0))],
        out_specs=pl.BlockSpec((None, kb, R, C), lambda i, l: (l, i, 0, 0)),
        out_shape=jax.ShapeDtypeStruct((NL, N1, R, C), F32),
        compiler_params=_params(("parallel", "parallel")),
        name="dft_spectrum",
    )(a, g)


def _dft2_body(a_ref, g_ref, ks_ref, o_ref, *, passes):
    H = a_ref.shape[2] // 2

    def step(k, carry):
        y = _mm(g_ref[k], a_ref[0, k], passes)
        yr, yi = y[:H], y[H:]
        kr, ki = ks_ref[k, :H, :], ks_ref[k, H:, :]
        z = jnp.concatenate([yr * kr - yi * ki, yr * ki + yi * kr], axis=0)
        o_ref[0, k] = _dot_tn(g_ref[k], z.astype(g_ref.dtype)).astype(o_ref.dtype)
        return carry

    lax.fori_loop(0, a_ref.shape[1], step, 0, unroll=DFT_UNROLL)


def _dft_stage2(a, g, ks, l, *, kb, passes, out_dtype):
    Bp, N1, R, C = a.shape
    dat = pl.BlockSpec((1, kb, R, C), lambda i, b: (b, i, 0, 0))
    return pl.pallas_call(
        functools.partial(_dft2_body, passes=passes),
        grid=(N1 // kb, Bp),
        in_specs=[dat, pl.BlockSpec((kb, R, R), lambda i, b: (i, 0, 0)),
                  pl.BlockSpec((None, kb, R, C), lambda i, b: (l, i, 0, 0))],
        out_specs=dat,
        out_shape=jax.ShapeDtypeStruct((Bp, N1, R, C), out_dtype),
        compiler_params=_params(("parallel", "parallel")),
        name="dft_stage2",
    )(a, g, ks)


def _dft3_body(t_ref, f_ref, zf_ref, x0_ref, bias_ref, o_ref, *, passes):
    _, N1, _, NB, C = t_ref.shape
    y = _mm(f_ref[...], t_ref[0].reshape(N1 * 2 * NB, C), passes).reshape(zf_ref.shape)
    o_ref[...] = ((y + zf_ref[...].astype(F32) * bias_ref[...]) * x0_ref[...].astype(F32)).astype(o_ref.dtype)


def _dft_stage3(t, f, proj4, bias, *, passes):
    Bp, N1, _, N2, C = t.shape
    B, N1h = proj4.shape[:2]
    pair = lambda col=0: pl.BlockSpec((2, N1h, DFT_NB, C), lambda b, j: (b, 0, j, col))
    return pl.pallas_call(
        functools.partial(_dft3_body, passes=passes),
        grid=(Bp, N2 // DFT_NB),
        in_specs=[pl.BlockSpec((1, N1, 2, DFT_NB, C), lambda b, j: (b, 0, 0, j, 0)),
                  pl.BlockSpec(f.shape, lambda b, j: (0, 0), pipeline_mode=pl.Buffered(1)),
                  pair(COL_ZF // C), pair(COL_X0C // C),
                  pl.BlockSpec((1, C), lambda b, j: (0, 0))],
        out_specs=pair(),
        out_shape=jax.ShapeDtypeStruct((B, N1h, N2, C), BF16),
        compiler_params=_params(("parallel", "parallel")),
        name="dft_stage3",
    )(t, f, proj4, proj4, bias)


def _hyena_mixer(proj, kspec, l, bias, tabs):
    B, L, _ = proj.shape
    N1, N1h, N2 = tabs["N1"], tabs["N1h"], tabs["N2"]
    C = HY_WIDTH
    ps = DFT_PASSES
    proj4 = proj.reshape(B, N1h, N2, IN_COLS)
    f1 = tabs["f1_pair"].astype(_stage_dtype(ps["s1"]))
    a = _dft_stage1(proj4, f1, P=2, passes=ps["s1"], out_dtype=_stage_dtype(ps["s2f"]), col=COL_ZF // C)
    t = _dft_stage2(a.reshape(B // 2, N1, 2 * N2, C), tabs["g_fwd"].astype(_stage_dtype(ps["s2f"])),
                    kspec, l, kb=DFT_KB, passes=ps["s2f"], out_dtype=_stage_dtype(ps["s3"]))
    f3 = tabs["f3_pair"].astype(_stage_dtype(ps["s3"]))
    y = _dft_stage3(t.reshape(B // 2, N1, 2, N2, C), f3, proj4, bias.astype(F32).reshape(1, C), passes=ps["s3"])
    return y.reshape(B * L, C)


def _hyena_spectra(feats, w1, b1, w2, b2, w3, L, tabs):
    N1, N2 = tabs["N1"], tabs["N2"]
    C = HY_WIDTH
    ps = DFT_PASSES
    NL = w1.shape[0]
    kfull = _hyena_filter(feats, w1, b1, w2, b2, w3, L)
    a = _dft_stage1(kfull.reshape(NL, N1, N2, C), tabs["f1_real"].astype(_stage_dtype(ps["k1"])), P=1, passes=ps["k1"],
                    out_dtype=_stage_dtype(ps["k2"]))
    return _dft_spectrum(a.reshape(NL, N1, 2 * N2, C), tabs["g_fwd"].astype(_stage_dtype(ps["k2"])), kb=DFT_KB,
                         passes=ps["k2"])


def _ret_body(lg_ref, q_ref, k_ref, v_ref, g_ref, o_ref, st_ref):
    C = RET_BLOCK
    L = q_ref.shape[1]
    nblk = L // C
    dk = RET_DK
    h = pl.program_id(1)
    lgf = lg_ref[0, h]
    lgb = lg_ref[1, h]
    ii = lax.broadcasted_iota(jnp.int32, (C, C), 0)
    jj = lax.broadcasted_iota(jnp.int32, (C, C), 1)
    diff = (ii - jj).astype(F32)
    dm = jnp.where(diff >= 0, jnp.exp(jnp.maximum(diff, 0.0) * lgf), jnp.exp(jnp.maximum(-diff, 0.0) * lgb))
    pos = lax.broadcasted_iota(jnp.int32, (C, dk), 0).astype(F32)
    qw_f = jnp.exp((pos + 1.0) * lgf)
    kw_f = jnp.exp((C - 1.0 - pos) * lgf)
    qw_b = jnp.exp((C - pos) * lgb)
    kw_b = jnp.exp(pos * lgb)
    cd_f = jnp.exp(C * lgf)
    cd_b = jnp.exp(C * lgb)

    def block_kv(n, carry):
        sl = pl.ds(pl.multiple_of(n * C, C), C)
        k = k_ref[0, sl, :].astype(F32)
        kk = jnp.concatenate([k * kw_f, k * kw_b], axis=1).astype(BF16)
        st_ref[n] = _dot_tn(kk, v_ref[0, sl, :].astype(BF16))
        return carry

    lax.fori_loop(0, nblk, block_kv, 0, unroll=RET_UNROLL)

    def scan_f(n, S):
        kv = st_ref[n, :dk, :]
        st_ref[n, :dk, :] = S
        return S * cd_f + kv

    lax.fori_loop(0, nblk, scan_f, jnp.zeros((dk, RET_DV), F32))

    def scan_b(m, S):
        n = nblk - 1 - m
        kv = st_ref[n, dk:, :]
        st_ref[n, dk:, :] = S
        return S * cd_b + kv

    lax.fori_loop(0, nblk, scan_b, jnp.zeros((dk, RET_DV), F32))

    def block_out(n, carry):
        sl = pl.ds(pl.multiple_of(n * C, C), C)
        q = q_ref[0, sl, :].astype(BF16)
        scores = _dot_nt(q, k_ref[0, sl, :].astype(BF16)) * dm
        qf = q.astype(F32)
        qq = jnp.concatenate([qf * qw_f, qf * qw_b], axis=1).astype(BF16)
        o = _dot(scores.astype(BF16), v_ref[0, sl, :].astype(BF16)) + _dot(qq, st_ref[n].astype(BF16))
        mu = jnp.mean(o, axis=-1, keepdims=True)
        oc = o - mu
        on = oc * lax.rsqrt(jnp.mean(oc * oc, axis=-1, keepdims=True) + EPS)
        g = g_ref[0, sl, :].astype(F32)
        o_ref[0, sl, :] = (g * _sigmoid(g) * on).astype(o_ref.dtype)
        return carry

    lax.fori_loop(0, nblk, block_out, 0, unroll=RET_UNROLL)


def _retention_mixer(proj, log_gamma):
    B, L, _ = proj.shape
    blk = lambda col: pl.BlockSpec((1, L, LANES), lambda b, h, c=col // LANES: (b, 0, c + h))
    y = pl.pallas_call(
        _ret_body,
        grid=(B, RET_HEADS),
        in_specs=[pl.BlockSpec(memory_space=pltpu.SMEM),
                  blk(COL_RQ), blk(COL_RK), blk(COL_RV), blk(COL_RG)],
        out_specs=pl.BlockSpec((1, L, LANES), lambda b, h: (b, 0, h)),
        out_shape=jax.ShapeDtypeStruct((B, L, RET_WIDTH), BF16),
        scratch_shapes=[pltpu.VMEM((L // RET_BLOCK, 2 * RET_DK, RET_DV), F32)],
        compiler_params=_params(("parallel", "parallel")),
        name="retention",
    )(log_gamma, proj, proj, proj, proj)
    return y.reshape(B * L, RET_WIDTH)


def _att_body(q_ref, k_ref, v_ref, o_ref):
    G = ATT_HEADS // ATT_KV_HEADS
    tq = q_ref.shape[1]
    kc = min(ATT_KEY_CHUNK, k_ref.shape[1])
    nchunk = k_ref.shape[1] // kc
    q = q_ref[0].astype(BF16)
    qs = jnp.concatenate([q[:, g * ATT_HD:(g + 1) * ATT_HD] for g in range(G)], axis=0)
    m = jnp.full((G * tq, 1), -jnp.inf, F32)
    l = jnp.zeros((G * tq, 1), F32)
    acc = jnp.zeros((G * tq, ATT_HD), F32)
    s_next = _dot_nt(qs, k_ref[0, 0:kc, :].astype(BF16))
    for c in range(nchunk):
        s = s_next
        if c + 1 < nchunk:
            s_next = _dot_nt(qs, k_ref[0, (c + 1) * kc:(c + 2) * kc, :].astype(BF16))
        m_new = jnp.maximum(m, jnp.max(s, axis=-1, keepdims=True))
        alpha = jnp.exp2(m - m_new)
        p = jnp.exp2(s - m_new)
        l = alpha * l + jnp.sum(p, axis=-1, keepdims=True)
        acc = alpha * acc + _dot(p.astype(BF16), v_ref[0, c * kc:(c + 1) * kc, :].astype(BF16))
        m = m_new
    o = acc * (1.0 / l)
    for g in range(G):
        o_ref[0, :, g * ATT_HD:(g + 1) * ATT_HD] = o[g * tq:(g + 1) * tq].astype(o_ref.dtype)


def _attention_mixer(proj, *, tq):
    B, L, _ = proj.shape
    G = ATT_HEADS // ATT_KV_HEADS
    tq = min(tq, L)
    y = pl.pallas_call(
        _att_body,
        grid=(B, ATT_KV_HEADS, L // tq),
        in_specs=[pl.BlockSpec((1, tq, G * ATT_HD), lambda b, h, i: (b, i, COL_AQ // (G * ATT_HD) + h)),
                  pl.BlockSpec((1, L, ATT_HD), lambda b, h, i: (b, 0, COL_AK // ATT_HD + h)),
                  pl.BlockSpec((1, L, ATT_HD), lambda b, h, i: (b, 0, COL_AV // ATT_HD + h))],
        out_specs=pl.BlockSpec((1, tq, G * ATT_HD), lambda b, h, i: (b, i, h)),
        out_shape=jax.ShapeDtypeStruct((B, L, ATT_WIDTH), BF16),
        compiler_params=_params(("parallel", "parallel", "arbitrary")),
        name="attention",
    )(proj, proj, proj)
    return y.reshape(B * L, ATT_WIDTH)


def _tail_body(yh_ref, yr_ref, ya_ref, ga_ref, gb_ref, x_ref, kv_ref, wb_ref, wo_ref, gmix_ref,
               gxpre_ref, wq_ref, wxo_ref, gxpost_ref, gfpre_ref, w1_ref, w2_ref, gfpost_ref, o_ref, u_ref):
    D = D_MODEL
    ph = _dot(yh_ref[0], wb_ref[0:HY_WIDTH, :])
    pr = _dot(yr_ref[0], wb_ref[HY_WIDTH:HY_WIDTH + RET_WIDTH, :])
    pa = _dot(ya_ref[0], wb_ref[HY_WIDTH + RET_WIDTH:, :])
    ga = ga_ref[0].astype(F32)
    gb = gb_ref[0].astype(F32)
    hd = D // 2
    g0 = _sigmoid(ga[:, :D])
    g1 = _sigmoid(jnp.concatenate([ga[:, D:], gb[:, :hd]], axis=1))
    g2 = _sigmoid(gb[:, hd:])
    merged = g0 * ph + g1 * pr + g2 * pa
    x = x_ref[0] + _rms(_dot(merged.astype(BF16), wo_ref[...]), gmix_ref[...])
    h = _rms(x, gxpre_ref[...]).astype(BF16)
    q = (_dot(h, wq_ref[...]) * (X_HD ** -0.5)).astype(BF16)
    kv = kv_ref[0]
    outs = []
    for hh in range(X_HEADS):
        sl = slice(hh * X_HD, (hh + 1) * X_HD)
        s = _dot_nt(q[:, sl], kv[:, sl])
        p = jnp.exp(s - jnp.max(s, axis=-1, keepdims=True))
        l = jnp.sum(p, axis=-1, keepdims=True)
        vh = kv[:, D + hh * X_HD:D + (hh + 1) * X_HD]
        outs.append(_dot(p.astype(BF16), vh) * (1.0 / l))
    o = jnp.concatenate(outs, axis=1).astype(BF16)
    x = x + _rms(_dot(o, wxo_ref[...]), gxpost_ref[...])
    h = _rms(x, gfpre_ref[...]).astype(BF16)
    for c in range(D_FF // FF_CHUNK):
        sl = slice(c * FF_CHUNK, (c + 1) * FF_CHUNK)
        u_ref[:, sl] = jnp.square(jnp.maximum(_dot(h, w1_ref[:, sl]), 0.0)).astype(BF16)
    o_ref[0] = x + _rms(_dot(u_ref[...], w2_ref[...]), gfpost_ref[...])


def _layer_tail(yh, yr, ya, proj, x, kv, w, l, *, tm):
    B, L, D = x.shape
    M = kv.shape[2]
    gw = N_BRANCH * D // 2
    tok = lambda width, col=0: pl.BlockSpec((1, tm, width), lambda b, i, col=col: (b, i, col))
    vec = pl.BlockSpec((1, D), lambda b, i: (0, 0))
    res = lambda shape: pl.BlockSpec((None,) + shape, lambda b, i: (l, 0, 0), pipeline_mode=pl.Buffered(1))
    g = lambda name: w[name][l].reshape(1, D)
    return pl.pallas_call(
        _tail_body,
        grid=(B, L // tm),
        in_specs=[tok(HY_WIDTH), tok(RET_WIDTH), tok(ATT_WIDTH), tok(gw, COL_GATE // gw), tok(gw, COL_GATE // gw + 1),
                  tok(D), pl.BlockSpec((None, 1, M, 2 * D), lambda b, i: (l, b, 0, 0)),
                  res((MIX_WIDTH, D)), res((D, D)), vec,
                  vec, res((D, D)), res((D, D)), vec,
                  vec, res((D, D_FF)), res((D_FF, D)), vec],
        out_specs=tok(D),
        out_shape=jax.ShapeDtypeStruct((B, L, D), F32),
        scratch_shapes=[pltpu.VMEM((tm, D_FF), BF16)],
        compiler_params=_params(("parallel", "parallel")),
        name="layer_tail",
    )(yh, yr, ya, proj, proj, x, kv, w["w_branch"], w["w_out"], g("g_mix_post"),
      g("g_x_pre"), w["w_xq"], w["w_xo"], g("g_x_post"),
      g("g_ff_pre"), w["w_ff1"], w["w_ff2"], g("g_ff_post"))


def _rotary_tables(pos, dim, theta):
    inv = theta ** (-jnp.arange(0, dim, 2, dtype=F32) / dim)
    ang = pos.astype(F32)[:, None] * inv[None, :]
    return jnp.cos(ang), jnp.sin(ang)


def _position_tables(L):
    pos = jnp.arange(L)
    rows = L // GRID_W
    row_ids = jnp.repeat(jnp.arange(rows), GRID_W)
    col_ids = jnp.tile(jnp.arange(GRID_W), rows)
    rcos, rsin = _rotary_tables(pos, RET_DK, RET_THETA)
    rc, rs = _rotary_tables(row_ids, ATT_HD // 2, ROPE_THETA)
    cc, cs = _rotary_tables(col_ids, ATT_HD // 2, ROPE_THETA)
    return dict(
        ret_cos=jnp.concatenate([rcos, rcos], axis=-1),
        ret_sin=jnp.concatenate([-rsin, rsin], axis=-1),
        att_cos=jnp.concatenate([rc, rc, cc, cc], axis=-1),
        att_sin=jnp.concatenate([-rs, rs, -cs, cs], axis=-1),
        feats=_hyena_features(L),
        dft=_dft_tables(L),
    )


def _trunk(x, mem, w, pt):
    B, L, D = x.shape
    M = mem.shape[1]
    kspec = _hyena_spectra(pt["feats"], w["hy_fw1"], w["hy_fb1"], w["hy_fw2"], w["hy_fb2"], w["hy_fw3"], L, pt["dft"])
    kv = _norm_matmul(mem.reshape(B * M, D), w["g_mem"], w["w_xkv"], tm=TOKEN_TILE, tn=FF_CHUNK)
    kv = kv.reshape(-1, B, M, 2 * D)
    for l in range(DEPTH):
        proj = _input_projection(x.reshape(B * L, D), w["g_mix_pre"][l], w["w_in"], l, w["hy_conv"][l],
                                 w["att_qnorm"][l], w["att_knorm"][l], pt, L, tm=TOKEN_TILE).reshape(B, L, IN_COLS)
        y_h = _hyena_mixer(proj, kspec, l, w["hy_bias"][l], pt["dft"])
        y_r = _retention_mixer(proj, w["ret_lg"][l])
        y_a = _attention_mixer(proj, tq=ATT_QUERY_TILE)
        x = _layer_tail(y_h.reshape(B, L, HY_WIDTH), y_r.reshape(B, L, RET_WIDTH), y_a.reshape(B, L, ATT_WIDTH),
                        proj, x, kv, w, l, tm=TOKEN_TILE)
    return x


def kernel(x_prompt, x_sample, mem_prompt, mem_sample, g_mix_pre, g_mix_post, w_in, hy_conv, hy_fw1, hy_fb1,
           hy_fw2, hy_fb2, hy_fw3, hy_bias, ret_decay, att_qnorm, att_knorm, w_branch, w_out, g_x_pre, g_x_post,
           g_mem, w_xq, w_xkv, w_xo, g_ff_pre, g_ff_post, w_ff1, w_ff2):
    w = dict(
        g_mix_pre=g_mix_pre, g_mix_post=g_mix_post, w_in=w_in.astype(BF16), hy_conv=hy_conv,
        hy_fw1=hy_fw1, hy_fb1=hy_fb1, hy_fw2=hy_fw2, hy_fb2=hy_fb2, hy_fw3=hy_fw3, hy_bias=hy_bias,
        ret_lg=jax.nn.log_sigmoid(ret_decay.astype(F32)), att_qnorm=att_qnorm, att_knorm=att_knorm,
        w_branch=w_branch.astype(BF16), w_out=w_out.astype(BF16), g_x_pre=g_x_pre, g_x_post=g_x_post,
        g_mem=g_mem, w_xq=w_xq.astype(BF16), w_xkv=w_xkv.astype(BF16), w_xo=w_xo.astype(BF16),
        g_ff_pre=g_ff_pre, g_ff_post=g_ff_post, w_ff1=w_ff1.astype(BF16), w_ff2=w_ff2.astype(BF16),
    )
    y_prompt = _trunk(x_prompt, mem_prompt, w, _position_tables(x_prompt.shape[1]))
    y_sample = _trunk(x_sample, mem_sample, w, _position_tables(x_sample.shape[1]))
    return (y_prompt, y_sample)
```

```python
import functools
import math

import jax
import jax.numpy as jnp
from jax import lax
from jax.experimental import pallas as pl
from jax.experimental.pallas import tpu as pltpu

F32 = jnp.float32
BF16 = jnp.bfloat16

D_MODEL = 1024
DEPTH = 4
GRID_W = 64
EPS = 1e-6
HY_WIDTH = 512
HY_SHORT = 3
HY_EMB = 33
HY_FH = 64
HY_SIN_FREQ = 1.0
HY_TARGET = 1e-2
HY_FAST = 0.3
HY_SLOW = 1.5
HY_SHIFT = 0.0
RET_HEADS = 4
RET_DK = 128
RET_DV = 128
RET_WIDTH = RET_HEADS * RET_DV
RET_THETA = 10000.0
ATT_HEADS = 4
ATT_KV_HEADS = 2
ATT_HD = 128
ATT_WIDTH = ATT_HEADS * ATT_HD
ROPE_THETA = 10000.0
X_HEADS = 4
X_HD = D_MODEL // X_HEADS
D_FF = 4 * D_MODEL
N_BRANCH = 3
MIX_WIDTH = HY_WIDTH + RET_WIDTH + ATT_WIDTH

COL_HY = 0
COL_X0C = 0
COL_ZF = HY_WIDTH
COL_RQ = 3 * HY_WIDTH
COL_RK = COL_RQ + RET_HEADS * RET_DK
COL_RV = COL_RK + RET_HEADS * RET_DK
COL_RG = COL_RV + RET_WIDTH
COL_AQ = COL_RG + RET_WIDTH
COL_AK = COL_AQ + ATT_WIDTH
COL_AV = COL_AK + ATT_KV_HEADS * ATT_HD
COL_GATE = COL_AV + ATT_KV_HEADS * ATT_HD
IN_COLS = COL_GATE + N_BRANCH * D_MODEL

LANES = 128
HALO = 8
VMEM_LIMIT = 56 * 1024 * 1024
DFT_N2 = 128
DFT_NB = 16
DFT_KB = 16
DFT_UNROLL = 8
RET_BLOCK = 256
RET_UNROLL = 16
ATT_KEY_CHUNK = 1024
ATT_QUERY_TILE = 1024
FF_CHUNK = 1024
TOKEN_TILE = 512
LOG2E = 1.4426950408889634


def _params(sem):
    return pltpu.CompilerParams(dimension_semantics=sem, vmem_limit_bytes=VMEM_LIMIT)


def _dot(a, b):
    return jnp.dot(a, b, preferred_element_type=F32)


def _dot_nt(a, b):
    return lax.dot_general(a, b, (((1,), (1,)), ((), ())), preferred_element_type=F32)


def _dot_tn(a, b):
    return lax.dot_general(a, b, (((0,), (0,)), ((), ())), preferred_element_type=F32)


def _rms(x, g):
    ms = jnp.mean(x * x, axis=-1, keepdims=True)
    return x * lax.rsqrt(ms + EPS) * g


def _sigmoid(x):
    return 0.5 * jnp.tanh(0.5 * x) + 0.5


def _split(x):
    hi = x.astype(BF16)
    lo = (x - hi.astype(F32)).astype(BF16)
    return hi, lo


def _dot3(a, b):
    return _dot(a[0], b[0]) + (_dot(a[0], b[1]) + _dot(a[1], b[0]))


def _bdot(a, b):
    return _dot(a.astype(BF16), b.astype(BF16))


def _nm_body(x_ref, g_ref, w_ref, o_ref, *, tn):
    h = _rms(x_ref[...], g_ref[...]).astype(BF16)
    for j in range(w_ref.shape[1] // tn):
        o_ref[:, j * tn:(j + 1) * tn] = _dot(h, w_ref[:, j * tn:(j + 1) * tn]).astype(o_ref.dtype)


def _norm_matmul(x, g, w, *, tm, tn):
    T, D = x.shape
    NL, _, N = w.shape
    return pl.pallas_call(
        functools.partial(_nm_body, tn=tn),
        grid=(NL, T // tm),
        in_specs=[pl.BlockSpec((tm, D), lambda l, i: (i, 0)),
                  pl.BlockSpec((None, 1, D), lambda l, i: (l, 0, 0)),
                  pl.BlockSpec((None, D, N), lambda l, i: (l, 0, 0))],
        out_specs=pl.BlockSpec((None, tm, N), lambda l, i: (l, i, 0)),
        out_shape=jax.ShapeDtypeStruct((NL, T, N), BF16),
        compiler_params=_params(("parallel", "parallel")),
        name="norm_matmul",
    )(x, g.reshape(NL, 1, D), w)


def _rotate_half(x, cos, sin):
    return x * cos + pltpu.roll(x, x.shape[-1] // 2, 1) * sin


def _axial_rope(x, gain, cos, sin, scale):
    x = _rms(x, gain)
    lane = lax.broadcasted_iota(jnp.int32, x.shape, 1)
    lower = (lane % (ATT_HD // 2)) < (ATT_HD // 4)
    swapped = jnp.where(lower, pltpu.roll(x, LANES - ATT_HD // 4, 1), pltpu.roll(x, ATT_HD // 4, 1))
    return (x * cos + swapped * sin) * scale


_WIN_SEGMENTS = (
    [(COL_RQ, RET_WIDTH, "ret_q"), (COL_RK, RET_WIDTH, "ret_k"), (COL_RV, RET_WIDTH, None),
       (COL_RG, RET_WIDTH, None), (COL_AQ, ATT_WIDTH, "att_q"), (COL_AK, COL_GATE - COL_AK, "att_kv")]
    + [(COL_GATE + i * D_MODEL, D_MODEL, None) for i in range(N_BRANCH)])


def _win_body(x_ref, xp_ref, xn_ref, g_ref, w_ref, cw_ref, qn_ref, kn_ref, rc_ref, rs_ref, ac_ref, as_ref, o_ref,
              *, per_seq):
    tm = x_ref.shape[0]
    pos = pl.program_id(0) % per_seq
    x_ext = jnp.concatenate([xp_ref[...], x_ref[...], xn_ref[...]], axis=0)
    r_ext = lax.broadcasted_iota(jnp.int32, (tm + 2 * HALO, 1), 0)
    pad = ((r_ext < HALO) & (pos == 0)) | ((r_ext >= tm + HALO) & (pos == per_seq - 1))
    hf = jnp.where(pad, 0.0, _rms(x_ext, g_ref[...]))
    h_ext = hf.astype(BF16)
    h = hf[HALO:HALO + tm].astype(BF16)
    row = lax.broadcasted_iota(jnp.int32, (tm, HY_WIDTH), 0)
    conv = []
    for grp in range(3):
        cols = slice(COL_HY + grp * HY_WIDTH, COL_HY + (grp + 1) * HY_WIDTH)
        y_ext = _dot(h_ext, w_ref[:, cols])
        y = y_ext[HALO:HALO + tm]
        y_prev = y_ext[HALO - 1:HALO]
        y_next = y_ext[HALO + tm:HALO + tm + 1]
        prev = jnp.where(row == 0, y_prev, pltpu.roll(y, 1, 0))
        nxt = jnp.where(row == tm - 1, y_next, pltpu.roll(y, tm - 1, 0))
        cw = cw_ref[:, cols]
        conv.append(prev * cw[0:1] + y * cw[1:2] + nxt * cw[2:3])
    x0c, x1c, vc = conv
    o_ref[:, COL_X0C:COL_X0C + HY_WIDTH] = x0c.astype(o_ref.dtype)
    o_ref[:, COL_ZF:COL_ZF + HY_WIDTH] = (vc * x1c).astype(o_ref.dtype)
    o_ref[:, COL_ZF + HY_WIDTH:COL_RQ] = vc.astype(o_ref.dtype)
    for start, width, kind in _WIN_SEGMENTS:
        y = _dot(h, w_ref[:, start:start + width])
        if kind is None:
            o_ref[:, start:start + width] = y.astype(o_ref.dtype)
            continue
        for hh in range(width // LANES):
            yh = y[:, hh * LANES:(hh + 1) * LANES]
            if kind == "ret_q":
                yh = _rotate_half(yh, rc_ref[...], rs_ref[...])
            elif kind == "ret_k":
                yh = _rotate_half(yh, rc_ref[...], rs_ref[...]) * (RET_DK ** -0.5)
            elif kind == "att_q":
                yh = _axial_rope(yh, qn_ref[...], ac_ref[...], as_ref[...], ATT_HD ** -0.5 * LOG2E)
            elif hh < ATT_KV_HEADS:
                yh = _axial_rope(yh, kn_ref[...], ac_ref[...], as_ref[...], 1.0)
            o_ref[:, start + hh * LANES:start + (hh + 1) * LANES] = yh.astype(o_ref.dtype)


def _input_projection(x, g, w_stack, l, conv_w, qn, kn, pt, L, *, tm):
    T, D = x.shape
    N = w_stack.shape[2]
    per_seq = L // tm
    halo_per_tile = tm // HALO
    vec = lambda n: pl.BlockSpec((1, n), lambda i: (0, 0))
    tab = pl.BlockSpec((tm, LANES), lambda i: (i % per_seq, 0))
    return pl.pallas_call(
        functools.partial(_win_body, per_seq=per_seq),
        grid=(T // tm,),
        in_specs=[pl.BlockSpec((tm, D), lambda i: (i, 0)),
                  pl.BlockSpec((HALO, D), lambda i: (jnp.maximum(i * halo_per_tile - 1, 0), 0)),
                  pl.BlockSpec((HALO, D), lambda i: (jnp.minimum((i + 1) * halo_per_tile, T // HALO - 1), 0)),
                  vec(D),
                  pl.BlockSpec((None, D, N), lambda i: (l, 0, 0), pipeline_mode=pl.Buffered(1)),
                  pl.BlockSpec((HY_SHORT, 3 * HY_WIDTH), lambda i: (0, 0)),
                  vec(ATT_HD), vec(ATT_HD), tab, tab, tab, tab],
        out_specs=pl.BlockSpec((tm, N), lambda i: (i, 0)),
        out_shape=jax.ShapeDtypeStruct((T, N), BF16),
        compiler_params=_params(("parallel",)),
        name="input_projection",
    )(x, x, x, g.reshape(1, D), w_stack, conv_w, qn.reshape(1, ATT_HD), kn.reshape(1, ATT_HD),
      pt["ret_cos"], pt["ret_sin"], pt["att_cos"], pt["att_sin"])


def _filt_body(f_ref, w1_ref, b1_ref, w2_ref, b2_ref, w3_ref, dl_ref, o_ref, *, L):
    tl = f_ref.shape[0]
    f = f_ref[...]
    h = jnp.sin(HY_SIN_FREQ * (_dot3(_split(f), _split(w1_ref[...])) + b1_ref[...]))
    h = jnp.sin(HY_SIN_FREQ * (_bdot(h, w2_ref[...]) + b2_ref[...]))
    filt = _bdot(h, w3_ref[...])
    mod = jnp.exp(-f[:, 0:1] * dl_ref[...]) + HY_SHIFT
    row = pl.program_id(1) * tl + lax.broadcasted_iota(jnp.int32, (tl, HY_WIDTH), 0)
    sel = jnp.where(row < L, filt[:, :HY_WIDTH], filt[:, HY_WIDTH:])
    o_ref[...] = jnp.where(row == L, 0.0, sel * mod)


def _hyena_features(L):
    t = jnp.linspace(0.0, 1.0, L, dtype=F32)[:, None]
    bands = (HY_EMB - 1) // 2
    w = 2.0 * math.pi * jnp.arange(L, dtype=F32)[:, None] / L
    f = jnp.linspace(1e-4, bands - 1, bands, dtype=F32)[None, :]
    z = jnp.concatenate([t, jnp.cos(f * w), -jnp.sin(f * w)], axis=-1)
    zc = jnp.concatenate([z, z[:1], jnp.flip(z[1:], axis=0)], axis=0)
    return jnp.pad(zc, ((0, 0), (0, LANES - HY_EMB)))


def _hyena_filter(feats, w1, b1, w2, b2, w3, L):
    tl = TOKEN_TILE
    NL = w1.shape[0]
    pad = LANES - HY_FH
    w1p = jnp.pad(w1, ((0, 0), (0, LANES - HY_EMB), (0, pad)))
    b1p = jnp.pad(b1, ((0, 0), (0, pad))).reshape(NL, 1, LANES)
    w2p = jnp.pad(w2, ((0, 0), (0, pad), (0, pad)))
    b2p = jnp.pad(b2, ((0, 0), (0, pad))).reshape(NL, 1, LANES)
    w3p = jnp.pad(w3, ((0, 0), (0, pad), (0, 0)))
    min_decay = math.log(HY_TARGET) / HY_SLOW
    max_decay = math.log(HY_TARGET) / HY_FAST
    deltas = jnp.abs(jnp.linspace(min_decay, max_decay, HY_WIDTH, dtype=F32)).reshape(1, HY_WIDTH)
    layer = lambda r, c: pl.BlockSpec((None, r, c), lambda l, i: (l, 0, 0))
    return pl.pallas_call(
        functools.partial(_filt_body, L=L),
        grid=(NL, 2 * L // tl),
        in_specs=[pl.BlockSpec((tl, LANES), lambda l, i: (i, 0)),
                  layer(LANES, LANES), layer(1, LANES), layer(LANES, LANES), layer(1, LANES),
                  layer(LANES, 2 * HY_WIDTH), pl.BlockSpec((1, HY_WIDTH), lambda l, i: (0, 0))],
        out_specs=pl.BlockSpec((None, tl, HY_WIDTH), lambda l, i: (l, i, 0)),
        out_shape=jax.ShapeDtypeStruct((NL, 2 * L, HY_WIDTH), F32),
        compiler_params=_params(("parallel", "parallel")),
        name="hyena_filter",
    )(feats, w1p, b1p, w2p, b2p, w3p, deltas)


def _expanded(fn, rows, cols):
    small = fn(lax.broadcasted_iota(jnp.int32, (rows, cols), 0), lax.broadcasted_iota(jnp.int32, (rows, cols), 1))
    hp = functools.partial(jnp.dot, precision=lax.Precision.HIGHEST)
    pick_r = (lax.broadcasted_iota(jnp.int32, (rows * DFT_NB, rows), 0) // DFT_NB
              == lax.broadcasted_iota(jnp.int32, (rows * DFT_NB, rows), 1)).astype(F32)
    pick_c = (lax.broadcasted_iota(jnp.int32, (cols, cols * DFT_NB), 1) // DFT_NB
              == lax.broadcasted_iota(jnp.int32, (cols, cols * DFT_NB), 0)).astype(F32)
    ri = lax.broadcasted_iota(jnp.int32, (rows * DFT_NB, cols * DFT_NB), 0)
    ci = lax.broadcasted_iota(jnp.int32, (rows * DFT_NB, cols * DFT_NB), 1)
    return jnp.where(ri % DFT_NB == ci % DFT_NB, hp(hp(pick_r, small), pick_c), 0.0).astype(BF16)


def _dft_tables(L):
    N = 2 * L
    N2 = DFT_N2
    N1 = N // N2
    N1h = N1 // 2

    def cs1(k1, n1):
        ang = ((k1 * n1) % N1).astype(F32) * (2.0 * math.pi / N1)
        return jnp.cos(ang), jnp.sin(ang)

    def f1_pair(r, c):
        cs, sn = cs1(r // 2, c % N1h)
        re_row, first = (r % 2) == 0, c < N1h
        return jnp.where(re_row, jnp.where(first, cs, sn), jnp.where(first, -sn, cs))

    def f1_real(r, c):
        cs, sn = cs1(r // 2, c)
        return jnp.where((r % 2) == 0, cs, -sn)

    def f3_pair(r, c):
        cs, sn = cs1(c // 2, r % N1h)
        re_out, re_in = r < N1h, (c % 2) == 0
        return jnp.where(re_out, jnp.where(re_in, cs, -sn), jnp.where(re_in, sn, cs)) * (1.0 / N)

    kk = (jnp.arange(N1, dtype=jnp.int32)[:, None, None]
          + N1 * jnp.arange(N2, dtype=jnp.int32)[None, :, None])
    n2 = jnp.arange(N2, dtype=jnp.int32)[None, None, :]
    a2 = ((kk * n2) % N).astype(F32) * (2.0 * math.pi / N)
    gr, gi = jnp.cos(a2), -jnp.sin(a2)
    g_fwd = jnp.concatenate([jnp.concatenate([gr, -gi], axis=2),
                             jnp.concatenate([gi, gr], axis=2)], axis=1)
    return dict(N1=N1, N1h=N1h, N2=N2,
                f1_pair=_expanded(f1_pair, 2 * N1, 2 * N1h), f1_real=_expanded(f1_real, 2 * N1, N1),
                f3_pair=_expanded(f3_pair, 2 * N1h, 2 * N1), g_fwd=g_fwd.astype(BF16))


def _dft1_body(x_ref, f_ref, o_ref):
    P, K, NB, C = x_ref.shape
    y = _bdot(f_ref[...], x_ref[...].reshape(P * K * NB, C))
    o_ref[0] = y.astype(o_ref.dtype).reshape(o_ref.shape[1:])


def _dft_stage1(x, f, *, P, col=0):
    Bx, K, N2, _ = x.shape
    C = HY_WIDTH
    N1 = f.shape[0] // (2 * DFT_NB)
    return pl.pallas_call(
        _dft1_body,
        grid=(Bx // P, N2 // DFT_NB),
        in_specs=[pl.BlockSpec((P, K, DFT_NB, C), lambda b, j: (b, 0, j, col)),
                  pl.BlockSpec(f.shape, lambda b, j: (0, 0), pipeline_mode=pl.Buffered(1))],
        out_specs=pl.BlockSpec((1, N1, 2, DFT_NB, C), lambda b, j: (b, 0, 0, j, 0)),
        out_shape=jax.ShapeDtypeStruct((Bx // P, N1, 2, N2, C), BF16),
        compiler_params=_params(("parallel", "parallel")),
        name="dft_stage1",
    )(x, f)


def _spec_body(a_ref, g_ref, o_ref):
    def step(k, carry):
        o_ref[k] = _bdot(g_ref[k], a_ref[0, k])
        return carry

    lax.fori_loop(0, a_ref.shape[1], step, 0, unroll=DFT_UNROLL)


def _dft_spectrum(a, g, *, kb):
    NL, N1, R, C = a.shape
    return pl.pallas_call(
        _spec_body,
        grid=(N1 // kb, NL),
        in_specs=[pl.BlockSpec((1, kb, R, C), lambda i, l: (l, i, 0, 0)),
                  pl.BlockSpec((kb, R, R), lambda i, l: (i, 0, 0))],
        out_specs=pl.BlockSpec((None, kb, R, C), lambda i, l: (l, i, 0, 0)),
        out_shape=jax.ShapeDtypeStruct((NL, N1, R, C), F32),
        compiler_params=_params(("parallel", "parallel")),
        name="dft_spectrum",
    )(a, g)


def _dft2_body(a_ref, g_ref, ks_ref, o_ref):
    H = a_ref.shape[2] // 2

    def step(k, carry):
        y = _bdot(g_ref[k], a_ref[0, k])
        yr, yi = y[:H], y[H:]
        kr, ki = ks_ref[k, :H, :], ks_ref[k, H:, :]
        z = jnp.concatenate([yr * kr - yi * ki, yr * ki + yi * kr], axis=0)
        o_ref[0, k] = _dot_tn(g_ref[k], z.astype(g_ref.dtype)).astype(o_ref.dtype)
        return carry

    lax.fori_loop(0, a_ref.shape[1], step, 0, unroll=DFT_UNROLL)


def _dft_stage2(a, g, ks, l, *, kb):
    Bp, N1, R, C = a.shape
    dat = pl.BlockSpec((1, kb, R, C), lambda i, b: (b, i, 0, 0))
    return pl.pallas_call(
        _dft2_body,
        grid=(N1 // kb, Bp),
        in_specs=[dat, pl.BlockSpec((kb, R, R), lambda i, b: (i, 0, 0)),
                  pl.BlockSpec((None, kb, R, C), lambda i, b: (l, i, 0, 0))],
        out_specs=dat,
        out_shape=jax.ShapeDtypeStruct((Bp, N1, R, C), BF16),
        compiler_params=_params(("parallel", "parallel")),
        name="dft_stage2",
    )(a, g, ks)


def _dft3_body(t_ref, f_ref, zf_ref, x0_ref, bias_ref, o_ref):
    _, N1, _, NB, C = t_ref.shape
    y = _bdot(f_ref[...], t_ref[0].reshape(N1 * 2 * NB, C)).reshape(zf_ref.shape)
    o_ref[...] = ((y + zf_ref[...].astype(F32) * bias_ref[...]) * x0_ref[...].astype(F32)).astype(o_ref.dtype)


def _dft_stage3(t, f, proj4, bias):
    Bp, N1, _, N2, C = t.shape
    B, N1h = proj4.shape[:2]
    pair = lambda col=0: pl.BlockSpec((2, N1h, DFT_NB, C), lambda b, j: (b, 0, j, col))
    return pl.pallas_call(
        _dft3_body,
        grid=(Bp, N2 // DFT_NB),
        in_specs=[pl.BlockSpec((1, N1, 2, DFT_NB, C), lambda b, j: (b, 0, 0, j, 0)),
                  pl.BlockSpec(f.shape, lambda b, j: (0, 0), pipeline_mode=pl.Buffered(1)),
                  pair(COL_ZF // C), pair(COL_X0C // C),
                  pl.BlockSpec((1, C), lambda b, j: (0, 0))],
        out_specs=pair(),
        out_shape=jax.ShapeDtypeStruct((B, N1h, N2, C), BF16),
        compiler_params=_params(("parallel", "parallel")),
        name="dft_stage3",
    )(t, f, proj4, proj4, bias)


def _hyena_mixer(proj, kspec, l, bias, tabs):
    B, L, _ = proj.shape
    N1, N1h, N2 = tabs["N1"], tabs["N1h"], tabs["N2"]
    C = HY_WIDTH
    proj4 = proj.reshape(B, N1h, N2, IN_COLS)
    a = _dft_stage1(proj4, tabs["f1_pair"], P=2, col=COL_ZF // C)
    t = _dft_stage2(a.reshape(B // 2, N1, 2 * N2, C), tabs["g_fwd"], kspec, l, kb=DFT_KB)
    y = _dft_stage3(t.reshape(B // 2, N1, 2, N2, C), tabs["f3_pair"], proj4, bias.astype(F32).reshape(1, C))
    return y.reshape(B * L, C)


def _hyena_spectra(feats, w1, b1, w2, b2, w3, L, tabs):
    N1, N2 = tabs["N1"], tabs["N2"]
    C = HY_WIDTH
    NL = w1.shape[0]
    kfull = _hyena_filter(feats, w1, b1, w2, b2, w3, L)
    a = _dft_stage1(kfull.reshape(NL, N1, N2, C), tabs["f1_real"], P=1)
    return _dft_spectrum(a.reshape(NL, N1, 2 * N2, C), tabs["g_fwd"], kb=DFT_KB)


def _ret_body(lg_ref, q_ref, k_ref, v_ref, g_ref, o_ref, st_ref):
    C = RET_BLOCK
    L = q_ref.shape[1]
    nblk = L // C
    dk = RET_DK
    h = pl.program_id(1)
    lgf = lg_ref[0, h]
    lgb = lg_ref[1, h]
    ii = lax.broadcasted_iota(jnp.int32, (C, C), 0)
    jj = lax.broadcasted_iota(jnp.int32, (C, C), 1)
    diff = (ii - jj).astype(F32)
    dm = jnp.where(diff >= 0, jnp.exp(jnp.maximum(diff, 0.0) * lgf), jnp.exp(jnp.maximum(-diff, 0.0) * lgb))
    pos = lax.broadcasted_iota(jnp.int32, (C, dk), 0).astype(F32)
    qw_f = jnp.exp((pos + 1.0) * lgf)
    kw_f = jnp.exp((C - 1.0 - pos) * lgf)
    qw_b = jnp.exp((C - pos) * lgb)
    kw_b = jnp.exp(pos * lgb)
    cd_f = jnp.exp(C * lgf)
    cd_b = jnp.exp(C * lgb)

    def block_kv(n, carry):
        sl = pl.ds(pl.multiple_of(n * C, C), C)
        k = k_ref[0, sl, :].astype(F32)
        kk = jnp.concatenate([k * kw_f, k * kw_b], axis=1).astype(BF16)
        st_ref[n] = _dot_tn(kk, v_ref[0, sl, :].astype(BF16))
        return carry

    lax.fori_loop(0, nblk, block_kv, 0, unroll=RET_UNROLL)

    def scan_f(n, S):
        kv = st_ref[n, :dk, :]
        st_ref[n, :dk, :] = S
        return S * cd_f + kv

    lax.fori_loop(0, nblk, scan_f, jnp.zeros((dk, RET_DV), F32))

    def scan_b(m, S):
        n = nblk - 1 - m
        kv = st_ref[n, dk:, :]
        st_ref[n, dk:, :] = S
        return S * cd_b + kv

    lax.fori_loop(0, nblk, scan_b, jnp.zeros((dk, RET_DV), F32))

    def block_out(n, carry):
        sl = pl.ds(pl.multiple_of(n * C, C), C)
        q = q_ref[0, sl, :].astype(BF16)
        scores = _dot_nt(q, k_ref[0, sl, :].astype(BF16)) * dm
        qf = q.astype(F32)
        qq = jnp.concatenate([qf * qw_f, qf * qw_b], axis=1).astype(BF16)
        o = _dot(scores.astype(BF16), v_ref[0, sl, :].astype(BF16)) + _dot(qq, st_ref[n].astype(BF16))
        mu = jnp.mean(o, axis=-1, keepdims=True)
        oc = o - mu
        on = oc * lax.rsqrt(jnp.mean(oc * oc, axis=-1, keepdims=True) + EPS)
        g = g_ref[0, sl, :].astype(F32)
        o_ref[0, sl, :] = (g * _sigmoid(g) * on).astype(o_ref.dtype)
        return carry

    lax.fori_loop(0, nblk, block_out, 0, unroll=RET_UNROLL)


def _retention_mixer(proj, log_gamma):
    B, L, _ = proj.shape
    blk = lambda col: pl.BlockSpec((1, L, LANES), lambda b, h, c=col // LANES: (b, 0, c + h))
    y = pl.pallas_call(
        _ret_body,
        grid=(B, RET_HEADS),
        in_specs=[pl.BlockSpec(memory_space=pltpu.SMEM),
                  blk(COL_RQ), blk(COL_RK), blk(COL_RV), blk(COL_RG)],
        out_specs=pl.BlockSpec((1, L, LANES), lambda b, h: (b, 0, h)),
        out_shape=jax.ShapeDtypeStruct((B, L, RET_WIDTH), BF16),
        scratch_shapes=[pltpu.VMEM((L // RET_BLOCK, 2 * RET_DK, RET_DV), F32)],
        compiler_params=_params(("parallel", "parallel")),
        name="retention",
    )(log_gamma, proj, proj, proj, proj)
    return y.reshape(B * L, RET_WIDTH)


def _att_body(q_ref, k_ref, v_ref, o_ref):
    G = ATT_HEADS // ATT_KV_HEADS
    tq = q_ref.shape[1]
    kc = min(ATT_KEY_CHUNK, k_ref.shape[1])
    nchunk = k_ref.shape[1] // kc
    q = q_ref[0].astype(BF16)
    qs = jnp.concatenate([q[:, g * ATT_HD:(g + 1) * ATT_HD] for g in range(G)], axis=0)
    m = jnp.full((G * tq, 1), -jnp.inf, F32)
    l = jnp.zeros((G * tq, 1), F32)
    acc = jnp.zeros((G * tq, ATT_HD), F32)
    s_next = _dot_nt(qs, k_ref[0, 0:kc, :].astype(BF16))
    for c in range(nchunk):
        s = s_next
        if c + 1 < nchunk:
            s_next = _dot_nt(qs, k_ref[0, (c + 1) * kc:(c + 2) * kc, :].astype(BF16))
        m_new = jnp.maximum(m, jnp.max(s, axis=-1, keepdims=True))
        alpha = jnp.exp2(m - m_new)
        p = jnp.exp2(s - m_new)
        l = alpha * l + jnp.sum(p, axis=-1, keepdims=True)
        acc = alpha * acc + _dot(p.astype(BF16), v_ref[0, c * kc:(c + 1) * kc, :].astype(BF16))
        m = m_new
    o = acc * (1.0 / l)
    for g in range(G):
        o_ref[0, :, g * ATT_HD:(g + 1) * ATT_HD] = o[g * tq:(g + 1) * tq].astype(o_ref.dtype)


def _attention_mixer(proj, *, tq):
    B, L, _ = proj.shape
    G = ATT_HEADS // ATT_KV_HEADS
    tq = min(tq, L)
    y = pl.pallas_call(
        _att_body,
        grid=(B, ATT_KV_HEADS, L // tq),
        in_specs=[pl.BlockSpec((1, tq, G * ATT_HD), lambda b, h, i: (b, i, COL_AQ // (G * ATT_HD) + h)),
                  pl.BlockSpec((1, L, ATT_HD), lambda b, h, i: (b, 0, COL_AK // ATT_HD + h)),
                  pl.BlockSpec((1, L, ATT_HD), lambda b, h, i: (b, 0, COL_AV // ATT_HD + h))],
        out_specs=pl.BlockSpec((1, tq, G * ATT_HD), lambda b, h, i: (b, i, h)),
        out_shape=jax.ShapeDtypeStruct((B, L, ATT_WIDTH), BF16),
        compiler_params=_params(("parallel", "parallel", "arbitrary")),
        name="attention",
    )(proj, proj, proj)
    return y.reshape(B * L, ATT_WIDTH)


def _tail_body(yh_ref, yr_ref, ya_ref, ga_ref, gb_ref, x_ref, kv_ref, wb_ref, wo_ref, gmix_ref,
               gxpre_ref, wq_ref, wxo_ref, gxpost_ref, gfpre_ref, w1_ref, w2_ref, gfpost_ref, o_ref, u_ref):
    D = D_MODEL
    ph = _dot(yh_ref[0], wb_ref[0:HY_WIDTH, :])
    pr = _dot(yr_ref[0], wb_ref[HY_WIDTH:HY_WIDTH + RET_WIDTH, :])
    pa = _dot(ya_ref[0], wb_ref[HY_WIDTH + RET_WIDTH:, :])
    ga = ga_ref[0].astype(F32)
    gb = gb_ref[0].astype(F32)
    hd = D // 2
    g0 = _sigmoid(ga[:, :D])
    g1 = _sigmoid(jnp.concatenate([ga[:, D:], gb[:, :hd]], axis=1))
    g2 = _sigmoid(gb[:, hd:])
    merged = g0 * ph + g1 * pr + g2 * pa
    x = x_ref[0] + _rms(_dot(merged.astype(BF16), wo_ref[...]), gmix_ref[...])
    h = _rms(x, gxpre_ref[...]).astype(BF16)
    q = (_dot(h, wq_ref[...]) * (X_HD ** -0.5)).astype(BF16)
    kv = kv_ref[0]
    outs = []
    for hh in range(X_HEADS):
        sl = slice(hh * X_HD, (hh + 1) * X_HD)
        s = _dot_nt(q[:, sl], kv[:, sl])
        p = jnp.exp(s - jnp.max(s, axis=-1, keepdims=True))
        l = jnp.sum(p, axis=-1, keepdims=True)
        vh = kv[:, D + hh * X_HD:D + (hh + 1) * X_HD]
        outs.append(_dot(p.astype(BF16), vh) * (1.0 / l))
    o = jnp.concatenate(outs, axis=1).astype(BF16)
    x = x + _rms(_dot(o, wxo_ref[...]), gxpost_ref[...])
    h = _rms(x, gfpre_ref[...]).astype(BF16)
    for c in range(D_FF // FF_CHUNK):
        sl = slice(c * FF_CHUNK, (c + 1) * FF_CHUNK)
        u_ref[:, sl] = jnp.square(jnp.maximum(_dot(h, w1_ref[:, sl]), 0.0)).astype(BF16)
    o_ref[0] = x + _rms(_dot(u_ref[...], w2_ref[...]), gfpost_ref[...])


def _layer_tail(yh, yr, ya, proj, x, kv, w, l, *, tm):
    B, L, D = x.shape
    M = kv.shape[2]
    gw = N_BRANCH * D // 2
    tok = lambda width, col=0: pl.BlockSpec((1, tm, width), lambda b, i, col=col: (b, i, col))
    vec = pl.BlockSpec((1, D), lambda b, i: (0, 0))
    res = lambda shape: pl.BlockSpec((None,) + shape, lambda b, i: (l, 0, 0), pipeline_mode=pl.Buffered(1))
    g = lambda name: w[name][l].reshape(1, D)
    return pl.pallas_call(
        _tail_body,
        grid=(B, L // tm),
        in_specs=[tok(HY_WIDTH), tok(RET_WIDTH), tok(ATT_WIDTH), tok(gw, COL_GATE // gw), tok(gw, COL_GATE // gw + 1),
                  tok(D), pl.BlockSpec((None, 1, M, 2 * D), lambda b, i: (l, b, 0, 0)),
                  res((MIX_WIDTH, D)), res((D, D)), vec,
                  vec, res((D, D)), res((D, D)), vec,
                  vec, res((D, D_FF)), res((D_FF, D)), vec],
        out_specs=tok(D),
        out_shape=jax.ShapeDtypeStruct((B, L, D), F32),
        scratch_shapes=[pltpu.VMEM((tm, D_FF), BF16)],
        compiler_params=_params(("parallel", "parallel")),
        name="layer_tail",
    )(yh, yr, ya, proj, proj, x, kv, w["w_branch"], w["w_out"], g("g_mix_post"),
      g("g_x_pre"), w["w_xq"], w["w_xo"], g("g_x_post"),
      g("g_ff_pre"), w["w_ff1"], w["w_ff2"], g("g_ff_post"))


def _rotary_tables(pos, dim, theta):
    inv = theta ** (-jnp.arange(0, dim, 2, dtype=F32) / dim)
    ang = pos.astype(F32)[:, None] * inv[None, :]
    return jnp.cos(ang), jnp.sin(ang)


def _position_tables(L):
    pos = jnp.arange(L)
    rows = L // GRID_W
    row_ids = jnp.repeat(jnp.arange(rows), GRID_W)
    col_ids = jnp.tile(jnp.arange(GRID_W), rows)
    rcos, rsin = _rotary_tables(pos, RET_DK, RET_THETA)
    rc, rs = _rotary_tables(row_ids, ATT_HD // 2, ROPE_THETA)
    cc, cs = _rotary_tables(col_ids, ATT_HD // 2, ROPE_THETA)
    return dict(
        ret_cos=jnp.concatenate([rcos, rcos], axis=-1),
        ret_sin=jnp.concatenate([-rsin, rsin], axis=-1),
        att_cos=jnp.concatenate([rc, rc, cc, cc], axis=-1),
        att_sin=jnp.concatenate([-rs, rs, -cs, cs], axis=-1),
        feats=_hyena_features(L),
        dft=_dft_tables(L),
    )


def _trunk(x, mem, w, pt):
    B, L, D = x.shape
    M = mem.shape[1]
    kspec = _hyena_spectra(pt["feats"], w["hy_fw1"], w["hy_fb1"], w["hy_fw2"], w["hy_fb2"], w["hy_fw3"], L, pt["dft"])
    kv = _norm_matmul(mem.reshape(B * M, D), w["g_mem"], w["w_xkv"], tm=TOKEN_TILE, tn=FF_CHUNK)
    kv = kv.reshape(-1, B, M, 2 * D)
    for l in range(DEPTH):
        proj = _input_projection(x.reshape(B * L, D), w["g_mix_pre"][l], w["w_in"], l, w["hy_conv"][l],
                                 w["att_qnorm"][l], w["att_knorm"][l], pt, L, tm=TOKEN_TILE).reshape(B, L, IN_COLS)
        y_h = _hyena_mixer(proj, kspec, l, w["hy_bias"][l], pt["dft"])
        y_r = _retention_mixer(proj, w["ret_lg"][l])
        y_a = _attention_mixer(proj, tq=ATT_QUERY_TILE)
        x = _layer_tail(y_h.reshape(B, L, HY_WIDTH), y_r.reshape(B, L, RET_WIDTH), y_a.reshape(B, L, ATT_WIDTH),
                        proj, x, kv, w, l, tm=TOKEN_TILE)
    return x


def kernel(x_prompt, x_sample, mem_prompt, mem_sample, g_mix_pre, g_mix_post, w_in, hy_conv, hy_fw1, hy_fb1,
           hy_fw2, hy_fb2, hy_fw3, hy_bias, ret_decay, att_qnorm, att_knorm, w_branch, w_out, g_x_pre, g_x_post,
           g_mem, w_xq, w_xkv, w_xo, g_ff_pre, g_ff_post, w_ff1, w_ff2):
    w = dict(
        g_mix_pre=g_mix_pre, g_mix_post=g_mix_post, w_in=w_in.astype(BF16), hy_conv=hy_conv,
        hy_fw1=hy_fw1, hy_fb1=hy_fb1, hy_fw2=hy_fw2, hy_fb2=hy_fb2, hy_fw3=hy_fw3, hy_bias=hy_bias,
        ret_lg=jax.nn.log_sigmoid(ret_decay.astype(F32)), att_qnorm=att_qnorm, att_knorm=att_knorm,
        w_branch=w_branch.astype(BF16), w_out=w_out.astype(BF16), g_x_pre=g_x_pre, g_x_post=g_x_post,
        g_mem=g_mem, w_xq=w_xq.astype(BF16), w_xkv=w_xkv.astype(BF16), w_xo=w_xo.astype(BF16),
        g_ff_pre=g_ff_pre, g_ff_post=g_ff_post, w_ff1=w_ff1.astype(BF16), w_ff2=w_ff2.astype(BF16),
    )
    y_prompt = _trunk(x_prompt, mem_prompt, w, _position_tables(x_prompt.shape[1]))
    y_sample = _trunk(x_sample, mem_sample, w, _position_tables(x_sample.shape[1]))
    return (y_prompt, y_sample)
```

```python
import functools
import math

import jax
import jax.numpy as jnp
from jax import lax
from jax.experimental import pallas as pl
from jax.experimental.pallas import tpu as pltpu

F32 = jnp.float32
BF16 = jnp.bfloat16

D_MODEL = 1024
DEPTH = 4
GRID_W = 64
EPS = 1e-6
HY_WIDTH = 512
HY_SHORT = 3
HY_EMB = 33
HY_FH = 64
HY_SIN_FREQ = 1.0
HY_TARGET = 1e-2
HY_FAST = 0.3
HY_SLOW = 1.5
HY_SHIFT = 0.0
RET_HEADS = 4
RET_DK = 128
RET_DV = 128
RET_WIDTH = RET_HEADS * RET_DV
RET_THETA = 10000.0
ATT_HEADS = 4
ATT_KV_HEADS = 2
ATT_HD = 128
ATT_WIDTH = ATT_HEADS * ATT_HD
ROPE_THETA = 10000.0
X_HEADS = 4
X_HD = D_MODEL // X_HEADS
D_FF = 4 * D_MODEL
N_BRANCH = 3
MIX_WIDTH = HY_WIDTH + RET_WIDTH + ATT_WIDTH

COL_HY = 0
COL_X0C = 0
COL_ZF = HY_WIDTH
COL_RQ = 3 * HY_WIDTH
COL_RK = COL_RQ + RET_HEADS * RET_DK
COL_RV = COL_RK + RET_HEADS * RET_DK
COL_RG = COL_RV + RET_WIDTH
COL_AQ = COL_RG + RET_WIDTH
COL_AK = COL_AQ + ATT_WIDTH
COL_AV = COL_AK + ATT_KV_HEADS * ATT_HD
COL_GATE = COL_AV + ATT_KV_HEADS * ATT_HD
IN_COLS = COL_GATE + N_BRANCH * D_MODEL

LANES = 128
HALO = 8
VMEM_LIMIT = 56 * 1024 * 1024
DFT_N2 = 128
DFT_NB = 16
DFT_KB = 16
DFT_UNROLL = 8
RET_BLOCK = 256
RET_UNROLL = 16
ATT_KEY_CHUNK = 1024
ATT_QUERY_TILE = 1024
FF_CHUNK = 1024
TOKEN_TILE = 512
LOG2E = 1.4426950408889634


def _params(sem):
    return pltpu.CompilerParams(dimension_semantics=sem, vmem_limit_bytes=VMEM_LIMIT)


def _dot(a, b):
    return jnp.dot(a, b, preferred_element_type=F32)


def _dot_nt(a, b):
    return lax.dot_general(a, b, (((1,), (1,)), ((), ())), preferred_element_type=F32)


def _dot_tn(a, b):
    return lax.dot_general(a, b, (((0,), (0,)), ((), ())), preferred_element_type=F32)


def _rms(x, g):
    ms = jnp.mean(x * x, axis=-1, keepdims=True)
    return x * lax.rsqrt(ms + EPS) * g


def _sigmoid(x):
    return 0.5 * jnp.tanh(0.5 * x) + 0.5


def _split(x):
    hi = x.astype(BF16)
    lo = (x - hi.astype(F32)).astype(BF16)
    return hi, lo


def _dot3(a, b):
    return _dot(a[0], b[0]) + (_dot(a[0], b[1]) + _dot(a[1], b[0]))


def _bdot(a, b):
    return _dot(a.astype(BF16), b.astype(BF16))


def _nm_body(x_ref, g_ref, w_ref, o_ref, *, tn):
    h = _rms(x_ref[...], g_ref[...]).astype(BF16)
    for j in range(w_ref.shape[1] // tn):
        o_ref[:, j * tn:(j + 1) * tn] = _dot(h, w_ref[:, j * tn:(j + 1) * tn]).astype(o_ref.dtype)


def _norm_matmul(x, g, w, *, tm, tn):
    T, D = x.shape
    NL, _, N = w.shape
    return pl.pallas_call(
        functools.partial(_nm_body, tn=tn),
        grid=(NL, T // tm),
        in_specs=[pl.BlockSpec((tm, D), lambda l, i: (i, 0)),
                  pl.BlockSpec((None, 1, D), lambda l, i: (l, 0, 0)),
                  pl.BlockSpec((None, D, N), lambda l, i: (l, 0, 0))],
        out_specs=pl.BlockSpec((None, tm, N), lambda l, i: (l, i, 0)),
        out_shape=jax.ShapeDtypeStruct((NL, T, N), BF16),
        compiler_params=_params(("parallel", "parallel")),
        name="norm_matmul",
    )(x, g.reshape(NL, 1, D), w)


def _rotate_half(x, cos, sin):
    return x * cos + pltpu.roll(x, x.shape[-1] // 2, 1) * sin


def _axial_rope(x, gain, cos, sin, scale):
    x = _rms(x, gain)
    lane = lax.broadcasted_iota(jnp.int32, x.shape, 1)
    lower = (lane % (ATT_HD // 2)) < (ATT_HD // 4)
    swapped = jnp.where(lower, pltpu.roll(x, LANES - ATT_HD // 4, 1), pltpu.roll(x, ATT_HD // 4, 1))
    return (x * cos + swapped * sin) * scale


_WIN_SEGMENTS = (
    [(COL_RQ, RET_WIDTH, "ret_q"), (COL_RK, RET_WIDTH, "ret_k"), (COL_RV, RET_WIDTH, None),
       (COL_RG, RET_WIDTH, None), (COL_AQ, ATT_WIDTH, "att_q"), (COL_AK, COL_GATE - COL_AK, "att_kv")]
    + [(COL_GATE + i * D_MODEL, D_MODEL, None) for i in range(N_BRANCH)])


def _win_body(x_ref, xp_ref, xn_ref, g_ref, w_ref, cw_ref, qn_ref, kn_ref, rc_ref, rs_ref, ac_ref, as_ref, o_ref,
              *, per_seq):
    tm = x_ref.shape[0]
    pos = pl.program_id(0) % per_seq
    x_ext = jnp.concatenate([xp_ref[...], x_ref[...], xn_ref[...]], axis=0)
    r_ext = lax.broadcasted_iota(jnp.int32, (tm + 2 * HALO, 1), 0)
    pad = ((r_ext < HALO) & (pos == 0)) | ((r_ext >= tm + HALO) & (pos == per_seq - 1))
    hf = jnp.where(pad, 0.0, _rms(x_ext, g_ref[...]))
    h_ext = hf.astype(BF16)
    h = hf[HALO:HALO + tm].astype(BF16)
    row = lax.broadcasted_iota(jnp.int32, (tm, HY_WIDTH), 0)
    conv = []
    for grp in range(3):
        cols = slice(COL_HY + grp * HY_WIDTH, COL_HY + (grp + 1) * HY_WIDTH)
        y_ext = _dot(h_ext, w_ref[:, cols])
        y = y_ext[HALO:HALO + tm]
        y_prev = y_ext[HALO - 1:HALO]
        y_next = y_ext[HALO + tm:HALO + tm + 1]
        prev = jnp.where(row == 0, y_prev, pltpu.roll(y, 1, 0))
        nxt = jnp.where(row == tm - 1, y_next, pltpu.roll(y, tm - 1, 0))
        cw = cw_ref[:, cols]
        conv.append(prev * cw[0:1] + y * cw[1:2] + nxt * cw[2:3])
    x0c, x1c, vc = conv
    o_ref[:, COL_X0C:COL_X0C + HY_WIDTH] = x0c.astype(o_ref.dtype)
    o_ref[:, COL_ZF:COL_ZF + HY_WIDTH] = (vc * x1c).astype(o_ref.dtype)
    o_ref[:, COL_ZF + HY_WIDTH:COL_RQ] = vc.astype(o_ref.dtype)
    for start, width, kind in _WIN_SEGMENTS:
        y = _dot(h, w_ref[:, start:start + width])
        if kind is None:
            o_ref[:, start:start + width] = y.astype(o_ref.dtype)
            continue
        for hh in range(width // LANES):
            yh = y[:, hh * LANES:(hh + 1) * LANES]
            if kind == "ret_q":
                yh = _rotate_half(yh, rc_ref[...], rs_ref[...])
            elif kind == "ret_k":
                yh = _rotate_half(yh, rc_ref[...], rs_ref[...]) * (RET_DK ** -0.5)
            elif kind == "att_q":
                yh = _axial_rope(yh, qn_ref[...], ac_ref[...], as_ref[...], ATT_HD ** -0.5 * LOG2E)
            elif hh < ATT_KV_HEADS:
                yh = _axial_rope(yh, kn_ref[...], ac_ref[...], as_ref[...], 1.0)
            o_ref[:, start + hh * LANES:start + (hh + 1) * LANES] = yh.astype(o_ref.dtype)


def _input_projection(x, g, w_stack, l, conv_w, qn, kn, pt, L, *, tm):
    T, D = x.shape
    N = w_stack.shape[2]
    per_seq = L // tm
    halo_per_tile = tm // HALO
    vec = lambda n: pl.BlockSpec((1, n), lambda i: (0, 0))
    tab = pl.BlockSpec((tm, LANES), lambda i: (i % per_seq, 0))
    return pl.pallas_call(
        functools.partial(_win_body, per_seq=per_seq),
        grid=(T // tm,),
        in_specs=[pl.BlockSpec((tm, D), lambda i: (i, 0)),
                  pl.BlockSpec((HALO, D), lambda i: (jnp.maximum(i * halo_per_tile - 1, 0), 0)),
                  pl.BlockSpec((HALO, D), lambda i: (jnp.minimum((i + 1) * halo_per_tile, T // HALO - 1), 0)),
                  vec(D),
                  pl.BlockSpec((None, D, N), lambda i: (l, 0, 0), pipeline_mode=pl.Buffered(1)),
                  pl.BlockSpec((HY_SHORT, 3 * HY_WIDTH), lambda i: (0, 0)),
                  vec(ATT_HD), vec(ATT_HD), tab, tab, tab, tab],
        out_specs=pl.BlockSpec((tm, N), lambda i: (i, 0)),
        out_shape=jax.ShapeDtypeStruct((T, N), BF16),
        compiler_params=_params(("parallel",)),
        name="input_projection",
    )(x, x, x, g.reshape(1, D), w_stack, conv_w, qn.reshape(1, ATT_HD), kn.reshape(1, ATT_HD),
      pt["ret_cos"], pt["ret_sin"], pt["att_cos"], pt["att_sin"])


def _filt_body(f_ref, w1a_ref, w1b_ref, b1_ref, w2_ref, b2_ref, w3a_ref, w3b_ref, dl_ref, o_ref, *, L):
    tl = f_ref.shape[0]
    f = f_ref[...]
    fa, fb = _split(f[:tl // 2]), _split(f[tl // 2:])
    pre = _dot3(fa, _split(w1a_ref[...])) + _dot3(fb, _split(w1b_ref[...]))
    h = jnp.sin(HY_SIN_FREQ * (pre + b1_ref[...]))
    h = jnp.sin(HY_SIN_FREQ * (_bdot(h, w2_ref[...]) + b2_ref[...]))
    filt = jnp.concatenate([_bdot(h, w3a_ref[...]), _bdot(h, w3b_ref[...])], axis=0)
    mod = jnp.exp(-f[:, 0:1] * dl_ref[...]) + HY_SHIFT
    row = pl.program_id(1) * tl + lax.broadcasted_iota(jnp.int32, (tl, HY_WIDTH), 0)
    sel = jnp.where(row < L, filt[:, :HY_WIDTH], filt[:, HY_WIDTH:])
    o_ref[...] = jnp.where(row == L, 0.0, sel * mod)


def _hyena_features(L):
    t = jnp.linspace(0.0, 1.0, L, dtype=F32)[:, None]
    bands = (HY_EMB - 1) // 2
    w = 2.0 * math.pi * jnp.arange(L, dtype=F32)[:, None] / L
    f = jnp.linspace(1e-4, bands - 1, bands, dtype=F32)[None, :]
    z = jnp.concatenate([t, jnp.cos(f * w), -jnp.sin(f * w)], axis=-1)
    zc = jnp.concatenate([z, z[:1], jnp.flip(z[1:], axis=0)], axis=0)
    return jnp.pad(zc, ((0, 0), (0, LANES - HY_EMB)))


def _hyena_filter(feats, w1, b1, w2, b2, w3, L):
    tl = TOKEN_TILE
    NL = w1.shape[0]
    assert 2 * HY_FH == LANES
    zw = jnp.zeros_like(w2)
    w1a = jnp.pad(w1, ((0, 0), (0, LANES - HY_EMB), (0, HY_FH)))
    w1b = jnp.pad(w1, ((0, 0), (0, LANES - HY_EMB), (HY_FH, 0)))
    b1p = jnp.concatenate([b1, b1], axis=1).reshape(NL, 1, LANES)
    w2p = jnp.concatenate([jnp.concatenate([w2, zw], axis=2), jnp.concatenate([zw, w2], axis=2)], axis=1)
    b2p = jnp.concatenate([b2, b2], axis=1).reshape(NL, 1, LANES)
    w3a = jnp.pad(w3, ((0, 0), (0, HY_FH), (0, 0)))
    w3b = jnp.pad(w3, ((0, 0), (HY_FH, 0), (0, 0)))
    min_decay = math.log(HY_TARGET) / HY_SLOW
    max_decay = math.log(HY_TARGET) / HY_FAST
    deltas = jnp.abs(jnp.linspace(min_decay, max_decay, HY_WIDTH, dtype=F32)).reshape(1, HY_WIDTH)
    layer = lambda r, c: pl.BlockSpec((None, r, c), lambda l, i: (l, 0, 0))
    return pl.pallas_call(
        functools.partial(_filt_body, L=L),
        grid=(NL, 2 * L // tl),
        in_specs=[pl.BlockSpec((tl, LANES), lambda l, i: (i, 0)),
                  layer(LANES, LANES), layer(LANES, LANES), layer(1, LANES), layer(LANES, LANES), layer(1, LANES),
                  layer(LANES, 2 * HY_WIDTH), layer(LANES, 2 * HY_WIDTH),
                  pl.BlockSpec((1, HY_WIDTH), lambda l, i: (0, 0))],
        out_specs=pl.BlockSpec((None, tl, HY_WIDTH), lambda l, i: (l, i, 0)),
        out_shape=jax.ShapeDtypeStruct((NL, 2 * L, HY_WIDTH), F32),
        compiler_params=_params(("parallel", "parallel")),
        name="hyena_filter",
    )(feats, w1a, w1b, b1p, w2p, b2p, w3a, w3b, deltas)


def _expanded(fn, rows, cols):
    small = fn(lax.broadcasted_iota(jnp.int32, (rows, cols), 0), lax.broadcasted_iota(jnp.int32, (rows, cols), 1))
    hp = functools.partial(jnp.dot, precision=lax.Precision.HIGHEST)
    pick_r = (lax.broadcasted_iota(jnp.int32, (rows * DFT_NB, rows), 0) // DFT_NB
              == lax.broadcasted_iota(jnp.int32, (rows * DFT_NB, rows), 1)).astype(F32)
    pick_c = (lax.broadcasted_iota(jnp.int32, (cols, cols * DFT_NB), 1) // DFT_NB
              == lax.broadcasted_iota(jnp.int32, (cols, cols * DFT_NB), 0)).astype(F32)
    ri = lax.broadcasted_iota(jnp.int32, (rows * DFT_NB, cols * DFT_NB), 0)
    ci = lax.broadcasted_iota(jnp.int32, (rows * DFT_NB, cols * DFT_NB), 1)
    return jnp.where(ri % DFT_NB == ci % DFT_NB, hp(hp(pick_r, small), pick_c), 0.0).astype(BF16)


def _dft_tables(L):
    N = 2 * L
    N2 = DFT_N2
    N1 = N // N2
    N1h = N1 // 2

    def cs1(k1, n1):
        ang = ((k1 * n1) % N1).astype(F32) * (2.0 * math.pi / N1)
        return jnp.cos(ang), jnp.sin(ang)

    def f1_pair(r, c):
        cs, sn = cs1(r // 2, c % N1h)
        re_row, first = (r % 2) == 0, c < N1h
        return jnp.where(re_row, jnp.where(first, cs, sn), jnp.where(first, -sn, cs))

    def f1_real(r, c):
        cs, sn = cs1(r // 2, c)
        return jnp.where((r % 2) == 0, cs, -sn)

    def f3_pair(r, c):
        cs, sn = cs1(c // 2, r % N1h)
        re_out, re_in = r < N1h, (c % 2) == 0
        return jnp.where(re_out, jnp.where(re_in, cs, -sn), jnp.where(re_in, sn, cs)) * (1.0 / N)

    kk = (jnp.arange(N1, dtype=jnp.int32)[:, None, None]
          + N1 * jnp.arange(N2, dtype=jnp.int32)[None, :, None])
    n2 = jnp.arange(N2, dtype=jnp.int32)[None, None, :]
    a2 = ((kk * n2) % N).astype(F32) * (2.0 * math.pi / N)
    gr, gi = jnp.cos(a2), -jnp.sin(a2)
    g_fwd = jnp.concatenate([jnp.concatenate([gr, -gi], axis=2),
                             jnp.concatenate([gi, gr], axis=2)], axis=1)
    return dict(N1=N1, N1h=N1h, N2=N2,
                f1_pair=_expanded(f1_pair, 2 * N1, 2 * N1h), f1_real=_expanded(f1_real, 2 * N1, N1),
                f3_pair=_expanded(f3_pair, 2 * N1h, 2 * N1), g_fwd=g_fwd.astype(BF16))


def _dft1_body(x_ref, f_ref, o_ref):
    P, K, NB, C = x_ref.shape
    y = _bdot(f_ref[...], x_ref[...].reshape(P * K * NB, C))
    o_ref[0] = y.astype(o_ref.dtype).reshape(o_ref.shape[1:])


def _dft_stage1(x, f, *, P, col=0):
    Bx, K, N2, _ = x.shape
    C = HY_WIDTH
    N1 = f.shape[0] // (2 * DFT_NB)
    return pl.pallas_call(
        _dft1_body,
        grid=(Bx // P, N2 // DFT_NB),
        in_specs=[pl.BlockSpec((P, K, DFT_NB, C), lambda b, j: (b, 0, j, col)),
                  pl.BlockSpec(f.shape, lambda b, j: (0, 0), pipeline_mode=pl.Buffered(1))],
        out_specs=pl.BlockSpec((1, N1, 2, DFT_NB, C), lambda b, j: (b, 0, 0, j, 0)),
        out_shape=jax.ShapeDtypeStruct((Bx // P, N1, 2, N2, C), BF16),
        compiler_params=_params(("parallel", "parallel")),
        name="dft_stage1",
    )(x, f)


def _spec_body(a_ref, g_ref, o_ref):
    def step(k, carry):
        o_ref[k] = _bdot(g_ref[k], a_ref[0, k])
        return carry

    lax.fori_loop(0, a_ref.shape[1], step, 0, unroll=DFT_UNROLL)


def _dft_spectrum(a, g, *, kb):
    NL, N1, R, C = a.shape
    return pl.pallas_call(
        _spec_body,
        grid=(N1 // kb, NL),
        in_specs=[pl.BlockSpec((1, kb, R, C), lambda i, l: (l, i, 0, 0)),
                  pl.BlockSpec((kb, R, R), lambda i, l: (i, 0, 0))],
        out_specs=pl.BlockSpec((None, kb, R, C), lambda i, l: (l, i, 0, 0)),
        out_shape=jax.ShapeDtypeStruct((NL, N1, R, C), F32),
        compiler_params=_params(("parallel", "parallel")),
        name="dft_spectrum",
    )(a, g)


def _dft2_body(a_ref, g_ref, ks_ref, o_ref):
    H = a_ref.shape[2] // 2

    def step(k, carry):
        y = _bdot(g_ref[k], a_ref[0, k])
        yr, yi = y[:H], y[H:]
        kr, ki = ks_ref[k, :H, :], ks_ref[k, H:, :]
        z = jnp.concatenate([yr * kr - yi * ki, yr * ki + yi * kr], axis=0)
        o_ref[0, k] = _dot_tn(g_ref[k], z.astype(g_ref.dtype)).astype(o_ref.dtype)
        return carry

    lax.fori_loop(0, a_ref.shape[1], step, 0, unroll=DFT_UNROLL)


def _dft_stage2(a, g, ks, l, *, kb):
    Bp, N1, R, C = a.shape
    dat = pl.BlockSpec((1, kb, R, C), lambda i, b: (b, i, 0, 0))
    return pl.pallas_call(
        _dft2_body,
        grid=(N1 // kb, Bp),
        in_specs=[dat, pl.BlockSpec((kb, R, R), lambda i, b: (i, 0, 0)),
                  pl.BlockSpec((None, kb, R, C), lambda i, b: (l, i, 0, 0))],
        out_specs=dat,
        out_shape=jax.ShapeDtypeStruct((Bp, N1, R, C), BF16),
        compiler_params=_params(("parallel", "parallel")),
        name="dft_stage2",
    )(a, g, ks)


def _dft3_body(t_ref, f_ref, zf_ref, x0_ref, bias_ref, o_ref):
    _, N1, _, NB, C = t_ref.shape
    y = _bdot(f_ref[...], t_ref[0].reshape(N1 * 2 * NB, C)).reshape(zf_ref.shape)
    o_ref[...] = ((y + zf_ref[...].astype(F32) * bias_ref[...]) * x0_ref[...].astype(F32)).astype(o_ref.dtype)


def _dft_stage3(t, f, proj4, bias):
    Bp, N1, _, N2, C = t.shape
    B, N1h = proj4.shape[:2]
    pair = lambda col=0: pl.BlockSpec((2, N1h, DFT_NB, C), lambda b, j: (b, 0, j, col))
    return pl.pallas_call(
        _dft3_body,
        grid=(Bp, N2 // DFT_NB),
        in_specs=[pl.BlockSpec((1, N1, 2, DFT_NB, C), lambda b, j: (b, 0, 0, j, 0)),
                  pl.BlockSpec(f.shape, lambda b, j: (0, 0), pipeline_mode=pl.Buffered(1)),
                  pair(COL_ZF // C), pair(COL_X0C // C),
                  pl.BlockSpec((1, C), lambda b, j: (0, 0))],
        out_specs=pair(),
        out_shape=jax.ShapeDtypeStruct((B, N1h, N2, C), BF16),
        compiler_params=_params(("parallel", "parallel")),
        name="dft_stage3",
    )(t, f, proj4, proj4, bias)


def _hyena_mixer(proj, kspec, l, bias, tabs):
    B, L, _ = proj.shape
    N1, N1h, N2 = tabs["N1"], tabs["N1h"], tabs["N2"]
    C = HY_WIDTH
    proj4 = proj.reshape(B, N1h, N2, IN_COLS)
    a = _dft_stage1(proj4, tabs["f1_pair"], P=2, col=COL_ZF // C)
    t = _dft_stage2(a.reshape(B // 2, N1, 2 * N2, C), tabs["g_fwd"], kspec, l, kb=DFT_KB)
    y = _dft_stage3(t.reshape(B // 2, N1, 2, N2, C), tabs["f3_pair"], proj4, bias.astype(F32).reshape(1, C))
    return y.reshape(B * L, C)


def _hyena_spectra(feats, w1, b1, w2, b2, w3, L, tabs):
    N1, N2 = tabs["N1"], tabs["N2"]
    C = HY_WIDTH
    NL = w1.shape[0]
    kfull = _hyena_filter(feats, w1, b1, w2, b2, w3, L)
    a = _dft_stage1(kfull.reshape(NL, N1, N2, C), tabs["f1_real"], P=1)
    return _dft_spectrum(a.reshape(NL, N1, 2 * N2, C), tabs["g_fwd"], kb=DFT_KB)


def _ret_body(lg_ref, q_ref, k_ref, v_ref, g_ref, o_ref, st_ref):
    C = RET_BLOCK
    L = q_ref.shape[1]
    nblk = L // C
    dk = RET_DK
    h = pl.program_id(1)
    lgf = lg_ref[0, h]
    lgb = lg_ref[1, h]
    ii = lax.broadcasted_iota(jnp.int32, (C, C), 0)
    jj = lax.broadcasted_iota(jnp.int32, (C, C), 1)
    diff = (ii - jj).astype(F32)
    dm = jnp.where(diff >= 0, jnp.exp(jnp.maximum(diff, 0.0) * lgf), jnp.exp(jnp.maximum(-diff, 0.0) * lgb))
    pos = lax.broadcasted_iota(jnp.int32, (C, dk), 0).astype(F32)
    qw_f = jnp.exp((pos + 1.0) * lgf)
    kw_f = jnp.exp((C - 1.0 - pos) * lgf)
    qw_b = jnp.exp((C - pos) * lgb)
    kw_b = jnp.exp(pos * lgb)
    cd_f = jnp.exp(C * lgf)
    cd_b = jnp.exp(C * lgb)

    def block_kv(n, carry):
        sl = pl.ds(pl.multiple_of(n * C, C), C)
        k = k_ref[0, sl, :].astype(F32)
        kk = jnp.concatenate([k * kw_f, k * kw_b], axis=1).astype(BF16)
        st_ref[n] = _dot_tn(kk, v_ref[0, sl, :].astype(BF16))
        return carry

    lax.fori_loop(0, nblk, block_kv, 0, unroll=RET_UNROLL)

    def scan_f(n, S):
        kv = st_ref[n, :dk, :]
        st_ref[n, :dk, :] = S
        return S * cd_f + kv

    lax.fori_loop(0, nblk, scan_f, jnp.zeros((dk, RET_DV), F32))

    def scan_b(m, S):
        n = nblk - 1 - m
        kv = st_ref[n, dk:, :]
        st_ref[n, dk:, :] = S
        return S * cd_b + kv

    lax.fori_loop(0, nblk, scan_b, jnp.zeros((dk, RET_DV), F32))

    def block_out(n, carry):
        sl = pl.ds(pl.multiple_of(n * C, C), C)
        q = q_ref[0, sl, :].astype(BF16)
        scores = _dot_nt(q, k_ref[0, sl, :].astype(BF16)) * dm
        qf = q.astype(F32)
        qq = jnp.concatenate([qf * qw_f, qf * qw_b], axis=1).astype(BF16)
        o = _dot(scores.astype(BF16), v_ref[0, sl, :].astype(BF16)) + _dot(qq, st_ref[n].astype(BF16))
        mu = jnp.mean(o, axis=-1, keepdims=True)
        oc = o - mu
        on = oc * lax.rsqrt(jnp.mean(oc * oc, axis=-1, keepdims=True) + EPS)
        g = g_ref[0, sl, :].astype(F32)
        o_ref[0, sl, :] = (g * _sigmoid(g) * on).astype(o_ref.dtype)
        return carry

    lax.fori_loop(0, nblk, block_out, 0, unroll=RET_UNROLL)


def _retention_mixer(proj, log_gamma):
    B, L, _ = proj.shape
    blk = lambda col: pl.BlockSpec((1, L, LANES), lambda b, h, c=col // LANES: (b, 0, c + h))
    y = pl.pallas_call(
        _ret_body,
        grid=(B, RET_HEADS),
        in_specs=[pl.BlockSpec(memory_space=pltpu.SMEM),
                  blk(COL_RQ), blk(COL_RK), blk(COL_RV), blk(COL_RG)],
        out_specs=pl.BlockSpec((1, L, LANES), lambda b, h: (b, 0, h)),
        out_shape=jax.ShapeDtypeStruct((B, L, RET_WIDTH), BF16),
        scratch_shapes=[pltpu.VMEM((L // RET_BLOCK, 2 * RET_DK, RET_DV), F32)],
        compiler_params=_params(("parallel", "parallel")),
        name="retention",
    )(log_gamma, proj, proj, proj, proj)
    return y.reshape(B * L, RET_WIDTH)


def _att_body(q_ref, k_ref, v_ref, o_ref):
    G = ATT_HEADS // ATT_KV_HEADS
    tq = q_ref.shape[1]
    kc = min(ATT_KEY_CHUNK, k_ref.shape[1])
    nchunk = k_ref.shape[1] // kc
    q = q_ref[0].astype(BF16)
    qs = jnp.concatenate([q[:, g * ATT_HD:(g + 1) * ATT_HD] for g in range(G)], axis=0)
    m = jnp.full((G * tq, 1), -jnp.inf, F32)
    l = jnp.zeros((G * tq, 1), F32)
    acc = jnp.zeros((G * tq, ATT_HD), F32)
    s_next = _dot_nt(qs, k_ref[0, 0:kc, :].astype(BF16))
    for c in range(nchunk):
        s = s_next
        if c + 1 < nchunk:
            s_next = _dot_nt(qs, k_ref[0, (c + 1) * kc:(c + 2) * kc, :].astype(BF16))
        m_new = jnp.maximum(m, jnp.max(s, axis=-1, keepdims=True))
        alpha = jnp.exp2(m - m_new)
        p = jnp.exp2(s - m_new)
        l = alpha * l + jnp.sum(p, axis=-1, keepdims=True)
        acc = alpha * acc + _dot(p.astype(BF16), v_ref[0, c * kc:(c + 1) * kc, :].astype(BF16))
        m = m_new
    o = acc * (1.0 / l)
    for g in range(G):
        o_ref[0, :, g * ATT_HD:(g + 1) * ATT_HD] = o[g * tq:(g + 1) * tq].astype(o_ref.dtype)


def _attention_mixer(proj, *, tq):
    B, L, _ = proj.shape
    G = ATT_HEADS // ATT_KV_HEADS
    tq = min(tq, L)
    y = pl.pallas_call(
        _att_body,
        grid=(B, ATT_KV_HEADS, L // tq),
        in_specs=[pl.BlockSpec((1, tq, G * ATT_HD), lambda b, h, i: (b, i, COL_AQ // (G * ATT_HD) + h)),
                  pl.BlockSpec((1, L, ATT_HD), lambda b, h, i: (b, 0, COL_AK // ATT_HD + h)),
                  pl.BlockSpec((1, L, ATT_HD), lambda b, h, i: (b, 0, COL_AV // ATT_HD + h))],
        out_specs=pl.BlockSpec((1, tq, G * ATT_HD), lambda b, h, i: (b, i, h)),
        out_shape=jax.ShapeDtypeStruct((B, L, ATT_WIDTH), BF16),
        compiler_params=_params(("parallel", "parallel", "arbitrary")),
        name="attention",
    )(proj, proj, proj)
    return y.reshape(B * L, ATT_WIDTH)


def _tail_body(yh_ref, yr_ref, ya_ref, ga_ref, gb_ref, x_ref, kv_ref, wb_ref, wo_ref, gmix_ref,
               gxpre_ref, wq_ref, wxo_ref, gxpost_ref, gfpre_ref, w1_ref, w2_ref, gfpost_ref, o_ref, u_ref):
    D = D_MODEL
    ph = _dot(yh_ref[0], wb_ref[0:HY_WIDTH, :])
    pr = _dot(yr_ref[0], wb_ref[HY_WIDTH:HY_WIDTH + RET_WIDTH, :])
    pa = _dot(ya_ref[0], wb_ref[HY_WIDTH + RET_WIDTH:, :])
    ga = ga_ref[0].astype(F32)
    gb = gb_ref[0].astype(F32)
    hd = D // 2
    g0 = _sigmoid(ga[:, :D])
    g1 = _sigmoid(jnp.concatenate([ga[:, D:], gb[:, :hd]], axis=1))
    g2 = _sigmoid(gb[:, hd:])
    merged = g0 * ph + g1 * pr + g2 * pa
    x = x_ref[0] + _rms(_dot(merged.astype(BF16), wo_ref[...]), gmix_ref[...])
    h = _rms(x, gxpre_ref[...]).astype(BF16)
    q = (_dot(h, wq_ref[...]) * (X_HD ** -0.5)).astype(BF16)
    kv = kv_ref[0]
    outs = []
    for hh in range(X_HEADS):
        sl = slice(hh * X_HD, (hh + 1) * X_HD)
        s = _dot_nt(q[:, sl], kv[:, sl])
        p = jnp.exp(s - jnp.max(s, axis=-1, keepdims=True))
        l = jnp.sum(p, axis=-1, keepdims=True)
        vh = kv[:, D + hh * X_HD:D + (hh + 1) * X_HD]
        outs.append(_dot(p.astype(BF16), vh) * (1.0 / l))
    o = jnp.concatenate(outs, axis=1).astype(BF16)
    x = x + _rms(_dot(o, wxo_ref[...]), gxpost_ref[...])
    h = _rms(x, gfpre_ref[...]).astype(BF16)
    for c in range(D_FF // FF_CHUNK):
        sl = slice(c * FF_CHUNK, (c + 1) * FF_CHUNK)
        u_ref[:, sl] = jnp.square(jnp.maximum(_dot(h, w1_ref[:, sl]), 0.0)).astype(BF16)
    o_ref[0] = x + _rms(_dot(u_ref[...], w2_ref[...]), gfpost_ref[...])


def _layer_tail(yh, yr, ya, proj, x, kv, w, l, *, tm):
    B, L, D = x.shape
    M = kv.shape[2]
    gw = N_BRANCH * D // 2
    tok = lambda width, col=0: pl.BlockSpec((1, tm, width), lambda b, i, col=col: (b, i, col))
    vec = pl.BlockSpec((1, D), lambda b, i: (0, 0))
    res = lambda shape: pl.BlockSpec((None,) + shape, lambda b, i: (l, 0, 0), pipeline_mode=pl.Buffered(1))
    g = lambda name: w[name][l].reshape(1, D)
    return pl.pallas_call(
        _tail_body,
        grid=(B, L // tm),
        in_specs=[tok(HY_WIDTH), tok(RET_WIDTH), tok(ATT_WIDTH), tok(gw, COL_GATE // gw), tok(gw, COL_GATE // gw + 1),
                  tok(D), pl.BlockSpec((None, 1, M, 2 * D), lambda b, i: (l, b, 0, 0)),
                  res((MIX_WIDTH, D)), res((D, D)), vec,
                  vec, res((D, D)), res((D, D)), vec,
                  vec, res((D, D_FF)), res((D_FF, D)), vec],
        out_specs=tok(D),
        out_shape=jax.ShapeDtypeStruct((B, L, D), F32),
        scratch_shapes=[pltpu.VMEM((tm, D_FF), BF16)],
        compiler_params=_params(("parallel", "parallel")),
        name="layer_tail",
    )(yh, yr, ya, proj, proj, x, kv, w["w_branch"], w["w_out"], g("g_mix_post"),
      g("g_x_pre"), w["w_xq"], w["w_xo"], g("g_x_post"),
      g("g_ff_pre"), w["w_ff1"], w["w_ff2"], g("g_ff_post"))


def _rotary_tables(pos, dim, theta):
    inv = theta ** (-jnp.arange(0, dim, 2, dtype=F32) / dim)
    ang = pos.astype(F32)[:, None] * inv[None, :]
    return jnp.cos(ang), jnp.sin(ang)


def _position_tables(L):
    pos = jnp.arange(L)
    rows = L // GRID_W
    row_ids = jnp.repeat(jnp.arange(rows), GRID_W)
    col_ids = jnp.tile(jnp.arange(GRID_W), rows)
    rcos, rsin = _rotary_tables(pos, RET_DK, RET_THETA)
    rc, rs = _rotary_tables(row_ids, ATT_HD // 2, ROPE_THETA)
    cc, cs = _rotary_tables(col_ids, ATT_HD // 2, ROPE_THETA)
    return dict(
        ret_cos=jnp.concatenate([rcos, rcos], axis=-1),
        ret_sin=jnp.concatenate([-rsin, rsin], axis=-1),
        att_cos=jnp.concatenate([rc, rc, cc, cc], axis=-1),
        att_sin=jnp.concatenate([-rs, rs, -cs, cs], axis=-1),
        feats=_hyena_features(L),
        dft=_dft_tables(L),
    )


def _trunk(x, mem, w, pt):
    B, L, D = x.shape
    M = mem.shape[1]
    kspec = _hyena_spectra(pt["feats"], w["hy_fw1"], w["hy_fb1"], w["hy_fw2"], w["hy_fb2"], w["hy_fw3"], L, pt["dft"])
    kv = _norm_matmul(mem.reshape(B * M, D), w["g_mem"], w["w_xkv"], tm=TOKEN_TILE, tn=FF_CHUNK)
    kv = kv.reshape(-1, B, M, 2 * D)
    for l in range(DEPTH):
        proj = _input_projection(x.reshape(B * L, D), w["g_mix_pre"][l], w["w_in"], l, w["hy_conv"][l],
                                 w["att_qnorm"][l], w["att_knorm"][l], pt, L, tm=TOKEN_TILE).reshape(B, L, IN_COLS)
        y_h = _hyena_mixer(proj, kspec, l, w["hy_bias"][l], pt["dft"])
        y_r = _retention_mixer(proj, w["ret_lg"][l])
        y_a = _attention_mixer(proj, tq=ATT_QUERY_TILE)
        x = _layer_tail(y_h.reshape(B, L, HY_WIDTH), y_r.reshape(B, L, RET_WIDTH), y_a.reshape(B, L, ATT_WIDTH),
                        proj, x, kv, w, l, tm=TOKEN_TILE)
    return x


def kernel(x_prompt, x_sample, mem_prompt, mem_sample, g_mix_pre, g_mix_post, w_in, hy_conv, hy_fw1, hy_fb1,
           hy_fw2, hy_fb2, hy_fw3, hy_bias, ret_decay, att_qnorm, att_knorm, w_branch, w_out, g_x_pre, g_x_post,
           g_mem, w_xq, w_xkv, w_xo, g_ff_pre, g_ff_post, w_ff1, w_ff2):
    w = dict(
        g_mix_pre=g_mix_pre, g_mix_post=g_mix_post, w_in=w_in.astype(BF16), hy_conv=hy_conv,
        hy_fw1=hy_fw1, hy_fb1=hy_fb1, hy_fw2=hy_fw2, hy_fb2=hy_fb2, hy_fw3=hy_fw3, hy_bias=hy_bias,
        ret_lg=jax.nn.log_sigmoid(ret_decay.astype(F32)), att_qnorm=att_qnorm, att_knorm=att_knorm,
        w_branch=w_branch.astype(BF16), w_out=w_out.astype(BF16), g_x_pre=g_x_pre, g_x_post=g_x_post,
        g_mem=g_mem, w_xq=w_xq.astype(BF16), w_xkv=w_xkv.astype(BF16), w_xo=w_xo.astype(BF16),
        g_ff_pre=g_ff_pre, g_ff_post=g_ff_post, w_ff1=w_ff1.astype(BF16), w_ff2=w_ff2.astype(BF16),
    )
    y_prompt = _trunk(x_prompt, mem_prompt, w, _position_tables(x_prompt.shape[1]))
    y_sample = _trunk(x_sample, mem_sample, w, _position_tables(x_sample.shape[1]))
    return (y_prompt, y_sample)
```

```python
import functools
import math

import jax
import jax.numpy as jnp
from jax import lax
from jax.experimental import pallas as pl
from jax.experimental.pallas import tpu as pltpu

F32 = jnp.float32
BF16 = jnp.bfloat16

D_MODEL = 1024
DEPTH = 4
GRID_W = 64
EPS = 1e-6
HY_WIDTH = 512
HY_SHORT = 3
HY_EMB = 33
HY_FH = 64
HY_SIN_FREQ = 1.0
HY_TARGET = 1e-2
HY_FAST = 0.3
HY_SLOW = 1.5
HY_SHIFT = 0.0
RET_HEADS = 4
RET_DK = 128
RET_DV = 128
RET_WIDTH = RET_HEADS * RET_DV
RET_THETA = 10000.0
ATT_HEADS = 4
ATT_KV_HEADS = 2
ATT_HD = 128
ATT_WIDTH = ATT_HEADS * ATT_HD
ROPE_THETA = 10000.0
X_HEADS = 4
X_HD = D_MODEL // X_HEADS
D_FF = 4 * D_MODEL
N_BRANCH = 3
MIX_WIDTH = HY_WIDTH + RET_WIDTH + ATT_WIDTH

COL_HY = 0
COL_X0C = 0
COL_ZF = HY_WIDTH
COL_RQ = 3 * HY_WIDTH
COL_RK = COL_RQ + RET_HEADS * RET_DK
COL_RV = COL_RK + RET_HEADS * RET_DK
COL_RG = COL_RV + RET_WIDTH
COL_AQ = COL_RG + RET_WIDTH
COL_AK = COL_AQ + ATT_WIDTH
COL_AV = COL_AK + ATT_KV_HEADS * ATT_HD
COL_GATE = COL_AV + ATT_KV_HEADS * ATT_HD
IN_COLS = COL_GATE + N_BRANCH * D_MODEL

LANES = 128
HALO = 8
VMEM_LIMIT = 56 * 1024 * 1024
DFT_N2 = 128
DFT_NB = 16
DFT_KB = 16
DFT_UNROLL = 8
RET_BLOCK = 256
RET_UNROLL = 16
ATT_KEY_CHUNK = 1024
ATT_QUERY_TILE = 1024
FF_CHUNK = 1024
TOKEN_TILE = 512
LOG2E = 1.4426950408889634


def _params(sem):
    return pltpu.CompilerParams(dimension_semantics=sem, vmem_limit_bytes=VMEM_LIMIT)


def _dot(a, b):
    return jnp.dot(a, b, preferred_element_type=F32)


def _dot_nt(a, b):
    return lax.dot_general(a, b, (((1,), (1,)), ((), ())), preferred_element_type=F32)


def _dot_tn(a, b):
    return lax.dot_general(a, b, (((0,), (0,)), ((), ())), preferred_element_type=F32)


def _rms(x, g):
    ms = jnp.mean(x * x, axis=-1, keepdims=True)
    return x * lax.rsqrt(ms + EPS) * g


def _sigmoid(x):
    return 0.5 * jnp.tanh(0.5 * x) + 0.5


def _split(x):
    hi = x.astype(BF16)
    lo = (x - hi.astype(F32)).astype(BF16)
    return hi, lo


def _dot3(a, b):
    return _dot(a[0], b[0]) + (_dot(a[0], b[1]) + _dot(a[1], b[0]))


def _bdot(a, b):
    return _dot(a.astype(BF16), b.astype(BF16))


def _nm_body(x_ref, g_ref, w_ref, o_ref, *, tn):
    h = _rms(x_ref[...], g_ref[...]).astype(BF16)
    for j in range(w_ref.shape[1] // tn):
        o_ref[:, j * tn:(j + 1) * tn] = _dot(h, w_ref[:, j * tn:(j + 1) * tn]).astype(o_ref.dtype)


def _norm_matmul(x, g, w, *, tm, tn):
    T, D = x.shape
    NL, _, N = w.shape
    return pl.pallas_call(
        functools.partial(_nm_body, tn=tn),
        grid=(NL, T // tm),
        in_specs=[pl.BlockSpec((tm, D), lambda l, i: (i, 0)),
                  pl.BlockSpec((None, 1, D), lambda l, i: (l, 0, 0)),
                  pl.BlockSpec((None, D, N), lambda l, i: (l, 0, 0))],
        out_specs=pl.BlockSpec((None, tm, N), lambda l, i: (l, i, 0)),
        out_shape=jax.ShapeDtypeStruct((NL, T, N), BF16),
        compiler_params=_params(("parallel", "parallel")),
        name="norm_matmul",
    )(x, g.reshape(NL, 1, D), w)


def _rotate_half(x, cos, sin):
    return x * cos + pltpu.roll(x, x.shape[-1] // 2, 1) * sin


def _axial_rope(x, gain, cos, sin, scale):
    x = _rms(x, gain)
    lane = lax.broadcasted_iota(jnp.int32, x.shape, 1)
    lower = (lane % (ATT_HD // 2)) < (ATT_HD // 4)
    swapped = jnp.where(lower, pltpu.roll(x, LANES - ATT_HD // 4, 1), pltpu.roll(x, ATT_HD // 4, 1))
    return (x * cos + swapped * sin) * scale


_WIN_SEGMENTS = (
    [(COL_RQ, RET_WIDTH, "ret_q"), (COL_RK, RET_WIDTH, "ret_k"), (COL_RV, RET_WIDTH, None),
       (COL_RG, RET_WIDTH, None), (COL_AQ, ATT_WIDTH, "att_q"), (COL_AK, COL_GATE - COL_AK, "att_kv")]
    + [(COL_GATE + i * D_MODEL, D_MODEL, None) for i in range(N_BRANCH)])


def _win_body(x_ref, xp_ref, xn_ref, g_ref, w_ref, cw_ref, qn_ref, kn_ref, rc_ref, rs_ref, ac_ref, as_ref, o_ref,
              *, per_seq):
    tm = x_ref.shape[0]
    pos = pl.program_id(0) % per_seq
    x_ext = jnp.concatenate([xp_ref[...], x_ref[...], xn_ref[...]], axis=0)
    r_ext = lax.broadcasted_iota(jnp.int32, (tm + 2 * HALO, 1), 0)
    pad = ((r_ext < HALO) & (pos == 0)) | ((r_ext >= tm + HALO) & (pos == per_seq - 1))
    hf = jnp.where(pad, 0.0, _rms(x_ext, g_ref[...]))
    h_ext = hf.astype(BF16)
    h = hf[HALO:HALO + tm].astype(BF16)
    row = lax.broadcasted_iota(jnp.int32, (tm, HY_WIDTH), 0)
    conv = []
    for grp in range(3):
        cols = slice(COL_HY + grp * HY_WIDTH, COL_HY + (grp + 1) * HY_WIDTH)
        y_ext = _dot(h_ext, w_ref[:, cols])
        y = y_ext[HALO:HALO + tm]
        y_prev = y_ext[HALO - 1:HALO]
        y_next = y_ext[HALO + tm:HALO + tm + 1]
        prev = jnp.where(row == 0, y_prev, pltpu.roll(y, 1, 0))
        nxt = jnp.where(row == tm - 1, y_next, pltpu.roll(y, tm - 1, 0))
        cw = cw_ref[:, cols]
        conv.append(prev * cw[0:1] + y * cw[1:2] + nxt * cw[2:3])
    x0c, x1c, vc = conv
    o_ref[:, COL_X0C:COL_X0C + HY_WIDTH] = x0c.astype(o_ref.dtype)
    o_ref[:, COL_ZF:COL_ZF + HY_WIDTH] = (vc * x1c).astype(o_ref.dtype)
    o_ref[:, COL_ZF + HY_WIDTH:COL_RQ] = vc.astype(o_ref.dtype)
    for start, width, kind in _WIN_SEGMENTS:
        y = _dot(h, w_ref[:, start:start + width])
        if kind is None:
            o_ref[:, start:start + width] = y.astype(o_ref.dtype)
            continue
        for hh in range(width // LANES):
            yh = y[:, hh * LANES:(hh + 1) * LANES]
            if kind == "ret_q":
                yh = _rotate_half(yh, rc_ref[...], rs_ref[...])
            elif kind == "ret_k":
                yh = _rotate_half(yh, rc_ref[...], rs_ref[...]) * (RET_DK ** -0.5)
            elif kind == "att_q":
                yh = _axial_rope(yh, qn_ref[...], ac_ref[...], as_ref[...], ATT_HD ** -0.5 * LOG2E)
            elif hh < ATT_KV_HEADS:
                yh = _axial_rope(yh, kn_ref[...], ac_ref[...], as_ref[...], 1.0)
            o_ref[:, start + hh * LANES:start + (hh + 1) * LANES] = yh.astype(o_ref.dtype)


def _input_projection(x, g, w_stack, l, conv_w, qn, kn, pt, L, *, tm):
    T, D = x.shape
    N = w_stack.shape[2]
    per_seq = L // tm
    halo_per_tile = tm // HALO
    vec = lambda n: pl.BlockSpec((1, n), lambda i: (0, 0))
    tab = pl.BlockSpec((tm, LANES), lambda i: (i % per_seq, 0))
    return pl.pallas_call(
        functools.partial(_win_body, per_seq=per_seq),
        grid=(T // tm,),
        in_specs=[pl.BlockSpec((tm, D), lambda i: (i, 0)),
                  pl.BlockSpec((HALO, D), lambda i: (jnp.maximum(i * halo_per_tile - 1, 0), 0)),
                  pl.BlockSpec((HALO, D), lambda i: (jnp.minimum((i + 1) * halo_per_tile, T // HALO - 1), 0)),
                  vec(D),
                  pl.BlockSpec((None, D, N), lambda i: (l, 0, 0), pipeline_mode=pl.Buffered(1)),
                  pl.BlockSpec((HY_SHORT, 3 * HY_WIDTH), lambda i: (0, 0)),
                  vec(ATT_HD), vec(ATT_HD), tab, tab, tab, tab],
        out_specs=pl.BlockSpec((tm, N), lambda i: (i, 0)),
        out_shape=jax.ShapeDtypeStruct((T, N), BF16),
        compiler_params=_params(("parallel",)),
        name="input_projection",
    )(x, x, x, g.reshape(1, D), w_stack, conv_w, qn.reshape(1, ATT_HD), kn.reshape(1, ATT_HD),
      pt["ret_cos"], pt["ret_sin"], pt["att_cos"], pt["att_sin"])


def _filt_body(f_ref, w1a_ref, w1b_ref, b1_ref, w2_ref, b2_ref, w3a_ref, w3b_ref, dl_ref, o_ref, *, L):
    tl = f_ref.shape[0]
    f = f_ref[...]
    fa, fb = _split(f[:tl // 2]), _split(f[tl // 2:])
    pre = _dot3(fa, _split(w1a_ref[...])) + _dot3(fb, _split(w1b_ref[...]))
    h = jnp.sin(HY_SIN_FREQ * (pre + b1_ref[...]))
    h = jnp.sin(HY_SIN_FREQ * (_bdot(h, w2_ref[...]) + b2_ref[...]))
    filt = jnp.concatenate([_bdot(h, w3a_ref[...]), _bdot(h, w3b_ref[...])], axis=0)
    mod = jnp.exp(-f[:, 0:1] * dl_ref[...]) + HY_SHIFT
    row = pl.program_id(1) * tl + lax.broadcasted_iota(jnp.int32, (tl, HY_WIDTH), 0)
    sel = jnp.where(row < L, filt[:, :HY_WIDTH], filt[:, HY_WIDTH:])
    o_ref[...] = jnp.where(row == L, 0.0, sel * mod)


def _hyena_features(L):
    t = jnp.linspace(0.0, 1.0, L, dtype=F32)[:, None]
    bands = (HY_EMB - 1) // 2
    w = 2.0 * math.pi * jnp.arange(L, dtype=F32)[:, None] / L
    f = jnp.linspace(1e-4, bands - 1, bands, dtype=F32)[None, :]
    z = jnp.concatenate([t, jnp.cos(f * w), -jnp.sin(f * w)], axis=-1)
    zc = jnp.concatenate([z, z[:1], jnp.flip(z[1:], axis=0)], axis=0)
    return jnp.pad(zc, ((0, 0), (0, LANES - HY_EMB)))


def _hyena_filter(feats, w1, b1, w2, b2, w3, L):
    tl = TOKEN_TILE
    NL = w1.shape[0]
    assert 2 * HY_FH == LANES
    zw = jnp.zeros_like(w2)
    w1a = jnp.pad(w1, ((0, 0), (0, LANES - HY_EMB), (0, HY_FH)))
    w1b = jnp.pad(w1, ((0, 0), (0, LANES - HY_EMB), (HY_FH, 0)))
    b1p = jnp.concatenate([b1, b1], axis=1).reshape(NL, 1, LANES)
    w2p = jnp.concatenate([jnp.concatenate([w2, zw], axis=2), jnp.concatenate([zw, w2], axis=2)], axis=1)
    b2p = jnp.concatenate([b2, b2], axis=1).reshape(NL, 1, LANES)
    w3a = jnp.pad(w3, ((0, 0), (0, HY_FH), (0, 0)))
    w3b = jnp.pad(w3, ((0, 0), (HY_FH, 0), (0, 0)))
    min_decay = math.log(HY_TARGET) / HY_SLOW
    max_decay = math.log(HY_TARGET) / HY_FAST
    deltas = jnp.abs(jnp.linspace(min_decay, max_decay, HY_WIDTH, dtype=F32)).reshape(1, HY_WIDTH)
    layer = lambda r, c: pl.BlockSpec((None, r, c), lambda l, i: (l, 0, 0))
    return pl.pallas_call(
        functools.partial(_filt_body, L=L),
        grid=(NL, 2 * L // tl),
        in_specs=[pl.BlockSpec((tl, LANES), lambda l, i: (i, 0)),
                  layer(LANES, LANES), layer(LANES, LANES), layer(1, LANES), layer(LANES, LANES), layer(1, LANES),
                  layer(LANES, 2 * HY_WIDTH), layer(LANES, 2 * HY_WIDTH),
                  pl.BlockSpec((1, HY_WIDTH), lambda l, i: (0, 0))],
        out_specs=pl.BlockSpec((None, tl, HY_WIDTH), lambda l, i: (l, i, 0)),
        out_shape=jax.ShapeDtypeStruct((NL, 2 * L, HY_WIDTH), F32),
        compiler_params=_params(("parallel", "parallel")),
        name="hyena_filter",
    )(feats, w1a, w1b, b1p, w2p, b2p, w3a, w3b, deltas)


def _expanded(fn, rows, cols):
    small = fn(lax.broadcasted_iota(jnp.int32, (rows, cols), 0), lax.broadcasted_iota(jnp.int32, (rows, cols), 1))
    hp = functools.partial(jnp.dot, precision=lax.Precision.HIGHEST)
    pick_r = (lax.broadcasted_iota(jnp.int32, (rows * DFT_NB, rows), 0) // DFT_NB
              == lax.broadcasted_iota(jnp.int32, (rows * DFT_NB, rows), 1)).astype(F32)
    pick_c = (lax.broadcasted_iota(jnp.int32, (cols, cols * DFT_NB), 1) // DFT_NB
              == lax.broadcasted_iota(jnp.int32, (cols, cols * DFT_NB), 0)).astype(F32)
    ri = lax.broadcasted_iota(jnp.int32, (rows * DFT_NB, cols * DFT_NB), 0)
    ci = lax.broadcasted_iota(jnp.int32, (rows * DFT_NB, cols * DFT_NB), 1)
    return jnp.where(ri % DFT_NB == ci % DFT_NB, hp(hp(pick_r, small), pick_c), 0.0).astype(BF16)


def _dft_tables(L):
    N = 2 * L
    N2 = DFT_N2
    N1 = N // N2
    N1h = N1 // 2

    def cs1(k1, n1):
        ang = ((k1 * n1) % N1).astype(F32) * (2.0 * math.pi / N1)
        return jnp.cos(ang), jnp.sin(ang)

    def f1_pair(r, c):
        cs, sn = cs1(r // 2, c % N1h)
        re_row, first = (r % 2) == 0, c < N1h
        return jnp.where(re_row, jnp.where(first, cs, sn), jnp.where(first, -sn, cs))

    def f1_real(r, c):
        cs, sn = cs1(r // 2, c)
        return jnp.where((r % 2) == 0, cs, -sn)

    def f3_pair(r, c):
        cs, sn = cs1(c // 2, r % N1h)
        re_out, re_in = r < N1h, (c % 2) == 0
        return jnp.where(re_out, jnp.where(re_in, cs, -sn), jnp.where(re_in, sn, cs)) * (1.0 / N)

    kk = (jnp.arange(N1, dtype=jnp.int32)[:, None, None]
          + N1 * jnp.arange(N2, dtype=jnp.int32)[None, :, None])
    n2 = jnp.arange(N2, dtype=jnp.int32)[None, None, :]
    a2 = ((kk * n2) % N).astype(F32) * (2.0 * math.pi / N)
    gr, gi = jnp.cos(a2), -jnp.sin(a2)
    g_fwd = jnp.concatenate([jnp.concatenate([gr, -gi], axis=2),
                             jnp.concatenate([gi, gr], axis=2)], axis=1)
    return dict(N1=N1, N1h=N1h, N2=N2,
                f1_pair=_expanded(f1_pair, 2 * N1, 2 * N1h), f1_real=_expanded(f1_real, 2 * N1, N1),
                f3_pair=_expanded(f3_pair, 2 * N1h, 2 * N1), g_fwd=g_fwd.astype(BF16))


def _dft1_body(x_ref, f_ref, o_ref):
    P, K, NB, C = x_ref.shape
    y = _bdot(f_ref[...], x_ref[...].reshape(P * K * NB, C))
    o_ref[0] = y.astype(o_ref.dtype).reshape(o_ref.shape[1:])


def _dft_stage1(x, f, *, P, col=0):
    Bx, K, N2, _ = x.shape
    C = HY_WIDTH
    N1 = f.shape[0] // (2 * DFT_NB)
    return pl.pallas_call(
        _dft1_body,
        grid=(Bx // P, N2 // DFT_NB),
        in_specs=[pl.BlockSpec((P, K, DFT_NB, C), lambda b, j: (b, 0, j, col)),
                  pl.BlockSpec(f.shape, lambda b, j: (0, 0), pipeline_mode=pl.Buffered(1))],
        out_specs=pl.BlockSpec((1, N1, 2, DFT_NB, C), lambda b, j: (b, 0, 0, j, 0)),
        out_shape=jax.ShapeDtypeStruct((Bx // P, N1, 2, N2, C), BF16),
        compiler_params=_params(("parallel", "parallel")),
        name="dft_stage1",
    )(x, f)


def _spec_body(a_ref, g_ref, o_ref):
    def step(k, carry):
        o_ref[k] = _bdot(g_ref[k], a_ref[0, k])
        return carry

    lax.fori_loop(0, a_ref.shape[1], step, 0, unroll=DFT_UNROLL)


def _dft_spectrum(a, g, *, kb):
    NL, N1, R, C = a.shape
    return pl.pallas_call(
        _spec_body,
        grid=(N1 // kb, NL),
        in_specs=[pl.BlockSpec((1, kb, R, C), lambda i, l: (l, i, 0, 0)),
                  pl.BlockSpec((kb, R, R), lambda i, l: (i, 0, 0))],
        out_specs=pl.BlockSpec((None, kb, R, C), lambda i, l: (l, i, 0, 0)),
        out_shape=jax.ShapeDtypeStruct((NL, N1, R, C), F32),
        compiler_params=_params(("parallel", "parallel")),
        name="dft_spectrum",
    )(a, g)


def _dft2_body(a_ref, g_ref, ks_ref, o_ref):
    H = a_ref.shape[2] // 2

    def step(k, carry):
        y = _bdot(g_ref[k], a_ref[0, k])
        yr, yi = y[:H], y[H:]
        kr, ki = ks_ref[k, :H, :], ks_ref[k, H:, :]
        z = jnp.concatenate([yr * kr - yi * ki, yr * ki + yi * kr], axis=0)
        o_ref[0, k] = _dot_tn(g_ref[k], z.astype(g_ref.dtype)).astype(o_ref.dtype)
        return carry

    lax.fori_loop(0, a_ref.shape[1], step, 0, unroll=DFT_UNROLL)


def _dft_stage2(a, g, ks, l, *, kb):
    Bp, N1, R, C = a.shape
    dat = pl.BlockSpec((1, kb, R, C), lambda i, b: (b, i, 0, 0))
    return pl.pallas_call(
        _dft2_body,
        grid=(N1 // kb, Bp),
        in_specs=[dat, pl.BlockSpec((kb, R, R), lambda i, b: (i, 0, 0)),
                  pl.BlockSpec((None, kb, R, C), lambda i, b: (l, i, 0, 0))],
        out_specs=dat,
        out_shape=jax.ShapeDtypeStruct((Bp, N1, R, C), BF16),
        compiler_params=_params(("parallel", "parallel")),
        name="dft_stage2",
    )(a, g, ks)


def _dft3_body(t_ref, f_ref, zf_ref, x0_ref, bias_ref, o_ref):
    _, N1, _, NB, C = t_ref.shape
    y = _bdot(f_ref[...], t_ref[0].reshape(N1 * 2 * NB, C)).reshape(zf_ref.shape)
    o_ref[...] = ((y + zf_ref[...].astype(F32) * bias_ref[...]) * x0_ref[...].astype(F32)).astype(o_ref.dtype)


def _dft_stage3(t, f, proj4, bias):
    Bp, N1, _, N2, C = t.shape
    B, N1h = proj4.shape[:2]
    pair = lambda col=0: pl.BlockSpec((2, N1h, DFT_NB, C), lambda b, j: (b, 0, j, col))
    return pl.pallas_call(
        _dft3_body,
        grid=(Bp, N2 // DFT_NB),
        in_specs=[pl.BlockSpec((1, N1, 2, DFT_NB, C), lambda b, j: (b, 0, 0, j, 0)),
                  pl.BlockSpec(f.shape, lambda b, j: (0, 0), pipeline_mode=pl.Buffered(1)),
                  pair(COL_ZF // C), pair(COL_X0C // C),
                  pl.BlockSpec((1, C), lambda b, j: (0, 0))],
        out_specs=pair(),
        out_shape=jax.ShapeDtypeStruct((B, N1h, N2, C), BF16),
        compiler_params=_params(("parallel", "parallel")),
        name="dft_stage3",
    )(t, f, proj4, proj4, bias)


def _hyena_mixer(proj, kspec, l, bias, tabs):
    B, L, _ = proj.shape
    N1, N1h, N2 = tabs["N1"], tabs["N1h"], tabs["N2"]
    C = HY_WIDTH
    proj4 = proj.reshape(B, N1h, N2, IN_COLS)
    a = _dft_stage1(proj4, tabs["f1_pair"], P=2, col=COL_ZF // C)
    t = _dft_stage2(a.reshape(B // 2, N1, 2 * N2, C), tabs["g_fwd"], kspec, l, kb=DFT_KB)
    y = _dft_stage3(t.reshape(B // 2, N1, 2, N2, C), tabs["f3_pair"], proj4, bias.astype(F32).reshape(1, C))
    return y.reshape(B * L, C)


def _hyena_spectra(feats, w1, b1, w2, b2, w3, L, tabs):
    N1, N2 = tabs["N1"], tabs["N2"]
    C = HY_WIDTH
    NL = w1.shape[0]
    kfull = _hyena_filter(feats, w1, b1, w2, b2, w3, L)
    a = _dft_stage1(kfull.reshape(NL, N1, N2, C), tabs["f1_real"], P=1)
    return _dft_spectrum(a.reshape(NL, N1, 2 * N2, C), tabs["g_fwd"], kb=DFT_KB)


def _ret_body(lg_ref, q_ref, k_ref, v_ref, g_ref, o_ref, st_ref):
    C = RET_BLOCK
    L = q_ref.shape[1]
    nblk = L // C
    dk = RET_DK
    HG = q_ref.shape[2] // dk
    ii = lax.broadcasted_iota(jnp.int32, (C, C), 0)
    jj = lax.broadcasted_iota(jnp.int32, (C, C), 1)
    diff = (ii - jj).astype(F32)
    pos = lax.broadcasted_iota(jnp.int32, (C, dk), 0).astype(F32)
    dm, qw_f, kw_f, qw_b, kw_b, cd_f, cd_b = [], [], [], [], [], [], []
    for hg in range(HG):
        h = pl.program_id(1) * HG + hg
        lgf = lg_ref[0, h]
        lgb = lg_ref[1, h]
        dm.append(jnp.where(diff >= 0, jnp.exp(jnp.maximum(diff, 0.0) * lgf), jnp.exp(jnp.maximum(-diff, 0.0) * lgb)))
        qw_f.append(jnp.exp((pos + 1.0) * lgf))
        kw_f.append(jnp.exp((C - 1.0 - pos) * lgf))
        qw_b.append(jnp.exp((C - pos) * lgb))
        kw_b.append(jnp.exp(pos * lgb))
        cd_f.append(jnp.exp(C * lgf))
        cd_b.append(jnp.exp(C * lgb))

    def block_kv(n, carry):
        sl = pl.ds(pl.multiple_of(n * C, C), C)
        for hg in range(HG):
            cols = slice(hg * dk, (hg + 1) * dk)
            k = k_ref[0, sl, cols].astype(F32)
            kk = jnp.concatenate([k * kw_f[hg], k * kw_b[hg]], axis=1).astype(BF16)
            st_ref[hg, n] = _dot_tn(kk, v_ref[0, sl, cols].astype(BF16))
        return carry

    lax.fori_loop(0, nblk, block_kv, 0, unroll=RET_UNROLL)

    for hg in range(HG):
        def scan_f(n, S, hg=hg):
            kv = st_ref[hg, n, :dk, :]
            st_ref[hg, n, :dk, :] = S
            return S * cd_f[hg] + kv

        lax.fori_loop(0, nblk, scan_f, jnp.zeros((dk, RET_DV), F32))

        def scan_b(m, S, hg=hg):
            n = nblk - 1 - m
            kv = st_ref[hg, n, dk:, :]
            st_ref[hg, n, dk:, :] = S
            return S * cd_b[hg] + kv

        lax.fori_loop(0, nblk, scan_b, jnp.zeros((dk, RET_DV), F32))

    def block_out(n, carry):
        sl = pl.ds(pl.multiple_of(n * C, C), C)
        for hg in range(HG):
            cols = slice(hg * dk, (hg + 1) * dk)
            q = q_ref[0, sl, cols].astype(BF16)
            scores = _dot_nt(q, k_ref[0, sl, cols].astype(BF16)) * dm[hg]
            qf = q.astype(F32)
            qq = jnp.concatenate([qf * qw_f[hg], qf * qw_b[hg]], axis=1).astype(BF16)
            o = _dot(scores.astype(BF16), v_ref[0, sl, cols].astype(BF16)) + _dot(qq, st_ref[hg, n].astype(BF16))
            mu = jnp.mean(o, axis=-1, keepdims=True)
            oc = o - mu
            on = oc * lax.rsqrt(jnp.mean(oc * oc, axis=-1, keepdims=True) + EPS)
            g = g_ref[0, sl, cols].astype(F32)
            o_ref[0, sl, cols] = (g * _sigmoid(g) * on).astype(o_ref.dtype)
        return carry

    lax.fori_loop(0, nblk, block_out, 0, unroll=RET_UNROLL)


def _retention_mixer(proj, log_gamma):
    B, L, _ = proj.shape
    nblk = L // RET_BLOCK
    HG = min(RET_HEADS, max(1, RET_UNROLL // nblk))
    width = HG * RET_DK
    blk = lambda col: pl.BlockSpec((1, L, width), lambda b, h, c=col // width: (b, 0, c + h))
    y = pl.pallas_call(
        _ret_body,
        grid=(B, RET_HEADS // HG),
        in_specs=[pl.BlockSpec(memory_space=pltpu.SMEM),
                  blk(COL_RQ), blk(COL_RK), blk(COL_RV), blk(COL_RG)],
        out_specs=pl.BlockSpec((1, L, width), lambda b, h: (b, 0, h)),
        out_shape=jax.ShapeDtypeStruct((B, L, RET_WIDTH), BF16),
        scratch_shapes=[pltpu.VMEM((HG, nblk, 2 * RET_DK, RET_DV), F32)],
        compiler_params=_params(("parallel", "parallel")),
        name="retention",
    )(log_gamma, proj, proj, proj, proj)
    return y.reshape(B * L, RET_WIDTH)


def _att_body(q_ref, k_ref, v_ref, o_ref):
    G = ATT_HEADS // ATT_KV_HEADS
    tq = q_ref.shape[1]
    kc = min(ATT_KEY_CHUNK, k_ref.shape[1])
    nchunk = k_ref.shape[1] // kc
    q = q_ref[0].astype(BF16)
    qs = jnp.concatenate([q[:, g * ATT_HD:(g + 1) * ATT_HD] for g in range(G)], axis=0)
    m = jnp.full((G * tq, 1), -jnp.inf, F32)
    l = jnp.zeros((G * tq, 1), F32)
    acc = jnp.zeros((G * tq, ATT_HD), F32)
    s_next = _dot_nt(qs, k_ref[0, 0:kc, :].astype(BF16))
    for c in range(nchunk):
        s = s_next
        if c + 1 < nchunk:
            s_next = _dot_nt(qs, k_ref[0, (c + 1) * kc:(c + 2) * kc, :].astype(BF16))
        m_new = jnp.maximum(m, jnp.max(s, axis=-1, keepdims=True))
        alpha = jnp.exp2(m - m_new)
        p = jnp.exp2(s - m_new)
        l = alpha * l + jnp.sum(p, axis=-1, keepdims=True)
        acc = alpha * acc + _dot(p.astype(BF16), v_ref[0, c * kc:(c + 1) * kc, :].astype(BF16))
        m = m_new
    o = acc * (1.0 / l)
    for g in range(G):
        o_ref[0, :, g * ATT_HD:(g + 1) * ATT_HD] = o[g * tq:(g + 1) * tq].astype(o_ref.dtype)


def _attention_mixer(proj, *, tq):
    B, L, _ = proj.shape
    G = ATT_HEADS // ATT_KV_HEADS
    tq = min(tq, L)
    y = pl.pallas_call(
        _att_body,
        grid=(B, ATT_KV_HEADS, L // tq),
        in_specs=[pl.BlockSpec((1, tq, G * ATT_HD), lambda b, h, i: (b, i, COL_AQ // (G * ATT_HD) + h)),
                  pl.BlockSpec((1, L, ATT_HD), lambda b, h, i: (b, 0, COL_AK // ATT_HD + h)),
                  pl.BlockSpec((1, L, ATT_HD), lambda b, h, i: (b, 0, COL_AV // ATT_HD + h))],
        out_specs=pl.BlockSpec((1, tq, G * ATT_HD), lambda b, h, i: (b, i, h)),
        out_shape=jax.ShapeDtypeStruct((B, L, ATT_WIDTH), BF16),
        compiler_params=_params(("parallel", "parallel", "arbitrary")),
        name="attention",
    )(proj, proj, proj)
    return y.reshape(B * L, ATT_WIDTH)


def _tail_body(yh_ref, yr_ref, ya_ref, ga_ref, gb_ref, x_ref, kv_ref, wb_ref, wo_ref, gmix_ref,
               gxpre_ref, wq_ref, wxo_ref, gxpost_ref, gfpre_ref, w1_ref, w2_ref, gfpost_ref, o_ref, u_ref):
    D = D_MODEL
    ph = _dot(yh_ref[0], wb_ref[0:HY_WIDTH, :])
    pr = _dot(yr_ref[0], wb_ref[HY_WIDTH:HY_WIDTH + RET_WIDTH, :])
    pa = _dot(ya_ref[0], wb_ref[HY_WIDTH + RET_WIDTH:, :])
    ga = ga_ref[0].astype(F32)
    gb = gb_ref[0].astype(F32)
    hd = D // 2
    g0 = _sigmoid(ga[:, :D])
    g1 = _sigmoid(jnp.concatenate([ga[:, D:], gb[:, :hd]], axis=1))
    g2 = _sigmoid(gb[:, hd:])
    merged = g0 * ph + g1 * pr + g2 * pa
    x = x_ref[0] + _rms(_dot(merged.astype(BF16), wo_ref[...]), gmix_ref[...])
    h = _rms(x, gxpre_ref[...]).astype(BF16)
    q = (_dot(h, wq_ref[...]) * (X_HD ** -0.5)).astype(BF16)
    kv = kv_ref[0]
    outs = []
    for hh in range(X_HEADS):
        sl = slice(hh * X_HD, (hh + 1) * X_HD)
        s = _dot_nt(q[:, sl], kv[:, sl])
        p = jnp.exp(s - jnp.max(s, axis=-1, keepdims=True))
        l = jnp.sum(p, axis=-1, keepdims=True)
        vh = kv[:, D + hh * X_HD:D + (hh + 1) * X_HD]
        outs.append(_dot(p.astype(BF16), vh) * (1.0 / l))
    o = jnp.concatenate(outs, axis=1).astype(BF16)
    x = x + _rms(_dot(o, wxo_ref[...]), gxpost_ref[...])
    h = _rms(x, gfpre_ref[...]).astype(BF16)
    for c in range(D_FF // FF_CHUNK):
        sl = slice(c * FF_CHUNK, (c + 1) * FF_CHUNK)
        u_ref[:, sl] = jnp.square(jnp.maximum(_dot(h, w1_ref[:, sl]), 0.0)).astype(BF16)
    o_ref[0] = x + _rms(_dot(u_ref[...], w2_ref[...]), gfpost_ref[...])


def _layer_tail(yh, yr, ya, proj, x, kv, w, l, *, tm):
    B, L, D = x.shape
    M = kv.shape[2]
    gw = N_BRANCH * D // 2
    tok = lambda width, col=0: pl.BlockSpec((1, tm, width), lambda b, i, col=col: (b, i, col))
    vec = pl.BlockSpec((1, D), lambda b, i: (0, 0))
    res = lambda shape: pl.BlockSpec((None,) + shape, lambda b, i: (l, 0, 0), pipeline_mode=pl.Buffered(1))
    g = lambda name: w[name][l].reshape(1, D)
    return pl.pallas_call(
        _tail_body,
        grid=(B, L // tm),
        in_specs=[tok(HY_WIDTH), tok(RET_WIDTH), tok(ATT_WIDTH), tok(gw, COL_GATE // gw), tok(gw, COL_GATE // gw + 1),
                  tok(D), pl.BlockSpec((None, 1, M, 2 * D), lambda b, i: (l, b, 0, 0)),
                  res((MIX_WIDTH, D)), res((D, D)), vec,
                  vec, res((D, D)), res((D, D)), vec,
                  vec, res((D, D_FF)), res((D_FF, D)), vec],
        out_specs=tok(D),
        out_shape=jax.ShapeDtypeStruct((B, L, D), F32),
        scratch_shapes=[pltpu.VMEM((tm, D_FF), BF16)],
        compiler_params=_params(("parallel", "parallel")),
        name="layer_tail",
    )(yh, yr, ya, proj, proj, x, kv, w["w_branch"], w["w_out"], g("g_mix_post"),
      g("g_x_pre"), w["w_xq"], w["w_xo"], g("g_x_post"),
      g("g_ff_pre"), w["w_ff1"], w["w_ff2"], g("g_ff_post"))


def _rotary_tables(pos, dim, theta):
    inv = theta ** (-jnp.arange(0, dim, 2, dtype=F32) / dim)
    ang = pos.astype(F32)[:, None] * inv[None, :]
    return jnp.cos(ang), jnp.sin(ang)


def _position_tables(L):
    pos = jnp.arange(L)
    rows = L // GRID_W
    row_ids = jnp.repeat(jnp.arange(rows), GRID_W)
    col_ids = jnp.tile(jnp.arange(GRID_W), rows)
    rcos, rsin = _rotary_tables(pos, RET_DK, RET_THETA)
    rc, rs = _rotary_tables(row_ids, ATT_HD // 2, ROPE_THETA)
    cc, cs = _rotary_tables(col_ids, ATT_HD // 2, ROPE_THETA)
    return dict(
        ret_cos=jnp.concatenate([rcos, rcos], axis=-1),
        ret_sin=jnp.concatenate([-rsin, rsin], axis=-1),
        att_cos=jnp.concatenate([rc, rc, cc, cc], axis=-1),
        att_sin=jnp.concatenate([-rs, rs, -cs, cs], axis=-1),
        feats=_hyena_features(L),
        dft=_dft_tables(L),
    )


def _trunk(x, mem, w, pt):
    B, L, D = x.shape
    M = mem.shape[1]
    kspec = _hyena_spectra(pt["feats"], w["hy_fw1"], w["hy_fb1"], w["hy_fw2"], w["hy_fb2"], w["hy_fw3"], L, pt["dft"])
    kv = _norm_matmul(mem.reshape(B * M, D), w["g_mem"], w["w_xkv"], tm=TOKEN_TILE, tn=FF_CHUNK)
    kv = kv.reshape(-1, B, M, 2 * D)
    for l in range(DEPTH):
        proj = _input_projection(x.reshape(B * L, D), w["g_mix_pre"][l], w["w_in"], l, w["hy_conv"][l],
                                 w["att_qnorm"][l], w["att_knorm"][l], pt, L, tm=TOKEN_TILE).reshape(B, L, IN_COLS)
        y_h = _hyena_mixer(proj, kspec, l, w["hy_bias"][l], pt["dft"])
        y_r = _retention_mixer(proj, w["ret_lg"][l])
        y_a = _attention_mixer(proj, tq=ATT_QUERY_TILE)
        x = _layer_tail(y_h.reshape(B, L, HY_WIDTH), y_r.reshape(B, L, RET_WIDTH), y_a.reshape(B, L, ATT_WIDTH),
                        proj, x, kv, w, l, tm=TOKEN_TILE)
    return x


def kernel(x_prompt, x_sample, mem_prompt, mem_sample, g_mix_pre, g_mix_post, w_in, hy_conv, hy_fw1, hy_fb1,
           hy_fw2, hy_fb2, hy_fw3, hy_bias, ret_decay, att_qnorm, att_knorm, w_branch, w_out, g_x_pre, g_x_post,
           g_mem, w_xq, w_xkv, w_xo, g_ff_pre, g_ff_post, w_ff1, w_ff2):
    w = dict(
        g_mix_pre=g_mix_pre, g_mix_post=g_mix_post, w_in=w_in.astype(BF16), hy_conv=hy_conv,
        hy_fw1=hy_fw1, hy_fb1=hy_fb1, hy_fw2=hy_fw2, hy_fb2=hy_fb2, hy_fw3=hy_fw3, hy_bias=hy_bias,
        ret_lg=jax.nn.log_sigmoid(ret_decay.astype(F32)), att_qnorm=att_qnorm, att_knorm=att_knorm,
        w_branch=w_branch.astype(BF16), w_out=w_out.astype(BF16), g_x_pre=g_x_pre, g_x_post=g_x_post,
        g_mem=g_mem, w_xq=w_xq.astype(BF16), w_xkv=w_xkv.astype(BF16), w_xo=w_xo.astype(BF16),
        g_ff_pre=g_ff_pre, g_ff_post=g_ff_post, w_ff1=w_ff1.astype(BF16), w_ff2=w_ff2.astype(BF16),
    )
    y_prompt = _trunk(x_prompt, mem_prompt, w, _position_tables(x_prompt.shape[1]))
    y_sample = _trunk(x_sample, mem_sample, w, _position_tables(x_sample.shape[1]))
    return (y_prompt, y_sample)
```

```python
import functools
import math

import jax
import jax.numpy as jnp
from jax import lax
from jax.experimental import pallas as pl
from jax.experimental.pallas import tpu as pltpu

F32 = jnp.float32
BF16 = jnp.bfloat16

D_MODEL = 1024
DEPTH = 4
GRID_W = 64
EPS = 1e-6
HY_WIDTH = 512
HY_SHORT = 3
HY_EMB = 33
HY_FH = 64
HY_SIN_FREQ = 1.0
HY_TARGET = 1e-2
HY_FAST = 0.3
HY_SLOW = 1.5
HY_SHIFT = 0.0
RET_HEADS = 4
RET_DK = 128
RET_DV = 128
RET_WIDTH = RET_HEADS * RET_DV
RET_THETA = 10000.0
ATT_HEADS = 4
ATT_KV_HEADS = 2
ATT_HD = 128
ATT_WIDTH = ATT_HEADS * ATT_HD
ROPE_THETA = 10000.0
X_HEADS = 4
X_HD = D_MODEL // X_HEADS
D_FF = 4 * D_MODEL
N_BRANCH = 3
MIX_WIDTH = HY_WIDTH + RET_WIDTH + ATT_WIDTH

COL_HY = 0
COL_X0C = 0
COL_ZF = HY_WIDTH
COL_RQ = 3 * HY_WIDTH
COL_RK = COL_RQ + RET_HEADS * RET_DK
COL_RV = COL_RK + RET_HEADS * RET_DK
COL_RG = COL_RV + RET_WIDTH
COL_AQ = COL_RG + RET_WIDTH
COL_AK = COL_AQ + ATT_WIDTH
COL_AV = COL_AK + ATT_KV_HEADS * ATT_HD
COL_GATE = COL_AV + ATT_KV_HEADS * ATT_HD
IN_COLS = COL_GATE + N_BRANCH * D_MODEL

LANES = 128
HALO = 8
VMEM_LIMIT = 56 * 1024 * 1024
DFT_N2 = 128
DFT_NB = 16
DFT_KB = 16
DFT_UNROLL = 8
RET_BLOCK = 256
RET_UNROLL = 32
ATT_KEY_CHUNK = 1024
ATT_QUERY_TILE = 1024
FF_CHUNK = 1024
TOKEN_TILE = 512
LOG2E = 1.4426950408889634


def _params(sem):
    return pltpu.CompilerParams(dimension_semantics=sem, vmem_limit_bytes=VMEM_LIMIT)


def _dot(a, b):
    return jnp.dot(a, b, preferred_element_type=F32)


def _dot_nt(a, b):
    return lax.dot_general(a, b, (((1,), (1,)), ((), ())), preferred_element_type=F32)


def _dot_tn(a, b):
    return lax.dot_general(a, b, (((0,), (0,)), ((), ())), preferred_element_type=F32)


def _rms(x, g):
    ms = jnp.mean(x * x, axis=-1, keepdims=True)
    return x * lax.rsqrt(ms + EPS) * g


def _sigmoid(x):
    return 0.5 * jnp.tanh(0.5 * x) + 0.5


def _split(x):
    hi = x.astype(BF16)
    lo = (x - hi.astype(F32)).astype(BF16)
    return hi, lo


def _dot3(a, b):
    return _dot(a[0], b[0]) + (_dot(a[0], b[1]) + _dot(a[1], b[0]))


def _bdot(a, b):
    return _dot(a.astype(BF16), b.astype(BF16))


def _nm_body(x_ref, g_ref, w_ref, o_ref, *, tn):
    h = _rms(x_ref[...], g_ref[...]).astype(BF16)
    for j in range(w_ref.shape[1] // tn):
        o_ref[:, j * tn:(j + 1) * tn] = _dot(h, w_ref[:, j * tn:(j + 1) * tn]).astype(o_ref.dtype)


def _norm_matmul(x, g, w, *, tm, tn):
    T, D = x.shape
    NL, _, N = w.shape
    return pl.pallas_call(
        functools.partial(_nm_body, tn=tn),
        grid=(NL, T // tm),
        in_specs=[pl.BlockSpec((tm, D), lambda l, i: (i, 0)),
                  pl.BlockSpec((None, 1, D), lambda l, i: (l, 0, 0)),
                  pl.BlockSpec((None, D, N), lambda l, i: (l, 0, 0))],
        out_specs=pl.BlockSpec((None, tm, N), lambda l, i: (l, i, 0)),
        out_shape=jax.ShapeDtypeStruct((NL, T, N), BF16),
        compiler_params=_params(("parallel", "parallel")),
        name="norm_matmul",
    )(x, g.reshape(NL, 1, D), w)


def _rotate_half(x, cos, sin):
    return x * cos + pltpu.roll(x, x.shape[-1] // 2, 1) * sin


def _axial_rope(x, gain, cos, sin, scale):
    x = _rms(x, gain)
    lane = lax.broadcasted_iota(jnp.int32, x.shape, 1)
    lower = (lane % (ATT_HD // 2)) < (ATT_HD // 4)
    swapped = jnp.where(lower, pltpu.roll(x, LANES - ATT_HD // 4, 1), pltpu.roll(x, ATT_HD // 4, 1))
    return (x * cos + swapped * sin) * scale


_WIN_SEGMENTS = (
    [(COL_RQ, RET_WIDTH, "ret_q"), (COL_RK, RET_WIDTH, "ret_k"), (COL_RV, RET_WIDTH, None),
       (COL_RG, RET_WIDTH, None), (COL_AQ, ATT_WIDTH, "att_q"), (COL_AK, COL_GATE - COL_AK, "att_kv")]
    + [(COL_GATE + i * D_MODEL, D_MODEL, None) for i in range(N_BRANCH)])


def _win_body(x_ref, xp_ref, xn_ref, g_ref, w_ref, cw_ref, qn_ref, kn_ref, rc_ref, rs_ref, ac_ref, as_ref, o_ref,
              *, per_seq):
    tm = x_ref.shape[0]
    pos = pl.program_id(0) % per_seq
    x_ext = jnp.concatenate([xp_ref[...], x_ref[...], xn_ref[...]], axis=0)
    r_ext = lax.broadcasted_iota(jnp.int32, (tm + 2 * HALO, 1), 0)
    pad = ((r_ext < HALO) & (pos == 0)) | ((r_ext >= tm + HALO) & (pos == per_seq - 1))
    hf = jnp.where(pad, 0.0, _rms(x_ext, g_ref[...]))
    h_ext = hf.astype(BF16)
    h = hf[HALO:HALO + tm].astype(BF16)
    row = lax.broadcasted_iota(jnp.int32, (tm, HY_WIDTH), 0)
    conv = []
    for grp in range(3):
        cols = slice(COL_HY + grp * HY_WIDTH, COL_HY + (grp + 1) * HY_WIDTH)
        y_ext = _dot(h_ext, w_ref[:, cols])
        y = y_ext[HALO:HALO + tm]
        y_prev = y_ext[HALO - 1:HALO]
        y_next = y_ext[HALO + tm:HALO + tm + 1]
        prev = jnp.where(row == 0, y_prev, pltpu.roll(y, 1, 0))
        nxt = jnp.where(row == tm - 1, y_next, pltpu.roll(y, tm - 1, 0))
        cw = cw_ref[:, cols]
        conv.append(prev * cw[0:1] + y * cw[1:2] + nxt * cw[2:3])
    x0c, x1c, vc = conv
    o_ref[:, COL_X0C:COL_X0C + HY_WIDTH] = x0c.astype(o_ref.dtype)
    o_ref[:, COL_ZF:COL_ZF + HY_WIDTH] = (vc * x1c).astype(o_ref.dtype)
    o_ref[:, COL_ZF + HY_WIDTH:COL_RQ] = vc.astype(o_ref.dtype)
    for start, width, kind in _WIN_SEGMENTS:
        y = _dot(h, w_ref[:, start:start + width])
        if kind is None:
            o_ref[:, start:start + width] = y.astype(o_ref.dtype)
            continue
        for hh in range(width // LANES):
            yh = y[:, hh * LANES:(hh + 1) * LANES]
            if kind == "ret_q":
                yh = _rotate_half(yh, rc_ref[...], rs_ref[...])
            elif kind == "ret_k":
                yh = _rotate_half(yh, rc_ref[...], rs_ref[...]) * (RET_DK ** -0.5)
            elif kind == "att_q":
                yh = _axial_rope(yh, qn_ref[...], ac_ref[...], as_ref[...], ATT_HD ** -0.5 * LOG2E)
            elif hh < ATT_KV_HEADS:
                yh = _axial_rope(yh, kn_ref[...], ac_ref[...], as_ref[...], 1.0)
            o_ref[:, start + hh * LANES:start + (hh + 1) * LANES] = yh.astype(o_ref.dtype)


def _input_projection(x, g, w_stack, l, conv_w, qn, kn, pt, L, *, tm):
    T, D = x.shape
    N = w_stack.shape[2]
    per_seq = L // tm
    halo_per_tile = tm // HALO
    vec = lambda n: pl.BlockSpec((1, n), lambda i: (0, 0))
    tab = pl.BlockSpec((tm, LANES), lambda i: (i % per_seq, 0))
    return pl.pallas_call(
        functools.partial(_win_body, per_seq=per_seq),
        grid=(T // tm,),
        in_specs=[pl.BlockSpec((tm, D), lambda i: (i, 0)),
                  pl.BlockSpec((HALO, D), lambda i: (jnp.maximum(i * halo_per_tile - 1, 0), 0)),
                  pl.BlockSpec((HALO, D), lambda i: (jnp.minimum((i + 1) * halo_per_tile, T // HALO - 1), 0)),
                  vec(D),
                  pl.BlockSpec((None, D, N), lambda i: (l, 0, 0), pipeline_mode=pl.Buffered(1)),
                  pl.BlockSpec((HY_SHORT, 3 * HY_WIDTH), lambda i: (0, 0)),
                  vec(ATT_HD), vec(ATT_HD), tab, tab, tab, tab],
        out_specs=pl.BlockSpec((tm, N), lambda i: (i, 0)),
        out_shape=jax.ShapeDtypeStruct((T, N), BF16),
        compiler_params=_params(("parallel",)),
        name="input_projection",
    )(x, x, x, g.reshape(1, D), w_stack, conv_w, qn.reshape(1, ATT_HD), kn.reshape(1, ATT_HD),
      pt["ret_cos"], pt["ret_sin"], pt["att_cos"], pt["att_sin"])


def _filt_body(f_ref, w1a_ref, w1b_ref, b1_ref, w2_ref, b2_ref, w3a_ref, w3b_ref, dl_ref, o_ref, *, L):
    tl = f_ref.shape[0]
    f = f_ref[...]
    fa, fb = _split(f[:tl // 2]), _split(f[tl // 2:])
    pre = _dot3(fa, _split(w1a_ref[...])) + _dot3(fb, _split(w1b_ref[...]))
    h = jnp.sin(HY_SIN_FREQ * (pre + b1_ref[...]))
    h = jnp.sin(HY_SIN_FREQ * (_bdot(h, w2_ref[...]) + b2_ref[...]))
    filt = jnp.concatenate([_bdot(h, w3a_ref[...]), _bdot(h, w3b_ref[...])], axis=0)
    mod = jnp.exp(-f[:, 0:1] * dl_ref[...]) + HY_SHIFT
    row = pl.program_id(1) * tl + lax.broadcasted_iota(jnp.int32, (tl, HY_WIDTH), 0)
    sel = jnp.where(row < L, filt[:, :HY_WIDTH], filt[:, HY_WIDTH:])
    o_ref[...] = jnp.where(row == L, 0.0, sel * mod)


def _hyena_features(L):
    t = jnp.linspace(0.0, 1.0, L, dtype=F32)[:, None]
    bands = (HY_EMB - 1) // 2
    w = 2.0 * math.pi * jnp.arange(L, dtype=F32)[:, None] / L
    f = jnp.linspace(1e-4, bands - 1, bands, dtype=F32)[None, :]
    z = jnp.concatenate([t, jnp.cos(f * w), -jnp.sin(f * w)], axis=-1)
    zc = jnp.concatenate([z, z[:1], jnp.flip(z[1:], axis=0)], axis=0)
    return jnp.pad(zc, ((0, 0), (0, LANES - HY_EMB)))


def _hyena_filter(feats, w1, b1, w2, b2, w3, L):
    tl = TOKEN_TILE
    NL = w1.shape[0]
    assert 2 * HY_FH == LANES
    zw = jnp.zeros_like(w2)
    w1a = jnp.pad(w1, ((0, 0), (0, LANES - HY_EMB), (0, HY_FH)))
    w1b = jnp.pad(w1, ((0, 0), (0, LANES - HY_EMB), (HY_FH, 0)))
    b1p = jnp.concatenate([b1, b1], axis=1).reshape(NL, 1, LANES)
    w2p = jnp.concatenate([jnp.concatenate([w2, zw], axis=2), jnp.concatenate([zw, w2], axis=2)], axis=1)
    b2p = jnp.concatenate([b2, b2], axis=1).reshape(NL, 1, LANES)
    w3a = jnp.pad(w3, ((0, 0), (0, HY_FH), (0, 0)))
    w3b = jnp.pad(w3, ((0, 0), (HY_FH, 0), (0, 0)))
    min_decay = math.log(HY_TARGET) / HY_SLOW
    max_decay = math.log(HY_TARGET) / HY_FAST
    deltas = jnp.abs(jnp.linspace(min_decay, max_decay, HY_WIDTH, dtype=F32)).reshape(1, HY_WIDTH)
    layer = lambda r, c: pl.BlockSpec((None, r, c), lambda l, i: (l, 0, 0))
    return pl.pallas_call(
        functools.partial(_filt_body, L=L),
        grid=(NL, 2 * L // tl),
        in_specs=[pl.BlockSpec((tl, LANES), lambda l, i: (i, 0)),
                  layer(LANES, LANES), layer(LANES, LANES), layer(1, LANES), layer(LANES, LANES), layer(1, LANES),
                  layer(LANES, 2 * HY_WIDTH), layer(LANES, 2 * HY_WIDTH),
                  pl.BlockSpec((1, HY_WIDTH), lambda l, i: (0, 0))],
        out_specs=pl.BlockSpec((None, tl, HY_WIDTH), lambda l, i: (l, i, 0)),
        out_shape=jax.ShapeDtypeStruct((NL, 2 * L, HY_WIDTH), F32),
        compiler_params=_params(("parallel", "parallel")),
        name="hyena_filter",
    )(feats, w1a, w1b, b1p, w2p, b2p, w3a, w3b, deltas)


def _expanded(fn, rows, cols):
    small = fn(lax.broadcasted_iota(jnp.int32, (rows, cols), 0), lax.broadcasted_iota(jnp.int32, (rows, cols), 1))
    hp = functools.partial(jnp.dot, precision=lax.Precision.HIGHEST)
    pick_r = (lax.broadcasted_iota(jnp.int32, (rows * DFT_NB, rows), 0) // DFT_NB
              == lax.broadcasted_iota(jnp.int32, (rows * DFT_NB, rows), 1)).astype(F32)
    pick_c = (lax.broadcasted_iota(jnp.int32, (cols, cols * DFT_NB), 1) // DFT_NB
              == lax.broadcasted_iota(jnp.int32, (cols, cols * DFT_NB), 0)).astype(F32)
    ri = lax.broadcasted_iota(jnp.int32, (rows * DFT_NB, cols * DFT_NB), 0)
    ci = lax.broadcasted_iota(jnp.int32, (rows * DFT_NB, cols * DFT_NB), 1)
    return jnp.where(ri % DFT_NB == ci % DFT_NB, hp(hp(pick_r, small), pick_c), 0.0).astype(BF16)


def _dft_tables(L):
    N = 2 * L
    N2 = DFT_N2
    N1 = N // N2
    N1h = N1 // 2

    def cs1(k1, n1):
        ang = ((k1 * n1) % N1).astype(F32) * (2.0 * math.pi / N1)
        return jnp.cos(ang), jnp.sin(ang)

    def f1_pair(r, c):
        cs, sn = cs1(r // 2, c % N1h)
        re_row, first = (r % 2) == 0, c < N1h
        return jnp.where(re_row, jnp.where(first, cs, sn), jnp.where(first, -sn, cs))

    def f1_real(r, c):
        cs, sn = cs1(r // 2, c)
        return jnp.where((r % 2) == 0, cs, -sn)

    def f3_pair(r, c):
        cs, sn = cs1(c // 2, r % N1h)
        re_out, re_in = r < N1h, (c % 2) == 0
        return jnp.where(re_out, jnp.where(re_in, cs, -sn), jnp.where(re_in, sn, cs)) * (1.0 / N)

    kk = (jnp.arange(N1, dtype=jnp.int32)[:, None, None]
          + N1 * jnp.arange(N2, dtype=jnp.int32)[None, :, None])
    n2 = jnp.arange(N2, dtype=jnp.int32)[None, None, :]
    a2 = ((kk * n2) % N).astype(F32) * (2.0 * math.pi / N)
    gr, gi = jnp.cos(a2), -jnp.sin(a2)
    g_fwd = jnp.concatenate([jnp.concatenate([gr, -gi], axis=2),
                             jnp.concatenate([gi, gr], axis=2)], axis=1)
    return dict(N1=N1, N1h=N1h, N2=N2,
                f1_pair=_expanded(f1_pair, 2 * N1, 2 * N1h), f1_real=_expanded(f1_real, 2 * N1, N1),
                f3_pair=_expanded(f3_pair, 2 * N1h, 2 * N1), g_fwd=g_fwd.astype(BF16))


def _dft1_body(x_ref, f_ref, o_ref):
    P, K, NB, C = x_ref.shape
    y = _bdot(f_ref[...], x_ref[...].reshape(P * K * NB, C))
    o_ref[0] = y.astype(o_ref.dtype).reshape(o_ref.shape[1:])


def _dft_stage1(x, f, *, P, col=0):
    Bx, K, N2, _ = x.shape
    C = HY_WIDTH
    N1 = f.shape[0] // (2 * DFT_NB)
    return pl.pallas_call(
        _dft1_body,
        grid=(Bx // P, N2 // DFT_NB),
        in_specs=[pl.BlockSpec((P, K, DFT_NB, C), lambda b, j: (b, 0, j, col)),
                  pl.BlockSpec(f.shape, lambda b, j: (0, 0), pipeline_mode=pl.Buffered(1))],
        out_specs=pl.BlockSpec((1, N1, 2, DFT_NB, C), lambda b, j: (b, 0, 0, j, 0)),
        out_shape=jax.ShapeDtypeStruct((Bx // P, N1, 2, N2, C), BF16),
        compiler_params=_params(("parallel", "parallel")),
        name="dft_stage1",
    )(x, f)


def _spec_body(a_ref, g_ref, o_ref):
    def step(k, carry):
        o_ref[k] = _bdot(g_ref[k], a_ref[0, k])
        return carry

    lax.fori_loop(0, a_ref.shape[1], step, 0, unroll=DFT_UNROLL)


def _dft_spectrum(a, g, *, kb):
    NL, N1, R, C = a.shape
    return pl.pallas_call(
        _spec_body,
        grid=(N1 // kb, NL),
        in_specs=[pl.BlockSpec((1, kb, R, C), lambda i, l: (l, i, 0, 0)),
                  pl.BlockSpec((kb, R, R), lambda i, l: (i, 0, 0))],
        out_specs=pl.BlockSpec((None, kb, R, C), lambda i, l: (l, i, 0, 0)),
        out_shape=jax.ShapeDtypeStruct((NL, N1, R, C), F32),
        compiler_params=_params(("parallel", "parallel")),
        name="dft_spectrum",
    )(a, g)


def _dft2_body(a_ref, g_ref, ks_ref, o_ref):
    H = a_ref.shape[2] // 2

    def step(k, carry):
        y = _bdot(g_ref[k], a_ref[0, k])
        yr, yi = y[:H], y[H:]
        kr, ki = ks_ref[k, :H, :], ks_ref[k, H:, :]
        z = jnp.concatenate([yr * kr - yi * ki, yr * ki + yi * kr], axis=0)
        o_ref[0, k] = _dot_tn(g_ref[k], z.astype(g_ref.dtype)).astype(o_ref.dtype)
        return carry

    lax.fori_loop(0, a_ref.shape[1], step, 0, unroll=DFT_UNROLL)


def _dft_stage2(a, g, ks, l, *, kb):
    Bp, N1, R, C = a.shape
    dat = pl.BlockSpec((1, kb, R, C), lambda i, b: (b, i, 0, 0))
    return pl.pallas_call(
        _dft2_body,
        grid=(N1 // kb, Bp),
        in_specs=[dat, pl.BlockSpec((kb, R, R), lambda i, b: (i, 0, 0)),
                  pl.BlockSpec((None, kb, R, C), lambda i, b: (l, i, 0, 0))],
        out_specs=dat,
        out_shape=jax.ShapeDtypeStruct((Bp, N1, R, C), BF16),
        compiler_params=_params(("parallel", "parallel")),
        name="dft_stage2",
    )(a, g, ks)


def _dft3_body(t_ref, f_ref, zf_ref, x0_ref, bias_ref, o_ref):
    _, N1, _, NB, C = t_ref.shape
    y = _bdot(f_ref[...], t_ref[0].reshape(N1 * 2 * NB, C)).reshape(zf_ref.shape)
    o_ref[...] = ((y + zf_ref[...].astype(F32) * bias_ref[...]) * x0_ref[...].astype(F32)).astype(o_ref.dtype)


def _dft_stage3(t, f, proj4, bias):
    Bp, N1, _, N2, C = t.shape
    B, N1h = proj4.shape[:2]
    pair = lambda col=0: pl.BlockSpec((2, N1h, DFT_NB, C), lambda b, j: (b, 0, j, col))
    return pl.pallas_call(
        _dft3_body,
        grid=(Bp, N2 // DFT_NB),
        in_specs=[pl.BlockSpec((1, N1, 2, DFT_NB, C), lambda b, j: (b, 0, 0, j, 0)),
                  pl.BlockSpec(f.shape, lambda b, j: (0, 0), pipeline_mode=pl.Buffered(1)),
                  pair(COL_ZF // C), pair(COL_X0C // C),
                  pl.BlockSpec((1, C), lambda b, j: (0, 0))],
        out_specs=pair(),
        out_shape=jax.ShapeDtypeStruct((B, N1h, N2, C), BF16),
        compiler_params=_params(("parallel", "parallel")),
        name="dft_stage3",
    )(t, f, proj4, proj4, bias)


def _hyena_mixer(proj, kspec, l, bias, tabs):
    B, L, _ = proj.shape
    N1, N1h, N2 = tabs["N1"], tabs["N1h"], tabs["N2"]
    C = HY_WIDTH
    proj4 = proj.reshape(B, N1h, N2, IN_COLS)
    a = _dft_stage1(proj4, tabs["f1_pair"], P=2, col=COL_ZF // C)
    t = _dft_stage2(a.reshape(B // 2, N1, 2 * N2, C), tabs["g_fwd"], kspec, l, kb=DFT_KB)
    y = _dft_stage3(t.reshape(B // 2, N1, 2, N2, C), tabs["f3_pair"], proj4, bias.astype(F32).reshape(1, C))
    return y.reshape(B * L, C)


def _hyena_spectra(feats, w1, b1, w2, b2, w3, L, tabs):
    N1, N2 = tabs["N1"], tabs["N2"]
    C = HY_WIDTH
    NL = w1.shape[0]
    kfull = _hyena_filter(feats, w1, b1, w2, b2, w3, L)
    a = _dft_stage1(kfull.reshape(NL, N1, N2, C), tabs["f1_real"], P=1)
    return _dft_spectrum(a.reshape(NL, N1, 2 * N2, C), tabs["g_fwd"], kb=DFT_KB)


def _ret_body(lg_ref, q_ref, k_ref, v_ref, g_ref, o_ref, st_ref):
    C = RET_BLOCK
    L = q_ref.shape[1]
    nblk = L // C
    dk = RET_DK
    HG = q_ref.shape[2] // dk
    ii = lax.broadcasted_iota(jnp.int32, (C, C), 0)
    jj = lax.broadcasted_iota(jnp.int32, (C, C), 1)
    diff = (ii - jj).astype(F32)
    pos = lax.broadcasted_iota(jnp.int32, (C, dk), 0).astype(F32)
    dm, qw_f, kw_f, qw_b, kw_b, cd_f, cd_b = [], [], [], [], [], [], []
    for hg in range(HG):
        h = pl.program_id(1) * HG + hg
        lgf = lg_ref[0, h]
        lgb = lg_ref[1, h]
        dm.append(jnp.where(diff >= 0, jnp.exp(jnp.maximum(diff, 0.0) * lgf), jnp.exp(jnp.maximum(-diff, 0.0) * lgb)))
        qw_f.append(jnp.exp((pos + 1.0) * lgf))
        kw_f.append(jnp.exp((C - 1.0 - pos) * lgf))
        qw_b.append(jnp.exp((C - pos) * lgb))
        kw_b.append(jnp.exp(pos * lgb))
        cd_f.append(jnp.exp(C * lgf))
        cd_b.append(jnp.exp(C * lgb))

    def block_kv(n, carry):
        sl = pl.ds(pl.multiple_of(n * C, C), C)
        for hg in range(HG):
            cols = slice(hg * dk, (hg + 1) * dk)
            k = k_ref[0, sl, cols].astype(F32)
            kk = jnp.concatenate([k * kw_f[hg], k * kw_b[hg]], axis=1).astype(BF16)
            st_ref[hg, n] = _dot_tn(kk, v_ref[0, sl, cols].astype(BF16))
        return carry

    lax.fori_loop(0, nblk, block_kv, 0, unroll=RET_UNROLL)

    for hg in range(HG):
        def scan_f(n, S, hg=hg):
            kv = st_ref[hg, n, :dk, :]
            st_ref[hg, n, :dk, :] = S
            return S * cd_f[hg] + kv

        lax.fori_loop(0, nblk, scan_f, jnp.zeros((dk, RET_DV), F32))

        def scan_b(m, S, hg=hg):
            n = nblk - 1 - m
            kv = st_ref[hg, n, dk:, :]
            st_ref[hg, n, dk:, :] = S
            return S * cd_b[hg] + kv

        lax.fori_loop(0, nblk, scan_b, jnp.zeros((dk, RET_DV), F32))

    def block_out(n, carry):
        sl = pl.ds(pl.multiple_of(n * C, C), C)
        for hg in range(HG):
            cols = slice(hg * dk, (hg + 1) * dk)
            q = q_ref[0, sl, cols].astype(BF16)
            scores = _dot_nt(q, k_ref[0, sl, cols].astype(BF16)) * dm[hg]
            qf = q.astype(F32)
            qq = jnp.concatenate([qf * qw_f[hg], qf * qw_b[hg]], axis=1).astype(BF16)
            o = _dot(scores.astype(BF16), v_ref[0, sl, cols].astype(BF16)) + _dot(qq, st_ref[hg, n].astype(BF16))
            mu = jnp.mean(o, axis=-1, keepdims=True)
            oc = o - mu
            on = oc * lax.rsqrt(jnp.mean(oc * oc, axis=-1, keepdims=True) + EPS)
            g = g_ref[0, sl, cols].astype(F32)
            o_ref[0, sl, cols] = (g * _sigmoid(g) * on).astype(o_ref.dtype)
        return carry

    lax.fori_loop(0, nblk, block_out, 0, unroll=RET_UNROLL)


def _retention_mixer(proj, log_gamma):
    B, L, _ = proj.shape
    nblk = L // RET_BLOCK
    HG = min(RET_HEADS, max(1, RET_UNROLL // nblk))
    width = HG * RET_DK
    blk = lambda col: pl.BlockSpec((1, L, width), lambda b, h, c=col // width: (b, 0, c + h))
    y = pl.pallas_call(
        _ret_body,
        grid=(B, RET_HEADS // HG),
        in_specs=[pl.BlockSpec(memory_space=pltpu.SMEM),
                  blk(COL_RQ), blk(COL_RK), blk(COL_RV), blk(COL_RG)],
        out_specs=pl.BlockSpec((1, L, width), lambda b, h: (b, 0, h)),
        out_shape=jax.ShapeDtypeStruct((B, L, RET_WIDTH), BF16),
        scratch_shapes=[pltpu.VMEM((HG, nblk, 2 * RET_DK, RET_DV), F32)],
        compiler_params=_params(("parallel", "parallel")),
        name="retention",
    )(log_gamma, proj, proj, proj, proj)
    return y.reshape(B * L, RET_WIDTH)


def _att_body(q_ref, k_ref, v_ref, o_ref):
    G = ATT_HEADS // ATT_KV_HEADS
    tq = q_ref.shape[1]
    kc = min(ATT_KEY_CHUNK, k_ref.shape[1])
    nchunk = k_ref.shape[1] // kc
    q = q_ref[0].astype(BF16)
    qs = jnp.concatenate([q[:, g * ATT_HD:(g + 1) * ATT_HD] for g in range(G)], axis=0)
    m = jnp.full((G * tq, 1), -jnp.inf, F32)
    l = jnp.zeros((G * tq, 1), F32)
    acc = jnp.zeros((G * tq, ATT_HD), F32)
    s_next = _dot_nt(qs, k_ref[0, 0:kc, :].astype(BF16))
    for c in range(nchunk):
        s = s_next
        if c + 1 < nchunk:
            s_next = _dot_nt(qs, k_ref[0, (c + 1) * kc:(c + 2) * kc, :].astype(BF16))
        m_new = jnp.maximum(m, jnp.max(s, axis=-1, keepdims=True))
        alpha = jnp.exp2(m - m_new)
        p = jnp.exp2(s - m_new)
        l = alpha * l + jnp.sum(p, axis=-1, keepdims=True)
        acc = alpha * acc + _dot(p.astype(BF16), v_ref[0, c * kc:(c + 1) * kc, :].astype(BF16))
        m = m_new
    o = acc * (1.0 / l)
    for g in range(G):
        o_ref[0, :, g * ATT_HD:(g + 1) * ATT_HD] = o[g * tq:(g + 1) * tq].astype(o_ref.dtype)


def _attention_mixer(proj, *, tq):
    B, L, _ = proj.shape
    G = ATT_HEADS // ATT_KV_HEADS
    tq = min(tq, L)
    y = pl.pallas_call(
        _att_body,
        grid=(B, ATT_KV_HEADS, L // tq),
        in_specs=[pl.BlockSpec((1, tq, G * ATT_HD), lambda b, h, i: (b, i, COL_AQ // (G * ATT_HD) + h)),
                  pl.BlockSpec((1, L, ATT_HD), lambda b, h, i: (b, 0, COL_AK // ATT_HD + h)),
                  pl.BlockSpec((1, L, ATT_HD), lambda b, h, i: (b, 0, COL_AV // ATT_HD + h))],
        out_specs=pl.BlockSpec((1, tq, G * ATT_HD), lambda b, h, i: (b, i, h)),
        out_shape=jax.ShapeDtypeStruct((B, L, ATT_WIDTH), BF16),
        compiler_params=_params(("parallel", "parallel", "arbitrary")),
        name="attention",
    )(proj, proj, proj)
    return y.reshape(B * L, ATT_WIDTH)


def _tail_body(yh_ref, yr_ref, ya_ref, ga_ref, gb_ref, x_ref, kv_ref, wb_ref, wo_ref, gmix_ref,
               gxpre_ref, wq_ref, wxo_ref, gxpost_ref, gfpre_ref, w1_ref, w2_ref, gfpost_ref, o_ref, u_ref):
    D = D_MODEL
    ph = _dot(yh_ref[0], wb_ref[0:HY_WIDTH, :])
    pr = _dot(yr_ref[0], wb_ref[HY_WIDTH:HY_WIDTH + RET_WIDTH, :])
    pa = _dot(ya_ref[0], wb_ref[HY_WIDTH + RET_WIDTH:, :])
    ga = ga_ref[0].astype(F32)
    gb = gb_ref[0].astype(F32)
    hd = D // 2
    g0 = _sigmoid(ga[:, :D])
    g1 = _sigmoid(jnp.concatenate([ga[:, D:], gb[:, :hd]], axis=1))
    g2 = _sigmoid(gb[:, hd:])
    merged = g0 * ph + g1 * pr + g2 * pa
    x = x_ref[0] + _rms(_dot(merged.astype(BF16), wo_ref[...]), gmix_ref[...])
    h = _rms(x, gxpre_ref[...]).astype(BF16)
    q = (_dot(h, wq_ref[...]) * (X_HD ** -0.5)).astype(BF16)
    kv = kv_ref[0]
    outs = []
    for hh in range(X_HEADS):
        sl = slice(hh * X_HD, (hh + 1) * X_HD)
        s = _dot_nt(q[:, sl], kv[:, sl])
        p = jnp.exp(s - jnp.max(s, axis=-1, keepdims=True))
        l = jnp.sum(p, axis=-1, keepdims=True)
        vh = kv[:, D + hh * X_HD:D + (hh + 1) * X_HD]
        outs.append(_dot(p.astype(BF16), vh) * (1.0 / l))
    o = jnp.concatenate(outs, axis=1).astype(BF16)
    x = x + _rms(_dot(o, wxo_ref[...]), gxpost_ref[...])
    h = _rms(x, gfpre_ref[...]).astype(BF16)
    for c in range(D_FF // FF_CHUNK):
        sl = slice(c * FF_CHUNK, (c + 1) * FF_CHUNK)
        u_ref[:, sl] = jnp.square(jnp.maximum(_dot(h, w1_ref[:, sl]), 0.0)).astype(BF16)
    o_ref[0] = x + _rms(_dot(u_ref[...], w2_ref[...]), gfpost_ref[...])


def _layer_tail(yh, yr, ya, proj, x, kv, w, l, *, tm):
    B, L, D = x.shape
    M = kv.shape[2]
    gw = N_BRANCH * D // 2
    tok = lambda width, col=0: pl.BlockSpec((1, tm, width), lambda b, i, col=col: (b, i, col))
    vec = pl.BlockSpec((1, D), lambda b, i: (0, 0))
    res = lambda shape: pl.BlockSpec((None,) + shape, lambda b, i: (l, 0, 0), pipeline_mode=pl.Buffered(1))
    g = lambda name: w[name][l].reshape(1, D)
    return pl.pallas_call(
        _tail_body,
        grid=(B, L // tm),
        in_specs=[tok(HY_WIDTH), tok(RET_WIDTH), tok(ATT_WIDTH), tok(gw, COL_GATE // gw), tok(gw, COL_GATE // gw + 1),
                  tok(D), pl.BlockSpec((None, 1, M, 2 * D), lambda b, i: (l, b, 0, 0)),
                  res((MIX_WIDTH, D)), res((D, D)), vec,
                  vec, res((D, D)), res((D, D)), vec,
                  vec, res((D, D_FF)), res((D_FF, D)), vec],
        out_specs=tok(D),
        out_shape=jax.ShapeDtypeStruct((B, L, D), F32),
        scratch_shapes=[pltpu.VMEM((tm, D_FF), BF16)],
        compiler_params=_params(("parallel", "parallel")),
        name="layer_tail",
    )(yh, yr, ya, proj, proj, x, kv, w["w_branch"], w["w_out"], g("g_mix_post"),
      g("g_x_pre"), w["w_xq"], w["w_xo"], g("g_x_post"),
      g("g_ff_pre"), w["w_ff1"], w["w_ff2"], g("g_ff_post"))


def _rotary_tables(pos, dim, theta):
    inv = theta ** (-jnp.arange(0, dim, 2, dtype=F32) / dim)
    ang = pos.astype(F32)[:, None] * inv[None, :]
    return jnp.cos(ang), jnp.sin(ang)


def _position_tables(L):
    pos = jnp.arange(L)
    rows = L // GRID_W
    row_ids = jnp.repeat(jnp.arange(rows), GRID_W)
    col_ids = jnp.tile(jnp.arange(GRID_W), rows)
    rcos, rsin = _rotary_tables(pos, RET_DK, RET_THETA)
    rc, rs = _rotary_tables(row_ids, ATT_HD // 2, ROPE_THETA)
    cc, cs = _rotary_tables(col_ids, ATT_HD // 2, ROPE_THETA)
    return dict(
        ret_cos=jnp.concatenate([rcos, rcos], axis=-1),
        ret_sin=jnp.concatenate([-rsin, rsin], axis=-1),
        att_cos=jnp.concatenate([rc, rc, cc, cc], axis=-1),
        att_sin=jnp.concatenate([-rs, rs, -cs, cs], axis=-1),
        feats=_hyena_features(L),
        dft=_dft_tables(L),
    )


def _trunk(x, mem, w, pt):
    B, L, D = x.shape
    M = mem.shape[1]
    kspec = _hyena_spectra(pt["feats"], w["hy_fw1"], w["hy_fb1"], w["hy_fw2"], w["hy_fb2"], w["hy_fw3"], L, pt["dft"])
    kv = _norm_matmul(mem.reshape(B * M, D), w["g_mem"], w["w_xkv"], tm=TOKEN_TILE, tn=FF_CHUNK)
    kv = kv.reshape(-1, B, M, 2 * D)
    for l in range(DEPTH):
        proj = _input_projection(x.reshape(B * L, D), w["g_mix_pre"][l], w["w_in"], l, w["hy_conv"][l],
                                 w["att_qnorm"][l], w["att_knorm"][l], pt, L, tm=TOKEN_TILE).reshape(B, L, IN_COLS)
        y_h = _hyena_mixer(proj, kspec, l, w["hy_bias"][l], pt["dft"])
        y_r = _retention_mixer(proj, w["ret_lg"][l])
        y_a = _attention_mixer(proj, tq=ATT_QUERY_TILE)
        x = _layer_tail(y_h.reshape(B, L, HY_WIDTH), y_r.reshape(B, L, RET_WIDTH), y_a.reshape(B, L, ATT_WIDTH),
                        proj, x, kv, w, l, tm=TOKEN_TILE)
    return x


def kernel(x_prompt, x_sample, mem_prompt, mem_sample, g_mix_pre, g_mix_post, w_in, hy_conv, hy_fw1, hy_fb1,
           hy_fw2, hy_fb2, hy_fw3, hy_bias, ret_decay, att_qnorm, att_knorm, w_branch, w_out, g_x_pre, g_x_post,
           g_mem, w_xq, w_xkv, w_xo, g_ff_pre, g_ff_post, w_ff1, w_ff2):
    w = dict(
        g_mix_pre=g_mix_pre, g_mix_post=g_mix_post, w_in=w_in.astype(BF16), hy_conv=hy_conv,
        hy_fw1=hy_fw1, hy_fb1=hy_fb1, hy_fw2=hy_fw2, hy_fb2=hy_fb2, hy_fw3=hy_fw3, hy_bias=hy_bias,
        ret_lg=jax.nn.log_sigmoid(ret_decay.astype(F32)), att_qnorm=att_qnorm, att_knorm=att_knorm,
        w_branch=w_branch.astype(BF16), w_out=w_out.astype(BF16), g_x_pre=g_x_pre, g_x_post=g_x_post,
        g_mem=g_mem, w_xq=w_xq.astype(BF16), w_xkv=w_xkv.astype(BF16), w_xo=w_xo.astype(BF16),
        g_ff_pre=g_ff_pre, g_ff_post=g_ff_post, w_ff1=w_ff1.astype(BF16), w_ff2=w_ff2.astype(BF16),
    )
    y_prompt = _trunk(x_prompt, mem_prompt, w, _position_tables(x_prompt.shape[1]))
    y_sample = _trunk(x_sample, mem_sample, w, _position_tables(x_sample.shape[1]))
    return (y_prompt, y_sample)
```
